```python
import math
import jax, jax.numpy as jnp
from jax import lax
import numpy as np

D_MODEL = 2048
BATCH = 4
SEQ = 2048
DEPTH = 2
DEC_BATCH = 8
DEC_SEQ = 8
PAST_LEN = 16384
PAGE_SIZE = 128

DH = 128
H_A = 8
HKV_A = 2
H_IDX = 16
D_IDX = 64
TOPK_IDX = 256
H_B = 8
HKV_B = 2
MOBA_BLOCK = 256
MOBA_TOPK = 3
H_C = 8
DK_C = 128
DV_C = 256
N_BUCKETS = 32
MAX_DIST = 128
D_FF = 5504
CONV_W = 3
EPS = 1e-6
ROPE_BASE = 10000.0
Q_BLOCK_A = 128
Q_BLOCK_B = 16
RET_CHUNK = 128

SPLITS = (H_A * DH, HKV_A * DH, HKV_A * DH, H_IDX * D_IDX, D_IDX, H_IDX,
          H_B * DH, HKV_B * DH, HKV_B * DH,
          H_C * DK_C, H_C * DK_C, H_C * DV_C, H_C * DV_C,
          3 * D_MODEL)
N_IN = sum(SPLITS)

kernel_name = 'hybrid_dsa_moba_retention_convffn_step'


def rmsnorm(x, g):
    xf = x.astype(jnp.float32)
    y = xf * lax.rsqrt(jnp.mean(xf * xf, axis=-1, keepdims=True) + EPS)
    return (y * g.astype(jnp.float32)).astype(x.dtype)


def t5_bucket(rel):
    n = jnp.maximum(rel, 0)
    max_exact = N_BUCKETS // 2
    nf = jnp.maximum(n, 1).astype(jnp.float32)
    large = max_exact + (jnp.log(nf / max_exact) / math.log(MAX_DIST / max_exact)
                         * (N_BUCKETS - max_exact)).astype(jnp.int32)
    large = jnp.minimum(large, N_BUCKETS - 1)
    return jnp.where(n < max_exact, n, large)


def to_blocks(x, qb):
    b, t = x.shape[:2]
    return jnp.moveaxis(x.reshape(b, t // qb, qb, *x.shape[2:]), 1, 0)


def from_blocks(y):
    y = jnp.moveaxis(y, 0, 1)
    return y.reshape(y.shape[0], y.shape[1] * y.shape[2], *y.shape[3:])


def rotary(x, pos):
    d = x.shape[-1]
    inv = ROPE_BASE ** (-jnp.arange(0, d, 2, dtype=jnp.float32) / d)
    ang = pos.astype(jnp.float32)[:, None] * inv[None, :]
    cos = jnp.cos(ang)[None, :, None, :]
    sin = jnp.sin(ang)[None, :, None, :]
    xf = x.astype(jnp.float32)
    x1, x2 = xf[..., :d // 2], xf[..., d // 2:]
    return jnp.concatenate([x1 * cos - x2 * sin, x1 * sin + x2 * cos], axis=-1)


def dsa_attention(q, k, v, q_idx, k_idx, w_idx, q_pos, bias_tab):
    b, t = q.shape[:2]
    l = k.shape[1]
    n_sel = min(TOPK_IDX, l // 4)
    grp = H_A // HKV_A
    qb = math.gcd(t, Q_BLOCK_A)
    k_pos = jnp.arange(l, dtype=jnp.int32)
    kif = k_idx.astype(jnp.float32)

    def one_block(xs):
        qh, qi, wi, qp = xs
        s = jnp.einsum('bqhd,bsd->bqhs', qi.astype(jnp.float32), kif) * (D_IDX ** -0.5)
        score = jnp.einsum('bqh,bqhs->bqs', wi.astype(jnp.float32) * (H_IDX ** -0.5), jax.nn.relu(s))
        score = jnp.where(k_pos[None, None, :] <= qp[None, :, None], score, -jnp.inf)
        _, sel = lax.top_k(score, n_sel)
        valid = sel <= qp[None, :, None]
        ks = jax.vmap(lambda kk, ii: kk[ii])(k, sel).astype(jnp.float32)
        vs = jax.vmap(lambda vv, ii: vv[ii])(v, sel).astype(jnp.float32)
        qg = qh.astype(jnp.float32).reshape(b, qb, HKV_A, grp, DH)
        logits = jnp.einsum('bqgrd,bqngd->bqgrn', qg, ks).reshape(b, qb, H_A, n_sel) * (DH ** -0.5)
        bias = jnp.transpose(bias_tab[t5_bucket(qp[None, :, None] - sel)], (0, 1, 3, 2))
        logits = jnp.where(valid[:, :, None, :], logits + bias, -jnp.inf)
        p = jax.nn.softmax(logits, axis=-1).reshape(b, qb, HKV_A, grp, n_sel)
        o = jnp.einsum('bqgrn,bqngd->bqgrd', p, vs)
        return o.reshape(b, qb, H_A * DH).astype(q.dtype)

    out = lax.map(one_block, (to_blocks(q, qb), to_blocks(q_idx, qb), to_blocks(w_idx, qb),
                              q_pos.reshape(t // qb, qb)))
    return from_blocks(out)


def moba_attention(q, k, v, q_pos, bias_tab):
    b, t = q.shape[:2]
    l = k.shape[1]
    grp = H_B // HKV_B
    nb = -(-l // MOBA_BLOCK)
    pad = nb * MOBA_BLOCK - l

    def blockify(x):
        x = jnp.pad(x, ((0, 0), (0, pad), (0, 0), (0, 0)))
        return jnp.transpose(x.reshape(b, nb, MOBA_BLOCK, HKV_B, DH), (0, 1, 3, 2, 4))

    kb, vb = blockify(k), blockify(v)
    k_mean = jnp.mean(kb.astype(jnp.float32), axis=3)
    n_sel = min(MOBA_TOPK, (l - 1) // MOBA_BLOCK)
    qb = math.gcd(t, Q_BLOCK_B)
    offs = jnp.arange(MOBA_BLOCK, dtype=jnp.int32)
    b_idx = jnp.arange(b)[:, None, None, None]
    h_kv = (jnp.arange(H_B) // grp)[None, None, :, None]
    h_all = jnp.arange(H_B)[None, None, :, None, None]
    scale = DH ** -0.5

    def one_block(xs):
        qh, qp = xs
        qf = qh.astype(jnp.float32)
        qg = qf.reshape(b, qb, HKV_B, grp, DH)
        own = qp // MOBA_BLOCK
        ko = kb[:, own].astype(jnp.float32)
        vo = vb[:, own].astype(jnp.float32)
        rel_o = qp[:, None] - (own[:, None] * MOBA_BLOCK + offs[None, :])
        lo = jnp.einsum('bqgrd,bqgsd->bqgrs', qg, ko).reshape(b, qb, H_B, MOBA_BLOCK) * scale
        bias_o = jnp.transpose(bias_tab[t5_bucket(rel_o)], (0, 2, 1))
        lo = jnp.where((rel_o >= 0)[None, :, None, :], lo + bias_o[None], -jnp.inf)
        vo_h = jnp.repeat(vo, grp, axis=2)
        if n_sel == 0:
            p = jax.nn.softmax(lo, axis=-1)
            o = jnp.einsum('bqhs,bqhsd->bqhd', p, vo_h)
        else:
            gate = jnp.einsum('bqgrd,bngd->bqgrn', qg, k_mean).reshape(b, qb, H_B, nb)
            past = jnp.arange(nb)[None, :] < own[:, None]
            gate = jnp.where(past[None, :, None, :], gate, -jnp.inf)
            _, sel = lax.top_k(gate, n_sel)
            valid = sel < own[None, :, None, None]
            ks = kb[b_idx, sel, h_kv].astype(jnp.float32)
            vs = vb[b_idx, sel, h_kv].astype(jnp.float32)
            rel_s = qp[None, :, None, None, None] - (sel[..., None] * MOBA_BLOCK + offs)
            ls = jnp.einsum('bqhd,bqhnsd->bqhns', qf, ks) * scale + bias_tab[t5_bucket(rel_s), h_all]
            ls = jnp.where(valid[..., None], ls, -jnp.inf).reshape(b, qb, H_B, n_sel * MOBA_BLOCK)
            logits = jnp.concatenate([ls, lo], axis=-1)
            vals = jnp.concatenate([vs.reshape(b, qb, H_B, n_sel * MOBA_BLOCK, DH), vo_h], axis=3)
            p = jax.nn.softmax(logits, axis=-1)
            o = jnp.einsum('bqhs,bqhsd->bqhd', p, vals)
        return o.reshape(b, qb, H_B * DH).astype(q.dtype)

    out = lax.map(one_block, (to_blocks(q, qb), q_pos.reshape(t // qb, qb)))
    return from_blocks(out)


def retention(q, k, v, s0):
    t = q.shape[1]
    c = math.gcd(t, RET_CHUNK)
    log_g = jnp.log1p(-jnp.exp2(-5.0 - jnp.arange(H_C, dtype=jnp.float32)))
    i = jnp.arange(c, dtype=jnp.float32)
    diff = i[:, None] - i[None, :]
    intra = jnp.where(diff[None] >= 0, jnp.exp(jnp.maximum(diff, 0.0)[None] * log_g[:, None, None]), 0.0)
    cross = jnp.exp((i[:, None] + 1.0) * log_g[None, :])
    tail = jnp.exp((c - 1.0 - i)[:, None] * log_g[None, :])
    chunk_decay = jnp.exp(c * log_g)

    def step(s, xs):
        qc, kc, vc = xs
        att = jnp.einsum('bihd,bjhd->bhij', qc, kc) * intra[None]
        o = (jnp.einsum('bhij,bjhv->bihv', att, vc)
             + jnp.einsum('bihd,bhdv->bihv', qc, s) * cross[None, :, :, None])
        s = s * chunk_decay[None, :, None, None] + jnp.einsum('bjhd,bjhv->bhdv', kc * tail[None, :, :, None], vc)
        return s, o

    s_fin, o = lax.scan(step, s0.astype(jnp.float32),
                        (to_blocks(q.astype(jnp.float32), c), to_blocks(k.astype(jnp.float32), c),
                         to_blocks(v.astype(jnp.float32), c)))
    return from_blocks(o), s_fin


def conv_ffn(h, conv_prev, ffn_up, ffn_gate, conv_w, conv_b, ffn_down):
    t = h.shape[1]
    u = h @ ffn_up
    g = h @ ffn_gate
    ext = jnp.concatenate([conv_prev.astype(u.dtype), u], axis=1)
    c = conv_b
    for j in range(CONV_W):
        c = c + conv_w[j] * ext[:, j:j + t]
    y = (jax.nn.gelu(c, approximate=False) * g) @ ffn_down
    return y, ext[:, t:]


def gather_pages(cache_l, page_table):
    pages = cache_l[page_table]
    return pages.reshape(pages.shape[0], pages.shape[1] * pages.shape[2], *pages.shape[3:])


def trunk_layer(x, q_pos, past, s0, cprev, rel_bias, norm_mix, w_in, ret_gn, w_pa, w_pb, w_pc, w_out,
                norm_ffn, ffn_up, ffn_gate, conv_w, conv_b, ffn_down):
    b, t = x.shape[:2]
    h = rmsnorm(x, norm_mix)
    z = h @ w_in
    cuts = np.cumsum(SPLITS)[:-1].tolist()
    qa, ka, va, qi, ki, wi, qb_, kb_, vb_, qc, kc, vc, gc, gates = jnp.split(z, cuts, axis=-1)
    qa = qa.reshape(b, t, H_A, DH)
    ka = ka.reshape(b, t, HKV_A, DH)
    va = va.reshape(b, t, HKV_A, DH)
    qi = qi.reshape(b, t, H_IDX, D_IDX)
    qb_ = qb_.reshape(b, t, H_B, DH)
    kb_ = kb_.reshape(b, t, HKV_B, DH)
    vb_ = vb_.reshape(b, t, HKV_B, DH)
    qc = qc.reshape(b, t, H_C, DK_C)
    kc = kc.reshape(b, t, H_C, DK_C)
    vc = vc.reshape(b, t, H_C, DV_C)
    if past is None:
        ka_all, va_all, ki_all, kb_all, vb_all = ka, va, ki, kb_, vb_
    else:
        pa_k, pa_v, pa_i, pb_k, pb_v = past
        ka_all = jnp.concatenate([pa_k.astype(ka.dtype), ka], axis=1)
        va_all = jnp.concatenate([pa_v.astype(va.dtype), va], axis=1)
        ki_all = jnp.concatenate([pa_i.astype(ki.dtype), ki], axis=1)
        kb_all = jnp.concatenate([pb_k.astype(kb_.dtype), kb_], axis=1)
        vb_all = jnp.concatenate([pb_v.astype(vb_.dtype), vb_], axis=1)
    o_a = dsa_attention(qa, ka_all, va_all, qi, ki_all, wi, q_pos, rel_bias[:, :H_A])
    o_b = moba_attention(qb_, kb_all, vb_all, q_pos, rel_bias[:, H_A:])
    o_c, s_new = retention(rotary(qc, q_pos), rotary(kc, q_pos) * (DK_C ** -0.5), vc, s0)
    o_c = o_c * lax.rsqrt(jnp.mean(o_c * o_c, axis=-1, keepdims=True) + EPS)
    o_c = (o_c.reshape(b, t, H_C * DV_C) * ret_gn.astype(jnp.float32)
           * jax.nn.silu(gc.astype(jnp.float32))).astype(x.dtype)
    g_a, g_b, g_c = jnp.split(gates, 3, axis=-1)
    m = (jax.nn.sigmoid(g_a) * (o_a @ w_pa) + jax.nn.sigmoid(g_b) * (o_b @ w_pb)
         + jax.nn.sigmoid(g_c) * (o_c @ w_pc))
    x = x + m @ w_out
    f, c_new = conv_ffn(rmsnorm(x, norm_ffn), cprev, ffn_up, ffn_gate, conv_w, conv_b, ffn_down)
    x = x + f
    return x, (ka, va, ki, kb_, vb_, s_new.astype(s0.dtype), c_new)


def setup_inputs(seed: int = 0) -> dict:
    key = jax.random.key(seed)
    ks = jax.random.split(key, 32)
    n_pages = PAST_LEN // PAGE_SIZE
    n_used = DEC_BATCH * n_pages
    n_pool = n_used + n_used // 4

    def nrm(k, shape, scale):
        return scale * jax.random.normal(k, shape, jnp.float32)

    return {
        'x_prompt': nrm(ks[0], (BATCH, SEQ, D_MODEL), 1.0),
        'x_sample': nrm(ks[1], (DEC_BATCH, DEC_SEQ, D_MODEL), 1.0),
        'cache_a_k': nrm(ks[2], (DEPTH, n_pool, PAGE_SIZE, HKV_A, DH), 1.0),
        'cache_a_v': nrm(ks[3], (DEPTH, n_pool, PAGE_SIZE, HKV_A, DH), 1.0),
        'cache_a_kidx': nrm(ks[4], (DEPTH, n_pool, PAGE_SIZE, D_IDX), 1.0),
        'cache_b_k': nrm(ks[5], (DEPTH, n_pool, PAGE_SIZE, HKV_B, DH), 1.0),
        'cache_b_v': nrm(ks[6], (DEPTH, n_pool, PAGE_SIZE, HKV_B, DH), 1.0),
        'state_ret': nrm(ks[7], (DEPTH, DEC_BATCH, H_C, DK_C, DV_C), 0.1),
        'state_conv': nrm(ks[8], (DEPTH, DEC_BATCH, CONV_W - 1, D_FF), 1.0),
        'page_table': jax.random.permutation(ks[9], n_pool)[:n_used].reshape(DEC_BATCH, n_pages).astype(jnp.int32),
        'rel_bias': nrm(ks[10], (N_BUCKETS, H_A + H_B), 0.1),
        'norm_mix': 1.0 + nrm(ks[11], (DEPTH, D_MODEL), 0.02),
        'w_in': nrm(ks[12], (DEPTH, D_MODEL, N_IN), D_MODEL ** -0.5),
        'ret_gn': 1.0 + nrm(ks[13], (DEPTH, H_C * DV_C), 0.02),
        'w_pa': nrm(ks[14], (DEPTH, H_A * DH, D_MODEL), (H_A * DH) ** -0.5),
        'w_pb': nrm(ks[15], (DEPTH, H_B * DH, D_MODEL), (H_B * DH) ** -0.5),
        'w_pc': nrm(ks[16], (DEPTH, H_C * DV_C, D_MODEL), (H_C * DV_C) ** -0.5),
        'w_out': nrm(ks[17], (DEPTH, D_MODEL, D_MODEL), D_MODEL ** -0.5),
        'norm_ffn': 1.0 + nrm(ks[18], (DEPTH, D_MODEL), 0.02),
        'ffn_up': nrm(ks[19], (DEPTH, D_MODEL, D_FF), D_MODEL ** -0.5),
        'ffn_gate': nrm(ks[20], (DEPTH, D_MODEL, D_FF), D_MODEL ** -0.5),
        'conv_w': nrm(ks[21], (DEPTH, CONV_W, D_FF), CONV_W ** -0.5),
        'conv_b': nrm(ks[22], (DEPTH, D_FF), 0.02),
        'ffn_down': nrm(ks[23], (DEPTH, D_FF, D_MODEL), D_FF ** -0.5),
        'norm_final': 1.0 + nrm(ks[24], (D_MODEL,), 0.02),
    }


def reference(x_prompt, x_sample, cache_a_k, cache_a_v, cache_a_kidx, cache_b_k, cache_b_v, state_ret, state_conv,
              page_table, rel_bias, norm_mix, w_in, ret_gn, w_pa, w_pb, w_pc, w_out, norm_ffn, ffn_up, ffn_gate,
              conv_w, conv_b, ffn_down, norm_final):
    bp, tp = x_prompt.shape[:2]
    ts = x_sample.shape[1]
    past_len = page_table.shape[1] * cache_a_k.shape[2]
    pos_p = jnp.arange(tp, dtype=jnp.int32)
    pos_s = past_len + jnp.arange(ts, dtype=jnp.int32)
    xp, xs = x_prompt, x_sample
    st_p, st_s = [], []
    for l in range(DEPTH):
        w_l = (rel_bias, norm_mix[l], w_in[l], ret_gn[l], w_pa[l], w_pb[l], w_pc[l], w_out[l], norm_ffn[l],
               ffn_up[l], ffn_gate[l], conv_w[l], conv_b[l], ffn_down[l])
        s0_p = jnp.zeros((bp, H_C, DK_C, DV_C), x_prompt.dtype)
        c0_p = jnp.zeros((bp, CONV_W - 1, D_FF), x_prompt.dtype)
        xp, sp = trunk_layer(xp, pos_p, None, s0_p, c0_p, *w_l)
        past = (gather_pages(cache_a_k[l], page_table), gather_pages(cache_a_v[l], page_table),
                gather_pages(cache_a_kidx[l], page_table), gather_pages(cache_b_k[l], page_table),
                gather_pages(cache_b_v[l], page_table))
        xs, ss = trunk_layer(xs, pos_s, past, state_ret[l], state_conv[l], *w_l)
        st_p.append(sp)
        st_s.append(ss)
    a_k_p, a_v_p, a_kidx_p, b_k_p, b_v_p, ret_p, conv_p = [jnp.stack(z) for z in zip(*st_p)]
    a_k_s, a_v_s, a_kidx_s, b_k_s, b_v_s, ret_s, conv_s = [jnp.stack(z) for z in zip(*st_s)]
    y_prompt = rmsnorm(xp, norm_final)
    y_sample = rmsnorm(xs, norm_final)
    return (y_prompt, y_sample, a_k_p, a_v_p, a_kidx_p, b_k_p, b_v_p, ret_p, conv_p,
            a_k_s, a_v_s, a_kidx_s, b_k_s, b_v_s, ret_s, conv_s)
```

```python
import functools
import math

import jax
import jax.numpy as jnp
from jax import lax
from jax.experimental import pallas as pl
from jax.experimental.pallas import tpu as pltpu

DH = 128
H_A, HKV_A = 8, 2
H_IDX, D_IDX = 16, 64
TOPK_IDX = 256
H_B, HKV_B = 8, 2
MOBA_BLOCK = 256
MOBA_TOPK = 3
H_C, DK_C, DV_C = 8, 128, 256
N_BUCKETS = 32
MAX_DIST = 128
CONV_W = 3
EPS = 1e-6
ROPE_BASE = 10000.0
RET_CHUNK = 128

_MXU = jnp.bfloat16
_NEG = -1e30
_TQ = 256
_LANES = 128
_VMEM_LIMIT = 56 * 1024 * 1024

_C_VC = 0
_C_GC = _C_VC + H_C * DV_C
_C_QA = _C_GC + H_C * DV_C
_C_QI = _C_QA + H_A * DH
_C_QB = _C_QI + H_IDX * D_IDX
_C_QC = _C_QB + H_B * DH
_C_KC = _C_QC + H_C * DK_C
_C_KA = _C_KC + H_C * DK_C
_C_VA = _C_KA + HKV_A * DH
_C_KB = _C_VA + HKV_A * DH
_C_VB = _C_KB + HKV_B * DH
_C_GATES = _C_VB + HKV_B * DH


def _params(sem):
    return pltpu.CompilerParams(dimension_semantics=sem, vmem_limit_bytes=_VMEM_LIMIT)


def _tile(n, target, mult=_LANES):
    best = None
    t = mult
    while t <= min(n, target):
        if n % t == 0:
            best = t
        t += mult
    return n if best is None else best


def _rmsnorm_body(x_ref, g_ref, o_ref):
    x = x_ref[...]
    y = x * lax.rsqrt(jnp.mean(x * x, axis=-1, keepdims=True) + EPS)
    o_ref[...] = (y * g_ref[...]).astype(o_ref.dtype)


def _rmsnorm(x, g, out_dtype):
    m, d = x.shape
    tm = _tile(m, 512, 8)
    return pl.pallas_call(
        _rmsnorm_body,
        grid=(m // tm,),
        in_specs=[pl.BlockSpec((tm, d), lambda i: (i, 0)), pl.BlockSpec((1, d), lambda i: (0, 0))],
        out_specs=pl.BlockSpec((tm, d), lambda i: (i, 0)),
        out_shape=jax.ShapeDtypeStruct((m, d), out_dtype),
        compiler_params=_params(("parallel",)),
        name="rmsnorm",
    )(x, g.reshape(1, d))


def _mm_body(*refs, nk, has_res):
    if has_res:
        x_ref, w_ref, r_ref, o_ref, acc_ref = refs
    else:
        x_ref, w_ref, o_ref, acc_ref = refs
    k = pl.program_id(2)

    @pl.when(k == 0)
    def _():
        acc_ref[...] = jnp.zeros_like(acc_ref)

    acc_ref[...] += jnp.dot(x_ref[...], w_ref[...], preferred_element_type=jnp.float32)

    @pl.when(k == nk - 1)
    def _():
        r = acc_ref[...]
        if has_res:
            r = r + r_ref[...]
        o_ref[...] = r.astype(o_ref.dtype)


def _matmul(x, w, out_dtype, residual=None, tm_target=1024, tn_target=1024, tk_target=2048):
    m, kdim = x.shape
    n = w.shape[1]
    tm = _tile(m, tm_target, 8)
    tn = _tile(n, tn_target)
    tk = _tile(kdim, tk_target)
    nk = kdim // tk
    in_specs = [pl.BlockSpec((tm, tk), lambda i, j, k: (i, k)), pl.BlockSpec((tk, tn), lambda i, j, k: (k, j))]
    args = [x, w]
    if residual is not None:
        in_specs.append(pl.BlockSpec((tm, tn), lambda i, j, k: (i, j)))
        args.append(residual)
    return pl.pallas_call(
        functools.partial(_mm_body, nk=nk, has_res=residual is not None),
        grid=(m // tm, n // tn, nk),
        in_specs=in_specs,
        out_specs=pl.BlockSpec((tm, tn), lambda i, j, k: (i, j)),
        out_shape=jax.ShapeDtypeStruct((m, n), out_dtype),
        scratch_shapes=[pltpu.VMEM((tm, tn), jnp.float32)],
        compiler_params=_params(("parallel", "parallel", "arbitrary")),
        name="matmul",
    )(*args)


def _t5_bucket(rel):
    n = jnp.maximum(rel, 0)
    max_exact = N_BUCKETS // 2
    nf = jnp.maximum(n, 1).astype(jnp.float32)
    large = max_exact + (jnp.log(nf / max_exact) / math.log(MAX_DIST / max_exact)
                         * (N_BUCKETS - max_exact)).astype(jnp.int32)
    large = jnp.minimum(large, N_BUCKETS - 1)
    return jnp.where(n < max_exact, n, large)


def _bias_body(tab_ref, bucket_ref, o_ref):
    h = pl.program_id(0)
    bucket = bucket_ref[...]
    acc = jnp.zeros(bucket.shape, jnp.float32)
    for b in range(N_BUCKETS):
        acc = jnp.where(bucket == b, tab_ref[b, h], acc)
    o_ref[...] = acc


def _bias_table(rel_bias, rel):
    bucket = _t5_bucket(rel)
    r, c = rel.shape
    nh = rel_bias.shape[1]
    return pl.pallas_call(
        _bias_body,
        grid=(nh,),
        in_specs=[pl.BlockSpec(memory_space=pltpu.SMEM), pl.BlockSpec((r, c), lambda h: (0, 0))],
        out_specs=pl.BlockSpec((None, r, c), lambda h: (h, 0, 0)),
        out_shape=jax.ShapeDtypeStruct((nh, r, c), jnp.float32),
        compiler_params=_params(("parallel",)),
        name="bias_table",
    )(rel_bias, bucket)


def _sortable(x):
    b = lax.bitcast_convert_type(x, jnp.int32)
    return b ^ ((b >> 31) & jnp.int32(0x7FFFFFFF))


_KEY_NEG_INF = -2147483648 + 0x7FFFFF


def _stack_heads(q_ref, g, n_grp, scale):
    parts = [q_ref[:, (g * n_grp + hh) * DH:(g * n_grp + hh + 1) * DH] for hh in range(n_grp)]
    return (jnp.concatenate(parts, axis=0) * scale).astype(_MXU)


def _flash_update(m_ref, l_ref, acc_ref, g, logits, v):
    m_old = m_ref[g]
    m_new = jnp.maximum(m_old, jnp.max(logits, axis=1, keepdims=True))
    alpha = jnp.exp(m_old - m_new)
    p = jnp.exp(logits - m_new)
    l_ref[g] = alpha * l_ref[g] + jnp.sum(p, axis=1, keepdims=True)
    acc_ref[g] = alpha * acc_ref[g] + jnp.dot(p.astype(_MXU), v, preferred_element_type=jnp.float32)
    m_ref[g] = m_new


def _flash_init(m_ref, l_ref, acc_ref):
    m_ref[...] = jnp.full(m_ref.shape, _NEG, jnp.float32)
    l_ref[...] = jnp.zeros(l_ref.shape, jnp.float32)
    acc_ref[...] = jnp.zeros(acc_ref.shape, jnp.float32)


def _flash_store(o_ref, l_ref, acc_ref, n_kv, n_grp, rows):
    for g in range(n_kv):
        o = acc_ref[g] / l_ref[g]
        for hh in range(n_grp):
            h = g * n_grp + hh
            o_ref[:, h * DH:(h + 1) * DH] = o[hh * rows:(hh + 1) * rows].astype(o_ref.dtype)


def _kth_largest_key(count_ge, n_sel, rows):
    int_min = jnp.int32(-2147483648)
    prefix = jnp.full((rows, 1), int_min, jnp.int32)
    zero = jnp.zeros((rows, 1), jnp.int32)
    prefix = jnp.where(count_ge(zero) >= n_sel, zero, prefix)

    def step(it, prefix):
        cand = prefix | (jnp.int32(1) << (30 - it))
        return jnp.where(count_ge(cand) >= n_sel, cand, prefix)

    return lax.fori_loop(0, 31, step, prefix)


def _first_index_cut(count_lt, need, rows, n_idx):
    nbits = max(1, int(n_idx).bit_length())
    cut = jnp.zeros((rows, 1), jnp.int32)

    def step(it, cut):
        cand = cut + (jnp.int32(1) << (nbits - 1 - it))
        return jnp.where(count_lt(cand) < need, cand, cut)

    return lax.fori_loop(0, nbits, step, cut)


def _dsa_p_body(qa_ref, qi_ref, kiwq_ref, kiw_ref, ka_ref, va_ref, bias_ref, o_ref,
                key_ref, neg_ref, m_ref, l_ref, acc_ref, *, n_sel, t):
    i = pl.program_id(1)
    tq = _TQ
    n_grp = H_A // HKV_A
    rows = lax.broadcasted_iota(jnp.int32, (tq, tq), 0)
    cols = lax.broadcasted_iota(jnp.int32, (tq, tq), 1)

    qi = [qi_ref[:, h * D_IDX:(h + 1) * D_IDX].astype(_MXU) for h in range(H_IDX)]
    wi = kiwq_ref[:, D_IDX:D_IDX + H_IDX] * (D_IDX ** -0.5 * H_IDX ** -0.5)

    def score_tile(j, carry):
        kx = kiw_ref[pl.ds(pl.multiple_of(j * tq, tq), tq), 0:D_IDX].astype(_MXU)
        sc = jnp.zeros((tq, tq), jnp.float32)
        for h in range(H_IDX):
            s = lax.dot_general(qi[h], kx, (((1,), (1,)), ((), ())), preferred_element_type=jnp.float32)
            sc = sc + wi[:, h:h + 1] * jnp.maximum(s, 0.0)
        causal = jnp.logical_or(j < i, cols <= rows)
        key_ref[j] = _sortable(jnp.where(causal, sc, -jnp.inf))
        return carry

    lax.fori_loop(0, i + 1, score_tile, 0)

    def count_tiles(pred):
        def body(j, acc):
            return acc + pred(key_ref[j], j).astype(jnp.int32)
        acc = lax.fori_loop(0, i + 1, body, jnp.zeros((tq, tq), jnp.int32))
        return jnp.sum(acc, axis=1, keepdims=True)

    thr = _kth_largest_key(lambda t: count_tiles(lambda k, j: k >= t), n_sel, tq)
    n_ge = count_tiles(lambda k, j: k >= thr)
    finite = thr > _KEY_NEG_INF
    tie = jnp.max(jnp.where(jnp.logical_and(finite, n_ge > n_sel), 1, 0)) > 0

    def write_neg(cut):
        def body(j, carry):
            k = key_ref[j]
            kpos = j * tq + cols
            keep = jnp.logical_or(k > thr, jnp.logical_and(k == thr, kpos <= cut))
            keep = jnp.logical_and(keep, jnp.logical_or(j < i, cols <= rows))
            neg_ref[j] = jnp.where(keep, 0.0, _NEG)
            return carry
        lax.fori_loop(0, i + 1, body, 0)

    def with_ties():
        n_gt = count_tiles(lambda k, j: k > thr)
        need = n_sel - n_gt
        cut = _first_index_cut(
            lambda p: count_tiles(lambda k, j: jnp.logical_and(k == thr, j * tq + cols < p)),
            need, tq, t)
        write_neg(jnp.where(finite, cut, jnp.int32(2 ** 30)))

    def without_ties():
        write_neg(jnp.full((tq, 1), 2 ** 30, jnp.int32))

    lax.cond(tie, with_ties, without_ties)

    _flash_init(m_ref, l_ref, acc_ref)
    scale = DH ** -0.5
    qs = [_stack_heads(qa_ref, g, n_grp, scale) for g in range(HKV_A)]

    def attend(j, bias_of_head):
        kt = ka_ref[pl.ds(pl.multiple_of(j * tq, tq), tq), :].astype(_MXU)
        vt = va_ref[pl.ds(pl.multiple_of(j * tq, tq), tq), :].astype(_MXU)
        neg = neg_ref[j]
        for g in range(HKV_A):
            logits = lax.dot_general(qs[g], kt[:, g * DH:(g + 1) * DH], (((1,), (1,)), ((), ())),
                                     preferred_element_type=jnp.float32)
            extra = jnp.concatenate([bias_of_head(g * n_grp + hh) + neg for hh in range(n_grp)], axis=0)
            _flash_update(m_ref, l_ref, acc_ref, g, logits + extra, vt[:, g * DH:(g + 1) * DH])

    def far(j, carry):
        attend(j, lambda h: bias_ref[h, tq - 1:tq, 0:1])
        return carry

    lax.fori_loop(0, jnp.maximum(i - 1, 0), far, 0)

    @pl.when(i >= 1)
    def _():
        attend(i - 1, lambda h: bias_ref[h, :, 0:tq])

    attend(i, lambda h: bias_ref[h, :, tq:2 * tq])
    _flash_store(o_ref, l_ref, acc_ref, HKV_A, n_grp, tq)


def _dsa_prompt(z, kiw, bias, n_batch, t):
    tq = _TQ
    nq = t // tq
    n_sel = min(TOPK_IDX, t // 4)
    assert n_sel <= tq and t % tq == 0
    n_grp = H_A // HKV_A
    return pl.pallas_call(
        functools.partial(_dsa_p_body, n_sel=n_sel, t=t),
        grid=(n_batch, nq),
        in_specs=[
            pl.BlockSpec((tq, H_A * DH), lambda b, i: (b * nq + i, _C_QA // (H_A * DH))),
            pl.BlockSpec((tq, H_IDX * D_IDX), lambda b, i: (b * nq + i, _C_QI // (H_IDX * D_IDX))),
            pl.BlockSpec((tq, _LANES), lambda b, i: (b * nq + i, 0)),
            pl.BlockSpec((t, _LANES), lambda b, i: (b, 0)),
            pl.BlockSpec((t, HKV_A * DH), lambda b, i: (b, _C_KA // (HKV_A * DH))),
            pl.BlockSpec((t, HKV_A * DH), lambda b, i: (b, _C_VA // (HKV_A * DH))),
            pl.BlockSpec((H_A, tq, 2 * tq), lambda b, i: (0, 0, 0)),
        ],
        out_specs=pl.BlockSpec((tq, H_A * DH), lambda b, i: (b * nq + i, 0)),
        out_shape=jax.ShapeDtypeStruct((n_batch * t, H_A * DH), _MXU),
        scratch_shapes=[
            pltpu.VMEM((nq, tq, tq), jnp.int32),
            pltpu.VMEM((nq, tq, tq), jnp.float32),
            pltpu.VMEM((HKV_A, n_grp * tq, 1), jnp.float32),
            pltpu.VMEM((HKV_A, n_grp * tq, 1), jnp.float32),
            pltpu.VMEM((HKV_A, n_grp * tq, DH), jnp.float32),
        ],
        compiler_params=_params(("parallel", "parallel")),
        name="dsa_prompt",
    )(z, z, kiw, kiw, z, z, bias)


def _topk_lanes(gate, n_cand, n_sel):
    lane = lax.broadcasted_iota(jnp.int32, gate.shape, 1)
    live = lane < n_cand
    chosen = jnp.zeros(gate.shape, jnp.bool_)
    big = jnp.int32(2 ** 30)
    for _ in range(n_sel):
        cand = jnp.logical_and(live, jnp.logical_not(chosen))
        best = jnp.max(jnp.where(cand, gate, -jnp.inf), axis=1, keepdims=True)
        first = jnp.min(jnp.where(jnp.logical_and(cand, gate == best), lane, big), axis=1, keepdims=True)
        chosen = jnp.logical_or(chosen, lane == first)
    return chosen


def _moba_p_body(qb_ref, kb_ref, vb_ref, bias_ref, o_ref, kmean_ref, allow_ref, m_ref, l_ref, acc_ref, *, n_sel):
    i = pl.program_id(1)
    tq = _TQ
    nb = kmean_ref.shape[0]
    n_grp = H_B // HKV_B
    rows = lax.broadcasted_iota(jnp.int32, (tq, tq), 0)
    cols = lax.broadcasted_iota(jnp.int32, (tq, tq), 1)

    @pl.when(i == 0)
    def _():
        for j in range(nb):
            kmean_ref[j:j + 1, :] = jnp.mean(kb_ref[j * tq:(j + 1) * tq, :], axis=0, keepdims=True)

    lane_w = allow_ref.shape[2]
    for g in range(HKV_B):
        qg = jnp.concatenate([qb_ref[:, (g * n_grp + hh) * DH:(g * n_grp + hh + 1) * DH] for hh in range(n_grp)],
                             axis=0)
        km = kmean_ref[:, g * DH:(g + 1) * DH]
        if lane_w > nb:
            km = jnp.concatenate([km, jnp.zeros((lane_w - nb, DH), jnp.float32)], axis=0)
        gate = lax.dot_general(qg, km, (((1,), (1,)), ((), ())), preferred_element_type=jnp.float32,
                               precision=lax.Precision.HIGHEST)
        allow_ref[g] = jnp.where(_topk_lanes(gate, i, n_sel), 0.0, _NEG)

    _flash_init(m_ref, l_ref, acc_ref)
    scale = DH ** -0.5
    qs = [_stack_heads(qb_ref, g, n_grp, scale) for g in range(HKV_B)]
    lane = lax.broadcasted_iota(jnp.int32, (n_grp * tq, lane_w), 1)
    causal_neg = jnp.where(cols <= rows, 0.0, _NEG)

    def attend(j, bias_of_head, own):
        kt = kb_ref[pl.ds(pl.multiple_of(j * tq, tq), tq), :].astype(_MXU)
        vt = vb_ref[pl.ds(pl.multiple_of(j * tq, tq), tq), :].astype(_MXU)
        for g in range(HKV_B):
            logits = lax.dot_general(qs[g], kt[:, g * DH:(g + 1) * DH], (((1,), (1,)), ((), ())),
                                     preferred_element_type=jnp.float32)
            if own:
                extra = jnp.concatenate([bias_of_head(g * n_grp + hh) + causal_neg for hh in range(n_grp)], axis=0)
            else:
                picked = jnp.max(jnp.where(lane == j, allow_ref[g], _NEG), axis=1, keepdims=True)
                extra = jnp.concatenate([jnp.broadcast_to(bias_of_head(g * n_grp + hh), (tq, tq))
                                         for hh in range(n_grp)], axis=0) + picked
            _flash_update(m_ref, l_ref, acc_ref, g, logits + extra, vt[:, g * DH:(g + 1) * DH])

    def far(j, carry):
        attend(j, lambda h: bias_ref[h, tq - 1:tq, 0:1], False)
        return carry

    lax.fori_loop(0, jnp.maximum(i - 1, 0), far, 0)

    @pl.when(i >= 1)
    def _():
        attend(i - 1, lambda h: bias_ref[h, :, 0:tq], False)

    attend(i, lambda h: bias_ref[h, :, tq:2 * tq], True)
    _flash_store(o_ref, l_ref, acc_ref, HKV_B, n_grp, tq)


def _moba_prompt(z, bias, n_batch, t):
    tq = _TQ
    assert t % tq == 0 and MOBA_BLOCK == tq
    nb = t // tq
    n_sel = min(MOBA_TOPK, (t - 1) // MOBA_BLOCK)
    n_grp = H_B // HKV_B
    lane_w = -(-nb // _LANES) * _LANES
    return pl.pallas_call(
        functools.partial(_moba_p_body, n_sel=n_sel),
        grid=(n_batch, nb),
        in_specs=[
            pl.BlockSpec((tq, H_B * DH), lambda b, i: (b * nb + i, _C_QB // (H_B * DH))),
            pl.BlockSpec((t, HKV_B * DH), lambda b, i: (b, _C_KB // (HKV_B * DH))),
            pl.BlockSpec((t, HKV_B * DH), lambda b, i: (b, _C_VB // (HKV_B * DH))),
            pl.BlockSpec((H_B, tq, 2 * tq), lambda b, i: (1, 0, 0)),
        ],
        out_specs=pl.BlockSpec((tq, H_B * DH), lambda b, i: (b * nb + i, 0)),
        out_shape=jax.ShapeDtypeStruct((n_batch * t, H_B * DH), _MXU),
        scratch_shapes=[
            pltpu.VMEM((nb, HKV_B * DH), jnp.float32),
            pltpu.VMEM((HKV_B, n_grp * tq, lane_w), jnp.float32),
            pltpu.VMEM((HKV_B, n_grp * tq, 1), jnp.float32),
            pltpu.VMEM((HKV_B, n_grp * tq, 1), jnp.float32),
            pltpu.VMEM((HKV_B, n_grp * tq, DH), jnp.float32),
        ],
        compiler_params=_params(("parallel", "arbitrary")),
        name="moba_prompt",
    )(z, z, z, bias)


def _ret_body(q_ref, k_ref, v_ref, g_ref, cos_ref, sin_ref, intra_ref, cross_ref, tail_ref, decay_ref, gn_ref,
              s0_ref, o_ref, s_out_ref, s_ref):
    t = pl.program_id(1)

    @pl.when(t == 0)
    def _():
        s_ref[...] = s0_ref[...]

    cos = cos_ref[...]
    sin = sin_ref[...]
    for h in range(H_C):
        qh = q_ref[:, h * DK_C:(h + 1) * DK_C]
        kh = k_ref[:, h * DK_C:(h + 1) * DK_C]
        q = qh * cos + pltpu.roll(qh, DK_C // 2, axis=1) * sin
        k = (kh * cos + pltpu.roll(kh, DK_C // 2, axis=1) * sin) * (DK_C ** -0.5)
        v = v_ref[:, h * DV_C:(h + 1) * DV_C].astype(_MXU)
        s = s_ref[h]
        qm = q.astype(_MXU)
        att = lax.dot_general(qm, k.astype(_MXU), (((1,), (1,)), ((), ())),
                              preferred_element_type=jnp.float32) * intra_ref[h]
        o = (jnp.dot(att.astype(_MXU), v, preferred_element_type=jnp.float32)
             + jnp.dot(qm, s.astype(_MXU), preferred_element_type=jnp.float32) * cross_ref[h])
        kt = (k * tail_ref[h]).astype(_MXU)
        s_ref[h] = s * decay_ref[h] + lax.dot_general(kt, v, (((0,), (0,)), ((), ())),
                                                      preferred_element_type=jnp.float32)
        o = o * lax.rsqrt(jnp.mean(o * o, axis=-1, keepdims=True) + EPS)
        gate = g_ref[:, h * DV_C:(h + 1) * DV_C]
        o = o * gn_ref[:, h * DV_C:(h + 1) * DV_C] * (gate * jax.nn.sigmoid(gate))
        o_ref[:, h * DV_C:(h + 1) * DV_C] = o.astype(o_ref.dtype)

    @pl.when(t == pl.num_programs(1) - 1)
    def _():
        s_out_ref[...] = s_ref[...]


def _retention(z, s0, ret_gn, pos, n_batch, t):
    c = math.gcd(t, RET_CHUNK)
    nc = t // c
    half = DK_C // 2
    inv = ROPE_BASE ** (-jnp.arange(0, DK_C, 2, dtype=jnp.float32) / DK_C)
    ang = pos.astype(jnp.float32)[:, None] * inv[None, :]
    cos = jnp.concatenate([jnp.cos(ang), jnp.cos(ang)], axis=1)
    sin = jnp.concatenate([-jnp.sin(ang), jnp.sin(ang)], axis=1)
    assert cos.shape == (t, 2 * half)
    log_g = jnp.log1p(-jnp.exp2(-5.0 - jnp.arange(H_C, dtype=jnp.float32)))
    idx = jnp.arange(c, dtype=jnp.float32)
    diff = idx[:, None] - idx[None, :]
    intra = jnp.where(diff[None] >= 0, jnp.exp(jnp.maximum(diff, 0.0)[None] * log_g[:, None, None]), 0.0)
    cross = jnp.exp((idx[None, :] + 1.0) * log_g[:, None])[:, :, None]
    tail = jnp.exp((c - 1.0 - idx)[None, :] * log_g[:, None])[:, :, None]
    decay = jnp.exp(c * log_g)[:, None, None]
    wq = H_C * DK_C
    wv = H_C * DV_C
    return pl.pallas_call(
        _ret_body,
        grid=(n_batch, nc),
        in_specs=[
            pl.BlockSpec((c, wq), lambda b, i: (b * nc + i, _C_QC // wq)),
            pl.BlockSpec((c, wq), lambda b, i: (b * nc + i, _C_KC // wq)),
            pl.BlockSpec((c, wv), lambda b, i: (b * nc + i, _C_VC // wv)),
            pl.BlockSpec((c, wv), lambda b, i: (b * nc + i, _C_GC // wv)),
            pl.BlockSpec((c, DK_C), lambda b, i: (i, 0)),
            pl.BlockSpec((c, DK_C), lambda b, i: (i, 0)),
            pl.BlockSpec((H_C, c, c), lambda b, i: (0, 0, 0)),
            pl.BlockSpec((H_C, c, 1), lambda b, i: (0, 0, 0)),
            pl.BlockSpec((H_C, c, 1), lambda b, i: (0, 0, 0)),
            pl.BlockSpec((H_C, 1, 1), lambda b, i: (0, 0, 0)),
            pl.BlockSpec((1, wv), lambda b, i: (0, 0)),
            pl.BlockSpec((None, H_C, DK_C, DV_C), lambda b, i: (b, 0, 0, 0)),
        ],
        out_specs=[
            pl.BlockSpec((c, wv), lambda b, i: (b * nc + i, 0)),
            pl.BlockSpec((None, H_C, DK_C, DV_C), lambda b, i: (b, 0, 0, 0)),
        ],
        out_shape=[
            jax.ShapeDtypeStruct((n_batch * t, wv), _MXU),
            jax.ShapeDtypeStruct((n_batch, H_C, DK_C, DV_C), jnp.float32),
        ],
        scratch_shapes=[pltpu.VMEM((H_C, DK_C, DV_C), jnp.float32)],
        compiler_params=_params(("parallel", "arbitrary")),
        name="retention",
    )(z, z, z, z, cos, sin, intra, cross, tail, decay, ret_gn.reshape(1, wv), s0)


def _merge_body(oa_ref, ob_ref, oc_ref, wa_ref, wb_ref, wc_ref, ga_ref, gb_ref, gc_ref, o_ref):
    def term(o, w, g):
        return jax.nn.sigmoid(g[...]) * jnp.dot(o[...], w[...], preferred_element_type=jnp.float32)
    o_ref[...] = (term(oa_ref, wa_ref, ga_ref) + term(ob_ref, wb_ref, gb_ref)
                  + term(oc_ref, wc_ref, gc_ref)).astype(o_ref.dtype)


def _merge(o_a, o_b, o_c, z, w_pa, w_pb, w_pc):
    m = o_a.shape[0]
    d = w_pa.shape[1]
    tm = _tile(m, 512, 8)
    tn = _tile(d, 512)
    gate_blk = [(_C_GATES + k * d) // tn for k in range(3)]
    assert all((_C_GATES + k * d) % tn == 0 for k in range(3))
    row = lambda w: pl.BlockSpec((tm, w), lambda i, j: (i, 0))
    col = lambda w: pl.BlockSpec((w, tn), lambda i, j: (0, j))
    gate = lambda k: pl.BlockSpec((tm, tn), lambda i, j: (i, gate_blk[k] + j))
    return pl.pallas_call(
        _merge_body,
        grid=(m // tm, d // tn),
        in_specs=[row(o_a.shape[1]), row(o_b.shape[1]), row(o_c.shape[1]),
                  col(w_pa.shape[0]), col(w_pb.shape[0]), col(w_pc.shape[0]), gate(0), gate(1), gate(2)],
        out_specs=pl.BlockSpec((tm, tn), lambda i, j: (i, j)),
        out_shape=jax.ShapeDtypeStruct((m, d), _MXU),
        compiler_params=_params(("parallel", "parallel")),
        name="merge",
    )(o_a, o_b, o_c, w_pa, w_pb, w_pc, z, z, z)


def _conv_gate(u, u1, u2, g, cw_ref, cb_ref):
    c = cb_ref[...] + cw_ref[0:1, :] * u2 + cw_ref[1:2, :] * u1 + cw_ref[2:3, :] * u
    return 0.5 * c * (1.0 + lax.erf(c * (2.0 ** -0.5))) * g


def _ffn_p_body(h_ref, wu_ref, wg_ref, cw_ref, cb_ref, a_ref, cs_ref, tail_ref, *, seq):
    i = pl.program_id(1)
    tm = h_ref.shape[0]
    h = h_ref[...]
    u = jnp.dot(h, wu_ref[...], preferred_element_type=jnp.float32)
    g = jnp.dot(h, wg_ref[...], preferred_element_type=jnp.float32)

    @pl.when((i * tm) % seq == 0)
    def _():
        tail_ref[...] = jnp.zeros_like(tail_ref)

    prev = tail_ref[...]
    row = lax.broadcasted_iota(jnp.int32, (tm, 1), 0)
    u1 = jnp.where(row == 0, prev[7:8, :], pltpu.roll(u, 1, axis=0))
    u2 = jnp.where(row == 0, prev[6:7, :], jnp.where(row == 1, prev[7:8, :], pltpu.roll(u, 2, axis=0)))
    a_ref[...] = _conv_gate(u, u1, u2, g, cw_ref, cb_ref).astype(a_ref.dtype)
    tail_ref[...] = u[tm - 8:tm, :]

    @pl.when((i * tm + tm) % seq == 0)
    def _():
        cs_ref[...] = u[tm - (CONV_W - 1):tm, :]


def _ffn_in_prompt(h, w_up, w_gate, conv_w, conv_b, n_batch, t):
    m, d = h.shape
    f = w_up.shape[1]
    tm = _tile(t, 1024, 8)
    tn = _tile(f, 512)
    per_seq = t // tm
    return pl.pallas_call(
        functools.partial(_ffn_p_body, seq=t),
        grid=(f // tn, m // tm),
        in_specs=[
            pl.BlockSpec((tm, d), lambda j, i: (i, 0)),
            pl.BlockSpec((d, tn), lambda j, i: (0, j)),
            pl.BlockSpec((d, tn), lambda j, i: (0, j)),
            pl.BlockSpec((CONV_W, tn), lambda j, i: (0, j)),
            pl.BlockSpec((1, tn), lambda j, i: (0, j)),
        ],
        out_specs=[
            pl.BlockSpec((tm, tn), lambda j, i: (i, j)),
            pl.BlockSpec((None, CONV_W - 1, tn), lambda j, i: (i // per_seq, 0, j)),
        ],
        out_shape=[
            jax.ShapeDtypeStruct((m, f), _MXU),
            jax.ShapeDtypeStruct((n_batch, CONV_W - 1, f), jnp.float32),
        ],
        scratch_shapes=[pltpu.VMEM((8, tn), jnp.float32)],
        compiler_params=_params(("parallel", "arbitrary")),
        name="ffn_in_prompt",
    )(h, w_up, w_gate, conv_w, conv_b)


def _ffn_s_body(h_ref, wu_ref, wg_ref, cw_ref, cb_ref, p1_ref, p2_ref, a_ref, u_ref, *, seq):
    tm = h_ref.shape[0]
    h = h_ref[...]
    u = jnp.dot(h, wu_ref[...], preferred_element_type=jnp.float32)
    g = jnp.dot(h, wg_ref[...], preferred_element_type=jnp.float32)
    pos = lax.broadcasted_iota(jnp.int32, (tm, 1), 0) % seq
    u1 = jnp.where(pos >= 1, pltpu.roll(u, 1, axis=0), p1_ref[...])
    u2 = jnp.where(pos >= 2, pltpu.roll(u, 2, axis=0), p2_ref[...])
    a_ref[...] = _conv_gate(u, u1, u2, g, cw_ref, cb_ref).astype(a_ref.dtype)
    u_ref[...] = u


def _ffn_in_sample(h, w_up, w_gate, conv_w, conv_b, conv_prev, n_batch, t):
    m, d = h.shape
    f = w_up.shape[1]
    assert t >= CONV_W - 1
    tn = _tile(f, 512)
    zeros = jnp.zeros((n_batch, t, f), jnp.float32)
    p1 = zeros.at[:, 0].set(conv_prev[:, 1]).reshape(m, f)
    p2 = zeros.at[:, 0].set(conv_prev[:, 0]).at[:, 1].set(conv_prev[:, 1]).reshape(m, f)
    full = lambda w: pl.BlockSpec((m, w), lambda j: (0, 0))
    colf = lambda r: pl.BlockSpec((r, tn), lambda j: (0, j))
    a, u = pl.pallas_call(
        functools.partial(_ffn_s_body, seq=t),
        grid=(f // tn,),
        in_specs=[full(d), colf(d), colf(d), colf(CONV_W), colf(1), colf(m), colf(m)],
        out_specs=[colf(m), colf(m)],
        out_shape=[jax.ShapeDtypeStruct((m, f), _MXU), jax.ShapeDtypeStruct((m, f), jnp.float32)],
        compiler_params=_params(("parallel",)),
        name="ffn_in_sample",
    )(h, w_up, w_gate, conv_w, conv_b, p1, p2)
    return a, u.reshape(n_batch, t, f)[:, t - (CONV_W - 1):]


def _page_specs(n_pages_per_step, layer, page_size, width):
    return [pl.BlockSpec((None, None, page_size, width),
                         functools.partial(lambda b, s, pt, g: (layer, pt[b, s * n_pages_per_step + g], 0, 0), g=g))
            for g in range(n_pages_per_step)]


def _dsa_s_score_body(pt_ref, qi_ref, kiw_ref, *refs, n_pg):
    pages = refs[:n_pg]
    o_ref = refs[n_pg]
    ts = qi_ref.shape[0]
    ps = pages[0].shape[0]
    qst = jnp.concatenate([qi_ref[:, h * D_IDX:(h + 1) * D_IDX] for h in range(H_IDX)], axis=0).astype(_MXU)
    wi = kiw_ref[:, D_IDX:D_IDX + H_IDX] * (D_IDX ** -0.5 * H_IDX ** -0.5)
    for g in range(n_pg):
        s = lax.dot_general(qst, pages[g][...].astype(_MXU), (((1,), (1,)), ((), ())),
                            preferred_element_type=jnp.float32)
        sc = jnp.zeros((ts, ps), jnp.float32)
        for h in range(H_IDX):
            sc = sc + wi[:, h:h + 1] * jnp.maximum(s[h * ts:(h + 1) * ts], 0.0)
        o_ref[:, g * ps:(g + 1) * ps] = sc


def _dsa_s_select_body(sc_ref, qi_ref, kiw_ref, o_ref, key_ref, *, n_sel):
    ts = qi_ref.shape[0]
    lp = sc_ref.shape[1]
    qi = qi_ref[...]
    kx = kiw_ref[:, 0:D_IDX].astype(_MXU)
    kx = jnp.concatenate([kx, jnp.zeros((_LANES - ts, D_IDX), _MXU)], axis=0)
    wi = kiw_ref[:, D_IDX:D_IDX + H_IDX] * (D_IDX ** -0.5 * H_IDX ** -0.5)
    sc = jnp.zeros((ts, _LANES), jnp.float32)
    for h in range(H_IDX):
        s = lax.dot_general(qi[:, h * D_IDX:(h + 1) * D_IDX].astype(_MXU), kx, (((1,), (1,)), ((), ())),
                            preferred_element_type=jnp.float32)
        sc = sc + wi[:, h:h + 1] * jnp.maximum(s, 0.0)
    r = lax.broadcasted_iota(jnp.int32, (ts, _LANES), 0)
    c = lax.broadcasted_iota(jnp.int32, (ts, _LANES), 1)
    key_ref[:, 0:lp] = _sortable(sc_ref[...])
    key_ref[:, lp:lp + _LANES] = _sortable(jnp.where(c <= r, sc, -jnp.inf))

    key = key_ref[...]
    pos = lax.broadcasted_iota(jnp.int32, key.shape, 1)
    count = lambda m: jnp.sum(m.astype(jnp.int32), axis=1, keepdims=True)
    thr = _kth_largest_key(lambda t: count(key_ref[...] >= t), n_sel, ts)
    finite = thr > _KEY_NEG_INF
    need = n_sel - count(key > thr)
    cut = _first_index_cut(lambda p: count(jnp.logical_and(key_ref[...] == thr, pos < p)), need, ts, lp + _LANES)
    cut = jnp.where(finite, cut, jnp.int32(2 ** 30))
    keep = jnp.logical_or(key > thr, jnp.logical_and(key == thr, pos <= cut))
    keep = jnp.logical_and(keep, jnp.logical_or(pos < lp, pos - lp <= lax.broadcasted_iota(jnp.int32, key.shape, 0)))
    o_ref[...] = jnp.where(keep, 0.0, _NEG)


def _dsa_s_attn_body(pt_ref, qa_ref, kn_ref, vn_ref, negn_ref, neg_ref, bias_ref, *refs, n_pg):
    kpages = refs[:n_pg]
    vpages = refs[n_pg:2 * n_pg]
    o_ref, m_ref, l_ref, acc_ref = refs[2 * n_pg:]
    s = pl.program_id(1)
    n_steps = pl.num_programs(1)
    ts = qa_ref.shape[0]
    ps = kpages[0].shape[0]
    n_grp = H_A // HKV_A
    scale = DH ** -0.5
    qs = [_stack_heads(qa_ref, g, n_grp, scale) for g in range(HKV_A)]
    wb = bias_ref.shape[2]

    def attend(kt, vt, neg, bias_of_head):
        for g in range(HKV_A):
            logits = lax.dot_general(qs[g], kt[:, g * DH:(g + 1) * DH], (((1,), (1,)), ((), ())),
                                     preferred_element_type=jnp.float32)
            extra = jnp.concatenate([bias_of_head(g * n_grp + hh) + neg for hh in range(n_grp)], axis=0)
            _flash_update(m_ref, l_ref, acc_ref, g, logits + extra, vt[:, g * DH:(g + 1) * DH])

    @pl.when(s == 0)
    def _():
        _flash_init(m_ref, l_ref, acc_ref)
        pad = jnp.zeros((_LANES - ts, HKV_A * DH), _MXU)
        kt = jnp.concatenate([kn_ref[...].astype(_MXU), pad], axis=0)
        vt = jnp.concatenate([vn_ref[...].astype(_MXU), pad], axis=0)
        attend(kt, vt, negn_ref[...], lambda h: bias_ref[h, :, wb - _LANES:wb])

    last = s == n_steps - 1
    for g in range(n_pg):
        kt = kpages[g][...].astype(_MXU)
        vt = vpages[g][...].astype(_MXU)
        neg = neg_ref[:, g * ps:(g + 1) * ps]
        if g == n_pg - 1:
            bias_of_head = lambda h: jnp.where(last, bias_ref[h, :, wb - _LANES - ps:wb - _LANES],
                                               bias_ref[h, :, 0:1])
        else:
            bias_of_head = lambda h: bias_ref[h, :, 0:1]
        attend(kt, vt, neg, bias_of_head)

    @pl.when(last)
    def _():
        _flash_store(o_ref, l_ref, acc_ref, HKV_A, n_grp, ts)


def _dsa_sample(zs, kiws, cache_k, cache_v, cache_kidx, page_table, bias, layer, n_batch, ts):
    n_pages = page_table.shape[1]
    ps = cache_k.shape[2]
    past = n_pages * ps
    assert ps == _LANES and ts <= _LANES and ts % 8 == 0
    n_sel = min(TOPK_IDX, (past + ts) // 4)
    n_grp = H_A // HKV_A

    gi = _tile(n_pages, 16, 1)
    scores = pl.pallas_call(
        functools.partial(_dsa_s_score_body, n_pg=gi),
        grid_spec=pltpu.PrefetchScalarGridSpec(
            num_scalar_prefetch=1,
            grid=(n_batch, n_pages // gi),
            in_specs=[pl.BlockSpec((ts, H_IDX * D_IDX), lambda b, s, pt: (b, _C_QI // (H_IDX * D_IDX))),
                      pl.BlockSpec((ts, _LANES), lambda b, s, pt: (b, 0))]
            + _page_specs(gi, layer, ps, D_IDX),
            out_specs=pl.BlockSpec((None, ts, gi * ps), lambda b, s, pt: (b, 0, s)),
        ),
        out_shape=jax.ShapeDtypeStruct((n_batch, ts, past), jnp.float32),
        compiler_params=_params(("parallel", "arbitrary")),
        name="dsa_sample_scores",
    )(page_table, zs, kiws, *([cache_kidx] * gi))

    neg = pl.pallas_call(
        functools.partial(_dsa_s_select_body, n_sel=n_sel),
        grid=(n_batch,),
        in_specs=[pl.BlockSpec((None, ts, past), lambda b: (b, 0, 0)),
                  pl.BlockSpec((ts, H_IDX * D_IDX), lambda b: (b, _C_QI // (H_IDX * D_IDX))),
                  pl.BlockSpec((ts, _LANES), lambda b: (b, 0))],
        out_specs=pl.BlockSpec((None, ts, past + _LANES), lambda b: (b, 0, 0)),
        out_shape=jax.ShapeDtypeStruct((n_batch, ts, past + _LANES), jnp.float32),
        scratch_shapes=[pltpu.VMEM((ts, past + _LANES), jnp.int32)],
        compiler_params=_params(("parallel",)),
        name="dsa_sample_select",
    )(scores, zs, kiws)

    ga = _tile(n_pages, 8, 1)
    wkv = HKV_A * DH
    return pl.pallas_call(
        functools.partial(_dsa_s_attn_body, n_pg=ga),
        grid_spec=pltpu.PrefetchScalarGridSpec(
            num_scalar_prefetch=1,
            grid=(n_batch, n_pages // ga),
            in_specs=[pl.BlockSpec((ts, H_A * DH), lambda b, s, pt: (b, _C_QA // (H_A * DH))),
                      pl.BlockSpec((ts, wkv), lambda b, s, pt: (b, _C_KA // wkv)),
                      pl.BlockSpec((ts, wkv), lambda b, s, pt: (b, _C_VA // wkv)),
                      pl.BlockSpec((None, ts, _LANES), lambda b, s, pt: (b, 0, past // _LANES)),
                      pl.BlockSpec((None, ts, ga * ps), lambda b, s, pt: (b, 0, s)),
                      pl.BlockSpec((H_A, ts, bias.shape[2]), lambda b, s, pt: (0, 0, 0))]
            + _page_specs(ga, layer, ps, wkv) + _page_specs(ga, layer, ps, wkv),
            out_specs=pl.BlockSpec((ts, H_A * DH), lambda b, s, pt: (b, 0)),
            scratch_shapes=[pltpu.VMEM((HKV_A, n_grp * ts, 1), jnp.float32),
                            pltpu.VMEM((HKV_A, n_grp * ts, 1), jnp.float32),
                            pltpu.VMEM((HKV_A, n_grp * ts, DH), jnp.float32)],
        ),
        out_shape=jax.ShapeDtypeStruct((n_batch * ts, H_A * DH), _MXU),
        compiler_params=_params(("parallel", "arbitrary")),
        name="dsa_sample_attn",
    )(page_table, zs, zs, zs, neg, neg, bias, *([cache_k] * ga), *([cache_v] * ga))


def _moba_s_body(pt_ref, qb_ref, kn_ref, vn_ref, bias_ref, *refs, n_pg, n_sel, n_blocks):
    kpages = refs[:n_pg]
    vpages = refs[n_pg:2 * n_pg]
    o_ref, gate_ref, mb_ref, lb_ref, ob_ref = refs[2 * n_pg:]
    s = pl.program_id(1)
    n_steps = pl.num_programs(1)
    ts = qb_ref.shape[0]
    ps = kpages[0].shape[0]
    n_grp = H_B // HKV_B
    rows = n_grp * ts
    ppb = MOBA_BLOCK // ps
    bps = n_pg // ppb
    scale = DH ** -0.5
    wb = bias_ref.shape[2]
    qf = [jnp.concatenate([qb_ref[:, (g * n_grp + hh) * DH:(g * n_grp + hh + 1) * DH] for hh in range(n_grp)], axis=0)
          for g in range(HKV_B)]
    qs = [(q * scale).astype(_MXU) for q in qf]
    lane = lax.broadcasted_iota(jnp.int32, (rows, gate_ref.shape[2]), 1)
    last = s == n_steps - 1

    @pl.when(s == 0)
    def _():
        gate_ref[...] = jnp.zeros(gate_ref.shape, jnp.float32)
        mb_ref[...] = jnp.full(mb_ref.shape, _NEG, jnp.float32)
        lb_ref[...] = jnp.zeros(lb_ref.shape, jnp.float32)

    for jb in range(bps):
        j = s * bps + jb
        kf = jnp.concatenate([kpages[jb * ppb + p][...] for p in range(ppb)], axis=0)
        vt = jnp.concatenate([vpages[jb * ppb + p][...] for p in range(ppb)], axis=0).astype(_MXU)
        kt = kf.astype(_MXU)
        kmean = jnp.mean(kf, axis=0, keepdims=True)
        for g in range(HKV_B):
            gate = jnp.sum(qf[g] * kmean[:, g * DH:(g + 1) * DH], axis=1, keepdims=True)
            logits = lax.dot_general(qs[g], kt[:, g * DH:(g + 1) * DH], (((1,), (1,)), ((), ())),
                                     preferred_element_type=jnp.float32)
            if jb == bps - 1:
                near = jnp.concatenate([bias_ref[g * n_grp + hh, :, wb - _LANES - MOBA_BLOCK:wb - _LANES]
                                        for hh in range(n_grp)], axis=0)
                far = jnp.concatenate([jnp.broadcast_to(bias_ref[g * n_grp + hh, :, 0:1], (ts, MOBA_BLOCK))
                                       for hh in range(n_grp)], axis=0)
                logits = logits + jnp.where(last, near, far)
            else:
                logits = logits + jnp.concatenate(
                    [jnp.broadcast_to(bias_ref[g * n_grp + hh, :, 0:1], (ts, MOBA_BLOCK)) for hh in range(n_grp)],
                    axis=0)
            mj = jnp.max(logits, axis=1, keepdims=True)
            p = jnp.exp(logits - mj)
            lj = jnp.sum(p, axis=1, keepdims=True)
            ob_ref[j, g] = jnp.dot(p.astype(_MXU), vt[:, g * DH:(g + 1) * DH], preferred_element_type=jnp.float32)
            here = lane == j
            gate_ref[g] = jnp.where(here, gate, gate_ref[g])
            mb_ref[g] = jnp.where(here, mj, mb_ref[g])
            lb_ref[g] = jnp.where(here, lj, lb_ref[g])

    @pl.when(last)
    def _():
        pad = jnp.zeros((_LANES - ts, HKV_B * DH), _MXU)
        kt = jnp.concatenate([kn_ref[...].astype(_MXU), pad], axis=0)
        vt = jnp.concatenate([vn_ref[...].astype(_MXU), pad], axis=0)
        r = lax.broadcasted_iota(jnp.int32, (ts, _LANES), 0)
        c = lax.broadcasted_iota(jnp.int32, (ts, _LANES), 1)
        causal_neg = jnp.where(c <= r, 0.0, _NEG)
        for g in range(HKV_B):
            logits = lax.dot_general(qs[g], kt[:, g * DH:(g + 1) * DH], (((1,), (1,)), ((), ())),
                                     preferred_element_type=jnp.float32)
            logits = logits + jnp.concatenate(
                [bias_ref[g * n_grp + hh, :, wb - _LANES:wb] + causal_neg for hh in range(n_grp)], axis=0)
            m_own = jnp.max(logits, axis=1, keepdims=True)
            p = jnp.exp(logits - m_own)
            l_own = jnp.sum(p, axis=1, keepdims=True)
            o_own = jnp.dot(p.astype(_MXU), vt[:, g * DH:(g + 1) * DH], preferred_element_type=jnp.float32)
            chosen = _topk_lanes(gate_ref[g], n_blocks, n_sel)
            mb = jnp.where(chosen, mb_ref[g], _NEG)
            m_all = jnp.maximum(m_own, jnp.max(mb, axis=1, keepdims=True))
            w = jnp.where(chosen, jnp.exp(mb - m_all), 0.0)
            w_own = jnp.exp(m_own - m_all)
            den = w_own * l_own + jnp.sum(w * lb_ref[g], axis=1, keepdims=True)

            def add_block(j, acc):
                wj = jnp.sum(jnp.where(lane == j, w, 0.0), axis=1, keepdims=True)
                return acc + wj * ob_ref[j, g]

            num = lax.fori_loop(0, n_blocks, add_block, w_own * o_own)
            o = num / den
            for hh in range(n_grp):
                h = g * n_grp + hh
                o_ref[:, h * DH:(h + 1) * DH] = o[hh * ts:(hh + 1) * ts].astype(o_ref.dtype)


def _moba_sample(zs, cache_k, cache_v, page_table, bias, layer, n_batch, ts):
    n_pages = page_table.shape[1]
    ps = cache_k.shape[2]
    past = n_pages * ps
    assert past % MOBA_BLOCK == 0 and MOBA_BLOCK % ps == 0 and ts <= _LANES and ts % 8 == 0
    n_blocks = past // MOBA_BLOCK
    n_sel = min(MOBA_TOPK, (past + ts - 1) // MOBA_BLOCK)
    ppb = MOBA_BLOCK // ps
    g_pg = ppb * _tile(n_blocks, 4, 1)
    n_grp = H_B // HKV_B
    lane_w = -(-n_blocks // _LANES) * _LANES
    wkv = HKV_B * DH
    return pl.pallas_call(
        functools.partial(_moba_s_body, n_pg=g_pg, n_sel=n_sel, n_blocks=n_blocks),
        grid_spec=pltpu.PrefetchScalarGridSpec(
            num_scalar_prefetch=1,
            grid=(n_batch, n_pages // g_pg),
            in_specs=[pl.BlockSpec((ts, H_B * DH), lambda b, s, pt: (b, _C_QB // (H_B * DH))),
                      pl.BlockSpec((ts, wkv), lambda b, s, pt: (b, _C_KB // wkv)),
                      pl.BlockSpec((ts, wkv), lambda b, s, pt: (b, _C_VB // wkv)),
                      pl.BlockSpec((H_B, ts, bias.shape[2]), lambda b, s, pt: (1, 0, 0))]
            + _page_specs(g_pg, layer, ps, wkv) + _page_specs(g_pg, layer, ps, wkv),
            out_specs=pl.BlockSpec((ts, H_B * DH), lambda b, s, pt: (b, 0)),
            scratch_shapes=[pltpu.VMEM((HKV_B, n_grp * ts, lane_w), jnp.float32),
                            pltpu.VMEM((HKV_B, n_grp * ts, lane_w), jnp.float32),
                            pltpu.VMEM((HKV_B, n_grp * ts, lane_w), jnp.float32),
                            pltpu.VMEM((n_blocks, HKV_B, n_grp * ts, DH), jnp.float32)],
        ),
        out_shape=jax.ShapeDtypeStruct((n_batch * ts, H_B * DH), _MXU),
        compiler_params=_params(("parallel", "arbitrary")),
        name="moba_sample",
    )(page_table, zs, zs, zs, bias, *([cache_k] * g_pg), *([cache_v] * g_pg))


def _relayout_w_in(w_in, d):
    sizes = dict(qa=H_A * DH, ka=HKV_A * DH, va=HKV_A * DH, qi=H_IDX * D_IDX, ki=D_IDX, wi=H_IDX,
                 qb=H_B * DH, kb=HKV_B * DH, vb=HKV_B * DH, qc=H_C * DK_C, kc=H_C * DK_C, vc=H_C * DV_C,
                 gc=H_C * DV_C, gates=3 * d)
    off, parts = 0, {}
    for name in ("qa", "ka", "va", "qi", "ki", "wi", "qb", "kb", "vb", "qc", "kc", "vc", "gc", "gates"):
        parts[name] = w_in[:, off:off + sizes[name]]
        off += sizes[name]
    assert off == w_in.shape[1]
    main = jnp.concatenate([parts[n] for n in ("vc", "gc", "qa", "qi", "qb", "qc", "kc", "ka", "va", "kb", "vb",
                                               "gates")], axis=1).astype(_MXU)
    kiw = jnp.concatenate([parts["ki"], parts["wi"],
                           jnp.zeros((d, _LANES - D_IDX - H_IDX), w_in.dtype)], axis=1).astype(_MXU)
    return main, kiw


def _pad_cols(w, n):
    return jnp.pad(w, ((0, 0), (0, n - w.shape[1])))


def _mix_and_ffn(x, z, o_a, o_b, o_c, wts, ffn_in):
    m = _merge(o_a, o_b, o_c, z, wts["w_pa"], wts["w_pb"], wts["w_pc"])
    x = _matmul(m, wts["w_out"], jnp.float32, residual=x)
    h2 = _rmsnorm(x, wts["norm_ffn"], _MXU)
    a, conv_new = ffn_in(h2)
    x = _matmul(a, wts["ffn_down"], jnp.float32, residual=x, tm_target=512, tn_target=1024, tk_target=512)
    return x, conv_new


def _state_slices(z, kiw, n_batch, t):
    ka = z[:, _C_KA:_C_KA + HKV_A * DH].reshape(n_batch, t, HKV_A, DH)
    va = z[:, _C_VA:_C_VA + HKV_A * DH].reshape(n_batch, t, HKV_A, DH)
    ki = kiw[:, :D_IDX].reshape(n_batch, t, D_IDX)
    kb = z[:, _C_KB:_C_KB + HKV_B * DH].reshape(n_batch, t, HKV_B, DH)
    vb = z[:, _C_VB:_C_VB + HKV_B * DH].reshape(n_batch, t, HKV_B, DH)
    return ka, va, ki, kb, vb


def kernel(x_prompt, x_sample, cache_a_k, cache_a_v, cache_a_kidx, cache_b_k, cache_b_v, state_ret, state_conv,
           page_table, rel_bias, norm_mix, w_in, ret_gn, w_pa, w_pb, w_pc, w_out, norm_ffn, ffn_up, ffn_gate,
           conv_w, conv_b, ffn_down, norm_final):
    bp, tp, d = x_prompt.shape
    bs, ts, _ = x_sample.shape
    depth = w_in.shape[0]
    n_pool, ps = cache_a_k.shape[1], cache_a_k.shape[2]
    past = page_table.shape[1] * ps
    f = ffn_up.shape[2]
    fp = -(-f // 512) * 512
    tq = _TQ
    assert d % _LANES == 0 and tp % tq == 0

    ar = jnp.arange
    bias_p = _bias_table(rel_bias, (ar(tq)[:, None] - ar(2 * tq)[None, :] + tq).astype(jnp.int32))
    wbs = 3 * _LANES
    bias_s = _bias_table(rel_bias, (ar(ts)[:, None] + (wbs - _LANES) - ar(wbs)[None, :]).astype(jnp.int32))
    assert H_A == H_B

    cache_a_k = cache_a_k.reshape(depth, n_pool, ps, HKV_A * DH)
    cache_a_v = cache_a_v.reshape(depth, n_pool, ps, HKV_A * DH)
    cache_b_k = cache_b_k.reshape(depth, n_pool, ps, HKV_B * DH)
    cache_b_v = cache_b_v.reshape(depth, n_pool, ps, HKV_B * DH)

    xp = x_prompt.reshape(bp * tp, d)
    xs = x_sample.reshape(bs * ts, d)
    pos_p = jnp.arange(tp, dtype=jnp.int32)
    pos_s = past + jnp.arange(ts, dtype=jnp.int32)
    st_p, st_s = [], []
    for l in range(depth):
        w_main, w_kiw = _relayout_w_in(w_in[l], d)
        wts = dict(w_pa=w_pa[l].astype(_MXU), w_pb=w_pb[l].astype(_MXU), w_pc=w_pc[l].astype(_MXU),
                   w_out=w_out[l].astype(_MXU), norm_ffn=norm_ffn[l],
                   ffn_down=jnp.pad(ffn_down[l], ((0, fp - f), (0, 0))).astype(_MXU))
        up = _pad_cols(ffn_up[l], fp).astype(_MXU)
        gate = _pad_cols(ffn_gate[l], fp).astype(_MXU)
        cw = _pad_cols(conv_w[l], fp)
        cb = _pad_cols(conv_b[l].reshape(1, f), fp)

        h = _rmsnorm(xp, norm_mix[l], _MXU)
        z = _matmul(h, w_main, jnp.float32)
        kiw = _matmul(h, w_kiw, jnp.float32)
        o_a = _dsa_prompt(z, kiw, bias_p, bp, tp)
        o_b = _moba_prompt(z, bias_p, bp, tp)
        o_c, ret_new = _retention(z, jnp.zeros((bp, H_C, DK_C, DV_C), jnp.float32), ret_gn[l], pos_p, bp, tp)
        xp, conv_new = _mix_and_ffn(xp, z, o_a, o_b, o_c, wts,
                                    lambda h2: _ffn_in_prompt(h2, up, gate, cw, cb, bp, tp))
        st_p.append(_state_slices(z, kiw, bp, tp) + (ret_new, conv_new[:, :, :f]))

        h = _rmsnorm(xs, norm_mix[l], _MXU)
        z = _matmul(h, w_main, jnp.float32)
        kiw = _matmul(h, w_kiw, jnp.float32)
        o_a = _dsa_sample(z, kiw, cache_a_k, cache_a_v, cache_a_kidx, page_table, bias_s, l, bs, ts)
        o_b = _moba_sample(z, cache_b_k, cache_b_v, page_table, bias_s, l, bs, ts)
        o_c, ret_new = _retention(z, state_ret[l], ret_gn[l], pos_s, bs, ts)
        conv_prev = _pad_cols(state_conv[l].reshape(bs * (CONV_W - 1), f), fp).reshape(bs, CONV_W - 1, fp)
        xs, conv_new = _mix_and_ffn(xs, z, o_a, o_b, o_c, wts,
                                    lambda h2: _ffn_in_sample(h2, up, gate, cw, cb, conv_prev, bs, ts))
        st_s.append(_state_slices(z, kiw, bs, ts) + (ret_new, conv_new[:, :, :f]))

    y_prompt = _rmsnorm(xp, norm_final, jnp.float32).reshape(bp, tp, d)
    y_sample = _rmsnorm(xs, norm_final, jnp.float32).reshape(bs, ts, d)
    outs_p = [jnp.stack(v) for v in zip(*st_p)]
    outs_s = [jnp.stack(v) for v in zip(*st_s)]
    return (y_prompt, y_sample, *outs_p, *outs_s)
```

```python
import functools
import math

import jax
import jax.numpy as jnp
from jax import lax
from jax.experimental import pallas as pl
from jax.experimental.pallas import tpu as pltpu

DH = 128
H_A, HKV_A = 8, 2
H_IDX, D_IDX = 16, 64
TOPK_IDX = 256
H_B, HKV_B = 8, 2
MOBA_BLOCK = 256
MOBA_TOPK = 3
H_C, DK_C, DV_C = 8, 128, 256
N_BUCKETS = 32
MAX_DIST = 128
CONV_W = 3
EPS = 1e-6
ROPE_BASE = 10000.0
RET_CHUNK = 128

_MXU = jnp.bfloat16
_NEG = -1e30
_TQ = 256
_LANES = 128
_VMEM_LIMIT = 56 * 1024 * 1024

_C_VC = 0
_C_GC = _C_VC + H_C * DV_C
_C_QA = _C_GC + H_C * DV_C
_C_QI = _C_QA + H_A * DH
_C_QB = _C_QI + H_IDX * D_IDX
_C_QC = _C_QB + H_B * DH
_C_KC = _C_QC + H_C * DK_C
_C_KA = _C_KC + H_C * DK_C
_C_VA = _C_KA + HKV_A * DH
_C_KB = _C_VA + HKV_A * DH
_C_VB = _C_KB + HKV_B * DH
_C_GATES = _C_VB + HKV_B * DH


def _params(sem):
    return pltpu.CompilerParams(dimension_semantics=sem, vmem_limit_bytes=_VMEM_LIMIT)


def _tile(n, target, mult=_LANES):
    best = None
    t = mult
    while t <= min(n, target):
        if n % t == 0:
            best = t
        t += mult
    return n if best is None else best


def _rmsnorm_body(x_ref, g_ref, o_ref):
    x = x_ref[...]
    y = x * lax.rsqrt(jnp.mean(x * x, axis=-1, keepdims=True) + EPS)
    o_ref[...] = (y * g_ref[...]).astype(o_ref.dtype)


def _rmsnorm(x, g, out_dtype):
    m, d = x.shape
    tm = _tile(m, 512, 8)
    return pl.pallas_call(
        _rmsnorm_body,
        grid=(m // tm,),
        in_specs=[pl.BlockSpec((tm, d), lambda i: (i, 0)), pl.BlockSpec((1, d), lambda i: (0, 0))],
        out_specs=pl.BlockSpec((tm, d), lambda i: (i, 0)),
        out_shape=jax.ShapeDtypeStruct((m, d), out_dtype),
        compiler_params=_params(("parallel",)),
        name="rmsnorm",
    )(x, g.reshape(1, d))


def _mm_body(*refs, nk, has_res):
    if has_res:
        x_ref, w_ref, r_ref, o_ref, acc_ref = refs
    else:
        x_ref, w_ref, o_ref, acc_ref = refs
    k = pl.program_id(2)

    @pl.when(k == 0)
    def _():
        acc_ref[...] = jnp.zeros_like(acc_ref)

    acc_ref[...] += jnp.dot(x_ref[...], w_ref[...], preferred_element_type=jnp.float32)

    @pl.when(k == nk - 1)
    def _():
        r = acc_ref[...]
        if has_res:
            r = r + r_ref[...]
        o_ref[...] = r.astype(o_ref.dtype)


def _matmul(x, w, out_dtype, residual=None, tm_target=1024, tn_target=1024, tk_target=2048):
    m, kdim = x.shape
    n = w.shape[1]
    tm = _tile(m, tm_target, 8)
    tn = _tile(n, tn_target)
    tk = _tile(kdim, tk_target)
    nk = kdim // tk
    in_specs = [pl.BlockSpec((tm, tk), lambda i, j, k: (i, k)), pl.BlockSpec((tk, tn), lambda i, j, k: (k, j))]
    args = [x, w]
    if residual is not None:
        in_specs.append(pl.BlockSpec((tm, tn), lambda i, j, k: (i, j)))
        args.append(residual)
    return pl.pallas_call(
        functools.partial(_mm_body, nk=nk, has_res=residual is not None),
        grid=(m // tm, n // tn, nk),
        in_specs=in_specs,
        out_specs=pl.BlockSpec((tm, tn), lambda i, j, k: (i, j)),
        out_shape=jax.ShapeDtypeStruct((m, n), out_dtype),
        scratch_shapes=[pltpu.VMEM((tm, tn), jnp.float32)],
        compiler_params=_params(("parallel", "parallel", "arbitrary")),
        name="matmul",
    )(*args)


def _t5_bucket(rel):
    n = jnp.maximum(rel, 0)
    max_exact = N_BUCKETS // 2
    nf = jnp.maximum(n, 1).astype(jnp.float32)
    large = max_exact + (jnp.log(nf / max_exact) / math.log(MAX_DIST / max_exact)
                         * (N_BUCKETS - max_exact)).astype(jnp.int32)
    large = jnp.minimum(large, N_BUCKETS - 1)
    return jnp.where(n < max_exact, n, large)


def _bias_body(tab_ref, bucket_ref, o_ref):
    h = pl.program_id(0)
    bucket = bucket_ref[...]
    acc = jnp.zeros(bucket.shape, jnp.float32)
    for b in range(N_BUCKETS):
        acc = jnp.where(bucket == b, tab_ref[b, h], acc)
    o_ref[...] = acc


def _bias_table(rel_bias, rel):
    bucket = _t5_bucket(rel)
    r, c = rel.shape
    nh = rel_bias.shape[1]
    return pl.pallas_call(
        _bias_body,
        grid=(nh,),
        in_specs=[pl.BlockSpec(memory_space=pltpu.SMEM), pl.BlockSpec((r, c), lambda h: (0, 0))],
        out_specs=pl.BlockSpec((None, r, c), lambda h: (h, 0, 0)),
        out_shape=jax.ShapeDtypeStruct((nh, r, c), jnp.float32),
        compiler_params=_params(("parallel",)),
        name="bias_table",
    )(rel_bias, bucket)


def _sortable(x):
    b = lax.bitcast_convert_type(x, jnp.int32)
    return b ^ ((b >> 31) & jnp.int32(0x7FFFFFFF))


_KEY_NEG_INF = -2147483648 + 0x7FFFFF


def _stack_heads(q_ref, g, n_grp, scale):
    parts = [q_ref[:, (g * n_grp + hh) * DH:(g * n_grp + hh + 1) * DH] for hh in range(n_grp)]
    return (jnp.concatenate(parts, axis=0) * scale).astype(_MXU)


def _flash_update(m_ref, l_ref, acc_ref, g, logits, v):
    m_old = m_ref[g]
    m_new = jnp.maximum(m_old, jnp.max(logits, axis=1, keepdims=True))
    alpha = jnp.exp(m_old - m_new)
    p = jnp.exp(logits - m_new)
    l_ref[g] = alpha * l_ref[g] + jnp.sum(p, axis=1, keepdims=True)
    acc_ref[g] = alpha * acc_ref[g] + jnp.dot(p.astype(_MXU), v, preferred_element_type=jnp.float32)
    m_ref[g] = m_new


def _flash_init(m_ref, l_ref, acc_ref):
    m_ref[...] = jnp.full(m_ref.shape, _NEG, jnp.float32)
    l_ref[...] = jnp.zeros(l_ref.shape, jnp.float32)
    acc_ref[...] = jnp.zeros(acc_ref.shape, jnp.float32)


def _flash_store(o_ref, l_ref, acc_ref, n_kv, n_grp, rows):
    for g in range(n_kv):
        o = acc_ref[g] / l_ref[g]
        for hh in range(n_grp):
            h = g * n_grp + hh
            o_ref[:, h * DH:(h + 1) * DH] = o[hh * rows:(hh + 1) * rows].astype(o_ref.dtype)


def _kth_largest_key(count_ge, n_sel, rows):
    int_min = jnp.int32(-2147483648)
    prefix = jnp.full((rows, 1), int_min, jnp.int32)
    zero = jnp.zeros((rows, 1), jnp.int32)
    prefix = jnp.where(count_ge(zero) >= n_sel, zero, prefix)

    def step(it, prefix):
        cand = prefix | (jnp.int32(1) << (30 - it))
        return jnp.where(count_ge(cand) >= n_sel, cand, prefix)

    return lax.fori_loop(0, 31, step, prefix)


def _first_index_cut(count_lt, need, rows, n_idx):
    nbits = max(1, int(n_idx).bit_length())
    cut = jnp.zeros((rows, 1), jnp.int32)

    def step(it, cut):
        cand = cut + (jnp.int32(1) << (nbits - 1 - it))
        return jnp.where(count_lt(cand) < need, cand, cut)

    return lax.fori_loop(0, nbits, step, cut)


def _dsa_p_body(qa_ref, qi_ref, kiwq_ref, kiw_ref, ka_ref, va_ref, bias_ref, o_ref,
                key_ref, neg_ref, m_ref, l_ref, acc_ref, *, n_sel, t):
    i = pl.program_id(1)
    tq = _TQ
    n_grp = H_A // HKV_A
    rows = lax.broadcasted_iota(jnp.int32, (tq, tq), 0)
    cols = lax.broadcasted_iota(jnp.int32, (tq, tq), 1)

    qi = [qi_ref[:, h * D_IDX:(h + 1) * D_IDX].astype(_MXU) for h in range(H_IDX)]
    wi = kiwq_ref[:, D_IDX:D_IDX + H_IDX] * (D_IDX ** -0.5 * H_IDX ** -0.5)

    def score_tile(j, carry):
        kx = kiw_ref[pl.ds(pl.multiple_of(j * tq, tq), tq), 0:D_IDX].astype(_MXU)
        sc = jnp.zeros((tq, tq), jnp.float32)
        for h in range(H_IDX):
            s = lax.dot_general(qi[h], kx, (((1,), (1,)), ((), ())), preferred_element_type=jnp.float32)
            sc = sc + wi[:, h:h + 1] * jnp.maximum(s, 0.0)
        causal = jnp.logical_or(j < i, cols <= rows)
        key_ref[j] = _sortable(jnp.where(causal, sc, -jnp.inf))
        return carry

    lax.fori_loop(0, i + 1, score_tile, 0)

    def count_tiles(pred):
        def body(j, acc):
            return acc + pred(key_ref[j], j).astype(jnp.int32)
        acc = lax.fori_loop(0, i + 1, body, jnp.zeros((tq, tq), jnp.int32))
        return jnp.sum(acc, axis=1, keepdims=True)

    thr = _kth_largest_key(lambda t: count_tiles(lambda k, j: k >= t), n_sel, tq)
    n_ge = count_tiles(lambda k, j: k >= thr)
    finite = thr > _KEY_NEG_INF
    tie = jnp.max(jnp.where(jnp.logical_and(finite, n_ge > n_sel), 1, 0)) > 0

    def write_neg(cut):
        def body(j, carry):
            k = key_ref[j]
            kpos = j * tq + cols
            keep = jnp.logical_or(k > thr, jnp.logical_and(k == thr, kpos <= cut))
            keep = jnp.logical_and(keep, jnp.logical_or(j < i, cols <= rows))
            neg_ref[j] = jnp.where(keep, 0.0, _NEG)
            return carry
        lax.fori_loop(0, i + 1, body, 0)

    def with_ties():
        n_gt = count_tiles(lambda k, j: k > thr)
        need = n_sel - n_gt
        cut = _first_index_cut(
            lambda p: count_tiles(lambda k, j: jnp.logical_and(k == thr, j * tq + cols < p)),
            need, tq, t)
        write_neg(jnp.where(finite, cut, jnp.int32(2 ** 30)))

    def without_ties():
        write_neg(jnp.full((tq, 1), 2 ** 30, jnp.int32))

    lax.cond(tie, with_ties, without_ties)

    _flash_init(m_ref, l_ref, acc_ref)
    scale = DH ** -0.5
    qs = [_stack_heads(qa_ref, g, n_grp, scale) for g in range(HKV_A)]

    def attend(j, bias_of_head):
        kt = ka_ref[pl.ds(pl.multiple_of(j * tq, tq), tq), :].astype(_MXU)
        vt = va_ref[pl.ds(pl.multiple_of(j * tq, tq), tq), :].astype(_MXU)
        neg = neg_ref[j]
        for g in range(HKV_A):
            logits = lax.dot_general(qs[g], kt[:, g * DH:(g + 1) * DH], (((1,), (1,)), ((), ())),
                                     preferred_element_type=jnp.float32)
            extra = jnp.concatenate([bias_of_head(g * n_grp + hh) + neg for hh in range(n_grp)], axis=0)
            _flash_update(m_ref, l_ref, acc_ref, g, logits + extra, vt[:, g * DH:(g + 1) * DH])

    def far(j, carry):
        attend(j, lambda h: bias_ref[h, tq - 1:tq, 0:1])
        return carry

    lax.fori_loop(0, jnp.maximum(i - 1, 0), far, 0)

    @pl.when(i >= 1)
    def _():
        attend(i - 1, lambda h: bias_ref[h, :, 0:tq])

    attend(i, lambda h: bias_ref[h, :, tq:2 * tq])
    _flash_store(o_ref, l_ref, acc_ref, HKV_A, n_grp, tq)


def _dsa_prompt(z, kiw, bias, n_batch, t):
    tq = _TQ
    nq = t // tq
    n_sel = min(TOPK_IDX, t // 4)
    assert n_sel <= tq and t % tq == 0
    n_grp = H_A // HKV_A
    return pl.pallas_call(
        functools.partial(_dsa_p_body, n_sel=n_sel, t=t),
        grid=(n_batch, nq),
        in_specs=[
            pl.BlockSpec((tq, H_A * DH), lambda b, i: (b * nq + i, _C_QA // (H_A * DH))),
            pl.BlockSpec((tq, H_IDX * D_IDX), lambda b, i: (b * nq + i, _C_QI // (H_IDX * D_IDX))),
            pl.BlockSpec((tq, _LANES), lambda b, i: (b * nq + i, 0)),
            pl.BlockSpec((t, _LANES), lambda b, i: (b, 0)),
            pl.BlockSpec((t, HKV_A * DH), lambda b, i: (b, _C_KA // (HKV_A * DH))),
            pl.BlockSpec((t, HKV_A * DH), lambda b, i: (b, _C_VA // (HKV_A * DH))),
            pl.BlockSpec((H_A, tq, 2 * tq), lambda b, i: (0, 0, 0)),
        ],
        out_specs=pl.BlockSpec((tq, H_A * DH), lambda b, i: (b * nq + i, 0)),
        out_shape=jax.ShapeDtypeStruct((n_batch * t, H_A * DH), _MXU),
        scratch_shapes=[
            pltpu.VMEM((nq, tq, tq), jnp.int32),
            pltpu.VMEM((nq, tq, tq), jnp.float32),
            pltpu.VMEM((HKV_A, n_grp * tq, 1), jnp.float32),
            pltpu.VMEM((HKV_A, n_grp * tq, 1), jnp.float32),
            pltpu.VMEM((HKV_A, n_grp * tq, DH), jnp.float32),
        ],
        compiler_params=_params(("parallel", "parallel")),
        name="dsa_prompt",
    )(z, z, kiw, kiw, z, z, bias)


def _topk_lanes(gate, n_cand, n_sel):
    lane = lax.broadcasted_iota(jnp.int32, gate.shape, 1)
    live = lane < n_cand
    chosen = jnp.zeros(gate.shape, jnp.bool_)
    big = jnp.int32(2 ** 30)
    for _ in range(n_sel):
        cand = jnp.logical_and(live, jnp.logical_not(chosen))
        best = jnp.max(jnp.where(cand, gate, -jnp.inf), axis=1, keepdims=True)
        first = jnp.min(jnp.where(jnp.logical_and(cand, gate == best), lane, big), axis=1, keepdims=True)
        chosen = jnp.logical_or(chosen, lane == first)
    return chosen


def _moba_p_body(qb_ref, kb_ref, vb_ref, bias_ref, o_ref, kmean_ref, allow_ref, m_ref, l_ref, acc_ref, *, n_sel):
    i = pl.program_id(1)
    tq = _TQ
    nb = kmean_ref.shape[0]
    n_grp = H_B // HKV_B
    rows = lax.broadcasted_iota(jnp.int32, (tq, tq), 0)
    cols = lax.broadcasted_iota(jnp.int32, (tq, tq), 1)

    @pl.when(i == 0)
    def _():
        for j in range(nb):
            kmean_ref[j:j + 1, :] = jnp.mean(kb_ref[j * tq:(j + 1) * tq, :], axis=0, keepdims=True)

    lane_w = allow_ref.shape[2]
    for g in range(HKV_B):
        qg = jnp.concatenate([qb_ref[:, (g * n_grp + hh) * DH:(g * n_grp + hh + 1) * DH] for hh in range(n_grp)],
                             axis=0)
        km = kmean_ref[:, g * DH:(g + 1) * DH]
        if lane_w > nb:
            km = jnp.concatenate([km, jnp.zeros((lane_w - nb, DH), jnp.float32)], axis=0)
        gate = lax.dot_general(qg, km, (((1,), (1,)), ((), ())), preferred_element_type=jnp.float32,
                               precision=lax.Precision.HIGHEST)
        allow_ref[g] = jnp.where(_topk_lanes(gate, i, n_sel), 0.0, _NEG)

    _flash_init(m_ref, l_ref, acc_ref)
    scale = DH ** -0.5
    qs = [_stack_heads(qb_ref, g, n_grp, scale) for g in range(HKV_B)]
    lane = lax.broadcasted_iota(jnp.int32, (n_grp * tq, lane_w), 1)
    causal_neg = jnp.where(cols <= rows, 0.0, _NEG)

    def attend(j, bias_of_head, own):
        kt = kb_ref[pl.ds(pl.multiple_of(j * tq, tq), tq), :].astype(_MXU)
        vt = vb_ref[pl.ds(pl.multiple_of(j * tq, tq), tq), :].astype(_MXU)
        for g in range(HKV_B):
            logits = lax.dot_general(qs[g], kt[:, g * DH:(g + 1) * DH], (((1,), (1,)), ((), ())),
                                     preferred_element_type=jnp.float32)
            if own:
                extra = jnp.concatenate([bias_of_head(g * n_grp + hh) + causal_neg for hh in range(n_grp)], axis=0)
            else:
                picked = jnp.max(jnp.where(lane == j, allow_ref[g], _NEG), axis=1, keepdims=True)
                extra = jnp.concatenate([jnp.broadcast_to(bias_of_head(g * n_grp + hh), (tq, tq))
                                         for hh in range(n_grp)], axis=0) + picked
            _flash_update(m_ref, l_ref, acc_ref, g, logits + extra, vt[:, g * DH:(g + 1) * DH])

    def far(j, carry):
        attend(j, lambda h: bias_ref[h, tq - 1:tq, 0:1], False)
        return carry

    lax.fori_loop(0, jnp.maximum(i - 1, 0), far, 0)

    @pl.when(i >= 1)
    def _():
        attend(i - 1, lambda h: bias_ref[h, :, 0:tq], False)

    attend(i, lambda h: bias_ref[h, :, tq:2 * tq], True)
    _flash_store(o_ref, l_ref, acc_ref, HKV_B, n_grp, tq)


def _moba_prompt(z, bias, n_batch, t):
    tq = _TQ
    assert t % tq == 0 and MOBA_BLOCK == tq
    nb = t // tq
    n_sel = min(MOBA_TOPK, (t - 1) // MOBA_BLOCK)
    n_grp = H_B // HKV_B
    lane_w = -(-nb // _LANES) * _LANES
    return pl.pallas_call(
        functools.partial(_moba_p_body, n_sel=n_sel),
        grid=(n_batch, nb),
        in_specs=[
            pl.BlockSpec((tq, H_B * DH), lambda b, i: (b * nb + i, _C_QB // (H_B * DH))),
            pl.BlockSpec((t, HKV_B * DH), lambda b, i: (b, _C_KB // (HKV_B * DH))),
            pl.BlockSpec((t, HKV_B * DH), lambda b, i: (b, _C_VB // (HKV_B * DH))),
            pl.BlockSpec((H_B, tq, 2 * tq), lambda b, i: (1, 0, 0)),
        ],
        out_specs=pl.BlockSpec((tq, H_B * DH), lambda b, i: (b * nb + i, 0)),
        out_shape=jax.ShapeDtypeStruct((n_batch * t, H_B * DH), _MXU),
        scratch_shapes=[
            pltpu.VMEM((nb, HKV_B * DH), jnp.float32),
            pltpu.VMEM((HKV_B, n_grp * tq, lane_w), jnp.float32),
            pltpu.VMEM((HKV_B, n_grp * tq, 1), jnp.float32),
            pltpu.VMEM((HKV_B, n_grp * tq, 1), jnp.float32),
            pltpu.VMEM((HKV_B, n_grp * tq, DH), jnp.float32),
        ],
        compiler_params=_params(("parallel", "arbitrary")),
        name="moba_prompt",
    )(z, z, z, bias)


def _ret_body(q_ref, k_ref, v_ref, g_ref, cos_ref, sin_ref, intra_ref, cross_ref, tail_ref, decay_ref, gn_ref,
              s0_ref, o_ref, s_out_ref, s_ref):
    t = pl.program_id(1)

    @pl.when(t == 0)
    def _():
        s_ref[...] = s0_ref[...]

    cos = cos_ref[...]
    sin = sin_ref[...]
    for h in range(H_C):
        qh = q_ref[:, h * DK_C:(h + 1) * DK_C]
        kh = k_ref[:, h * DK_C:(h + 1) * DK_C]
        q = qh * cos + pltpu.roll(qh, DK_C // 2, axis=1) * sin
        k = (kh * cos + pltpu.roll(kh, DK_C // 2, axis=1) * sin) * (DK_C ** -0.5)
        v = v_ref[:, h * DV_C:(h + 1) * DV_C].astype(_MXU)
        s = s_ref[h]
        qm = q.astype(_MXU)
        att = lax.dot_general(qm, k.astype(_MXU), (((1,), (1,)), ((), ())),
                              preferred_element_type=jnp.float32) * intra_ref[h]
        o = (jnp.dot(att.astype(_MXU), v, preferred_element_type=jnp.float32)
             + jnp.dot(qm, s.astype(_MXU), preferred_element_type=jnp.float32) * cross_ref[h])
        kt = (k * tail_ref[h]).astype(_MXU)
        s_ref[h] = s * decay_ref[h] + lax.dot_general(kt, v, (((0,), (0,)), ((), ())),
                                                      preferred_element_type=jnp.float32)
        o = o * lax.rsqrt(jnp.mean(o * o, axis=-1, keepdims=True) + EPS)
        gate = g_ref[:, h * DV_C:(h + 1) * DV_C]
        o = o * gn_ref[:, h * DV_C:(h + 1) * DV_C] * (gate * jax.nn.sigmoid(gate))
        o_ref[:, h * DV_C:(h + 1) * DV_C] = o.astype(o_ref.dtype)

    @pl.when(t == pl.num_programs(1) - 1)
    def _():
        s_out_ref[...] = s_ref[...]


def _retention(z, s0, ret_gn, pos, n_batch, t):
    c = math.gcd(t, RET_CHUNK)
    nc = t // c
    half = DK_C // 2
    inv = ROPE_BASE ** (-jnp.arange(0, DK_C, 2, dtype=jnp.float32) / DK_C)
    ang = pos.astype(jnp.float32)[:, None] * inv[None, :]
    cos = jnp.concatenate([jnp.cos(ang), jnp.cos(ang)], axis=1)
    sin = jnp.concatenate([-jnp.sin(ang), jnp.sin(ang)], axis=1)
    assert cos.shape == (t, 2 * half)
    log_g = jnp.log1p(-jnp.exp2(-5.0 - jnp.arange(H_C, dtype=jnp.float32)))
    idx = jnp.arange(c, dtype=jnp.float32)
    diff = idx[:, None] - idx[None, :]
    intra = jnp.where(diff[None] >= 0, jnp.exp(jnp.maximum(diff, 0.0)[None] * log_g[:, None, None]), 0.0)
    cross = jnp.exp((idx[None, :] + 1.0) * log_g[:, None])[:, :, None]
    tail = jnp.exp((c - 1.0 - idx)[None, :] * log_g[:, None])[:, :, None]
    decay = jnp.exp(c * log_g)[:, None, None]
    wq = H_C * DK_C
    wv = H_C * DV_C
    return pl.pallas_call(
        _ret_body,
        grid=(n_batch, nc),
        in_specs=[
            pl.BlockSpec((c, wq), lambda b, i: (b * nc + i, _C_QC // wq)),
            pl.BlockSpec((c, wq), lambda b, i: (b * nc + i, _C_KC // wq)),
            pl.BlockSpec((c, wv), lambda b, i: (b * nc + i, _C_VC // wv)),
            pl.BlockSpec((c, wv), lambda b, i: (b * nc + i, _C_GC // wv)),
            pl.BlockSpec((c, DK_C), lambda b, i: (i, 0)),
            pl.BlockSpec((c, DK_C), lambda b, i: (i, 0)),
            pl.BlockSpec((H_C, c, c), lambda b, i: (0, 0, 0)),
            pl.BlockSpec((H_C, c, 1), lambda b, i: (0, 0, 0)),
            pl.BlockSpec((H_C, c, 1), lambda b, i: (0, 0, 0)),
            pl.BlockSpec((H_C, 1, 1), lambda b, i: (0, 0, 0)),
            pl.BlockSpec((1, wv), lambda b, i: (0, 0)),
            pl.BlockSpec((None, H_C, DK_C, DV_C), lambda b, i: (b, 0, 0, 0)),
        ],
        out_specs=[
            pl.BlockSpec((c, wv), lambda b, i: (b * nc + i, 0)),
            pl.BlockSpec((None, H_C, DK_C, DV_C), lambda b, i: (b, 0, 0, 0)),
        ],
        out_shape=[
            jax.ShapeDtypeStruct((n_batch * t, wv), _MXU),
            jax.ShapeDtypeStruct((n_batch, H_C, DK_C, DV_C), jnp.float32),
        ],
        scratch_shapes=[pltpu.VMEM((H_C, DK_C, DV_C), jnp.float32)],
        compiler_params=_params(("parallel", "arbitrary")),
        name="retention",
    )(z, z, z, z, cos, sin, intra, cross, tail, decay, ret_gn.reshape(1, wv), s0)


def _merge_body(oa_ref, ob_ref, oc_ref, wa_ref, wb_ref, wc_ref, ga_ref, gb_ref, gc_ref, o_ref):
    def term(o, w, g):
        return jax.nn.sigmoid(g[...]) * jnp.dot(o[...], w[...], preferred_element_type=jnp.float32)
    o_ref[...] = (term(oa_ref, wa_ref, ga_ref) + term(ob_ref, wb_ref, gb_ref)
                  + term(oc_ref, wc_ref, gc_ref)).astype(o_ref.dtype)


def _merge(o_a, o_b, o_c, z, w_pa, w_pb, w_pc):
    m = o_a.shape[0]
    d = w_pa.shape[1]
    tm = _tile(m, 512, 8)
    tn = _tile(d, 512)
    gate_blk = [(_C_GATES + k * d) // tn for k in range(3)]
    assert all((_C_GATES + k * d) % tn == 0 for k in range(3))
    row = lambda w: pl.BlockSpec((tm, w), lambda i, j: (i, 0))
    col = lambda w: pl.BlockSpec((w, tn), lambda i, j: (0, j))
    gate = lambda k: pl.BlockSpec((tm, tn), lambda i, j: (i, gate_blk[k] + j))
    return pl.pallas_call(
        _merge_body,
        grid=(m // tm, d // tn),
        in_specs=[row(o_a.shape[1]), row(o_b.shape[1]), row(o_c.shape[1]),
                  col(w_pa.shape[0]), col(w_pb.shape[0]), col(w_pc.shape[0]), gate(0), gate(1), gate(2)],
        out_specs=pl.BlockSpec((tm, tn), lambda i, j: (i, j)),
        out_shape=jax.ShapeDtypeStruct((m, d), _MXU),
        compiler_params=_params(("parallel", "parallel")),
        name="merge",
    )(o_a, o_b, o_c, w_pa, w_pb, w_pc, z, z, z)


def _conv_gate(u, u1, u2, g, cw_ref, cb_ref):
    c = cb_ref[...] + cw_ref[0:1, :] * u2 + cw_ref[1:2, :] * u1 + cw_ref[2:3, :] * u
    return 0.5 * c * (1.0 + lax.erf(c * (2.0 ** -0.5))) * g


def _ffn_p_body(h_ref, wu_ref, wg_ref, cw_ref, cb_ref, a_ref, cs_ref, tail_ref, *, seq):
    i = pl.program_id(1)
    tm = h_ref.shape[0]
    h = h_ref[...]
    u = jnp.dot(h, wu_ref[...], preferred_element_type=jnp.float32)
    g = jnp.dot(h, wg_ref[...], preferred_element_type=jnp.float32)

    @pl.when((i * tm) % seq == 0)
    def _():
        tail_ref[...] = jnp.zeros_like(tail_ref)

    prev = tail_ref[...]
    row = lax.broadcasted_iota(jnp.int32, (tm, 1), 0)
    u1 = jnp.where(row == 0, prev[7:8, :], pltpu.roll(u, 1, axis=0))
    u2 = jnp.where(row == 0, prev[6:7, :], jnp.where(row == 1, prev[7:8, :], pltpu.roll(u, 2, axis=0)))
    a_ref[...] = _conv_gate(u, u1, u2, g, cw_ref, cb_ref).astype(a_ref.dtype)
    tail_ref[...] = u[tm - 8:tm, :]

    @pl.when((i * tm + tm) % seq == 0)
    def _():
        cs_ref[...] = u[tm - (CONV_W - 1):tm, :]


def _ffn_in_prompt(h, w_up, w_gate, conv_w, conv_b, n_batch, t):
    m, d = h.shape
    f = w_up.shape[1]
    tm = _tile(t, 1024, 8)
    tn = _tile(f, 512)
    per_seq = t // tm
    return pl.pallas_call(
        functools.partial(_ffn_p_body, seq=t),
        grid=(f // tn, m // tm),
        in_specs=[
            pl.BlockSpec((tm, d), lambda j, i: (i, 0)),
            pl.BlockSpec((d, tn), lambda j, i: (0, j)),
            pl.BlockSpec((d, tn), lambda j, i: (0, j)),
            pl.BlockSpec((CONV_W, tn), lambda j, i: (0, j)),
            pl.BlockSpec((1, tn), lambda j, i: (0, j)),
        ],
        out_specs=[
            pl.BlockSpec((tm, tn), lambda j, i: (i, j)),
            pl.BlockSpec((None, CONV_W - 1, tn), lambda j, i: (i // per_seq, 0, j)),
        ],
        out_shape=[
            jax.ShapeDtypeStruct((m, f), _MXU),
            jax.ShapeDtypeStruct((n_batch, CONV_W - 1, f), jnp.float32),
        ],
        scratch_shapes=[pltpu.VMEM((8, tn), jnp.float32)],
        compiler_params=_params(("parallel", "arbitrary")),
        name="ffn_in_prompt",
    )(h, w_up, w_gate, conv_w, conv_b)


def _ffn_s_body(h_ref, wu_ref, wg_ref, cw_ref, cb_ref, p1_ref, p2_ref, a_ref, u_ref, *, seq):
    tm = h_ref.shape[0]
    h = h_ref[...]
    u = jnp.dot(h, wu_ref[...], preferred_element_type=jnp.float32)
    g = jnp.dot(h, wg_ref[...], preferred_element_type=jnp.float32)
    pos = lax.broadcasted_iota(jnp.int32, (tm, 1), 0) % seq
    u1 = jnp.where(pos >= 1, pltpu.roll(u, 1, axis=0), p1_ref[...])
    u2 = jnp.where(pos >= 2, pltpu.roll(u, 2, axis=0), p2_ref[...])
    a_ref[...] = _conv_gate(u, u1, u2, g, cw_ref, cb_ref).astype(a_ref.dtype)
    u_ref[...] = u


def _ffn_in_sample(h, w_up, w_gate, conv_w, conv_b, conv_prev, n_batch, t):
    m, d = h.shape
    f = w_up.shape[1]
    assert t >= CONV_W - 1
    tn = _tile(f, 512)
    zeros = jnp.zeros((n_batch, t, f), jnp.float32)
    p1 = zeros.at[:, 0].set(conv_prev[:, 1]).reshape(m, f)
    p2 = zeros.at[:, 0].set(conv_prev[:, 0]).at[:, 1].set(conv_prev[:, 1]).reshape(m, f)
    full = lambda w: pl.BlockSpec((m, w), lambda j: (0, 0))
    colf = lambda r: pl.BlockSpec((r, tn), lambda j: (0, j))
    a, u = pl.pallas_call(
        functools.partial(_ffn_s_body, seq=t),
        grid=(f // tn,),
        in_specs=[full(d), colf(d), colf(d), colf(CONV_W), colf(1), colf(m), colf(m)],
        out_specs=[colf(m), colf(m)],
        out_shape=[jax.ShapeDtypeStruct((m, f), _MXU), jax.ShapeDtypeStruct((m, f), jnp.float32)],
        compiler_params=_params(("parallel",)),
        name="ffn_in_sample",
    )(h, w_up, w_gate, conv_w, conv_b, p1, p2)
    return a, u.reshape(n_batch, t, f)[:, t - (CONV_W - 1):]


def _page_specs(n_pages_per_step, layer, page_shape):
    zeros = (0,) * len(page_shape)
    return [pl.BlockSpec((None, None) + tuple(page_shape),
                         functools.partial(lambda b, s, pt, g: (layer, pt[b, s * n_pages_per_step + g]) + zeros, g=g))
            for g in range(n_pages_per_step)]


def _dsa_s_score_body(pt_ref, qi_ref, kiw_ref, *refs, n_pg):
    pages = refs[:n_pg]
    o_ref = refs[n_pg]
    ts = qi_ref.shape[0]
    ps = pages[0].shape[0]
    qst = jnp.concatenate([qi_ref[:, h * D_IDX:(h + 1) * D_IDX] for h in range(H_IDX)], axis=0).astype(_MXU)
    wi = kiw_ref[:, D_IDX:D_IDX + H_IDX] * (D_IDX ** -0.5 * H_IDX ** -0.5)
    for g in range(n_pg):
        s = lax.dot_general(qst, pages[g][...].astype(_MXU), (((1,), (1,)), ((), ())),
                            preferred_element_type=jnp.float32)
        sc = jnp.zeros((ts, ps), jnp.float32)
        for h in range(H_IDX):
            sc = sc + wi[:, h:h + 1] * jnp.maximum(s[h * ts:(h + 1) * ts], 0.0)
        o_ref[:, g * ps:(g + 1) * ps] = sc


def _dsa_s_select_body(sc_ref, qi_ref, kiw_ref, o_ref, key_ref, *, n_sel):
    ts = qi_ref.shape[0]
    lp = sc_ref.shape[1]
    qi = qi_ref[...]
    kx = kiw_ref[:, 0:D_IDX].astype(_MXU)
    kx = jnp.concatenate([kx, jnp.zeros((_LANES - ts, D_IDX), _MXU)], axis=0)
    wi = kiw_ref[:, D_IDX:D_IDX + H_IDX] * (D_IDX ** -0.5 * H_IDX ** -0.5)
    sc = jnp.zeros((ts, _LANES), jnp.float32)
    for h in range(H_IDX):
        s = lax.dot_general(qi[:, h * D_IDX:(h + 1) * D_IDX].astype(_MXU), kx, (((1,), (1,)), ((), ())),
                            preferred_element_type=jnp.float32)
        sc = sc + wi[:, h:h + 1] * jnp.maximum(s, 0.0)
    r = lax.broadcasted_iota(jnp.int32, (ts, _LANES), 0)
    c = lax.broadcasted_iota(jnp.int32, (ts, _LANES), 1)
    key_ref[:, 0:lp] = _sortable(sc_ref[...])
    key_ref[:, lp:lp + _LANES] = _sortable(jnp.where(c <= r, sc, -jnp.inf))

    key = key_ref[...]
    pos = lax.broadcasted_iota(jnp.int32, key.shape, 1)
    count = lambda m: jnp.sum(m.astype(jnp.int32), axis=1, keepdims=True)
    thr = _kth_largest_key(lambda t: count(key_ref[...] >= t), n_sel, ts)
    finite = thr > _KEY_NEG_INF
    need = n_sel - count(key > thr)
    cut = _first_index_cut(lambda p: count(jnp.logical_and(key_ref[...] == thr, pos < p)), need, ts, lp + _LANES)
    cut = jnp.where(finite, cut, jnp.int32(2 ** 30))
    keep = jnp.logical_or(key > thr, jnp.logical_and(key == thr, pos <= cut))
    keep = jnp.logical_and(keep, jnp.logical_or(pos < lp, pos - lp <= lax.broadcasted_iota(jnp.int32, key.shape, 0)))
    o_ref[...] = jnp.where(keep, 0.0, _NEG)


def _dsa_s_attn_body(pt_ref, qa_ref, kn_ref, vn_ref, negn_ref, neg_ref, bias_ref, *refs, n_pg):
    kpages = refs[:n_pg]
    vpages = refs[n_pg:2 * n_pg]
    o_ref, m_ref, l_ref, acc_ref = refs[2 * n_pg:]
    s = pl.program_id(1)
    n_steps = pl.num_programs(1)
    ts = qa_ref.shape[0]
    ps = kpages[0].shape[0]
    n_grp = H_A // HKV_A
    scale = DH ** -0.5
    qs = [_stack_heads(qa_ref, g, n_grp, scale) for g in range(HKV_A)]
    wb = bias_ref.shape[2]

    def attend(kt_of_group, vt_of_group, extra_of_head):
        for g in range(HKV_A):
            logits = lax.dot_general(qs[g], kt_of_group(g), (((1,), (1,)), ((), ())),
                                     preferred_element_type=jnp.float32)
            extra = jnp.concatenate([extra_of_head(g * n_grp + hh) for hh in range(n_grp)], axis=0)
            _flash_update(m_ref, l_ref, acc_ref, g, logits + extra, vt_of_group(g))

    @pl.when(s == 0)
    def _():
        _flash_init(m_ref, l_ref, acc_ref)
        pad = jnp.zeros((_LANES - ts, DH), _MXU)
        attend(lambda g: jnp.concatenate([kn_ref[:, g * DH:(g + 1) * DH].astype(_MXU), pad], axis=0),
               lambda g: jnp.concatenate([vn_ref[:, g * DH:(g + 1) * DH].astype(_MXU), pad], axis=0),
               lambda h: bias_ref[h, :, wb - _LANES:wb] + negn_ref[...])

    last = s == n_steps - 1
    w = n_pg * ps

    def extra_of_head(h):
        far = bias_ref[h, :, 0:1]
        base = neg_ref[...] + far
        near = jnp.where(last, bias_ref[h, :, wb - _LANES - ps:wb - _LANES] - far, 0.0)
        return jnp.concatenate([base[:, 0:w - ps], base[:, w - ps:w] + near], axis=1)

    attend(lambda g: jnp.concatenate([kpages[p][:, g, :] for p in range(n_pg)], axis=0).astype(_MXU),
           lambda g: jnp.concatenate([vpages[p][:, g, :] for p in range(n_pg)], axis=0).astype(_MXU),
           extra_of_head)

    @pl.when(last)
    def _():
        _flash_store(o_ref, l_ref, acc_ref, HKV_A, n_grp, ts)


def _dsa_sample(zs, kiws, cache_k, cache_v, cache_kidx, page_table, bias, layer, n_batch, ts):
    n_pages = page_table.shape[1]
    ps = cache_k.shape[2]
    past = n_pages * ps
    assert ps == _LANES and ts <= _LANES and ts % 8 == 0
    n_sel = min(TOPK_IDX, (past + ts) // 4)
    n_grp = H_A // HKV_A

    gi = _tile(n_pages, 16, 1)
    scores = pl.pallas_call(
        functools.partial(_dsa_s_score_body, n_pg=gi),
        grid_spec=pltpu.PrefetchScalarGridSpec(
            num_scalar_prefetch=1,
            grid=(n_batch, n_pages // gi),
            in_specs=[pl.BlockSpec((ts, H_IDX * D_IDX), lambda b, s, pt: (b, _C_QI // (H_IDX * D_IDX))),
                      pl.BlockSpec((ts, _LANES), lambda b, s, pt: (b, 0))]
            + _page_specs(gi, layer, (ps, D_IDX)),
            out_specs=pl.BlockSpec((None, ts, gi * ps), lambda b, s, pt: (b, 0, s)),
        ),
        out_shape=jax.ShapeDtypeStruct((n_batch, ts, past), jnp.float32),
        compiler_params=_params(("parallel", "arbitrary")),
        name="dsa_sample_scores",
    )(page_table, zs, kiws, *([cache_kidx] * gi))

    neg = pl.pallas_call(
        functools.partial(_dsa_s_select_body, n_sel=n_sel),
        grid=(n_batch,),
        in_specs=[pl.BlockSpec((None, ts, past), lambda b: (b, 0, 0)),
                  pl.BlockSpec((ts, H_IDX * D_IDX), lambda b: (b, _C_QI // (H_IDX * D_IDX))),
                  pl.BlockSpec((ts, _LANES), lambda b: (b, 0))],
        out_specs=pl.BlockSpec((None, ts, past + _LANES), lambda b: (b, 0, 0)),
        out_shape=jax.ShapeDtypeStruct((n_batch, ts, past + _LANES), jnp.float32),
        scratch_shapes=[pltpu.VMEM((ts, past + _LANES), jnp.int32)],
        compiler_params=_params(("parallel",)),
        name="dsa_sample_select",
    )(scores, zs, kiws)

    ga = _tile(n_pages, 8, 1)
    wkv = HKV_A * DH
    return pl.pallas_call(
        functools.partial(_dsa_s_attn_body, n_pg=ga),
        grid_spec=pltpu.PrefetchScalarGridSpec(
            num_scalar_prefetch=1,
            grid=(n_batch, n_pages // ga),
            in_specs=[pl.BlockSpec((ts, H_A * DH), lambda b, s, pt: (b, _C_QA // (H_A * DH))),
                      pl.BlockSpec((ts, wkv), lambda b, s, pt: (b, _C_KA // wkv)),
                      pl.BlockSpec((ts, wkv), lambda b, s, pt: (b, _C_VA // wkv)),
                      pl.BlockSpec((None, ts, _LANES), lambda b, s, pt: (b, 0, past // _LANES)),
                      pl.BlockSpec((None, ts, ga * ps), lambda b, s, pt: (b, 0, s)),
                      pl.BlockSpec((H_A, ts, bias.shape[2]), lambda b, s, pt: (0, 0, 0))]
            + _page_specs(ga, layer, (ps, HKV_A, DH)) + _page_specs(ga, layer, (ps, HKV_A, DH)),
            out_specs=pl.BlockSpec((ts, H_A * DH), lambda b, s, pt: (b, 0)),
            scratch_shapes=[pltpu.VMEM((HKV_A, n_grp * ts, 1), jnp.float32),
                            pltpu.VMEM((HKV_A, n_grp * ts, 1), jnp.float32),
                            pltpu.VMEM((HKV_A, n_grp * ts, DH), jnp.float32)],
        ),
        out_shape=jax.ShapeDtypeStruct((n_batch * ts, H_A * DH), _MXU),
        compiler_params=_params(("parallel", "arbitrary")),
        name="dsa_sample_attn",
    )(page_table, zs, zs, zs, neg, neg, bias, *([cache_k] * ga), *([cache_v] * ga))


def _moba_s_body(pt_ref, qb_ref, kn_ref, vn_ref, bias_ref, *refs, n_pg, n_sel, n_blocks):
    kpages = refs[:n_pg]
    vpages = refs[n_pg:2 * n_pg]
    o_ref, gate_ref, mb_ref, lb_ref, ob_ref = refs[2 * n_pg:]
    s = pl.program_id(1)
    n_steps = pl.num_programs(1)
    ts = qb_ref.shape[0]
    ps = kpages[0].shape[0]
    n_grp = H_B // HKV_B
    rows = n_grp * ts
    ppb = MOBA_BLOCK // ps
    bps = n_pg // ppb
    scale = DH ** -0.5
    wb = bias_ref.shape[2]
    qf = [jnp.concatenate([qb_ref[:, (g * n_grp + hh) * DH:(g * n_grp + hh + 1) * DH] for hh in range(n_grp)], axis=0)
          for g in range(HKV_B)]
    qs = [(q * scale).astype(_MXU) for q in qf]
    lane = lax.broadcasted_iota(jnp.int32, (rows, gate_ref.shape[2]), 1)
    last = s == n_steps - 1

    @pl.when(s == 0)
    def _():
        gate_ref[...] = jnp.zeros(gate_ref.shape, jnp.float32)
        mb_ref[...] = jnp.full(mb_ref.shape, _NEG, jnp.float32)
        lb_ref[...] = jnp.zeros(lb_ref.shape, jnp.float32)

    blk = MOBA_BLOCK
    for g in range(HKV_B):
        kf = jnp.concatenate([kpages[p][:, g, :] for p in range(n_pg)], axis=0)
        vt = jnp.concatenate([vpages[p][:, g, :] for p in range(n_pg)], axis=0).astype(_MXU)
        logits = lax.dot_general(qs[g], kf.astype(_MXU), (((1,), (1,)), ((), ())),
                                 preferred_element_type=jnp.float32)
        far = jnp.concatenate([bias_ref[g * n_grp + hh, :, 0:1] for hh in range(n_grp)], axis=0)
        near = jnp.concatenate([bias_ref[g * n_grp + hh, :, wb - _LANES - blk:wb - _LANES] for hh in range(n_grp)],
                               axis=0)
        gates, ms, ls, ps_ = gate_ref[g], mb_ref[g], lb_ref[g], []
        for jb in range(bps):
            seg = logits[:, jb * blk:(jb + 1) * blk] + far
            if jb == bps - 1:
                seg = seg + jnp.where(last, near - far, 0.0)
            kmean = jnp.mean(kf[jb * blk:(jb + 1) * blk], axis=0, keepdims=True)
            gate = jnp.sum(qf[g] * kmean, axis=1, keepdims=True)
            mj = jnp.max(seg, axis=1, keepdims=True)
            p = jnp.exp(seg - mj)
            lj = jnp.sum(p, axis=1, keepdims=True)
            here = lane == s * bps + jb
            gates = jnp.where(here, gate, gates)
            ms = jnp.where(here, mj, ms)
            ls = jnp.where(here, lj, ls)
            ps_.append(jnp.concatenate([p if k == jb else jnp.zeros_like(p) for k in range(bps)], axis=1))
        gate_ref[g], mb_ref[g], lb_ref[g] = gates, ms, ls
        o_blocks = jnp.dot(jnp.concatenate(ps_, axis=0).astype(_MXU), vt, preferred_element_type=jnp.float32)
        for jb in range(bps):
            ob_ref[s * bps + jb, g] = o_blocks[jb * rows:(jb + 1) * rows]

    @pl.when(last)
    def _():
        pad = jnp.zeros((_LANES - ts, HKV_B * DH), _MXU)
        kt = jnp.concatenate([kn_ref[...].astype(_MXU), pad], axis=0)
        vt = jnp.concatenate([vn_ref[...].astype(_MXU), pad], axis=0)
        r = lax.broadcasted_iota(jnp.int32, (ts, _LANES), 0)
        c = lax.broadcasted_iota(jnp.int32, (ts, _LANES), 1)
        causal_neg = jnp.where(c <= r, 0.0, _NEG)
        for g in range(HKV_B):
            logits = lax.dot_general(qs[g], kt[:, g * DH:(g + 1) * DH], (((1,), (1,)), ((), ())),
                                     preferred_element_type=jnp.float32)
            logits = logits + jnp.concatenate(
                [bias_ref[g * n_grp + hh, :, wb - _LANES:wb] + causal_neg for hh in range(n_grp)], axis=0)
            m_own = jnp.max(logits, axis=1, keepdims=True)
            p = jnp.exp(logits - m_own)
            l_own = jnp.sum(p, axis=1, keepdims=True)
            o_own = jnp.dot(p.astype(_MXU), vt[:, g * DH:(g + 1) * DH], preferred_element_type=jnp.float32)
            chosen = _topk_lanes(gate_ref[g], n_blocks, n_sel)
            mb = jnp.where(chosen, mb_ref[g], _NEG)
            m_all = jnp.maximum(m_own, jnp.max(mb, axis=1, keepdims=True))
            w = jnp.where(chosen, jnp.exp(mb - m_all), 0.0)
            w_own = jnp.exp(m_own - m_all)
            den = w_own * l_own + jnp.sum(w * lb_ref[g], axis=1, keepdims=True)

            def add_block(j, acc):
                wj = jnp.sum(jnp.where(lane == j, w, 0.0), axis=1, keepdims=True)
                return acc + wj * ob_ref[j, g]

            num = lax.fori_loop(0, n_blocks, add_block, w_own * o_own)
            o = num / den
            for hh in range(n_grp):
                h = g * n_grp + hh
                o_ref[:, h * DH:(h + 1) * DH] = o[hh * ts:(hh + 1) * ts].astype(o_ref.dtype)


def _moba_sample(zs, cache_k, cache_v, page_table, bias, layer, n_batch, ts):
    n_pages = page_table.shape[1]
    ps = cache_k.shape[2]
    past = n_pages * ps
    assert past % MOBA_BLOCK == 0 and MOBA_BLOCK % ps == 0 and ts <= _LANES and ts % 8 == 0
    n_blocks = past // MOBA_BLOCK
    n_sel = min(MOBA_TOPK, (past + ts - 1) // MOBA_BLOCK)
    ppb = MOBA_BLOCK // ps
    g_pg = ppb * _tile(n_blocks, 4, 1)
    n_grp = H_B // HKV_B
    lane_w = -(-n_blocks // _LANES) * _LANES
    wkv = HKV_B * DH
    return pl.pallas_call(
        functools.partial(_moba_s_body, n_pg=g_pg, n_sel=n_sel, n_blocks=n_blocks),
        grid_spec=pltpu.PrefetchScalarGridSpec(
            num_scalar_prefetch=1,
            grid=(n_batch, n_pages // g_pg),
            in_specs=[pl.BlockSpec((ts, H_B * DH), lambda b, s, pt: (b, _C_QB // (H_B * DH))),
                      pl.BlockSpec((ts, wkv), lambda b, s, pt: (b, _C_KB // wkv)),
                      pl.BlockSpec((ts, wkv), lambda b, s, pt: (b, _C_VB // wkv)),
                      pl.BlockSpec((H_B, ts, bias.shape[2]), lambda b, s, pt: (1, 0, 0))]
            + _page_specs(g_pg, layer, (ps, HKV_B, DH)) + _page_specs(g_pg, layer, (ps, HKV_B, DH)),
            out_specs=pl.BlockSpec((ts, H_B * DH), lambda b, s, pt: (b, 0)),
            scratch_shapes=[pltpu.VMEM((HKV_B, n_grp * ts, lane_w), jnp.float32),
                            pltpu.VMEM((HKV_B, n_grp * ts, lane_w), jnp.float32),
                            pltpu.VMEM((HKV_B, n_grp * ts, lane_w), jnp.float32),
                            pltpu.VMEM((n_blocks, HKV_B, n_grp * ts, DH), jnp.float32)],
        ),
        out_shape=jax.ShapeDtypeStruct((n_batch * ts, H_B * DH), _MXU),
        compiler_params=_params(("parallel", "arbitrary")),
        name="moba_sample",
    )(page_table, zs, zs, zs, bias, *([cache_k] * g_pg), *([cache_v] * g_pg))


def _relayout_w_in(w_in, d):
    sizes = dict(qa=H_A * DH, ka=HKV_A * DH, va=HKV_A * DH, qi=H_IDX * D_IDX, ki=D_IDX, wi=H_IDX,
                 qb=H_B * DH, kb=HKV_B * DH, vb=HKV_B * DH, qc=H_C * DK_C, kc=H_C * DK_C, vc=H_C * DV_C,
                 gc=H_C * DV_C, gates=3 * d)
    off, parts = 0, {}
    for name in ("qa", "ka", "va", "qi", "ki", "wi", "qb", "kb", "vb", "qc", "kc", "vc", "gc", "gates"):
        parts[name] = w_in[:, off:off + sizes[name]]
        off += sizes[name]
    assert off == w_in.shape[1]
    main = jnp.concatenate([parts[n] for n in ("vc", "gc", "qa", "qi", "qb", "qc", "kc", "ka", "va", "kb", "vb",
                                               "gates")], axis=1).astype(_MXU)
    kiw = jnp.concatenate([parts["ki"], parts["wi"],
                           jnp.zeros((d, _LANES - D_IDX - H_IDX), w_in.dtype)], axis=1).astype(_MXU)
    return main, kiw


def _pad_cols(w, n):
    return jnp.pad(w, ((0, 0), (0, n - w.shape[1])))


def _mix_and_ffn(x, z, o_a, o_b, o_c, wts, ffn_in):
    m = _merge(o_a, o_b, o_c, z, wts["w_pa"], wts["w_pb"], wts["w_pc"])
    x = _matmul(m, wts["w_out"], jnp.float32, residual=x)
    h2 = _rmsnorm(x, wts["norm_ffn"], _MXU)
    a, conv_new = ffn_in(h2)
    x = _matmul(a, wts["ffn_down"], jnp.float32, residual=x, tk_target=3072)
    return x, conv_new


def _state_slices(z, kiw, n_batch, t):
    ka = z[:, _C_KA:_C_KA + HKV_A * DH].reshape(n_batch, t, HKV_A, DH)
    va = z[:, _C_VA:_C_VA + HKV_A * DH].reshape(n_batch, t, HKV_A, DH)
    ki = kiw[:, :D_IDX].reshape(n_batch, t, D_IDX)
    kb = z[:, _C_KB:_C_KB + HKV_B * DH].reshape(n_batch, t, HKV_B, DH)
    vb = z[:, _C_VB:_C_VB + HKV_B * DH].reshape(n_batch, t, HKV_B, DH)
    return ka, va, ki, kb, vb


def kernel(x_prompt, x_sample, cache_a_k, cache_a_v, cache_a_kidx, cache_b_k, cache_b_v, state_ret, state_conv,
           page_table, rel_bias, norm_mix, w_in, ret_gn, w_pa, w_pb, w_pc, w_out, norm_ffn, ffn_up, ffn_gate,
           conv_w, conv_b, ffn_down, norm_final):
    bp, tp, d = x_prompt.shape
    bs, ts, _ = x_sample.shape
    depth = w_in.shape[0]
    n_pool, ps = cache_a_k.shape[1], cache_a_k.shape[2]
    past = page_table.shape[1] * ps
    f = ffn_up.shape[2]
    fp = -(-f // 512) * 512
    tq = _TQ
    assert d % _LANES == 0 and tp % tq == 0

    ar = jnp.arange
    bias_p = _bias_table(rel_bias, (ar(tq)[:, None] - ar(2 * tq)[None, :] + tq).astype(jnp.int32))
    wbs = 3 * _LANES
    bias_s = _bias_table(rel_bias, (ar(ts)[:, None] + (wbs - _LANES) - ar(wbs)[None, :]).astype(jnp.int32))
    assert H_A == H_B

    xp = x_prompt.reshape(bp * tp, d)
    xs = x_sample.reshape(bs * ts, d)
    pos_p = jnp.arange(tp, dtype=jnp.int32)
    pos_s = past + jnp.arange(ts, dtype=jnp.int32)
    st_p, st_s = [], []
    for l in range(depth):
        w_main, w_kiw = _relayout_w_in(w_in[l], d)
        wts = dict(w_pa=w_pa[l].astype(_MXU), w_pb=w_pb[l].astype(_MXU), w_pc=w_pc[l].astype(_MXU),
                   w_out=w_out[l].astype(_MXU), norm_ffn=norm_ffn[l],
                   ffn_down=jnp.pad(ffn_down[l], ((0, fp - f), (0, 0))).astype(_MXU))
        up = _pad_cols(ffn_up[l], fp).astype(_MXU)
        gate = _pad_cols(ffn_gate[l], fp).astype(_MXU)
        cw = _pad_cols(conv_w[l], fp)
        cb = _pad_cols(conv_b[l].reshape(1, f), fp)

        h = _rmsnorm(xp, norm_mix[l], _MXU)
        z = _matmul(h, w_main, jnp.float32)
        kiw = _matmul(h, w_kiw, jnp.float32)
        o_a = _dsa_prompt(z, kiw, bias_p, bp, tp)
        o_b = _moba_prompt(z, bias_p, bp, tp)
        o_c, ret_new = _retention(z, jnp.zeros((bp, H_C, DK_C, DV_C), jnp.float32), ret_gn[l], pos_p, bp, tp)
        xp, conv_new = _mix_and_ffn(xp, z, o_a, o_b, o_c, wts,
                                    lambda h2: _ffn_in_prompt(h2, up, gate, cw, cb, bp, tp))
        st_p.append(_state_slices(z, kiw, bp, tp) + (ret_new, conv_new[:, :, :f]))

        h = _rmsnorm(xs, norm_mix[l], _MXU)
        z = _matmul(h, w_main, jnp.float32)
        kiw = _matmul(h, w_kiw, jnp.float32)
        o_a = _dsa_sample(z, kiw, cache_a_k, cache_a_v, cache_a_kidx, page_table, bias_s, l, bs, ts)
        o_b = _moba_sample(z, cache_b_k, cache_b_v, page_table, bias_s, l, bs, ts)
        o_c, ret_new = _retention(z, state_ret[l], ret_gn[l], pos_s, bs, ts)
        conv_prev = _pad_cols(state_conv[l].reshape(bs * (CONV_W - 1), f), fp).reshape(bs, CONV_W - 1, fp)
        xs, conv_new = _mix_and_ffn(xs, z, o_a, o_b, o_c, wts,
                                    lambda h2: _ffn_in_sample(h2, up, gate, cw, cb, conv_prev, bs, ts))
        st_s.append(_state_slices(z, kiw, bs, ts) + (ret_new, conv_new[:, :, :f]))

    y_prompt = _rmsnorm(xp, norm_final, jnp.float32).reshape(bp, tp, d)
    y_sample = _rmsnorm(xs, norm_final, jnp.float32).reshape(bs, ts, d)
    outs_p = [jnp.stack(v) for v in zip(*st_p)]
    outs_s = [jnp.stack(v) for v in zip(*st_s)]
    return (y_prompt, y_sample, *outs_p, *outs_s)
```

```python
import functools
import math

import jax
import jax.numpy as jnp
from jax import lax
from jax.experimental import pallas as pl
from jax.experimental.pallas import tpu as pltpu

DH = 128
H_A, HKV_A = 8, 2
H_IDX, D_IDX = 16, 64
TOPK_IDX = 256
H_B, HKV_B = 8, 2
MOBA_BLOCK = 256
MOBA_TOPK = 3
H_C, DK_C, DV_C = 8, 128, 256
N_BUCKETS = 32
MAX_DIST = 128
CONV_W = 3
EPS = 1e-6
ROPE_BASE = 10000.0
RET_CHUNK = 128

_MXU = jnp.bfloat16
_NEG = -1e30
_TQ = 256
_LANES = 128
_VMEM_LIMIT = 56 * 1024 * 1024

_C_VC = 0
_C_GC = _C_VC + H_C * DV_C
_C_QA = _C_GC + H_C * DV_C
_C_QI = _C_QA + H_A * DH
_C_QB = _C_QI + H_IDX * D_IDX
_C_QC = _C_QB + H_B * DH
_C_KC = _C_QC + H_C * DK_C
_C_KA = _C_KC + H_C * DK_C
_C_VA = _C_KA + HKV_A * DH
_C_KB = _C_VA + HKV_A * DH
_C_VB = _C_KB + HKV_B * DH
_C_GATES = _C_VB + HKV_B * DH


def _params(sem):
    return pltpu.CompilerParams(dimension_semantics=sem, vmem_limit_bytes=_VMEM_LIMIT)


def _tile(n, target, mult=_LANES):
    best = None
    t = mult
    while t <= min(n, target):
        if n % t == 0:
            best = t
        t += mult
    return n if best is None else best


def _rmsnorm_body(x_ref, g_ref, o_ref):
    x = x_ref[...]
    y = x * lax.rsqrt(jnp.mean(x * x, axis=-1, keepdims=True) + EPS)
    o_ref[...] = (y * g_ref[...]).astype(o_ref.dtype)


def _rmsnorm(x, g, out_dtype):
    m, d = x.shape
    tm = _tile(m, 512, 8)
    return pl.pallas_call(
        _rmsnorm_body,
        grid=(m // tm,),
        in_specs=[pl.BlockSpec((tm, d), lambda i: (i, 0)), pl.BlockSpec((1, d), lambda i: (0, 0))],
        out_specs=pl.BlockSpec((tm, d), lambda i: (i, 0)),
        out_shape=jax.ShapeDtypeStruct((m, d), out_dtype),
        compiler_params=_params(("parallel",)),
        name="rmsnorm",
    )(x, g.reshape(1, d))


def _mm_body(*refs, nk, has_res):
    if has_res:
        x_ref, w_ref, r_ref, o_ref, acc_ref = refs
    else:
        x_ref, w_ref, o_ref, acc_ref = refs
    k = pl.program_id(2)

    @pl.when(k == 0)
    def _():
        acc_ref[...] = jnp.zeros_like(acc_ref)

    acc_ref[...] += jnp.dot(x_ref[...], w_ref[...], preferred_element_type=jnp.float32)

    @pl.when(k == nk - 1)
    def _():
        r = acc_ref[...]
        if has_res:
            r = r + r_ref[...]
        o_ref[...] = r.astype(o_ref.dtype)


def _matmul(x, w, out_dtype, residual=None, tm_target=1024, tn_target=1024, tk_target=2048):
    m, kdim = x.shape
    n = w.shape[1]
    tm = _tile(m, tm_target, 8)
    tn = _tile(n, tn_target)
    tk = _tile(kdim, tk_target)
    nk = kdim // tk
    in_specs = [pl.BlockSpec((tm, tk), lambda i, j, k: (i, k)), pl.BlockSpec((tk, tn), lambda i, j, k: (k, j))]
    args = [x, w]
    if residual is not None:
        in_specs.append(pl.BlockSpec((tm, tn), lambda i, j, k: (i, j)))
        args.append(residual)
    return pl.pallas_call(
        functools.partial(_mm_body, nk=nk, has_res=residual is not None),
        grid=(m // tm, n // tn, nk),
        in_specs=in_specs,
        out_specs=pl.BlockSpec((tm, tn), lambda i, j, k: (i, j)),
        out_shape=jax.ShapeDtypeStruct((m, n), out_dtype),
        scratch_shapes=[pltpu.VMEM((tm, tn), jnp.float32)],
        compiler_params=_params(("parallel", "parallel", "arbitrary")),
        name="matmul",
    )(*args)


def _t5_bucket(rel):
    n = jnp.maximum(rel, 0)
    max_exact = N_BUCKETS // 2
    nf = jnp.maximum(n, 1).astype(jnp.float32)
    large = max_exact + (jnp.log(nf / max_exact) / math.log(MAX_DIST / max_exact)
                         * (N_BUCKETS - max_exact)).astype(jnp.int32)
    large = jnp.minimum(large, N_BUCKETS - 1)
    return jnp.where(n < max_exact, n, large)


def _bias_body(tab_ref, bucket_ref, o_ref):
    h = pl.program_id(0)
    bucket = bucket_ref[...]
    acc = jnp.zeros(bucket.shape, jnp.float32)
    for b in range(N_BUCKETS):
        acc = jnp.where(bucket == b, tab_ref[b, h], acc)
    o_ref[...] = acc


def _bias_table(rel_bias, rel):
    bucket = _t5_bucket(rel)
    r, c = rel.shape
    nh = rel_bias.shape[1]
    return pl.pallas_call(
        _bias_body,
        grid=(nh,),
        in_specs=[pl.BlockSpec(memory_space=pltpu.SMEM), pl.BlockSpec((r, c), lambda h: (0, 0))],
        out_specs=pl.BlockSpec((None, r, c), lambda h: (h, 0, 0)),
        out_shape=jax.ShapeDtypeStruct((nh, r, c), jnp.float32),
        compiler_params=_params(("parallel",)),
        name="bias_table",
    )(rel_bias, bucket)


def _sortable(x):
    b = lax.bitcast_convert_type(x, jnp.int32)
    return b ^ ((b >> 31) & jnp.int32(0x7FFFFFFF))


_KEY_NEG_INF = -2147483648 + 0x7FFFFF


def _stack_heads(q_ref, g, n_grp, scale):
    parts = [q_ref[:, (g * n_grp + hh) * DH:(g * n_grp + hh + 1) * DH] for hh in range(n_grp)]
    return (jnp.concatenate(parts, axis=0) * scale).astype(_MXU)


def _flash_update(m_ref, l_ref, acc_ref, g, logits, v):
    m_old = m_ref[g]
    m_new = jnp.maximum(m_old, jnp.max(logits, axis=1, keepdims=True))
    alpha = jnp.exp(m_old - m_new)
    p = jnp.exp(logits - m_new)
    l_ref[g] = alpha * l_ref[g] + jnp.sum(p, axis=1, keepdims=True)
    acc_ref[g] = alpha * acc_ref[g] + jnp.dot(p.astype(_MXU), v, preferred_element_type=jnp.float32)
    m_ref[g] = m_new


def _flash_init(m_ref, l_ref, acc_ref):
    m_ref[...] = jnp.full(m_ref.shape, _NEG, jnp.float32)
    l_ref[...] = jnp.zeros(l_ref.shape, jnp.float32)
    acc_ref[...] = jnp.zeros(acc_ref.shape, jnp.float32)


def _flash_store(o_ref, l_ref, acc_ref, n_kv, n_grp, rows):
    for g in range(n_kv):
        o = acc_ref[g] / l_ref[g]
        for hh in range(n_grp):
            h = g * n_grp + hh
            o_ref[:, h * DH:(h + 1) * DH] = o[hh * rows:(hh + 1) * rows].astype(o_ref.dtype)


def _flash_update_t(m_ref, l_ref, acc_ref, g, logits_t, v_t):
    m_old = m_ref[g]
    m_new = jnp.maximum(m_old, jnp.max(logits_t, axis=0, keepdims=True))
    alpha = jnp.exp(m_old - m_new)
    p = jnp.exp(logits_t - m_new)
    l_ref[g] = alpha * l_ref[g] + jnp.sum(p, axis=0, keepdims=True)
    acc_ref[g] = alpha * acc_ref[g] + jnp.dot(v_t, p.astype(_MXU), preferred_element_type=jnp.float32)
    m_ref[g] = m_new


def _flash_store_t(o_ref, l_ref, acc_ref, n_kv, n_grp, rows):
    for g in range(n_kv):
        o = acc_ref[g] / l_ref[g]
        for hh in range(n_grp):
            h = g * n_grp + hh
            o_ref[:, h * DH:(h + 1) * DH] = o[:, hh * rows:(hh + 1) * rows].T.astype(o_ref.dtype)


def _kth_largest_key(count_ge, n_sel, shape):
    int_min = jnp.int32(-2147483648)
    prefix = jnp.full(shape, int_min, jnp.int32)
    zero = jnp.zeros(shape, jnp.int32)
    prefix = jnp.where(count_ge(zero) >= n_sel, zero, prefix)

    def step(it, prefix):
        cand = prefix | (jnp.int32(1) << (30 - it))
        return jnp.where(count_ge(cand) >= n_sel, cand, prefix)

    return lax.fori_loop(0, 31, step, prefix)


def _first_index_cut(count_lt, need, shape, n_idx):
    nbits = max(1, int(n_idx).bit_length())
    cut = jnp.zeros(shape, jnp.int32)

    def step(it, cut):
        cand = cut + (jnp.int32(1) << (nbits - 1 - it))
        return jnp.where(count_lt(cand) < need, cand, cut)

    return lax.fori_loop(0, nbits, step, cut)


def _stage_kv(k_in_ref, v_in_ref, kb_ref, vt_ref, tq):
    for j in range(kb_ref.shape[0]):
        kb_ref[j] = k_in_ref[j * tq:(j + 1) * tq, :].astype(_MXU)
        vt_ref[j] = v_in_ref[j * tq:(j + 1) * tq, :].T.astype(_MXU)


def _attend_t(j, qs, kb_ref, vt_ref, m_ref, l_ref, acc_ref, extra_of_group):
    for g in range(len(qs)):
        logits_t = lax.dot_general(kb_ref[j, :, g * DH:(g + 1) * DH], qs[g], (((1,), (1,)), ((), ())),
                                   preferred_element_type=jnp.float32)
        _flash_update_t(m_ref, l_ref, acc_ref, g, logits_t + extra_of_group(g), vt_ref[j, g * DH:(g + 1) * DH, :])


def _dsa_p_body(qa_ref, qi_ref, kiwq_ref, kiw_ref, ka_ref, va_ref, bias_ref, o_ref,
                kx_ref, kb_ref, vt_ref, key_ref, neg_ref, m_ref, l_ref, acc_ref, *, n_sel, t):
    i = pl.program_id(1)
    tq = _TQ
    n_grp = H_A // HKV_A
    krow = lax.broadcasted_iota(jnp.int32, (tq, tq), 0)
    qcol = lax.broadcasted_iota(jnp.int32, (tq, tq), 1)

    @pl.when(i == 0)
    def _():
        _stage_kv(ka_ref, va_ref, kb_ref, vt_ref, tq)
        for j in range(kx_ref.shape[0]):
            kx_ref[j] = kiw_ref[j * tq:(j + 1) * tq, 0:D_IDX].astype(_MXU)

    qi = [qi_ref[:, h * D_IDX:(h + 1) * D_IDX].astype(_MXU) for h in range(H_IDX)]
    wt = kiwq_ref[...].T[D_IDX:D_IDX + H_IDX, :] * (D_IDX ** -0.5 * H_IDX ** -0.5)

    def score_tile(j, carry):
        kx = kx_ref[j]
        sc = jnp.zeros((tq, tq), jnp.float32)
        for h in range(H_IDX):
            s = lax.dot_general(kx, qi[h], (((1,), (1,)), ((), ())), preferred_element_type=jnp.float32)
            sc = sc + wt[h:h + 1, :] * jnp.maximum(s, 0.0)
        causal = jnp.logical_or(j < i, krow <= qcol)
        key_ref[j] = _sortable(jnp.where(causal, sc, -jnp.inf))
        return carry

    lax.fori_loop(0, i + 1, score_tile, 0)

    def count_tiles(pred):
        def body(j, acc):
            return acc + jnp.sum(pred(key_ref[j], j).astype(jnp.int32).reshape(tq // 8, 8, tq), axis=0)
        acc = lax.fori_loop(0, i + 1, body, jnp.zeros((8, tq), jnp.int32))
        return jnp.sum(acc, axis=0, keepdims=True)

    thr = _kth_largest_key(lambda c: count_tiles(lambda k, j: k >= c), n_sel, (1, tq))
    n_ge = count_tiles(lambda k, j: k >= thr)
    finite = thr > _KEY_NEG_INF
    tie = jnp.max(jnp.where(jnp.logical_and(finite, n_ge > n_sel), 1, 0)) > 0

    def write_neg(cut):
        def body(j, carry):
            k = key_ref[j]
            kpos = j * tq + krow
            keep = jnp.logical_or(k > thr, jnp.logical_and(k == thr, kpos <= cut))
            keep = jnp.logical_and(keep, jnp.logical_or(j < i, krow <= qcol))
            neg_ref[j] = jnp.where(keep, 0.0, _NEG)
            return carry
        lax.fori_loop(0, i + 1, body, 0)

    def with_ties():
        n_gt = count_tiles(lambda k, j: k > thr)
        need = n_sel - n_gt
        cut = _first_index_cut(
            lambda p: count_tiles(lambda k, j: jnp.logical_and(k == thr, j * tq + krow < p)),
            need, (1, tq), t)
        write_neg(jnp.where(finite, cut, jnp.int32(2 ** 30)))

    def without_ties():
        write_neg(jnp.full((1, tq), 2 ** 30, jnp.int32))

    lax.cond(tie, with_ties, without_ties)

    _flash_init(m_ref, l_ref, acc_ref)
    scale = DH ** -0.5
    qs = [_stack_heads(qa_ref, g, n_grp, scale) for g in range(HKV_A)]

    def attend(j, bias_of_head):
        neg = neg_ref[j]
        _attend_t(j, qs, kb_ref, vt_ref, m_ref, l_ref, acc_ref,
                  lambda g: jnp.concatenate([bias_of_head(g * n_grp + hh) + neg for hh in range(n_grp)], axis=1))

    def far(j, carry):
        attend(j, lambda h: bias_ref[h, 0:1, tq - 1:tq])
        return carry

    lax.fori_loop(0, jnp.maximum(i - 1, 0), far, 0)

    @pl.when(i >= 1)
    def _():
        attend(i - 1, lambda h: bias_ref[h, 0:tq, :])

    attend(i, lambda h: bias_ref[h, tq:2 * tq, :])
    _flash_store_t(o_ref, l_ref, acc_ref, HKV_A, n_grp, tq)


def _kv_scratch(nk, tq, n_kv, n_grp):
    return [
        pltpu.VMEM((nk, tq, n_kv * DH), _MXU),
        pltpu.VMEM((nk, n_kv * DH, tq), _MXU),
    ], [
        pltpu.VMEM((n_kv, 1, n_grp * tq), jnp.float32),
        pltpu.VMEM((n_kv, 1, n_grp * tq), jnp.float32),
        pltpu.VMEM((n_kv, DH, n_grp * tq), jnp.float32),
    ]


def _dsa_prompt(z, kiw, bias, n_batch, t):
    tq = _TQ
    nq = t // tq
    n_sel = min(TOPK_IDX, t // 4)
    assert n_sel <= tq and t % tq == 0
    n_grp = H_A // HKV_A
    kv_scratch, flash_scratch = _kv_scratch(nq, tq, HKV_A, n_grp)
    return pl.pallas_call(
        functools.partial(_dsa_p_body, n_sel=n_sel, t=t),
        grid=(n_batch, nq),
        in_specs=[
            pl.BlockSpec((tq, H_A * DH), lambda b, i: (b * nq + i, _C_QA // (H_A * DH))),
            pl.BlockSpec((tq, H_IDX * D_IDX), lambda b, i: (b * nq + i, _C_QI // (H_IDX * D_IDX))),
            pl.BlockSpec((tq, _LANES), lambda b, i: (b * nq + i, 0)),
            pl.BlockSpec((t, _LANES), lambda b, i: (b, 0)),
            pl.BlockSpec((t, HKV_A * DH), lambda b, i: (b, _C_KA // (HKV_A * DH))),
            pl.BlockSpec((t, HKV_A * DH), lambda b, i: (b, _C_VA // (HKV_A * DH))),
            pl.BlockSpec((H_A, 2 * tq, tq), lambda b, i: (0, 0, 0)),
        ],
        out_specs=pl.BlockSpec((tq, H_A * DH), lambda b, i: (b * nq + i, 0)),
        out_shape=jax.ShapeDtypeStruct((n_batch * t, H_A * DH), _MXU),
        scratch_shapes=[pltpu.VMEM((nq, tq, D_IDX), _MXU)] + kv_scratch + [
            pltpu.VMEM((nq, tq, tq), jnp.int32),
            pltpu.VMEM((nq, tq, tq), jnp.float32),
        ] + flash_scratch,
        compiler_params=_params(("parallel", "arbitrary")),
        name="dsa_prompt",
    )(z, z, kiw, kiw, z, z, bias)


def _topk_lanes(gate, n_cand, n_sel):
    lane = lax.broadcasted_iota(jnp.int32, gate.shape, 1)
    live = lane < n_cand
    chosen = jnp.zeros(gate.shape, jnp.bool_)
    big = jnp.int32(2 ** 30)
    for _ in range(n_sel):
        cand = jnp.logical_and(live, jnp.logical_not(chosen))
        best = jnp.max(jnp.where(cand, gate, -jnp.inf), axis=1, keepdims=True)
        first = jnp.min(jnp.where(jnp.logical_and(cand, gate == best), lane, big), axis=1, keepdims=True)
        chosen = jnp.logical_or(chosen, lane == first)
    return chosen


def _topk_rows(gate, n_cand, n_sel):
    row = lax.broadcasted_iota(jnp.int32, gate.shape, 0)
    live = row < n_cand
    chosen = jnp.zeros(gate.shape, jnp.bool_)
    big = jnp.int32(2 ** 30)
    for _ in range(n_sel):
        cand = jnp.logical_and(live, jnp.logical_not(chosen))
        best = jnp.max(jnp.where(cand, gate, -jnp.inf), axis=0, keepdims=True)
        first = jnp.min(jnp.where(jnp.logical_and(cand, gate == best), row, big), axis=0, keepdims=True)
        chosen = jnp.logical_or(chosen, row == first)
    return chosen


def _moba_p_body(qb_ref, k_in_ref, v_in_ref, bias_ref, o_ref, kmean_ref, kb_ref, vt_ref, allow_ref,
                 m_ref, l_ref, acc_ref, *, n_sel):
    i = pl.program_id(1)
    tq = _TQ
    nb = kb_ref.shape[0]
    n_grp = H_B // HKV_B
    krow = lax.broadcasted_iota(jnp.int32, (tq, tq), 0)
    qcol = lax.broadcasted_iota(jnp.int32, (tq, tq), 1)

    @pl.when(i == 0)
    def _():
        _stage_kv(k_in_ref, v_in_ref, kb_ref, vt_ref, tq)
        kmean_ref[...] = jnp.zeros(kmean_ref.shape, jnp.float32)
        for j in range(nb):
            kmean_ref[j:j + 1, :] = jnp.mean(k_in_ref[j * tq:(j + 1) * tq, :], axis=0, keepdims=True)

    for g in range(HKV_B):
        qg = jnp.concatenate([qb_ref[:, (g * n_grp + hh) * DH:(g * n_grp + hh + 1) * DH] for hh in range(n_grp)],
                             axis=0)
        gate_t = lax.dot_general(kmean_ref[:, g * DH:(g + 1) * DH], qg, (((1,), (1,)), ((), ())),
                                 preferred_element_type=jnp.float32, precision=lax.Precision.HIGHEST)
        allow_ref[g] = jnp.where(_topk_rows(gate_t, i, n_sel), 0.0, _NEG)

    _flash_init(m_ref, l_ref, acc_ref)
    scale = DH ** -0.5
    qs = [_stack_heads(qb_ref, g, n_grp, scale) for g in range(HKV_B)]
    causal_neg = jnp.where(krow <= qcol, 0.0, _NEG)

    def attend_past(j, bias_of_head):
        _attend_t(j, qs, kb_ref, vt_ref, m_ref, l_ref, acc_ref,
                  lambda g: jnp.concatenate([jnp.broadcast_to(bias_of_head(g * n_grp + hh), (tq, tq))
                                             for hh in range(n_grp)], axis=1) + allow_ref[g, pl.ds(j, 1), :])

    def far(j, carry):
        attend_past(j, lambda h: bias_ref[h, 0:1, tq - 1:tq])
        return carry

    lax.fori_loop(0, jnp.maximum(i - 1, 0), far, 0)

    @pl.when(i >= 1)
    def _():
        attend_past(i - 1, lambda h: bias_ref[h, 0:tq, :])

    _attend_t(i, qs, kb_ref, vt_ref, m_ref, l_ref, acc_ref,
              lambda g: jnp.concatenate([bias_ref[g * n_grp + hh, tq:2 * tq, :] + causal_neg
                                         for hh in range(n_grp)], axis=1))
    _flash_store_t(o_ref, l_ref, acc_ref, HKV_B, n_grp, tq)


def _moba_prompt(z, bias, n_batch, t):
    tq = _TQ
    assert t % tq == 0 and MOBA_BLOCK == tq
    nb = t // tq
    n_sel = min(MOBA_TOPK, (t - 1) // MOBA_BLOCK)
    n_grp = H_B // HKV_B
    nb_pad = -(-nb // 8) * 8
    kv_scratch, flash_scratch = _kv_scratch(nb, tq, HKV_B, n_grp)
    return pl.pallas_call(
        functools.partial(_moba_p_body, n_sel=n_sel),
        grid=(n_batch, nb),
        in_specs=[
            pl.BlockSpec((tq, H_B * DH), lambda b, i: (b * nb + i, _C_QB // (H_B * DH))),
            pl.BlockSpec((t, HKV_B * DH), lambda b, i: (b, _C_KB // (HKV_B * DH))),
            pl.BlockSpec((t, HKV_B * DH), lambda b, i: (b, _C_VB // (HKV_B * DH))),
            pl.BlockSpec((H_B, 2 * tq, tq), lambda b, i: (1, 0, 0)),
        ],
        out_specs=pl.BlockSpec((tq, H_B * DH), lambda b, i: (b * nb + i, 0)),
        out_shape=jax.ShapeDtypeStruct((n_batch * t, H_B * DH), _MXU),
        scratch_shapes=[pltpu.VMEM((nb_pad, HKV_B * DH), jnp.float32)] + kv_scratch
        + [pltpu.VMEM((HKV_B, nb_pad, n_grp * tq), jnp.float32)] + flash_scratch,
        compiler_params=_params(("parallel", "arbitrary")),
        name="moba_prompt",
    )(z, z, z, bias)


def _ret_body(q_ref, k_ref, v_ref, g_ref, cos_ref, sin_ref, intra_ref, cross_ref, tail_ref, decay_ref, gn_ref,
              s0_ref, o_ref, s_out_ref, s_ref):
    t = pl.program_id(1)

    @pl.when(t == 0)
    def _():
        s_ref[...] = s0_ref[...]

    cos = cos_ref[...]
    sin = sin_ref[...]
    for h in range(H_C):
        qh = q_ref[:, h * DK_C:(h + 1) * DK_C]
        kh = k_ref[:, h * DK_C:(h + 1) * DK_C]
        q = qh * cos + pltpu.roll(qh, DK_C // 2, axis=1) * sin
        k = (kh * cos + pltpu.roll(kh, DK_C // 2, axis=1) * sin) * (DK_C ** -0.5)
        v = v_ref[:, h * DV_C:(h + 1) * DV_C].astype(_MXU)
        s = s_ref[h]
        qm = q.astype(_MXU)
        att = lax.dot_general(qm, k.astype(_MXU), (((1,), (1,)), ((), ())),
                              preferred_element_type=jnp.float32) * intra_ref[h]
        o = (jnp.dot(att.astype(_MXU), v, preferred_element_type=jnp.float32)
             + jnp.dot(qm, s.astype(_MXU), preferred_element_type=jnp.float32) * cross_ref[h])
        kt = (k * tail_ref[h]).astype(_MXU)
        s_ref[h] = s * decay_ref[h] + lax.dot_general(kt, v, (((0,), (0,)), ((), ())),
                                                      preferred_element_type=jnp.float32)
        o = o * lax.rsqrt(jnp.mean(o * o, axis=-1, keepdims=True) + EPS)
        gate = g_ref[:, h * DV_C:(h + 1) * DV_C]
        o = o * gn_ref[:, h * DV_C:(h + 1) * DV_C] * (gate * jax.nn.sigmoid(gate))
        o_ref[:, h * DV_C:(h + 1) * DV_C] = o.astype(o_ref.dtype)

    @pl.when(t == pl.num_programs(1) - 1)
    def _():
        s_out_ref[...] = s_ref[...]


def _retention(z, s0, ret_gn, pos, n_batch, t):
    c = math.gcd(t, RET_CHUNK)
    nc = t // c
    half = DK_C // 2
    inv = ROPE_BASE ** (-jnp.arange(0, DK_C, 2, dtype=jnp.float32) / DK_C)
    ang = pos.astype(jnp.float32)[:, None] * inv[None, :]
    cos = jnp.concatenate([jnp.cos(ang), jnp.cos(ang)], axis=1)
    sin = jnp.concatenate([-jnp.sin(ang), jnp.sin(ang)], axis=1)
    assert cos.shape == (t, 2 * half)
    log_g = jnp.log1p(-jnp.exp2(-5.0 - jnp.arange(H_C, dtype=jnp.float32)))
    idx = jnp.arange(c, dtype=jnp.float32)
    diff = idx[:, None] - idx[None, :]
    intra = jnp.where(diff[None] >= 0, jnp.exp(jnp.maximum(diff, 0.0)[None] * log_g[:, None, None]), 0.0)
    cross = jnp.exp((idx[None, :] + 1.0) * log_g[:, None])[:, :, None]
    tail = jnp.exp((c - 1.0 - idx)[None, :] * log_g[:, None])[:, :, None]
    decay = jnp.exp(c * log_g)[:, None, None]
    wq = H_C * DK_C
    wv = H_C * DV_C
    return pl.pallas_call(
        _ret_body,
        grid=(n_batch, nc),
        in_specs=[
            pl.BlockSpec((c, wq), lambda b, i: (b * nc + i, _C_QC // wq)),
            pl.BlockSpec((c, wq), lambda b, i: (b * nc + i, _C_KC // wq)),
            pl.BlockSpec((c, wv), lambda b, i: (b * nc + i, _C_VC // wv)),
            pl.BlockSpec((c, wv), lambda b, i: (b * nc + i, _C_GC // wv)),
            pl.BlockSpec((c, DK_C), lambda b, i: (i, 0)),
            pl.BlockSpec((c, DK_C), lambda b, i: (i, 0)),
            pl.BlockSpec((H_C, c, c), lambda b, i: (0, 0, 0)),
            pl.BlockSpec((H_C, c, 1), lambda b, i: (0, 0, 0)),
            pl.BlockSpec((H_C, c, 1), lambda b, i: (0, 0, 0)),
            pl.BlockSpec((H_C, 1, 1), lambda b, i: (0, 0, 0)),
            pl.BlockSpec((1, wv), lambda b, i: (0, 0)),
            pl.BlockSpec((None, H_C, DK_C, DV_C), lambda b, i: (b, 0, 0, 0)),
        ],
        out_specs=[
            pl.BlockSpec((c, wv), lambda b, i: (b * nc + i, 0)),
            pl.BlockSpec((None, H_C, DK_C, DV_C), lambda b, i: (b, 0, 0, 0)),
        ],
        out_shape=[
            jax.ShapeDtypeStruct((n_batch * t, wv), _MXU),
            jax.ShapeDtypeStruct((n_batch, H_C, DK_C, DV_C), jnp.float32),
        ],
        scratch_shapes=[pltpu.VMEM((H_C, DK_C, DV_C), jnp.float32)],
        compiler_params=_params(("parallel", "arbitrary")),
        name="retention",
    )(z, z, z, z, cos, sin, intra, cross, tail, decay, ret_gn.reshape(1, wv), s0)


def _merge_body(oa_ref, ob_ref, oc_ref, wa_ref, wb_ref, wc_ref, ga_ref, gb_ref, gc_ref, o_ref):
    def term(o, w, g):
        return jax.nn.sigmoid(g[...]) * jnp.dot(o[...], w[...], preferred_element_type=jnp.float32)
    o_ref[...] = (term(oa_ref, wa_ref, ga_ref) + term(ob_ref, wb_ref, gb_ref)
                  + term(oc_ref, wc_ref, gc_ref)).astype(o_ref.dtype)


def _merge(o_a, o_b, o_c, z, w_pa, w_pb, w_pc):
    m = o_a.shape[0]
    d = w_pa.shape[1]
    tm = _tile(m, 512, 8)
    tn = _tile(d, 512)
    gate_blk = [(_C_GATES + k * d) // tn for k in range(3)]
    assert all((_C_GATES + k * d) % tn == 0 for k in range(3))
    row = lambda w: pl.BlockSpec((tm, w), lambda i, j: (i, 0))
    col = lambda w: pl.BlockSpec((w, tn), lambda i, j: (0, j))
    gate = lambda k: pl.BlockSpec((tm, tn), lambda i, j: (i, gate_blk[k] + j))
    return pl.pallas_call(
        _merge_body,
        grid=(m // tm, d // tn),
        in_specs=[row(o_a.shape[1]), row(o_b.shape[1]), row(o_c.shape[1]),
                  col(w_pa.shape[0]), col(w_pb.shape[0]), col(w_pc.shape[0]), gate(0), gate(1), gate(2)],
        out_specs=pl.BlockSpec((tm, tn), lambda i, j: (i, j)),
        out_shape=jax.ShapeDtypeStruct((m, d), _MXU),
        compiler_params=_params(("parallel", "parallel")),
        name="merge",
    )(o_a, o_b, o_c, w_pa, w_pb, w_pc, z, z, z)


def _conv_gate(u, u1, u2, g, cw_ref, cb_ref):
    c = cb_ref[...] + cw_ref[0:1, :] * u2 + cw_ref[1:2, :] * u1 + cw_ref[2:3, :] * u
    return 0.5 * c * (1.0 + lax.erf(c * (2.0 ** -0.5))) * g


def _ffn_p_body(h_ref, wu_ref, wg_ref, cw_ref, cb_ref, a_ref, cs_ref, tail_ref, *, seq):
    i = pl.program_id(1)
    tm = h_ref.shape[0]
    h = h_ref[...]
    u = jnp.dot(h, wu_ref[...], preferred_element_type=jnp.float32)
    g = jnp.dot(h, wg_ref[...], preferred_element_type=jnp.float32)

    @pl.when((i * tm) % seq == 0)
    def _():
        tail_ref[...] = jnp.zeros_like(tail_ref)

    prev = tail_ref[...]
    row = lax.broadcasted_iota(jnp.int32, (tm, 1), 0)
    u1 = jnp.where(row == 0, prev[7:8, :], pltpu.roll(u, 1, axis=0))
    u2 = jnp.where(row == 0, prev[6:7, :], jnp.where(row == 1, prev[7:8, :], pltpu.roll(u, 2, axis=0)))
    a_ref[...] = _conv_gate(u, u1, u2, g, cw_ref, cb_ref).astype(a_ref.dtype)
    tail_ref[...] = u[tm - 8:tm, :]

    @pl.when((i * tm + tm) % seq == 0)
    def _():
        cs_ref[...] = u[tm - (CONV_W - 1):tm, :]


def _ffn_in_prompt(h, w_up, w_gate, conv_w, conv_b, n_batch, t):
    m, d = h.shape
    f = w_up.shape[1]
    tm = _tile(t, 1024, 8)
    tn = _tile(f, 512)
    per_seq = t // tm
    return pl.pallas_call(
        functools.partial(_ffn_p_body, seq=t),
        grid=(f // tn, m // tm),
        in_specs=[
            pl.BlockSpec((tm, d), lambda j, i: (i, 0)),
            pl.BlockSpec((d, tn), lambda j, i: (0, j)),
            pl.BlockSpec((d, tn), lambda j, i: (0, j)),
            pl.BlockSpec((CONV_W, tn), lambda j, i: (0, j)),
            pl.BlockSpec((1, tn), lambda j, i: (0, j)),
        ],
        out_specs=[
            pl.BlockSpec((tm, tn), lambda j, i: (i, j)),
            pl.BlockSpec((None, CONV_W - 1, tn), lambda j, i: (i // per_seq, 0, j)),
        ],
        out_shape=[
            jax.ShapeDtypeStruct((m, f), _MXU),
            jax.ShapeDtypeStruct((n_batch, CONV_W - 1, f), jnp.float32),
        ],
        scratch_shapes=[pltpu.VMEM((8, tn), jnp.float32)],
        compiler_params=_params(("parallel", "arbitrary")),
        name="ffn_in_prompt",
    )(h, w_up, w_gate, conv_w, conv_b)


def _ffn_s_body(h_ref, wu_ref, wg_ref, cw_ref, cb_ref, p1_ref, p2_ref, a_ref, u_ref, *, seq):
    tm = h_ref.shape[0]
    h = h_ref[...]
    u = jnp.dot(h, wu_ref[...], preferred_element_type=jnp.float32)
    g = jnp.dot(h, wg_ref[...], preferred_element_type=jnp.float32)
    pos = lax.broadcasted_iota(jnp.int32, (tm, 1), 0) % seq
    u1 = jnp.where(pos >= 1, pltpu.roll(u, 1, axis=0), p1_ref[...])
    u2 = jnp.where(pos >= 2, pltpu.roll(u, 2, axis=0), p2_ref[...])
    a_ref[...] = _conv_gate(u, u1, u2, g, cw_ref, cb_ref).astype(a_ref.dtype)
    u_ref[...] = u


def _ffn_in_sample(h, w_up, w_gate, conv_w, conv_b, conv_prev, n_batch, t):
    m, d = h.shape
    f = w_up.shape[1]
    assert t >= CONV_W - 1
    tn = _tile(f, 512)
    zeros = jnp.zeros((n_batch, t, f), jnp.float32)
    p1 = zeros.at[:, 0].set(conv_prev[:, 1]).reshape(m, f)
    p2 = zeros.at[:, 0].set(conv_prev[:, 0]).at[:, 1].set(conv_prev[:, 1]).reshape(m, f)
    full = lambda w: pl.BlockSpec((m, w), lambda j: (0, 0))
    colf = lambda r: pl.BlockSpec((r, tn), lambda j: (0, j))
    a, u = pl.pallas_call(
        functools.partial(_ffn_s_body, seq=t),
        grid=(f // tn,),
        in_specs=[full(d), colf(d), colf(d), colf(CONV_W), colf(1), colf(m), colf(m)],
        out_specs=[colf(m), colf(m)],
        out_shape=[jax.ShapeDtypeStruct((m, f), _MXU), jax.ShapeDtypeStruct((m, f), jnp.float32)],
        compiler_params=_params(("parallel",)),
        name="ffn_in_sample",
    )(h, w_up, w_gate, conv_w, conv_b, p1, p2)
    return a, u.reshape(n_batch, t, f)[:, t - (CONV_W - 1):]


def _page_specs(n_pages_per_step, layer, page_shape):
    zeros = (0,) * len(page_shape)
    return [pl.BlockSpec((None, None) + tuple(page_shape),
                         functools.partial(lambda b, s, pt, g: (layer, pt[b, s * n_pages_per_step + g]) + zeros, g=g))
            for g in range(n_pages_per_step)]


def _dsa_s_score_body(pt_ref, qi_ref, kiw_ref, *refs, n_pg):
    pages = refs[:n_pg]
    o_ref = refs[n_pg]
    ts = qi_ref.shape[0]
    ps = pages[0].shape[0]
    qst = jnp.concatenate([qi_ref[:, h * D_IDX:(h + 1) * D_IDX] for h in range(H_IDX)], axis=0).astype(_MXU)
    wi = kiw_ref[:, D_IDX:D_IDX + H_IDX] * (D_IDX ** -0.5 * H_IDX ** -0.5)
    for g in range(n_pg):
        s = lax.dot_general(qst, pages[g][...].astype(_MXU), (((1,), (1,)), ((), ())),
                            preferred_element_type=jnp.float32)
        sc = jnp.zeros((ts, ps), jnp.float32)
        for h in range(H_IDX):
            sc = sc + wi[:, h:h + 1] * jnp.maximum(s[h * ts:(h + 1) * ts], 0.0)
        o_ref[:, g * ps:(g + 1) * ps] = sc


def _dsa_s_select_body(sc_ref, qi_ref, kiw_ref, o_ref, on_ref, key_ref, *, n_sel, n_kv):
    ts = qi_ref.shape[0]
    lp = sc_ref.shape[1]
    qi = qi_ref[...]
    kx = kiw_ref[:, 0:D_IDX].astype(_MXU)
    kx = jnp.concatenate([kx, jnp.zeros((_LANES - ts, D_IDX), _MXU)], axis=0)
    wi = kiw_ref[:, D_IDX:D_IDX + H_IDX] * (D_IDX ** -0.5 * H_IDX ** -0.5)
    sc = jnp.zeros((ts, _LANES), jnp.float32)
    for h in range(H_IDX):
        s = lax.dot_general(qi[:, h * D_IDX:(h + 1) * D_IDX].astype(_MXU), kx, (((1,), (1,)), ((), ())),
                            preferred_element_type=jnp.float32)
        sc = sc + wi[:, h:h + 1] * jnp.maximum(s, 0.0)
    r = lax.broadcasted_iota(jnp.int32, (ts, _LANES), 0)
    c = lax.broadcasted_iota(jnp.int32, (ts, _LANES), 1)
    key_ref[:, 0:lp] = _sortable(sc_ref[...])
    key_ref[:, lp:lp + _LANES] = _sortable(jnp.where(c <= r, sc, -jnp.inf))

    key = key_ref[...]
    pos = lax.broadcasted_iota(jnp.int32, key.shape, 1)
    count = lambda m: jnp.sum(m.astype(jnp.int32), axis=1, keepdims=True)
    thr = _kth_largest_key(lambda t: count(key_ref[...] >= t), n_sel, (ts, 1))
    finite = thr > _KEY_NEG_INF
    need = n_sel - count(key > thr)
    cut = _first_index_cut(lambda p: count(jnp.logical_and(key_ref[...] == thr, pos < p)), need, (ts, 1),
                           lp + _LANES)
    cut = jnp.where(finite, cut, jnp.int32(2 ** 30))
    keep = jnp.logical_or(key > thr, jnp.logical_and(key == thr, pos <= cut))
    keep = jnp.logical_and(keep, jnp.logical_or(pos < lp, pos - lp <= lax.broadcasted_iota(jnp.int32, key.shape, 0)))
    on_ref[...] = jnp.where(keep[:, lp:lp + _LANES], 0.0, _NEG)
    keep_f = jnp.where(keep, 1.0, 0.0).astype(_MXU)
    spread = (lax.broadcasted_iota(jnp.int32, (_LANES, n_kv * _LANES), 1) // n_kv
              == lax.broadcasted_iota(jnp.int32, (_LANES, n_kv * _LANES), 0))
    spread = jnp.where(spread, 1.0, 0.0).astype(_MXU)
    for c in range(lp // _LANES):
        dup = jnp.dot(keep_f[:, c * _LANES:(c + 1) * _LANES], spread, preferred_element_type=jnp.float32)
        o_ref[:, c * n_kv * _LANES:(c + 1) * n_kv * _LANES] = jnp.where(dup > 0.5, 0.0, _NEG)


def _head_rows(pieces):
    return jnp.concatenate(pieces, axis=0)


def _parity_neg(n_heads, n_kv, ts, width):
    row_kv = lax.broadcasted_iota(jnp.int32, (n_heads * ts, width), 0) // ((n_heads // n_kv) * ts)
    col_kv = lax.broadcasted_iota(jnp.int32, (n_heads * ts, width), 1) % n_kv
    return jnp.where(row_kv == col_kv, 0.0, _NEG)


def _own_block(q_of_group, kn_ref, vn_ref, extra_of_head, n_kv, n_grp, ts):
    pad = jnp.zeros((_LANES - ts, DH), _MXU)
    ms, ls, os_ = [], [], []
    for g in range(n_kv):
        kt = jnp.concatenate([kn_ref[:, g * DH:(g + 1) * DH].astype(_MXU), pad], axis=0)
        vt = jnp.concatenate([vn_ref[:, g * DH:(g + 1) * DH].astype(_MXU), pad], axis=0)
        logits = lax.dot_general(q_of_group(g), kt, (((1,), (1,)), ((), ())), preferred_element_type=jnp.float32)
        logits = logits + _head_rows([extra_of_head(g * n_grp + hh) for hh in range(n_grp)])
        m = jnp.max(logits, axis=1, keepdims=True)
        p = jnp.exp(logits - m)
        ms.append(m)
        ls.append(jnp.sum(p, axis=1, keepdims=True))
        os_.append(jnp.dot(p.astype(_MXU), vt, preferred_element_type=jnp.float32))
    return _head_rows(ms), _head_rows(ls), _head_rows(os_)


def _dsa_s_attn_body(pt_ref, qa_ref, kn_ref, vn_ref, negn_ref, neg_ref, bias_ref, *refs, n_pg):
    kpages = refs[:n_pg]
    vpages = refs[n_pg:2 * n_pg]
    o_ref, m_ref, l_ref, acc_ref = refs[2 * n_pg:]
    s = pl.program_id(1)
    n_steps = pl.num_programs(1)
    ts = qa_ref.shape[0]
    rows_pg = kpages[0].shape[0]
    n_grp = H_A // HKV_A
    scale = DH ** -0.5
    qs = [_stack_heads(qa_ref, g, n_grp, scale) for g in range(HKV_A)]
    q_all = _head_rows(qs)
    wb = bias_ref.shape[2]
    own_w = HKV_A * _LANES

    @pl.when(s == 0)
    def _():
        m, l, o = _own_block(lambda g: qs[g], kn_ref, vn_ref,
                             lambda h: bias_ref[h, :, wb - own_w:wb - own_w + _LANES] + negn_ref[...],
                             HKV_A, n_grp, ts)
        m_ref[0], l_ref[0], acc_ref[0] = m, l, o

    last = s == n_steps - 1
    w = n_pg * rows_pg
    kt = jnp.concatenate([kpages[p][...] for p in range(n_pg)], axis=0).astype(_MXU)
    vt = jnp.concatenate([vpages[p][...] for p in range(n_pg)], axis=0).astype(_MXU)
    logits = lax.dot_general(q_all, kt, (((1,), (1,)), ((), ())), preferred_element_type=jnp.float32)
    neg = neg_ref[...]

    def extra_of_head(h):
        far = bias_ref[h, :, 0:1]
        base = neg + far
        near = jnp.where(last, bias_ref[h, :, wb - own_w - rows_pg:wb - own_w] - far, 0.0)
        return jnp.concatenate([base[:, 0:w - rows_pg], base[:, w - rows_pg:w] + near], axis=1)

    extra = _head_rows([extra_of_head(h) for h in range(H_A)]) + _parity_neg(H_A, HKV_A, ts, w)
    _flash_update(m_ref, l_ref, acc_ref, 0, logits + extra, vt)

    @pl.when(last)
    def _():
        o = acc_ref[0] / l_ref[0]
        for h in range(H_A):
            o_ref[:, h * DH:(h + 1) * DH] = o[h * ts:(h + 1) * ts].astype(o_ref.dtype)


def _dsa_sample(zs, kiws, cache_k, cache_v, cache_kidx, page_table, bias, layer, n_batch, ts):
    n_pages = page_table.shape[1]
    ps = cache_kidx.shape[2]
    past = n_pages * ps
    assert ps == _LANES and ts <= _LANES and ts % 8 == 0 and cache_k.shape[2] == ps * HKV_A
    n_sel = min(TOPK_IDX, (past + ts) // 4)
    n_grp = H_A // HKV_A

    gi = _tile(n_pages, 16, 1)
    scores = pl.pallas_call(
        functools.partial(_dsa_s_score_body, n_pg=gi),
        grid_spec=pltpu.PrefetchScalarGridSpec(
            num_scalar_prefetch=1,
            grid=(n_batch, n_pages // gi),
            in_specs=[pl.BlockSpec((ts, H_IDX * D_IDX), lambda b, s, pt: (b, _C_QI // (H_IDX * D_IDX))),
                      pl.BlockSpec((ts, _LANES), lambda b, s, pt: (b, 0))]
            + _page_specs(gi, layer, (ps, D_IDX)),
            out_specs=pl.BlockSpec((None, ts, gi * ps), lambda b, s, pt: (b, 0, s)),
        ),
        out_shape=jax.ShapeDtypeStruct((n_batch, ts, past), jnp.float32),
        compiler_params=_params(("parallel", "arbitrary")),
        name="dsa_sample_scores",
    )(page_table, zs, kiws, *([cache_kidx] * gi))

    neg, neg_own = pl.pallas_call(
        functools.partial(_dsa_s_select_body, n_sel=n_sel, n_kv=HKV_A),
        grid=(n_batch,),
        in_specs=[pl.BlockSpec((None, ts, past), lambda b: (b, 0, 0)),
                  pl.BlockSpec((ts, H_IDX * D_IDX), lambda b: (b, _C_QI // (H_IDX * D_IDX))),
                  pl.BlockSpec((ts, _LANES), lambda b: (b, 0))],
        out_specs=[pl.BlockSpec((None, ts, HKV_A * past), lambda b: (b, 0, 0)),
                   pl.BlockSpec((None, ts, _LANES), lambda b: (b, 0, 0))],
        out_shape=[jax.ShapeDtypeStruct((n_batch, ts, HKV_A * past), jnp.float32),
                   jax.ShapeDtypeStruct((n_batch, ts, _LANES), jnp.float32)],
        scratch_shapes=[pltpu.VMEM((ts, past + _LANES), jnp.int32)],
        compiler_params=_params(("parallel",)),
        name="dsa_sample_select",
    )(scores, zs, kiws)

    ga = _tile(n_pages, 8, 1)
    wkv = HKV_A * DH
    rows_pg = ps * HKV_A
    return pl.pallas_call(
        functools.partial(_dsa_s_attn_body, n_pg=ga),
        grid_spec=pltpu.PrefetchScalarGridSpec(
            num_scalar_prefetch=1,
            grid=(n_batch, n_pages // ga),
            in_specs=[pl.BlockSpec((ts, H_A * DH), lambda b, s, pt: (b, _C_QA // (H_A * DH))),
                      pl.BlockSpec((ts, wkv), lambda b, s, pt: (b, _C_KA // wkv)),
                      pl.BlockSpec((ts, wkv), lambda b, s, pt: (b, _C_VA // wkv)),
                      pl.BlockSpec((None, ts, _LANES), lambda b, s, pt: (b, 0, 0)),
                      pl.BlockSpec((None, ts, ga * rows_pg), lambda b, s, pt: (b, 0, s)),
                      pl.BlockSpec((H_A, ts, bias.shape[2]), lambda b, s, pt: (0, 0, 0))]
            + _page_specs(ga, layer, (rows_pg, DH)) + _page_specs(ga, layer, (rows_pg, DH)),
            out_specs=pl.BlockSpec((ts, H_A * DH), lambda b, s, pt: (b, 0)),
            scratch_shapes=[pltpu.VMEM((1, H_A * ts, 1), jnp.float32),
                            pltpu.VMEM((1, H_A * ts, 1), jnp.float32),
                            pltpu.VMEM((1, H_A * ts, DH), jnp.float32)],
        ),
        out_shape=jax.ShapeDtypeStruct((n_batch * ts, H_A * DH), _MXU),
        compiler_params=_params(("parallel", "arbitrary")),
        name="dsa_sample_attn",
    )(page_table, zs, zs, zs, neg_own, neg, bias, *([cache_k] * ga), *([cache_v] * ga))


def _moba_s_body(pt_ref, qb_ref, kn_ref, vn_ref, bias_ref, *refs, n_pg, n_sel, n_blocks):
    kpages = refs[:n_pg]
    vpages = refs[n_pg:2 * n_pg]
    o_ref, gate_ref, mb_ref, lb_ref, ob_ref = refs[2 * n_pg:]
    s = pl.program_id(1)
    n_steps = pl.num_programs(1)
    ts = qb_ref.shape[0]
    rows_pg = kpages[0].shape[0]
    n_grp = H_B // HKV_B
    rows = H_B * ts
    blk = MOBA_BLOCK * HKV_B
    bps = n_pg * rows_pg // blk
    scale = DH ** -0.5
    wb = bias_ref.shape[2]
    own_w = HKV_B * _LANES
    qf = [jnp.concatenate([qb_ref[:, (g * n_grp + hh) * DH:(g * n_grp + hh + 1) * DH] for hh in range(n_grp)], axis=0)
          for g in range(HKV_B)]
    qs = [(q * scale).astype(_MXU) for q in qf]
    q_all = _head_rows(qs)
    lane = lax.broadcasted_iota(jnp.int32, (rows, gate_ref.shape[1]), 1)
    last = s == n_steps - 1

    @pl.when(s == 0)
    def _():
        gate_ref[...] = jnp.zeros(gate_ref.shape, jnp.float32)
        mb_ref[...] = jnp.full(mb_ref.shape, _NEG, jnp.float32)
        lb_ref[...] = jnp.zeros(lb_ref.shape, jnp.float32)

    kf = jnp.concatenate([kpages[p][...] for p in range(n_pg)], axis=0)
    vt = jnp.concatenate([vpages[p][...] for p in range(n_pg)], axis=0).astype(_MXU)
    logits = lax.dot_general(q_all, kf.astype(_MXU), (((1,), (1,)), ((), ())), preferred_element_type=jnp.float32)
    far = _head_rows([bias_ref[h, :, 0:1] for h in range(H_B)])
    near = _head_rows([bias_ref[h, :, wb - own_w - blk:wb - own_w] for h in range(H_B)])
    parity = _parity_neg(H_B, HKV_B, ts, blk)
    sub_kv = lax.broadcasted_iota(jnp.int32, (8, DH), 0) % HKV_B
    gates, ms, ls, ps_ = gate_ref[...], mb_ref[...], lb_ref[...], []
    for jb in range(bps):
        seg = logits[:, jb * blk:(jb + 1) * blk] + (far + parity)
        if jb == bps - 1:
            seg = seg + jnp.where(last, near - far, 0.0)
        ksum = jnp.sum(kf[jb * blk:(jb + 1) * blk].reshape(blk // 8, 8, DH), axis=0)
        gate = _head_rows([
            jnp.sum(qf[g] * (jnp.sum(jnp.where(sub_kv == g, ksum, 0.0), axis=0, keepdims=True) / MOBA_BLOCK),
                    axis=1, keepdims=True) for g in range(HKV_B)])
        mj = jnp.max(seg, axis=1, keepdims=True)
        p = jnp.exp(seg - mj)
        lj = jnp.sum(p, axis=1, keepdims=True)
        here = lane == s * bps + jb
        gates = jnp.where(here, gate, gates)
        ms = jnp.where(here, mj, ms)
        ls = jnp.where(here, lj, ls)
        ps_.append(jnp.concatenate([p if k == jb else jnp.zeros_like(p) for k in range(bps)], axis=1))
    gate_ref[...], mb_ref[...], lb_ref[...] = gates, ms, ls
    o_blocks = jnp.dot(jnp.concatenate(ps_, axis=0).astype(_MXU), vt, preferred_element_type=jnp.float32)
    for jb in range(bps):
        ob_ref[s * bps + jb] = o_blocks[jb * rows:(jb + 1) * rows]

    @pl.when(last)
    def _():
        r = lax.broadcasted_iota(jnp.int32, (ts, _LANES), 0)
        c = lax.broadcasted_iota(jnp.int32, (ts, _LANES), 1)
        causal_neg = jnp.where(c <= r, 0.0, _NEG)
        m_own, l_own, o_own = _own_block(lambda g: qs[g], kn_ref, vn_ref,
                                         lambda h: bias_ref[h, :, wb - own_w:wb - own_w + _LANES] + causal_neg,
                                         HKV_B, n_grp, ts)
        chosen = _topk_lanes(gate_ref[...], n_blocks, n_sel)
        mb = jnp.where(chosen, mb_ref[...], _NEG)
        m_all = jnp.maximum(m_own, jnp.max(mb, axis=1, keepdims=True))
        w = jnp.where(chosen, jnp.exp(mb - m_all), 0.0)
        w_own = jnp.exp(m_own - m_all)
        den = w_own * l_own + jnp.sum(w * lb_ref[...], axis=1, keepdims=True)

        def add_block(j, acc):
            wj = jnp.sum(jnp.where(lane == j, w, 0.0), axis=1, keepdims=True)
            return acc + wj * ob_ref[j]

        num = lax.fori_loop(0, n_blocks, add_block, w_own * o_own)
        o = num / den
        for h in range(H_B):
            o_ref[:, h * DH:(h + 1) * DH] = o[h * ts:(h + 1) * ts].astype(o_ref.dtype)


def _moba_sample(zs, cache_k, cache_v, page_table, bias, layer, n_batch, ts):
    n_pages = page_table.shape[1]
    rows_pg = cache_k.shape[2]
    ps = rows_pg // HKV_B
    past = n_pages * ps
    assert past % MOBA_BLOCK == 0 and MOBA_BLOCK % ps == 0 and ts <= _LANES and ts % 8 == 0
    n_blocks = past // MOBA_BLOCK
    n_sel = min(MOBA_TOPK, (past + ts - 1) // MOBA_BLOCK)
    ppb = MOBA_BLOCK // ps
    g_pg = ppb * _tile(n_blocks, 4, 1)
    lane_w = -(-n_blocks // _LANES) * _LANES
    wkv = HKV_B * DH
    return pl.pallas_call(
        functools.partial(_moba_s_body, n_pg=g_pg, n_sel=n_sel, n_blocks=n_blocks),
        grid_spec=pltpu.PrefetchScalarGridSpec(
            num_scalar_prefetch=1,
            grid=(n_batch, n_pages // g_pg),
            in_specs=[pl.BlockSpec((ts, H_B * DH), lambda b, s, pt: (b, _C_QB // (H_B * DH))),
                      pl.BlockSpec((ts, wkv), lambda b, s, pt: (b, _C_KB // wkv)),
                      pl.BlockSpec((ts, wkv), lambda b, s, pt: (b, _C_VB // wkv)),
                      pl.BlockSpec((H_B, ts, bias.shape[2]), lambda b, s, pt: (1, 0, 0))]
            + _page_specs(g_pg, layer, (rows_pg, DH)) + _page_specs(g_pg, layer, (rows_pg, DH)),
            out_specs=pl.BlockSpec((ts, H_B * DH), lambda b, s, pt: (b, 0)),
            scratch_shapes=[pltpu.VMEM((H_B * ts, lane_w), jnp.float32),
                            pltpu.VMEM((H_B * ts, lane_w), jnp.float32),
                            pltpu.VMEM((H_B * ts, lane_w), jnp.float32),
                            pltpu.VMEM((n_blocks, H_B * ts, DH), jnp.float32)],
        ),
        out_shape=jax.ShapeDtypeStruct((n_batch * ts, H_B * DH), _MXU),
        compiler_params=_params(("parallel", "arbitrary")),
        name="moba_sample",
    )(page_table, zs, zs, zs, bias, *([cache_k] * g_pg), *([cache_v] * g_pg))


def _relayout_w_in(w_in, d):
    sizes = dict(qa=H_A * DH, ka=HKV_A * DH, va=HKV_A * DH, qi=H_IDX * D_IDX, ki=D_IDX, wi=H_IDX,
                 qb=H_B * DH, kb=HKV_B * DH, vb=HKV_B * DH, qc=H_C * DK_C, kc=H_C * DK_C, vc=H_C * DV_C,
                 gc=H_C * DV_C, gates=3 * d)
    off, parts = 0, {}
    for name in ("qa", "ka", "va", "qi", "ki", "wi", "qb", "kb", "vb", "qc", "kc", "vc", "gc", "gates"):
        parts[name] = w_in[:, off:off + sizes[name]]
        off += sizes[name]
    assert off == w_in.shape[1]
    main = jnp.concatenate([parts[n] for n in ("vc", "gc", "qa", "qi", "qb", "qc", "kc", "ka", "va", "kb", "vb",
                                               "gates")], axis=1).astype(_MXU)
    kiw = jnp.concatenate([parts["ki"], parts["wi"],
                           jnp.zeros((d, _LANES - D_IDX - H_IDX), w_in.dtype)], axis=1).astype(_MXU)
    return main, kiw


def _pad_cols(w, n):
    return jnp.pad(w, ((0, 0), (0, n - w.shape[1])))


def _mix_and_ffn(x, z, o_a, o_b, o_c, wts, ffn_in):
    m = _merge(o_a, o_b, o_c, z, wts["w_pa"], wts["w_pb"], wts["w_pc"])
    x = _matmul(m, wts["w_out"], jnp.float32, residual=x)
    h2 = _rmsnorm(x, wts["norm_ffn"], _MXU)
    a, conv_new = ffn_in(h2)
    x = _matmul(a, wts["ffn_down"], jnp.float32, residual=x, tk_target=3072)
    return x, conv_new


def _state_slices(z, kiw, n_batch, t):
    ka = z[:, _C_KA:_C_KA + HKV_A * DH].reshape(n_batch, t, HKV_A, DH)
    va = z[:, _C_VA:_C_VA + HKV_A * DH].reshape(n_batch, t, HKV_A, DH)
    ki = kiw[:, :D_IDX].reshape(n_batch, t, D_IDX)
    kb = z[:, _C_KB:_C_KB + HKV_B * DH].reshape(n_batch, t, HKV_B, DH)
    vb = z[:, _C_VB:_C_VB + HKV_B * DH].reshape(n_batch, t, HKV_B, DH)
    return ka, va, ki, kb, vb


def kernel(x_prompt, x_sample, cache_a_k, cache_a_v, cache_a_kidx, cache_b_k, cache_b_v, state_ret, state_conv,
           page_table, rel_bias, norm_mix, w_in, ret_gn, w_pa, w_pb, w_pc, w_out, norm_ffn, ffn_up, ffn_gate,
           conv_w, conv_b, ffn_down, norm_final):
    bp, tp, d = x_prompt.shape
    bs, ts, _ = x_sample.shape
    depth = w_in.shape[0]
    n_pool, ps = cache_a_k.shape[1], cache_a_k.shape[2]
    past = page_table.shape[1] * ps
    f = ffn_up.shape[2]
    fp = -(-f // 512) * 512
    tq = _TQ
    assert d % _LANES == 0 and tp % tq == 0

    ar = jnp.arange
    bias_p = _bias_table(rel_bias, (ar(tq)[None, :] - ar(2 * tq)[:, None] + tq).astype(jnp.int32))
    assert H_A == H_B and HKV_A == HKV_B
    rel_cached = (ar(ts)[:, None] + MOBA_BLOCK - ar(MOBA_BLOCK)[None, :]).astype(jnp.int32)
    rel_own = (ar(ts)[:, None] - ar(_LANES)[None, :]).astype(jnp.int32)
    bias_s = _bias_table(rel_bias, jnp.concatenate([jnp.repeat(rel_cached, HKV_A, axis=1), rel_own, rel_own], axis=1))
    cache_a_k = cache_a_k.reshape(depth, n_pool, ps * HKV_A, DH)
    cache_a_v = cache_a_v.reshape(depth, n_pool, ps * HKV_A, DH)
    cache_b_k = cache_b_k.reshape(depth, n_pool, ps * HKV_B, DH)
    cache_b_v = cache_b_v.reshape(depth, n_pool, ps * HKV_B, DH)

    xp = x_prompt.reshape(bp * tp, d)
    xs = x_sample.reshape(bs * ts, d)
    pos_p = jnp.arange(tp, dtype=jnp.int32)
    pos_s = past + jnp.arange(ts, dtype=jnp.int32)
    st_p, st_s = [], []
    for l in range(depth):
        w_main, w_kiw = _relayout_w_in(w_in[l], d)
        wts = dict(w_pa=w_pa[l].astype(_MXU), w_pb=w_pb[l].astype(_MXU), w_pc=w_pc[l].astype(_MXU),
                   w_out=w_out[l].astype(_MXU), norm_ffn=norm_ffn[l],
                   ffn_down=jnp.pad(ffn_down[l], ((0, fp - f), (0, 0))).astype(_MXU))
        up = _pad_cols(ffn_up[l], fp).astype(_MXU)
        gate = _pad_cols(ffn_gate[l], fp).astype(_MXU)
        cw = _pad_cols(conv_w[l], fp)
        cb = _pad_cols(conv_b[l].reshape(1, f), fp)

        h = _rmsnorm(xp, norm_mix[l], _MXU)
        z = _matmul(h, w_main, jnp.float32)
        kiw = _matmul(h, w_kiw, jnp.float32)
        o_a = _dsa_prompt(z, kiw, bias_p, bp, tp)
        o_b = _moba_prompt(z, bias_p, bp, tp)
        o_c, ret_new = _retention(z, jnp.zeros((bp, H_C, DK_C, DV_C), jnp.float32), ret_gn[l], pos_p, bp, tp)
        xp, conv_new = _mix_and_ffn(xp, z, o_a, o_b, o_c, wts,
                                    lambda h2: _ffn_in_prompt(h2, up, gate, cw, cb, bp, tp))
        st_p.append(_state_slices(z, kiw, bp, tp) + (ret_new, conv_new[:, :, :f]))

        h = _rmsnorm(xs, norm_mix[l], _MXU)
        z = _matmul(h, w_main, jnp.float32)
        kiw = _matmul(h, w_kiw, jnp.float32)
        o_a = _dsa_sample(z, kiw, cache_a_k, cache_a_v, cache_a_kidx, page_table, bias_s, l, bs, ts)
        o_b = _moba_sample(z, cache_b_k, cache_b_v, page_table, bias_s, l, bs, ts)
        o_c, ret_new = _retention(z, state_ret[l], ret_gn[l], pos_s, bs, ts)
        conv_prev = _pad_cols(state_conv[l].reshape(bs * (CONV_W - 1), f), fp).reshape(bs, CONV_W - 1, fp)
        xs, conv_new = _mix_and_ffn(xs, z, o_a, o_b, o_c, wts,
                                    lambda h2: _ffn_in_sample(h2, up, gate, cw, cb, conv_prev, bs, ts))
        st_s.append(_state_slices(z, kiw, bs, ts) + (ret_new, conv_new[:, :, :f]))

    y_prompt = _rmsnorm(xp, norm_final, jnp.float32).reshape(bp, tp, d)
    y_sample = _rmsnorm(xs, norm_final, jnp.float32).reshape(bs, ts, d)
    outs_p = [jnp.stack(v) for v in zip(*st_p)]
    outs_s = [jnp.stack(v) for v in zip(*st_s)]
    return (y_prompt, y_sample, *outs_p, *outs_s)
```

```python
import functools
import math

import jax
import jax.numpy as jnp
from jax import lax
from jax.experimental import pallas as pl
from jax.experimental.pallas import tpu as pltpu

DH = 128
H_A, HKV_A = 8, 2
H_IDX, D_IDX = 16, 64
TOPK_IDX = 256
H_B, HKV_B = 8, 2
MOBA_BLOCK = 256
MOBA_TOPK = 3
H_C, DK_C, DV_C = 8, 128, 256
N_BUCKETS = 32
MAX_DIST = 128
CONV_W = 3
EPS = 1e-6
ROPE_BASE = 10000.0
RET_CHUNK = 128

_MXU = jnp.bfloat16
_NEG = -1e30
_TQ = 256
_LANES = 128
_VMEM_LIMIT = 56 * 1024 * 1024

_C_VC = 0
_C_GC = _C_VC + H_C * DV_C
_C_QA = _C_GC + H_C * DV_C
_C_QI = _C_QA + H_A * DH
_C_QB = _C_QI + H_IDX * D_IDX
_C_QC = _C_QB + H_B * DH
_C_KC = _C_QC + H_C * DK_C
_C_KA = _C_KC + H_C * DK_C
_C_VA = _C_KA + HKV_A * DH
_C_KB = _C_VA + HKV_A * DH
_C_VB = _C_KB + HKV_B * DH
_C_GATES = _C_VB + HKV_B * DH


def _params(sem):
    return pltpu.CompilerParams(dimension_semantics=sem, vmem_limit_bytes=_VMEM_LIMIT)


def _tile(n, target, mult=_LANES):
    best = None
    t = mult
    while t <= min(n, target):
        if n % t == 0:
            best = t
        t += mult
    return n if best is None else best


def _rmsnorm_body(x_ref, g_ref, o_ref):
    x = x_ref[...]
    y = x * lax.rsqrt(jnp.mean(x * x, axis=-1, keepdims=True) + EPS)
    o_ref[...] = (y * g_ref[...]).astype(o_ref.dtype)


def _rmsnorm(x, g, out_dtype):
    m, d = x.shape
    tm = _tile(m, 512, 8)
    return pl.pallas_call(
        _rmsnorm_body,
        grid=(m // tm,),
        in_specs=[pl.BlockSpec((tm, d), lambda i: (i, 0)), pl.BlockSpec((1, d), lambda i: (0, 0))],
        out_specs=pl.BlockSpec((tm, d), lambda i: (i, 0)),
        out_shape=jax.ShapeDtypeStruct((m, d), out_dtype),
        compiler_params=_params(("parallel",)),
        name="rmsnorm",
    )(x, g.reshape(1, d))


def _mm_body(*refs, nk, has_res):
    if has_res:
        x_ref, w_ref, r_ref, o_ref, acc_ref = refs
    else:
        x_ref, w_ref, o_ref, acc_ref = refs
    k = pl.program_id(2)

    @pl.when(k == 0)
    def _():
        acc_ref[...] = jnp.zeros_like(acc_ref)

    acc_ref[...] += jnp.dot(x_ref[...], w_ref[...], preferred_element_type=jnp.float32)

    @pl.when(k == nk - 1)
    def _():
        r = acc_ref[...]
        if has_res:
            r = r + r_ref[...]
        o_ref[...] = r.astype(o_ref.dtype)


def _matmul(x, w, out_dtype, residual=None, tm_target=1024, tn_target=1024, tk_target=2048):
    m, kdim = x.shape
    n = w.shape[1]
    tm = _tile(m, tm_target, 8)
    tn = _tile(n, tn_target)
    tk = _tile(kdim, tk_target)
    nk = kdim // tk
    in_specs = [pl.BlockSpec((tm, tk), lambda i, j, k: (i, k)), pl.BlockSpec((tk, tn), lambda i, j, k: (k, j))]
    args = [x, w]
    if residual is not None:
        in_specs.append(pl.BlockSpec((tm, tn), lambda i, j, k: (i, j)))
        args.append(residual)
    return pl.pallas_call(
        functools.partial(_mm_body, nk=nk, has_res=residual is not None),
        grid=(m // tm, n // tn, nk),
        in_specs=in_specs,
        out_specs=pl.BlockSpec((tm, tn), lambda i, j, k: (i, j)),
        out_shape=jax.ShapeDtypeStruct((m, n), out_dtype),
        scratch_shapes=[pltpu.VMEM((tm, tn), jnp.float32)],
        compiler_params=_params(("parallel", "parallel", "arbitrary")),
        name="matmul",
    )(*args)


def _in_proj_body(*refs, n_prev, j_state):
    x_ref, w_ref = refs[0], refs[1]
    z_ref, ka_ref, va_ref, kb_ref, vb_ref = refs[2 + n_prev:]
    r = jnp.dot(x_ref[...], w_ref[...], preferred_element_type=jnp.float32)
    z_ref[...] = r

    @pl.when(pl.program_id(1) == j_state)
    def _():
        ka_ref[...] = r[:, 0 * HKV_A * DH:1 * HKV_A * DH]
        va_ref[...] = r[:, 1 * HKV_A * DH:2 * HKV_A * DH]
        kb_ref[...] = r[:, 2 * HKV_A * DH:2 * HKV_A * DH + HKV_B * DH]
        vb_ref[...] = r[:, 2 * HKV_A * DH + HKV_B * DH:2 * HKV_A * DH + 2 * HKV_B * DH]


def _in_proj(h, w_main, layer, depth, prev_states):
    m, kdim = h.shape
    n = w_main.shape[1]
    tm = _tile(m, 1024, 8)
    tn = 2 * HKV_A * DH + 2 * HKV_B * DH
    assert n % tn == 0 and _C_KA % tn == 0 and _C_VB + HKV_B * DH == _C_KA + tn
    widths = (HKV_A * DH, HKV_A * DH, HKV_B * DH, HKV_B * DH)
    prev = [] if prev_states is None else list(prev_states)
    return pl.pallas_call(
        functools.partial(_in_proj_body, n_prev=len(prev), j_state=_C_KA // tn),
        grid=(m // tm, n // tn),
        in_specs=[pl.BlockSpec((tm, kdim), lambda i, j: (i, 0)), pl.BlockSpec((kdim, tn), lambda i, j: (0, j))]
        + [pl.BlockSpec(memory_space=pl.ANY)] * len(prev),
        out_specs=[pl.BlockSpec((tm, tn), lambda i, j: (i, j))]
        + [pl.BlockSpec((None, tm, w), lambda i, j: (layer, i, 0)) for w in widths],
        out_shape=[jax.ShapeDtypeStruct((m, n), jnp.float32)]
        + [jax.ShapeDtypeStruct((depth, m, w), jnp.float32) for w in widths],
        input_output_aliases={2 + k: 1 + k for k in range(len(prev))},
        compiler_params=_params(("parallel", "arbitrary")),
        name="in_proj",
    )(h, w_main, *prev)


def _t5_bucket(rel):
    n = jnp.maximum(rel, 0)
    max_exact = N_BUCKETS // 2
    nf = jnp.maximum(n, 1).astype(jnp.float32)
    large = max_exact + (jnp.log(nf / max_exact) / math.log(MAX_DIST / max_exact)
                         * (N_BUCKETS - max_exact)).astype(jnp.int32)
    large = jnp.minimum(large, N_BUCKETS - 1)
    return jnp.where(n < max_exact, n, large)


def _bias_body(tab_ref, bucket_ref, o_ref):
    h = pl.program_id(0)
    bucket = bucket_ref[...]
    acc = jnp.zeros(bucket.shape, jnp.float32)
    for b in range(N_BUCKETS):
        acc = jnp.where(bucket == b, tab_ref[b, h], acc)
    o_ref[...] = acc


def _bias_table(rel_bias, rel):
    bucket = _t5_bucket(rel)
    r, c = rel.shape
    nh = rel_bias.shape[1]
    return pl.pallas_call(
        _bias_body,
        grid=(nh,),
        in_specs=[pl.BlockSpec(memory_space=pltpu.SMEM), pl.BlockSpec((r, c), lambda h: (0, 0))],
        out_specs=pl.BlockSpec((None, r, c), lambda h: (h, 0, 0)),
        out_shape=jax.ShapeDtypeStruct((nh, r, c), jnp.float32),
        compiler_params=_params(("parallel",)),
        name="bias_table",
    )(rel_bias, bucket)


def _sortable(x):
    b = lax.bitcast_convert_type(x, jnp.int32)
    return b ^ ((b >> 31) & jnp.int32(0x7FFFFFFF))


_KEY_NEG_INF = -2147483648 + 0x7FFFFF


def _stack_heads(q_ref, g, n_grp, scale):
    parts = [q_ref[:, (g * n_grp + hh) * DH:(g * n_grp + hh + 1) * DH] for hh in range(n_grp)]
    return (jnp.concatenate(parts, axis=0) * scale).astype(_MXU)


def _flash_update(m_ref, l_ref, acc_ref, g, logits, v):
    m_old = m_ref[g]
    m_new = jnp.maximum(m_old, jnp.max(logits, axis=1, keepdims=True))
    alpha = jnp.exp(m_old - m_new)
    p = jnp.exp(logits - m_new)
    l_ref[g] = alpha * l_ref[g] + jnp.sum(p, axis=1, keepdims=True)
    acc_ref[g] = alpha * acc_ref[g] + jnp.dot(p.astype(_MXU), v, preferred_element_type=jnp.float32)
    m_ref[g] = m_new


def _flash_init(m_ref, l_ref, acc_ref):
    m_ref[...] = jnp.full(m_ref.shape, _NEG, jnp.float32)
    l_ref[...] = jnp.zeros(l_ref.shape, jnp.float32)
    acc_ref[...] = jnp.zeros(acc_ref.shape, jnp.float32)


def _flash_store(o_ref, l_ref, acc_ref, n_kv, n_grp, rows):
    for g in range(n_kv):
        o = acc_ref[g] / l_ref[g]
        for hh in range(n_grp):
            h = g * n_grp + hh
            o_ref[:, h * DH:(h + 1) * DH] = o[hh * rows:(hh + 1) * rows].astype(o_ref.dtype)


def _flash_update_t(m_ref, l_ref, acc_ref, g, logits_t, v_t):
    m_old = m_ref[g]
    m_new = jnp.maximum(m_old, jnp.max(logits_t, axis=0, keepdims=True))
    alpha = jnp.exp(m_old - m_new)
    p = jnp.exp(logits_t - m_new)
    l_ref[g] = alpha * l_ref[g] + jnp.sum(p, axis=0, keepdims=True)
    acc_ref[g] = alpha * acc_ref[g] + jnp.dot(v_t, p.astype(_MXU), preferred_element_type=jnp.float32)
    m_ref[g] = m_new


def _flash_store_t(o_ref, l_ref, acc_ref, n_kv, n_grp, rows):
    for g in range(n_kv):
        o = acc_ref[g] / l_ref[g]
        for hh in range(n_grp):
            h = g * n_grp + hh
            o_ref[:, h * DH:(h + 1) * DH] = o[:, hh * rows:(hh + 1) * rows].T.astype(o_ref.dtype)


def _kth_largest_key(count_ge, n_sel, shape):
    int_min = jnp.int32(-2147483648)
    prefix = jnp.full(shape, int_min, jnp.int32)
    zero = jnp.zeros(shape, jnp.int32)
    prefix = jnp.where(count_ge(zero) >= n_sel, zero, prefix)

    def step(it, prefix):
        cand = prefix | (jnp.int32(1) << (30 - it))
        return jnp.where(count_ge(cand) >= n_sel, cand, prefix)

    return lax.fori_loop(0, 31, step, prefix)


def _first_index_cut(count_lt, need, shape, n_idx):
    nbits = max(1, int(n_idx).bit_length())
    cut = jnp.zeros(shape, jnp.int32)

    def step(it, cut):
        cand = cut + (jnp.int32(1) << (nbits - 1 - it))
        return jnp.where(count_lt(cand) < need, cand, cut)

    return lax.fori_loop(0, nbits, step, cut)


def _stage_kv(k_in_ref, v_in_ref, kb_ref, vt_ref, tq):
    for j in range(kb_ref.shape[0]):
        kb_ref[j] = k_in_ref[j * tq:(j + 1) * tq, :].astype(_MXU)
        vt_ref[j] = v_in_ref[j * tq:(j + 1) * tq, :].T.astype(_MXU)


def _attend_t(j, qs, kb_ref, vt_ref, m_ref, l_ref, acc_ref, extra_of_group):
    for g in range(len(qs)):
        logits_t = lax.dot_general(kb_ref[j, :, g * DH:(g + 1) * DH], qs[g], (((1,), (1,)), ((), ())),
                                   preferred_element_type=jnp.float32)
        _flash_update_t(m_ref, l_ref, acc_ref, g, logits_t + extra_of_group(g), vt_ref[j, g * DH:(g + 1) * DH, :])


def _dsa_p_body(qa_ref, qi_ref, kiwq_ref, kiw_ref, ka_ref, va_ref, bias_ref, o_ref,
                kx_ref, kb_ref, vt_ref, key_ref, neg_ref, m_ref, l_ref, acc_ref, *, n_sel, t):
    i = pl.program_id(1)
    tq = _TQ
    n_grp = H_A // HKV_A
    krow = lax.broadcasted_iota(jnp.int32, (tq, tq), 0)
    qcol = lax.broadcasted_iota(jnp.int32, (tq, tq), 1)

    @pl.when(i == 0)
    def _():
        _stage_kv(ka_ref, va_ref, kb_ref, vt_ref, tq)
        for j in range(kx_ref.shape[0]):
            kx_ref[j] = kiw_ref[j * tq:(j + 1) * tq, 0:D_IDX].astype(_MXU)

    qi = [qi_ref[:, h * D_IDX:(h + 1) * D_IDX].astype(_MXU) for h in range(H_IDX)]
    wt = kiwq_ref[...].T[D_IDX:D_IDX + H_IDX, :] * (D_IDX ** -0.5 * H_IDX ** -0.5)

    def score_tile(j, carry):
        kx = kx_ref[j]
        sc = jnp.zeros((tq, tq), jnp.float32)
        for h in range(H_IDX):
            s = lax.dot_general(kx, qi[h], (((1,), (1,)), ((), ())), preferred_element_type=jnp.float32)
            sc = sc + wt[h:h + 1, :] * jnp.maximum(s, 0.0)
        causal = jnp.logical_or(j < i, krow <= qcol)
        key_ref[j] = _sortable(jnp.where(causal, sc, -jnp.inf))
        return carry

    lax.fori_loop(0, i + 1, score_tile, 0)

    def count_tiles(pred):
        def body(j, acc):
            return acc + jnp.sum(pred(key_ref[j], j).astype(jnp.int32).reshape(tq // 8, 8, tq), axis=0)
        acc = lax.fori_loop(0, i + 1, body, jnp.zeros((8, tq), jnp.int32))
        return jnp.sum(acc, axis=0, keepdims=True)

    thr = _kth_largest_key(lambda c: count_tiles(lambda k, j: k >= c), n_sel, (1, tq))
    n_ge = count_tiles(lambda k, j: k >= thr)
    finite = thr > _KEY_NEG_INF
    tie = jnp.max(jnp.where(jnp.logical_and(finite, n_ge > n_sel), 1, 0)) > 0

    def write_neg(cut):
        def body(j, carry):
            k = key_ref[j]
            kpos = j * tq + krow
            keep = jnp.logical_or(k > thr, jnp.logical_and(k == thr, kpos <= cut))
            keep = jnp.logical_and(keep, jnp.logical_or(j < i, krow <= qcol))
            neg_ref[j] = jnp.where(keep, 0.0, _NEG)
            return carry
        lax.fori_loop(0, i + 1, body, 0)

    def with_ties():
        n_gt = count_tiles(lambda k, j: k > thr)
        need = n_sel - n_gt
        cut = _first_index_cut(
            lambda p: count_tiles(lambda k, j: jnp.logical_and(k == thr, j * tq + krow < p)),
            need, (1, tq), t)
        write_neg(jnp.where(finite, cut, jnp.int32(2 ** 30)))

    def without_ties():
        write_neg(jnp.full((1, tq), 2 ** 30, jnp.int32))

    lax.cond(tie, with_ties, without_ties)

    _flash_init(m_ref, l_ref, acc_ref)
    scale = DH ** -0.5
    qs = [_stack_heads(qa_ref, g, n_grp, scale) for g in range(HKV_A)]

    def attend(j, bias_of_head):
        neg = neg_ref[j]
        _attend_t(j, qs, kb_ref, vt_ref, m_ref, l_ref, acc_ref,
                  lambda g: jnp.concatenate([bias_of_head(g * n_grp + hh) + neg for hh in range(n_grp)], axis=1))

    def far(j, carry):
        attend(j, lambda h: bias_ref[h, 0:1, tq - 1:tq])
        return carry

    lax.fori_loop(0, jnp.maximum(i - 1, 0), far, 0)

    @pl.when(i >= 1)
    def _():
        attend(i - 1, lambda h: bias_ref[h, 0:tq, :])

    attend(i, lambda h: bias_ref[h, tq:2 * tq, :])
    _flash_store_t(o_ref, l_ref, acc_ref, HKV_A, n_grp, tq)


def _kv_scratch(nk, tq, n_kv, n_grp):
    return [
        pltpu.VMEM((nk, tq, n_kv * DH), _MXU),
        pltpu.VMEM((nk, n_kv * DH, tq), _MXU),
    ], [
        pltpu.VMEM((n_kv, 1, n_grp * tq), jnp.float32),
        pltpu.VMEM((n_kv, 1, n_grp * tq), jnp.float32),
        pltpu.VMEM((n_kv, DH, n_grp * tq), jnp.float32),
    ]


def _dsa_prompt(z, kiw, bias, n_batch, t):
    tq = _TQ
    nq = t // tq
    n_sel = min(TOPK_IDX, t // 4)
    assert n_sel <= tq and t % tq == 0
    n_grp = H_A // HKV_A
    kv_scratch, flash_scratch = _kv_scratch(nq, tq, HKV_A, n_grp)
    return pl.pallas_call(
        functools.partial(_dsa_p_body, n_sel=n_sel, t=t),
        grid=(n_batch, nq),
        in_specs=[
            pl.BlockSpec((tq, H_A * DH), lambda b, i: (b * nq + i, _C_QA // (H_A * DH))),
            pl.BlockSpec((tq, H_IDX * D_IDX), lambda b, i: (b * nq + i, _C_QI // (H_IDX * D_IDX))),
            pl.BlockSpec((tq, _LANES), lambda b, i: (b * nq + i, 0)),
            pl.BlockSpec((t, _LANES), lambda b, i: (b, 0)),
            pl.BlockSpec((t, HKV_A * DH), lambda b, i: (b, _C_KA // (HKV_A * DH))),
            pl.BlockSpec((t, HKV_A * DH), lambda b, i: (b, _C_VA // (HKV_A * DH))),
            pl.BlockSpec((H_A, 2 * tq, tq), lambda b, i: (0, 0, 0)),
        ],
        out_specs=pl.BlockSpec((tq, H_A * DH), lambda b, i: (b * nq + i, 0)),
        out_shape=jax.ShapeDtypeStruct((n_batch * t, H_A * DH), _MXU),
        scratch_shapes=[pltpu.VMEM((nq, tq, D_IDX), _MXU)] + kv_scratch + [
            pltpu.VMEM((nq, tq, tq), jnp.int32),
            pltpu.VMEM((nq, tq, tq), jnp.float32),
        ] + flash_scratch,
        compiler_params=_params(("parallel", "arbitrary")),
        name="dsa_prompt",
    )(z, z, kiw, kiw, z, z, bias)


def _topk_lanes(gate, n_cand, n_sel):
    lane = lax.broadcasted_iota(jnp.int32, gate.shape, 1)
    live = lane < n_cand
    chosen = jnp.zeros(gate.shape, jnp.bool_)
    big = jnp.int32(2 ** 30)
    for _ in range(n_sel):
        cand = jnp.logical_and(live, jnp.logical_not(chosen))
        best = jnp.max(jnp.where(cand, gate, -jnp.inf), axis=1, keepdims=True)
        first = jnp.min(jnp.where(jnp.logical_and(cand, gate == best), lane, big), axis=1, keepdims=True)
        chosen = jnp.logical_or(chosen, lane == first)
    return chosen


def _topk_rows(gate, n_cand, n_sel):
    row = lax.broadcasted_iota(jnp.int32, gate.shape, 0)
    live = row < n_cand
    chosen = jnp.zeros(gate.shape, jnp.bool_)
    big = jnp.int32(2 ** 30)
    for _ in range(n_sel):
        cand = jnp.logical_and(live, jnp.logical_not(chosen))
        best = jnp.max(jnp.where(cand, gate, -jnp.inf), axis=0, keepdims=True)
        first = jnp.min(jnp.where(jnp.logical_and(cand, gate == best), row, big), axis=0, keepdims=True)
        chosen = jnp.logical_or(chosen, row == first)
    return chosen


def _moba_p_body(qb_ref, k_in_ref, v_in_ref, bias_ref, o_ref, kmean_ref, kb_ref, vt_ref, allow_ref,
                 m_ref, l_ref, acc_ref, *, n_sel):
    i = pl.program_id(1)
    tq = _TQ
    nb = kb_ref.shape[0]
    n_grp = H_B // HKV_B
    krow = lax.broadcasted_iota(jnp.int32, (tq, tq), 0)
    qcol = lax.broadcasted_iota(jnp.int32, (tq, tq), 1)

    @pl.when(i == 0)
    def _():
        _stage_kv(k_in_ref, v_in_ref, kb_ref, vt_ref, tq)
        kmean_ref[...] = jnp.zeros(kmean_ref.shape, jnp.float32)
        for j in range(nb):
            kmean_ref[j:j + 1, :] = jnp.mean(k_in_ref[j * tq:(j + 1) * tq, :], axis=0, keepdims=True)

    for g in range(HKV_B):
        qg = jnp.concatenate([qb_ref[:, (g * n_grp + hh) * DH:(g * n_grp + hh + 1) * DH] for hh in range(n_grp)],
                             axis=0)
        gate_t = lax.dot_general(kmean_ref[:, g * DH:(g + 1) * DH], qg, (((1,), (1,)), ((), ())),
                                 preferred_element_type=jnp.float32, precision=lax.Precision.HIGHEST)
        allow_ref[g] = jnp.where(_topk_rows(gate_t, i, n_sel), 0.0, _NEG)

    _flash_init(m_ref, l_ref, acc_ref)
    scale = DH ** -0.5
    qs = [_stack_heads(qb_ref, g, n_grp, scale) for g in range(HKV_B)]
    causal_neg = jnp.where(krow <= qcol, 0.0, _NEG)

    def attend_past(j, bias_of_head):
        _attend_t(j, qs, kb_ref, vt_ref, m_ref, l_ref, acc_ref,
                  lambda g: jnp.concatenate([jnp.broadcast_to(bias_of_head(g * n_grp + hh), (tq, tq))
                                             for hh in range(n_grp)], axis=1) + allow_ref[g, pl.ds(j, 1), :])

    def far(j, carry):
        attend_past(j, lambda h: bias_ref[h, 0:1, tq - 1:tq])
        return carry

    lax.fori_loop(0, jnp.maximum(i - 1, 0), far, 0)

    @pl.when(i >= 1)
    def _():
        attend_past(i - 1, lambda h: bias_ref[h, 0:tq, :])

    _attend_t(i, qs, kb_ref, vt_ref, m_ref, l_ref, acc_ref,
              lambda g: jnp.concatenate([bias_ref[g * n_grp + hh, tq:2 * tq, :] + causal_neg
                                         for hh in range(n_grp)], axis=1))
    _flash_store_t(o_ref, l_ref, acc_ref, HKV_B, n_grp, tq)


def _moba_prompt(z, bias, n_batch, t):
    tq = _TQ
    assert t % tq == 0 and MOBA_BLOCK == tq
    nb = t // tq
    n_sel = min(MOBA_TOPK, (t - 1) // MOBA_BLOCK)
    n_grp = H_B // HKV_B
    nb_pad = -(-nb // 8) * 8
    kv_scratch, flash_scratch = _kv_scratch(nb, tq, HKV_B, n_grp)
    return pl.pallas_call(
        functools.partial(_moba_p_body, n_sel=n_sel),
        grid=(n_batch, nb),
        in_specs=[
            pl.BlockSpec((tq, H_B * DH), lambda b, i: (b * nb + i, _C_QB // (H_B * DH))),
            pl.BlockSpec((t, HKV_B * DH), lambda b, i: (b, _C_KB // (HKV_B * DH))),
            pl.BlockSpec((t, HKV_B * DH), lambda b, i: (b, _C_VB // (HKV_B * DH))),
            pl.BlockSpec((H_B, 2 * tq, tq), lambda b, i: (1, 0, 0)),
        ],
        out_specs=pl.BlockSpec((tq, H_B * DH), lambda b, i: (b * nb + i, 0)),
        out_shape=jax.ShapeDtypeStruct((n_batch * t, H_B * DH), _MXU),
        scratch_shapes=[pltpu.VMEM((nb_pad, HKV_B * DH), jnp.float32)] + kv_scratch
        + [pltpu.VMEM((HKV_B, nb_pad, n_grp * tq), jnp.float32)] + flash_scratch,
        compiler_params=_params(("parallel", "arbitrary")),
        name="moba_prompt",
    )(z, z, z, bias)


def _ret_body(q_ref, k_ref, v_ref, g_ref, cos_ref, sin_ref, intra_ref, cross_ref, tail_ref, decay_ref, gn_ref,
              s0_ref, o_ref, s_out_ref, s_ref):
    t = pl.program_id(1)

    @pl.when(t == 0)
    def _():
        s_ref[...] = s0_ref[...]

    cos = cos_ref[...]
    sin = sin_ref[...]
    for h in range(H_C):
        qh = q_ref[:, h * DK_C:(h + 1) * DK_C]
        kh = k_ref[:, h * DK_C:(h + 1) * DK_C]
        q = qh * cos + pltpu.roll(qh, DK_C // 2, axis=1) * sin
        k = (kh * cos + pltpu.roll(kh, DK_C // 2, axis=1) * sin) * (DK_C ** -0.5)
        v = v_ref[:, h * DV_C:(h + 1) * DV_C].astype(_MXU)
        s = s_ref[h]
        qm = q.astype(_MXU)
        att = lax.dot_general(qm, k.astype(_MXU), (((1,), (1,)), ((), ())),
                              preferred_element_type=jnp.float32) * intra_ref[h]
        o = (jnp.dot(att.astype(_MXU), v, preferred_element_type=jnp.float32)
             + jnp.dot(qm, s.astype(_MXU), preferred_element_type=jnp.float32) * cross_ref[h])
        kt = (k * tail_ref[h]).astype(_MXU)
        s_ref[h] = s * decay_ref[h] + lax.dot_general(kt, v, (((0,), (0,)), ((), ())),
                                                      preferred_element_type=jnp.float32)
        o = o * lax.rsqrt(jnp.mean(o * o, axis=-1, keepdims=True) + EPS)
        gate = g_ref[:, h * DV_C:(h + 1) * DV_C]
        o = o * gn_ref[:, h * DV_C:(h + 1) * DV_C] * (gate * jax.nn.sigmoid(gate))
        o_ref[:, h * DV_C:(h + 1) * DV_C] = o.astype(o_ref.dtype)

    @pl.when(t == pl.num_programs(1) - 1)
    def _():
        s_out_ref[...] = s_ref[...]


def _retention(z, s0, ret_gn, pos, n_batch, t):
    c = math.gcd(t, RET_CHUNK)
    nc = t // c
    half = DK_C // 2
    inv = ROPE_BASE ** (-jnp.arange(0, DK_C, 2, dtype=jnp.float32) / DK_C)
    ang = pos.astype(jnp.float32)[:, None] * inv[None, :]
    cos = jnp.concatenate([jnp.cos(ang), jnp.cos(ang)], axis=1)
    sin = jnp.concatenate([-jnp.sin(ang), jnp.sin(ang)], axis=1)
    assert cos.shape == (t, 2 * half)
    log_g = jnp.log1p(-jnp.exp2(-5.0 - jnp.arange(H_C, dtype=jnp.float32)))
    idx = jnp.arange(c, dtype=jnp.float32)
    diff = idx[:, None] - idx[None, :]
    intra = jnp.where(diff[None] >= 0, jnp.exp(jnp.maximum(diff, 0.0)[None] * log_g[:, None, None]), 0.0)
    cross = jnp.exp((idx[None, :] + 1.0) * log_g[:, None])[:, :, None]
    tail = jnp.exp((c - 1.0 - idx)[None, :] * log_g[:, None])[:, :, None]
    decay = jnp.exp(c * log_g)[:, None, None]
    wq = H_C * DK_C
    wv = H_C * DV_C
    return pl.pallas_call(
        _ret_body,
        grid=(n_batch, nc),
        in_specs=[
            pl.BlockSpec((c, wq), lambda b, i: (b * nc + i, _C_QC // wq)),
            pl.BlockSpec((c, wq), lambda b, i: (b * nc + i, _C_KC // wq)),
            pl.BlockSpec((c, wv), lambda b, i: (b * nc + i, _C_VC // wv)),
            pl.BlockSpec((c, wv), lambda b, i: (b * nc + i, _C_GC // wv)),
            pl.BlockSpec((c, DK_C), lambda b, i: (i, 0)),
            pl.BlockSpec((c, DK_C), lambda b, i: (i, 0)),
            pl.BlockSpec((H_C, c, c), lambda b, i: (0, 0, 0)),
            pl.BlockSpec((H_C, c, 1), lambda b, i: (0, 0, 0)),
            pl.BlockSpec((H_C, c, 1), lambda b, i: (0, 0, 0)),
            pl.BlockSpec((H_C, 1, 1), lambda b, i: (0, 0, 0)),
            pl.BlockSpec((1, wv), lambda b, i: (0, 0)),
            pl.BlockSpec((None, H_C, DK_C, DV_C), lambda b, i: (b, 0, 0, 0)),
        ],
        out_specs=[
            pl.BlockSpec((c, wv), lambda b, i: (b * nc + i, 0)),
            pl.BlockSpec((None, H_C, DK_C, DV_C), lambda b, i: (b, 0, 0, 0)),
        ],
        out_shape=[
            jax.ShapeDtypeStruct((n_batch * t, wv), _MXU),
            jax.ShapeDtypeStruct((n_batch, H_C, DK_C, DV_C), jnp.float32),
        ],
        scratch_shapes=[pltpu.VMEM((H_C, DK_C, DV_C), jnp.float32)],
        compiler_params=_params(("parallel", "arbitrary")),
        name="retention",
    )(z, z, z, z, cos, sin, intra, cross, tail, decay, ret_gn.reshape(1, wv), s0)


def _merge_body(oa_ref, ob_ref, oc_ref, wa_ref, wb_ref, wc_ref, ga_ref, gb_ref, gc_ref, o_ref):
    def term(o, w, g):
        return jax.nn.sigmoid(g[...]) * jnp.dot(o[...], w[...], preferred_element_type=jnp.float32)
    o_ref[...] = (term(oa_ref, wa_ref, ga_ref) + term(ob_ref, wb_ref, gb_ref)
                  + term(oc_ref, wc_ref, gc_ref)).astype(o_ref.dtype)


def _merge(o_a, o_b, o_c, z, w_pa, w_pb, w_pc):
    m = o_a.shape[0]
    d = w_pa.shape[1]
    tm = _tile(m, 1024, 8)
    tn = _tile(d, 512)
    gate_blk = [(_C_GATES + k * d) // tn for k in range(3)]
    assert all((_C_GATES + k * d) % tn == 0 for k in range(3))
    row = lambda w: pl.BlockSpec((tm, w), lambda i, j: (i, 0))
    col = lambda w: pl.BlockSpec((w, tn), lambda i, j: (0, j))
    gate = lambda k: pl.BlockSpec((tm, tn), lambda i, j: (i, gate_blk[k] + j))
    return pl.pallas_call(
        _merge_body,
        grid=(m // tm, d // tn),
        in_specs=[row(o_a.shape[1]), row(o_b.shape[1]), row(o_c.shape[1]),
                  col(w_pa.shape[0]), col(w_pb.shape[0]), col(w_pc.shape[0]), gate(0), gate(1), gate(2)],
        out_specs=pl.BlockSpec((tm, tn), lambda i, j: (i, j)),
        out_shape=jax.ShapeDtypeStruct((m, d), _MXU),
        compiler_params=_params(("parallel", "parallel")),
        name="merge",
    )(o_a, o_b, o_c, w_pa, w_pb, w_pc, z, z, z)


def _conv_gate(u, u1, u2, g, cw_ref, cb_ref):
    c = cb_ref[...] + cw_ref[0:1, :] * u2 + cw_ref[1:2, :] * u1 + cw_ref[2:3, :] * u
    return 0.5 * c * (1.0 + lax.erf(c * (2.0 ** -0.5))) * g


def _ffn_p_body(h_ref, wu_ref, wg_ref, cw_ref, cb_ref, a_ref, cs_ref, tail_ref, *, seq):
    i = pl.program_id(1)
    tm = h_ref.shape[0]
    h = h_ref[...]
    u = jnp.dot(h, wu_ref[...], preferred_element_type=jnp.float32)
    g = jnp.dot(h, wg_ref[...], preferred_element_type=jnp.float32)

    @pl.when((i * tm) % seq == 0)
    def _():
        tail_ref[...] = jnp.zeros_like(tail_ref)

    prev = tail_ref[...]
    row = lax.broadcasted_iota(jnp.int32, (tm, 1), 0)
    u1 = jnp.where(row == 0, prev[7:8, :], pltpu.roll(u, 1, axis=0))
    u2 = jnp.where(row == 0, prev[6:7, :], jnp.where(row == 1, prev[7:8, :], pltpu.roll(u, 2, axis=0)))
    a_ref[...] = _conv_gate(u, u1, u2, g, cw_ref, cb_ref).astype(a_ref.dtype)
    tail_ref[...] = u[tm - 8:tm, :]

    @pl.when((i * tm + tm) % seq == 0)
    def _():
        cs_ref[...] = u[tm - (CONV_W - 1):tm, :]


def _ffn_in_prompt(h, w_up, w_gate, conv_w, conv_b, n_batch, t):
    m, d = h.shape
    f = w_up.shape[1]
    tm = _tile(t, 1024, 8)
    tn = _tile(f, 512)
    per_seq = t // tm
    return pl.pallas_call(
        functools.partial(_ffn_p_body, seq=t),
        grid=(f // tn, m // tm),
        in_specs=[
            pl.BlockSpec((tm, d), lambda j, i: (i, 0)),
            pl.BlockSpec((d, tn), lambda j, i: (0, j)),
            pl.BlockSpec((d, tn), lambda j, i: (0, j)),
            pl.BlockSpec((CONV_W, tn), lambda j, i: (0, j)),
            pl.BlockSpec((1, tn), lambda j, i: (0, j)),
        ],
        out_specs=[
            pl.BlockSpec((tm, tn), lambda j, i: (i, j)),
            pl.BlockSpec((None, CONV_W - 1, tn), lambda j, i: (i // per_seq, 0, j)),
        ],
        out_shape=[
            jax.ShapeDtypeStruct((m, f), _MXU),
            jax.ShapeDtypeStruct((n_batch, CONV_W - 1, f), jnp.float32),
        ],
        scratch_shapes=[pltpu.VMEM((8, tn), jnp.float32)],
        compiler_params=_params(("parallel", "arbitrary")),
        name="ffn_in_prompt",
    )(h, w_up, w_gate, conv_w, conv_b)


def _ffn_s_body(h_ref, wu_ref, wg_ref, cw_ref, cb_ref, p1_ref, p2_ref, a_ref, u_ref, *, seq):
    tm = h_ref.shape[0]
    h = h_ref[...]
    u = jnp.dot(h, wu_ref[...], preferred_element_type=jnp.float32)
    g = jnp.dot(h, wg_ref[...], preferred_element_type=jnp.float32)
    pos = lax.broadcasted_iota(jnp.int32, (tm, 1), 0) % seq
    u1 = jnp.where(pos >= 1, pltpu.roll(u, 1, axis=0), p1_ref[...])
    u2 = jnp.where(pos >= 2, pltpu.roll(u, 2, axis=0), p2_ref[...])
    a_ref[...] = _conv_gate(u, u1, u2, g, cw_ref, cb_ref).astype(a_ref.dtype)
    u_ref[...] = u


def _ffn_in_sample(h, w_up, w_gate, conv_w, conv_b, conv_prev, n_batch, t):
    m, d = h.shape
    f = w_up.shape[1]
    assert t >= CONV_W - 1
    tn = _tile(f, 512)
    zeros = jnp.zeros((n_batch, t, f), jnp.float32)
    p1 = zeros.at[:, 0].set(conv_prev[:, 1]).reshape(m, f)
    p2 = zeros.at[:, 0].set(conv_prev[:, 0]).at[:, 1].set(conv_prev[:, 1]).reshape(m, f)
    full = lambda w: pl.BlockSpec((m, w), lambda j: (0, 0))
    colf = lambda r: pl.BlockSpec((r, tn), lambda j: (0, j))
    a, u = pl.pallas_call(
        functools.partial(_ffn_s_body, seq=t),
        grid=(f // tn,),
        in_specs=[full(d), colf(d), colf(d), colf(CONV_W), colf(1), colf(m), colf(m)],
        out_specs=[colf(m), colf(m)],
        out_shape=[jax.ShapeDtypeStruct((m, f), _MXU), jax.ShapeDtypeStruct((m, f), jnp.float32)],
        compiler_params=_params(("parallel",)),
        name="ffn_in_sample",
    )(h, w_up, w_gate, conv_w, conv_b, p1, p2)
    return a, u.reshape(n_batch, t, f)[:, t - (CONV_W - 1):]


def _page_specs(n_pages_per_step, layer, page_shape):
    zeros = (0,) * len(page_shape)
    return [pl.BlockSpec((None, None) + tuple(page_shape),
                         functools.partial(lambda b, s, pt, g: (layer, pt[b, s * n_pages_per_step + g]) + zeros, g=g))
            for g in range(n_pages_per_step)]


def _dsa_s_score_body(pt_ref, qi_ref, kiw_ref, *refs, n_pg):
    pages = refs[:n_pg]
    o_ref = refs[n_pg]
    ts = qi_ref.shape[0]
    ps = pages[0].shape[1]
    qst = jnp.concatenate([qi_ref[:, h * D_IDX:(h + 1) * D_IDX] for h in range(H_IDX)], axis=0).astype(_MXU)
    wi = kiw_ref[:, D_IDX:D_IDX + H_IDX] * (D_IDX ** -0.5 * H_IDX ** -0.5)
    for g in range(n_pg):
        s = jnp.dot(qst, pages[g][...].astype(_MXU), preferred_element_type=jnp.float32)
        sc = jnp.zeros((ts, ps), jnp.float32)
        for h in range(H_IDX):
            sc = sc + wi[:, h:h + 1] * jnp.maximum(s[h * ts:(h + 1) * ts], 0.0)
        o_ref[:, g * ps:(g + 1) * ps] = sc


def _dsa_s_select_body(sc_ref, qi_ref, kiw_ref, o_ref, on_ref, key_ref, *, n_sel, n_kv, ts):
    rows = qi_ref.shape[0]
    lp = sc_ref.shape[1]
    qi = qi_ref[...]
    kx = kiw_ref[:, 0:D_IDX].astype(_MXU)
    kx = jnp.concatenate([kx, jnp.zeros((_LANES - rows, D_IDX), _MXU)], axis=0)
    wi = kiw_ref[:, D_IDX:D_IDX + H_IDX] * (D_IDX ** -0.5 * H_IDX ** -0.5)
    sc = jnp.zeros((rows, _LANES), jnp.float32)
    for h in range(H_IDX):
        s = lax.dot_general(qi[:, h * D_IDX:(h + 1) * D_IDX].astype(_MXU), kx, (((1,), (1,)), ((), ())),
                            preferred_element_type=jnp.float32)
        sc = sc + wi[:, h:h + 1] * jnp.maximum(s, 0.0)
    sc = jnp.concatenate([pltpu.roll(sc[b * ts:(b + 1) * ts], (_LANES - b * ts) % _LANES, axis=1)
                          for b in range(rows // ts)], axis=0)
    r = lax.broadcasted_iota(jnp.int32, (rows, _LANES), 0) % ts
    c = lax.broadcasted_iota(jnp.int32, (rows, _LANES), 1)
    key_ref[:, 0:lp] = _sortable(sc_ref[...])
    key_ref[:, lp:lp + _LANES] = _sortable(jnp.where(c <= r, sc, -jnp.inf))

    width = lp + _LANES
    pos = lambda: lax.broadcasted_iota(jnp.int32, (rows, width), 1)
    count = lambda m: jnp.sum(m.astype(jnp.int32), axis=1, keepdims=True)
    thr = _kth_largest_key(lambda t: count(key_ref[...] >= t), n_sel, (rows, 1))
    finite = thr > _KEY_NEG_INF
    n_ge = count(key_ref[...] >= thr)
    tie = jnp.max(jnp.where(jnp.logical_and(finite, n_ge > n_sel), 1, 0)) > 0

    def with_ties():
        need = n_sel - count(key_ref[...] > thr)
        cut = _first_index_cut(lambda p: count(jnp.logical_and(key_ref[...] == thr, pos() < p)), need, (rows, 1),
                               width)
        return jnp.where(finite, cut, jnp.int32(2 ** 30))

    cut = lax.cond(tie, with_ties, lambda: jnp.full((rows, 1), 2 ** 30, jnp.int32))
    key = key_ref[...]
    keep = jnp.logical_or(key > thr, jnp.logical_and(key == thr, pos() <= cut))
    own = keep[:, lp:width]
    on_ref[...] = jnp.where(jnp.logical_and(own, c <= r), 0.0, _NEG)
    keep_f = jnp.where(keep, 1.0, 0.0).astype(_MXU)
    spread = (lax.broadcasted_iota(jnp.int32, (_LANES, n_kv * _LANES), 1) // n_kv
              == lax.broadcasted_iota(jnp.int32, (_LANES, n_kv * _LANES), 0))
    spread = jnp.where(spread, 1.0, 0.0).astype(_MXU)
    for cc in range(lp // _LANES):
        dup = jnp.dot(keep_f[:, cc * _LANES:(cc + 1) * _LANES], spread, preferred_element_type=jnp.float32)
        o_ref[:, cc * n_kv * _LANES:(cc + 1) * n_kv * _LANES] = jnp.where(dup > 0.5, 0.0, _NEG)


def _head_rows(pieces):
    return jnp.concatenate(pieces, axis=0)


def _parity_neg(n_heads, n_kv, ts, width):
    row_kv = lax.broadcasted_iota(jnp.int32, (n_heads * ts, width), 0) // ((n_heads // n_kv) * ts)
    col_kv = lax.broadcasted_iota(jnp.int32, (n_heads * ts, width), 1) % n_kv
    return jnp.where(row_kv == col_kv, 0.0, _NEG)


def _own_block(q_of_group, kn_ref, vn_ref, extra_of_head, n_kv, n_grp, ts):
    pad = jnp.zeros((_LANES - ts, DH), _MXU)
    ms, ls, os_ = [], [], []
    for g in range(n_kv):
        kt = jnp.concatenate([kn_ref[:, g * DH:(g + 1) * DH].astype(_MXU), pad], axis=0)
        vt = jnp.concatenate([vn_ref[:, g * DH:(g + 1) * DH].astype(_MXU), pad], axis=0)
        logits = lax.dot_general(q_of_group(g), kt, (((1,), (1,)), ((), ())), preferred_element_type=jnp.float32)
        logits = logits + _head_rows([extra_of_head(g * n_grp + hh) for hh in range(n_grp)])
        m = jnp.max(logits, axis=1, keepdims=True)
        p = jnp.exp(logits - m)
        ms.append(m)
        ls.append(jnp.sum(p, axis=1, keepdims=True))
        os_.append(jnp.dot(p.astype(_MXU), vt, preferred_element_type=jnp.float32))
    return _head_rows(ms), _head_rows(ls), _head_rows(os_)


def _dsa_s_attn_body(pt_ref, qa_ref, kn_ref, vn_ref, negn_ref, neg_ref, bias_ref, *refs, n_pg):
    kpages = refs[:n_pg]
    vpages = refs[n_pg:2 * n_pg]
    o_ref, m_ref, l_ref, acc_ref = refs[2 * n_pg:]
    s = pl.program_id(1)
    n_steps = pl.num_programs(1)
    ts = qa_ref.shape[0]
    rows_pg = kpages[0].shape[0]
    n_grp = H_A // HKV_A
    scale = DH ** -0.5
    qs = [_stack_heads(qa_ref, g, n_grp, scale) for g in range(HKV_A)]
    q_all = _head_rows(qs)
    wb = bias_ref.shape[2]
    own_w = HKV_A * _LANES

    @pl.when(s == 0)
    def _():
        m, l, o = _own_block(lambda g: qs[g], kn_ref, vn_ref,
                             lambda h: bias_ref[h, :, wb - own_w:wb - own_w + _LANES] + negn_ref[...],
                             HKV_A, n_grp, ts)
        m_ref[0], l_ref[0], acc_ref[0] = m, l, o

    last = s == n_steps - 1
    w = n_pg * rows_pg
    kt = jnp.concatenate([kpages[p][...] for p in range(n_pg)], axis=0).astype(_MXU)
    vt = jnp.concatenate([vpages[p][...] for p in range(n_pg)], axis=0).astype(_MXU)
    logits = lax.dot_general(q_all, kt, (((1,), (1,)), ((), ())), preferred_element_type=jnp.float32)
    neg = neg_ref[...]

    def extra_of_head(h):
        far = bias_ref[h, :, 0:1]
        base = neg + far
        near = jnp.where(last, bias_ref[h, :, wb - own_w - rows_pg:wb - own_w] - far, 0.0)
        return jnp.concatenate([base[:, 0:w - rows_pg], base[:, w - rows_pg:w] + near], axis=1)

    extra = _head_rows([extra_of_head(h) for h in range(H_A)]) + _parity_neg(H_A, HKV_A, ts, w)
    _flash_update(m_ref, l_ref, acc_ref, 0, logits + extra, vt)

    @pl.when(last)
    def _():
        o = acc_ref[0] / l_ref[0]
        for h in range(H_A):
            o_ref[:, h * DH:(h + 1) * DH] = o[h * ts:(h + 1) * ts].astype(o_ref.dtype)


def _dsa_sample(zs, kiws, cache_k, cache_v, cache_kidx, page_table, bias, layer, n_batch, ts):
    n_pages = page_table.shape[1]
    ps = cache_kidx.shape[3]
    past = n_pages * ps
    assert ps == _LANES and ts <= _LANES and ts % 8 == 0 and cache_k.shape[2] == ps * HKV_A
    n_sel = min(TOPK_IDX, (past + ts) // 4)
    n_grp = H_A // HKV_A

    gi = _tile(n_pages, 16, 1)
    scores = pl.pallas_call(
        functools.partial(_dsa_s_score_body, n_pg=gi),
        grid_spec=pltpu.PrefetchScalarGridSpec(
            num_scalar_prefetch=1,
            grid=(n_batch, n_pages // gi),
            in_specs=[pl.BlockSpec((ts, H_IDX * D_IDX), lambda b, s, pt: (b, _C_QI // (H_IDX * D_IDX))),
                      pl.BlockSpec((ts, _LANES), lambda b, s, pt: (b, 0))]
            + _page_specs(gi, layer, (D_IDX, ps)),
            out_specs=pl.BlockSpec((None, ts, gi * ps), lambda b, s, pt: (b, 0, s)),
        ),
        out_shape=jax.ShapeDtypeStruct((n_batch, ts, past), jnp.float32),
        compiler_params=_params(("parallel", "arbitrary")),
        name="dsa_sample_scores",
    )(page_table, zs, kiws, *([cache_kidx] * gi))

    rows = ts * _tile(n_batch, max(1, 32 // ts), 1)
    assert rows <= _LANES
    neg, neg_own = pl.pallas_call(
        functools.partial(_dsa_s_select_body, n_sel=n_sel, n_kv=HKV_A, ts=ts),
        grid=(n_batch * ts // rows,),
        in_specs=[pl.BlockSpec((rows, past), lambda i: (i, 0)),
                  pl.BlockSpec((rows, H_IDX * D_IDX), lambda i: (i, _C_QI // (H_IDX * D_IDX))),
                  pl.BlockSpec((rows, _LANES), lambda i: (i, 0))],
        out_specs=[pl.BlockSpec((rows, HKV_A * past), lambda i: (i, 0)),
                   pl.BlockSpec((rows, _LANES), lambda i: (i, 0))],
        out_shape=[jax.ShapeDtypeStruct((n_batch * ts, HKV_A * past), jnp.float32),
                   jax.ShapeDtypeStruct((n_batch * ts, _LANES), jnp.float32)],
        scratch_shapes=[pltpu.VMEM((rows, past + _LANES), jnp.int32)],
        compiler_params=_params(("parallel",)),
        name="dsa_sample_select",
    )(scores.reshape(n_batch * ts, past), zs, kiws)

    ga = _tile(n_pages, 8, 1)
    wkv = HKV_A * DH
    rows_pg = ps * HKV_A
    return pl.pallas_call(
        functools.partial(_dsa_s_attn_body, n_pg=ga),
        grid_spec=pltpu.PrefetchScalarGridSpec(
            num_scalar_prefetch=1,
            grid=(n_batch, n_pages // ga),
            in_specs=[pl.BlockSpec((ts, H_A * DH), lambda b, s, pt: (b, _C_QA // (H_A * DH))),
                      pl.BlockSpec((ts, wkv), lambda b, s, pt: (b, _C_KA // wkv)),
                      pl.BlockSpec((ts, wkv), lambda b, s, pt: (b, _C_VA // wkv)),
                      pl.BlockSpec((ts, _LANES), lambda b, s, pt: (b, 0)),
                      pl.BlockSpec((ts, ga * rows_pg), lambda b, s, pt: (b, s)),
                      pl.BlockSpec((H_A, ts, bias.shape[2]), lambda b, s, pt: (0, 0, 0))]
            + _page_specs(ga, layer, (rows_pg, DH)) + _page_specs(ga, layer, (rows_pg, DH)),
            out_specs=pl.BlockSpec((ts, H_A * DH), lambda b, s, pt: (b, 0)),
            scratch_shapes=[pltpu.VMEM((1, H_A * ts, 1), jnp.float32),
                            pltpu.VMEM((1, H_A * ts, 1), jnp.float32),
                            pltpu.VMEM((1, H_A * ts, DH), jnp.float32)],
        ),
        out_shape=jax.ShapeDtypeStruct((n_batch * ts, H_A * DH), _MXU),
        compiler_params=_params(("parallel", "arbitrary")),
        name="dsa_sample_attn",
    )(page_table, zs, zs, zs, neg_own, neg, bias, *([cache_k] * ga), *([cache_v] * ga))


def _moba_s_body(pt_ref, qb_ref, kn_ref, vn_ref, bias_ref, *refs, n_pg, n_sel, n_blocks):
    kpages = refs[:n_pg]
    vpages = refs[n_pg:2 * n_pg]
    o_ref, gate_ref, mb_ref, lb_ref, ob_ref = refs[2 * n_pg:]
    s = pl.program_id(1)
    n_steps = pl.num_programs(1)
    ts = qb_ref.shape[0]
    rows_pg = kpages[0].shape[0]
    n_grp = H_B // HKV_B
    rows = H_B * ts
    blk = MOBA_BLOCK * HKV_B
    bps = n_pg * rows_pg // blk
    scale = DH ** -0.5
    wb = bias_ref.shape[2]
    own_w = HKV_B * _LANES
    qf = [jnp.concatenate([qb_ref[:, (g * n_grp + hh) * DH:(g * n_grp + hh + 1) * DH] for hh in range(n_grp)], axis=0)
          for g in range(HKV_B)]
    qs = [(q * scale).astype(_MXU) for q in qf]
    q_all = _head_rows(qs)
    lane = lax.broadcasted_iota(jnp.int32, (rows, gate_ref.shape[1]), 1)
    last = s == n_steps - 1

    @pl.when(s == 0)
    def _():
        gate_ref[...] = jnp.zeros(gate_ref.shape, jnp.float32)
        mb_ref[...] = jnp.full(mb_ref.shape, _NEG, jnp.float32)
        lb_ref[...] = jnp.zeros(lb_ref.shape, jnp.float32)

    kf = jnp.concatenate([kpages[p][...] for p in range(n_pg)], axis=0)
    vt = jnp.concatenate([vpages[p][...] for p in range(n_pg)], axis=0).astype(_MXU)
    logits = lax.dot_general(q_all, kf.astype(_MXU), (((1,), (1,)), ((), ())), preferred_element_type=jnp.float32)
    far = _head_rows([bias_ref[h, :, 0:1] for h in range(H_B)])
    near = _head_rows([bias_ref[h, :, wb - own_w - blk:wb - own_w] for h in range(H_B)])
    parity = _parity_neg(H_B, HKV_B, ts, blk)
    sub_kv = lax.broadcasted_iota(jnp.int32, (8, DH), 0) % HKV_B
    gates, ms, ls, ps_ = gate_ref[...], mb_ref[...], lb_ref[...], []
    for jb in range(bps):
        seg = logits[:, jb * blk:(jb + 1) * blk] + (far + parity)
        if jb == bps - 1:
            seg = seg + jnp.where(last, near - far, 0.0)
        ksum = jnp.sum(kf[jb * blk:(jb + 1) * blk].reshape(blk // 8, 8, DH), axis=0)
        gate = _head_rows([
            jnp.sum(qf[g] * (jnp.sum(jnp.where(sub_kv == g, ksum, 0.0), axis=0, keepdims=True) / MOBA_BLOCK),
                    axis=1, keepdims=True) for g in range(HKV_B)])
        mj = jnp.max(seg, axis=1, keepdims=True)
        p = jnp.exp(seg - mj)
        lj = jnp.sum(p, axis=1, keepdims=True)
        here = lane == s * bps + jb
        gates = jnp.where(here, gate, gates)
        ms = jnp.where(here, mj, ms)
        ls = jnp.where(here, lj, ls)
        ps_.append(jnp.concatenate([p if k == jb else jnp.zeros_like(p) for k in range(bps)], axis=1))
    gate_ref[...], mb_ref[...], lb_ref[...] = gates, ms, ls
    o_blocks = jnp.dot(jnp.concatenate(ps_, axis=0).astype(_MXU), vt, preferred_element_type=jnp.float32)
    for jb in range(bps):
        ob_ref[s * bps + jb] = o_blocks[jb * rows:(jb + 1) * rows]

    @pl.when(last)
    def _():
        r = lax.broadcasted_iota(jnp.int32, (ts, _LANES), 0)
        c = lax.broadcasted_iota(jnp.int32, (ts, _LANES), 1)
        causal_neg = jnp.where(c <= r, 0.0, _NEG)
        m_own, l_own, o_own = _own_block(lambda g: qs[g], kn_ref, vn_ref,
                                         lambda h: bias_ref[h, :, wb - own_w:wb - own_w + _LANES] + causal_neg,
                                         HKV_B, n_grp, ts)
        chosen = _topk_lanes(gate_ref[...], n_blocks, n_sel)
        mb = jnp.where(chosen, mb_ref[...], _NEG)
        m_all = jnp.maximum(m_own, jnp.max(mb, axis=1, keepdims=True))
        w = jnp.where(chosen, jnp.exp(mb - m_all), 0.0)
        w_own = jnp.exp(m_own - m_all)
        den = w_own * l_own + jnp.sum(w * lb_ref[...], axis=1, keepdims=True)

        def add_block(j, acc):
            wj = jnp.sum(jnp.where(lane == j, w, 0.0), axis=1, keepdims=True)
            return acc + wj * ob_ref[j]

        num = lax.fori_loop(0, n_blocks, add_block, w_own * o_own)
        o = num / den
        for h in range(H_B):
            o_ref[:, h * DH:(h + 1) * DH] = o[h * ts:(h + 1) * ts].astype(o_ref.dtype)


def _moba_sample(zs, cache_k, cache_v, page_table, bias, layer, n_batch, ts):
    n_pages = page_table.shape[1]
    rows_pg = cache_k.shape[2]
    ps = rows_pg // HKV_B
    past = n_pages * ps
    assert past % MOBA_BLOCK == 0 and MOBA_BLOCK % ps == 0 and ts <= _LANES and ts % 8 == 0
    n_blocks = past // MOBA_BLOCK
    n_sel = min(MOBA_TOPK, (past + ts - 1) // MOBA_BLOCK)
    ppb = MOBA_BLOCK // ps
    g_pg = ppb * _tile(n_blocks, 4, 1)
    lane_w = -(-n_blocks // _LANES) * _LANES
    wkv = HKV_B * DH
    return pl.pallas_call(
        functools.partial(_moba_s_body, n_pg=g_pg, n_sel=n_sel, n_blocks=n_blocks),
        grid_spec=pltpu.PrefetchScalarGridSpec(
            num_scalar_prefetch=1,
            grid=(n_batch, n_pages // g_pg),
            in_specs=[pl.BlockSpec((ts, H_B * DH), lambda b, s, pt: (b, _C_QB // (H_B * DH))),
                      pl.BlockSpec((ts, wkv), lambda b, s, pt: (b, _C_KB // wkv)),
                      pl.BlockSpec((ts, wkv), lambda b, s, pt: (b, _C_VB // wkv)),
                      pl.BlockSpec((H_B, ts, bias.shape[2]), lambda b, s, pt: (1, 0, 0))]
            + _page_specs(g_pg, layer, (rows_pg, DH)) + _page_specs(g_pg, layer, (rows_pg, DH)),
            out_specs=pl.BlockSpec((ts, H_B * DH), lambda b, s, pt: (b, 0)),
            scratch_shapes=[pltpu.VMEM((H_B * ts, lane_w), jnp.float32),
                            pltpu.VMEM((H_B * ts, lane_w), jnp.float32),
                            pltpu.VMEM((H_B * ts, lane_w), jnp.float32),
                            pltpu.VMEM((n_blocks, H_B * ts, DH), jnp.float32)],
        ),
        out_shape=jax.ShapeDtypeStruct((n_batch * ts, H_B * DH), _MXU),
        compiler_params=_params(("parallel", "arbitrary")),
        name="moba_sample",
    )(page_table, zs, zs, zs, bias, *([cache_k] * g_pg), *([cache_v] * g_pg))


def _relayout_w_in(w_in, d):
    sizes = dict(qa=H_A * DH, ka=HKV_A * DH, va=HKV_A * DH, qi=H_IDX * D_IDX, ki=D_IDX, wi=H_IDX,
                 qb=H_B * DH, kb=HKV_B * DH, vb=HKV_B * DH, qc=H_C * DK_C, kc=H_C * DK_C, vc=H_C * DV_C,
                 gc=H_C * DV_C, gates=3 * d)
    off, parts = 0, {}
    for name in ("qa", "ka", "va", "qi", "ki", "wi", "qb", "kb", "vb", "qc", "kc", "vc", "gc", "gates"):
        parts[name] = w_in[:, off:off + sizes[name]]
        off += sizes[name]
    assert off == w_in.shape[1]
    main = jnp.concatenate([parts[n] for n in ("vc", "gc", "qa", "qi", "qb", "qc", "kc", "ka", "va", "kb", "vb",
                                               "gates")], axis=1).astype(_MXU)
    kiw = jnp.concatenate([parts["ki"], parts["wi"],
                           jnp.zeros((d, _LANES - D_IDX - H_IDX), w_in.dtype)], axis=1).astype(_MXU)
    return main, kiw


def _pad_cols(w, n):
    return jnp.pad(w, ((0, 0), (0, n - w.shape[1])))


def _mix_and_ffn(x, z, o_a, o_b, o_c, wts, ffn_in):
    m = _merge(o_a, o_b, o_c, z, wts["w_pa"], wts["w_pb"], wts["w_pc"])
    x = _matmul(m, wts["w_out"], jnp.float32, residual=x)
    h2 = _rmsnorm(x, wts["norm_ffn"], _MXU)
    a, conv_new = ffn_in(h2)
    x = _matmul(a, wts["ffn_down"], jnp.float32, residual=x, tk_target=3072)
    return x, conv_new


def _state_slices(z, kiw, n_batch, t):
    ka = z[:, _C_KA:_C_KA + HKV_A * DH].reshape(n_batch, t, HKV_A, DH)
    va = z[:, _C_VA:_C_VA + HKV_A * DH].reshape(n_batch, t, HKV_A, DH)
    ki = kiw[:, :D_IDX].reshape(n_batch, t, D_IDX)
    kb = z[:, _C_KB:_C_KB + HKV_B * DH].reshape(n_batch, t, HKV_B, DH)
    vb = z[:, _C_VB:_C_VB + HKV_B * DH].reshape(n_batch, t, HKV_B, DH)
    return ka, va, ki, kb, vb


def kernel(x_prompt, x_sample, cache_a_k, cache_a_v, cache_a_kidx, cache_b_k, cache_b_v, state_ret, state_conv,
           page_table, rel_bias, norm_mix, w_in, ret_gn, w_pa, w_pb, w_pc, w_out, norm_ffn, ffn_up, ffn_gate,
           conv_w, conv_b, ffn_down, norm_final):
    bp, tp, d = x_prompt.shape
    bs, ts, _ = x_sample.shape
    depth = w_in.shape[0]
    n_pool, ps = cache_a_k.shape[1], cache_a_k.shape[2]
    past = page_table.shape[1] * ps
    f = ffn_up.shape[2]
    fp = -(-f // 512) * 512
    tq = _TQ
    assert d % _LANES == 0 and tp % tq == 0

    ar = jnp.arange
    bias_p = _bias_table(rel_bias, (ar(tq)[None, :] - ar(2 * tq)[:, None] + tq).astype(jnp.int32))
    assert H_A == H_B and HKV_A == HKV_B
    rel_cached = (ar(ts)[:, None] + MOBA_BLOCK - ar(MOBA_BLOCK)[None, :]).astype(jnp.int32)
    rel_own = (ar(ts)[:, None] - ar(_LANES)[None, :]).astype(jnp.int32)
    bias_s = _bias_table(rel_bias, jnp.concatenate([jnp.repeat(rel_cached, HKV_A, axis=1), rel_own, rel_own], axis=1))
    cache_a_k = cache_a_k.reshape(depth, n_pool, ps * HKV_A, DH)
    cache_a_v = cache_a_v.reshape(depth, n_pool, ps * HKV_A, DH)
    cache_b_k = cache_b_k.reshape(depth, n_pool, ps * HKV_B, DH)
    cache_b_v = cache_b_v.reshape(depth, n_pool, ps * HKV_B, DH)
    cache_a_kidx = jnp.swapaxes(cache_a_kidx, 2, 3)

    xp = x_prompt.reshape(bp * tp, d)
    xs = x_sample.reshape(bs * ts, d)
    pos_p = jnp.arange(tp, dtype=jnp.int32)
    pos_s = past + jnp.arange(ts, dtype=jnp.int32)
    st_p, st_s, kv_p = [], [], None
    for l in range(depth):
        w_main, w_kiw = _relayout_w_in(w_in[l], d)
        wts = dict(w_pa=w_pa[l].astype(_MXU), w_pb=w_pb[l].astype(_MXU), w_pc=w_pc[l].astype(_MXU),
                   w_out=w_out[l].astype(_MXU), norm_ffn=norm_ffn[l],
                   ffn_down=jnp.pad(ffn_down[l], ((0, fp - f), (0, 0))).astype(_MXU))
        up = _pad_cols(ffn_up[l], fp).astype(_MXU)
        gate = _pad_cols(ffn_gate[l], fp).astype(_MXU)
        cw = _pad_cols(conv_w[l], fp)
        cb = _pad_cols(conv_b[l].reshape(1, f), fp)

        h = _rmsnorm(xp, norm_mix[l], _MXU)
        z, *kv_p = _in_proj(h, w_main, l, depth, kv_p)
        kiw = _matmul(h, w_kiw, jnp.float32)
        o_a = _dsa_prompt(z, kiw, bias_p, bp, tp)
        o_b = _moba_prompt(z, bias_p, bp, tp)
        o_c, ret_new = _retention(z, jnp.zeros((bp, H_C, DK_C, DV_C), jnp.float32), ret_gn[l], pos_p, bp, tp)
        xp, conv_new = _mix_and_ffn(xp, z, o_a, o_b, o_c, wts,
                                    lambda h2: _ffn_in_prompt(h2, up, gate, cw, cb, bp, tp))
        st_p.append((kiw[:, :D_IDX].reshape(bp, tp, D_IDX), ret_new, conv_new[:, :, :f]))

        h = _rmsnorm(xs, norm_mix[l], _MXU)
        z = _matmul(h, w_main, jnp.float32)
        kiw = _matmul(h, w_kiw, jnp.float32)
        o_a = _dsa_sample(z, kiw, cache_a_k, cache_a_v, cache_a_kidx, page_table, bias_s, l, bs, ts)
        o_b = _moba_sample(z, cache_b_k, cache_b_v, page_table, bias_s, l, bs, ts)
        o_c, ret_new = _retention(z, state_ret[l], ret_gn[l], pos_s, bs, ts)
        conv_prev = _pad_cols(state_conv[l].reshape(bs * (CONV_W - 1), f), fp).reshape(bs, CONV_W - 1, fp)
        xs, conv_new = _mix_and_ffn(xs, z, o_a, o_b, o_c, wts,
                                    lambda h2: _ffn_in_sample(h2, up, gate, cw, cb, conv_prev, bs, ts))
        st_s.append(_state_slices(z, kiw, bs, ts) + (ret_new, conv_new[:, :, :f]))

    y_prompt = _rmsnorm(xp, norm_final, jnp.float32).reshape(bp, tp, d)
    y_sample = _rmsnorm(xs, norm_final, jnp.float32).reshape(bs, ts, d)
    ka_p, va_p, kb_p, vb_p = kv_p
    ki_p, ret_p, conv_p = [jnp.stack(v) for v in zip(*st_p)]
    outs_s = [jnp.stack(v) for v in zip(*st_s)]
    return (y_prompt, y_sample,
            ka_p.reshape(depth, bp, tp, HKV_A, DH), va_p.reshape(depth, bp, tp, HKV_A, DH), ki_p,
            kb_p.reshape(depth, bp, tp, HKV_B, DH), vb_p.reshape(depth, bp, tp, HKV_B, DH), ret_p, conv_p, *outs_s)
```

```python
import functools
import math

import jax
import jax.numpy as jnp
from jax import lax
from jax.experimental import pallas as pl
from jax.experimental.pallas import tpu as pltpu

DH = 128
H_A, HKV_A = 8, 2
H_IDX, D_IDX = 16, 64
TOPK_IDX = 256
H_B, HKV_B = 8, 2
MOBA_BLOCK = 256
MOBA_TOPK = 3
H_C, DK_C, DV_C = 8, 128, 256
N_BUCKETS = 32
MAX_DIST = 128
CONV_W = 3
EPS = 1e-6
ROPE_BASE = 10000.0
RET_CHUNK = 128

_MXU = jnp.bfloat16
_NEG = -1e30
_TQ = 256
_LANES = 128
_VMEM_LIMIT = 56 * 1024 * 1024

_C_VC = 0
_C_GC = _C_VC + H_C * DV_C
_C_QA = _C_GC + H_C * DV_C
_C_QI = _C_QA + H_A * DH
_C_QB = _C_QI + H_IDX * D_IDX
_C_QC = _C_QB + H_B * DH
_C_KC = _C_QC + H_C * DK_C
_C_KA = _C_KC + H_C * DK_C
_C_VA = _C_KA + HKV_A * DH
_C_KB = _C_VA + HKV_A * DH
_C_VB = _C_KB + HKV_B * DH
_C_GATES = _C_VB + HKV_B * DH


def _params(sem):
    return pltpu.CompilerParams(dimension_semantics=sem, vmem_limit_bytes=_VMEM_LIMIT)


def _tile(n, target, mult=_LANES):
    best = None
    t = mult
    while t <= min(n, target):
        if n % t == 0:
            best = t
        t += mult
    return n if best is None else best


def _rmsnorm_body(x_ref, g_ref, o_ref):
    x = x_ref[...]
    y = x * lax.rsqrt(jnp.mean(x * x, axis=-1, keepdims=True) + EPS)
    o_ref[...] = (y * g_ref[...]).astype(o_ref.dtype)


def _rmsnorm(x, g, out_dtype):
    m, d = x.shape
    tm = _tile(m, 512, 8)
    return pl.pallas_call(
        _rmsnorm_body,
        grid=(m // tm,),
        in_specs=[pl.BlockSpec((tm, d), lambda i: (i, 0)), pl.BlockSpec((1, d), lambda i: (0, 0))],
        out_specs=pl.BlockSpec((tm, d), lambda i: (i, 0)),
        out_shape=jax.ShapeDtypeStruct((m, d), out_dtype),
        compiler_params=_params(("parallel",)),
        name="rmsnorm",
    )(x, g.reshape(1, d))


def _dot_w(x, w, w_t):
    if w_t:
        return lax.dot_general(x, w, (((1,), (1,)), ((), ())), preferred_element_type=jnp.float32)
    return jnp.dot(x, w, preferred_element_type=jnp.float32)


def _mm_body(*refs, nk, has_res, w_t):
    if has_res:
        x_ref, w_ref, r_ref, o_ref, acc_ref = refs
    else:
        x_ref, w_ref, o_ref, acc_ref = refs
    k = pl.program_id(2)

    @pl.when(k == 0)
    def _():
        acc_ref[...] = jnp.zeros_like(acc_ref)

    acc_ref[...] += _dot_w(x_ref[...], w_ref[...], w_t)

    @pl.when(k == nk - 1)
    def _():
        r = acc_ref[...]
        if has_res:
            r = r + r_ref[...]
        o_ref[...] = r.astype(o_ref.dtype)


def _mm_few_rows_body(x_ref, w_ref, o_ref):
    x = x_ref[...]
    x = jnp.concatenate([x, jnp.zeros((_LANES - x.shape[0], x.shape[1]), x.dtype)], axis=0)
    r_t = lax.dot_general(w_ref[...], x, (((1,), (1,)), ((), ())), preferred_element_type=jnp.float32)
    o_ref[...] = r_t.T[0:o_ref.shape[0]].astype(o_ref.dtype)


def _matmul_few_rows(x, w, layer, out_dtype, tn_target=1024):
    m, kdim = x.shape
    n = w.shape[1]
    tn = _tile(n, tn_target)
    assert m <= _LANES and m % 8 == 0
    return pl.pallas_call(
        _mm_few_rows_body,
        grid=(n // tn,),
        in_specs=[pl.BlockSpec((m, kdim), lambda j: (0, 0)), pl.BlockSpec((None, tn, kdim), lambda j: (layer, j, 0))],
        out_specs=pl.BlockSpec((m, tn), lambda j: (0, j)),
        out_shape=jax.ShapeDtypeStruct((m, n), out_dtype),
        compiler_params=_params(("parallel",)),
        name="matmul_few_rows",
    )(x, w)


def _matmul(x, w, layer, out_dtype, residual=None, w_t=False, tm_target=1024, tn_target=1024, tk_target=2048):
    m, kdim = x.shape
    n = w.shape[1] if w_t else w.shape[2]
    tm = _tile(m, tm_target, 8)
    tn = _tile(n, tn_target)
    tk = _tile(kdim, tk_target)
    nk = kdim // tk
    if w_t:
        w_spec = pl.BlockSpec((None, tn, tk), lambda i, j, k: (layer, j, k))
    else:
        w_spec = pl.BlockSpec((None, tk, tn), lambda i, j, k: (layer, k, j))
    in_specs = [pl.BlockSpec((tm, tk), lambda i, j, k: (i, k)), w_spec]
    args = [x, w]
    if residual is not None:
        in_specs.append(pl.BlockSpec((tm, tn), lambda i, j, k: (i, j)))
        args.append(residual)
    return pl.pallas_call(
        functools.partial(_mm_body, nk=nk, has_res=residual is not None, w_t=w_t),
        grid=(m // tm, n // tn, nk),
        in_specs=in_specs,
        out_specs=pl.BlockSpec((tm, tn), lambda i, j, k: (i, j)),
        out_shape=jax.ShapeDtypeStruct((m, n), out_dtype),
        scratch_shapes=[pltpu.VMEM((tm, tn), jnp.float32)],
        compiler_params=_params(("parallel", "parallel", "arbitrary")),
        name="matmul",
    )(*args)


def _in_proj_body(*refs, n_prev, j_state):
    x_ref, g_ref, w_ref, wk_ref = refs[:4]
    z_ref, kiw_ref, ka_ref, va_ref, kb_ref, vb_ref, h_ref = refs[4 + n_prev:]

    @pl.when(pl.program_id(1) == 0)
    def _():
        x = x_ref[...]
        y = x * lax.rsqrt(jnp.mean(x * x, axis=-1, keepdims=True) + EPS)
        h_ref[...] = (y * g_ref[...]).astype(h_ref.dtype)
        kiw_ref[...] = _dot_w(h_ref[...], wk_ref[...], True)

    r = _dot_w(h_ref[...], w_ref[...], True)
    z_ref[...] = r

    @pl.when(pl.program_id(1) == j_state)
    def _():
        ka_ref[...] = r[:, 0 * HKV_A * DH:1 * HKV_A * DH]
        va_ref[...] = r[:, 1 * HKV_A * DH:2 * HKV_A * DH]
        kb_ref[...] = r[:, 2 * HKV_A * DH:2 * HKV_A * DH + HKV_B * DH]
        vb_ref[...] = r[:, 2 * HKV_A * DH + HKV_B * DH:2 * HKV_A * DH + 2 * HKV_B * DH]


def _in_proj(x, g, w_main, w_kiw, layer, depth, prev_states):
    m, kdim = x.shape
    n = w_main.shape[1]
    tm = _tile(m, 1024, 8)
    tn = 2 * HKV_A * DH + 2 * HKV_B * DH
    assert n % tn == 0 and _C_KA % tn == 0 and _C_VB + HKV_B * DH == _C_KA + tn
    widths = (HKV_A * DH, HKV_A * DH, HKV_B * DH, HKV_B * DH)
    prev = [] if prev_states is None else list(prev_states)
    return pl.pallas_call(
        functools.partial(_in_proj_body, n_prev=len(prev), j_state=_C_KA // tn),
        grid=(m // tm, n // tn),
        in_specs=[pl.BlockSpec((tm, kdim), lambda i, j: (i, 0)),
                  pl.BlockSpec((1, kdim), lambda i, j: (0, 0)),
                  pl.BlockSpec((None, tn, kdim), lambda i, j: (layer, j, 0)),
                  pl.BlockSpec((None, _LANES, kdim), lambda i, j: (layer, 0, 0))]
        + [pl.BlockSpec(memory_space=pl.ANY)] * len(prev),
        out_specs=[pl.BlockSpec((tm, tn), lambda i, j: (i, j)), pl.BlockSpec((tm, _LANES), lambda i, j: (i, 0))]
        + [pl.BlockSpec((None, tm, w), lambda i, j: (layer, i, 0)) for w in widths],
        out_shape=[jax.ShapeDtypeStruct((m, n), jnp.float32), jax.ShapeDtypeStruct((m, _LANES), jnp.float32)]
        + [jax.ShapeDtypeStruct((depth, m, w), jnp.float32) for w in widths],
        scratch_shapes=[pltpu.VMEM((tm, kdim), _MXU)],
        input_output_aliases={4 + k: 2 + k for k in range(len(prev))},
        compiler_params=_params(("parallel", "arbitrary")),
        name="in_proj",
    )(x, g.reshape(1, kdim), w_main, w_kiw, *prev)


def _t5_bucket(rel):
    n = jnp.maximum(rel, 0)
    max_exact = N_BUCKETS // 2
    nf = jnp.maximum(n, 1).astype(jnp.float32)
    large = max_exact + (jnp.log(nf / max_exact) / math.log(MAX_DIST / max_exact)
                         * (N_BUCKETS - max_exact)).astype(jnp.int32)
    large = jnp.minimum(large, N_BUCKETS - 1)
    return jnp.where(n < max_exact, n, large)


def _bias_body(tab_ref, bucket_ref, o_ref):
    h = pl.program_id(0)
    bucket = bucket_ref[...]
    acc = jnp.zeros(bucket.shape, jnp.float32)
    for b in range(N_BUCKETS):
        acc = jnp.where(bucket == b, tab_ref[b, h], acc)
    o_ref[...] = acc


def _bias_table(rel_bias, rel):
    bucket = _t5_bucket(rel)
    r, c = rel.shape
    nh = rel_bias.shape[1]
    return pl.pallas_call(
        _bias_body,
        grid=(nh,),
        in_specs=[pl.BlockSpec(memory_space=pltpu.SMEM), pl.BlockSpec((r, c), lambda h: (0, 0))],
        out_specs=pl.BlockSpec((None, r, c), lambda h: (h, 0, 0)),
        out_shape=jax.ShapeDtypeStruct((nh, r, c), jnp.float32),
        compiler_params=_params(("parallel",)),
        name="bias_table",
    )(rel_bias, bucket)


def _sortable(x):
    b = lax.bitcast_convert_type(x, jnp.int32)
    return b ^ ((b >> 31) & jnp.int32(0x7FFFFFFF))


_KEY_NEG_INF = -2147483648 + 0x7FFFFF


def _stack_heads(q_ref, g, n_grp, scale):
    parts = [q_ref[:, (g * n_grp + hh) * DH:(g * n_grp + hh + 1) * DH] for hh in range(n_grp)]
    return (jnp.concatenate(parts, axis=0) * scale).astype(_MXU)


def _flash_update(m_ref, l_ref, acc_ref, g, logits, v):
    m_old = m_ref[g]
    m_new = jnp.maximum(m_old, jnp.max(logits, axis=1, keepdims=True))
    alpha = jnp.exp(m_old - m_new)
    p = jnp.exp(logits - m_new)
    l_ref[g] = alpha * l_ref[g] + jnp.sum(p, axis=1, keepdims=True)
    acc_ref[g] = alpha * acc_ref[g] + jnp.dot(p.astype(_MXU), v, preferred_element_type=jnp.float32)
    m_ref[g] = m_new


def _flash_init(m_ref, l_ref, acc_ref):
    m_ref[...] = jnp.full(m_ref.shape, _NEG, jnp.float32)
    l_ref[...] = jnp.zeros(l_ref.shape, jnp.float32)
    acc_ref[...] = jnp.zeros(acc_ref.shape, jnp.float32)


def _flash_store(o_ref, l_ref, acc_ref, n_kv, n_grp, rows):
    for g in range(n_kv):
        o = acc_ref[g] / l_ref[g]
        for hh in range(n_grp):
            h = g * n_grp + hh
            o_ref[:, h * DH:(h + 1) * DH] = o[hh * rows:(hh + 1) * rows].astype(o_ref.dtype)


def _flash_update_t(m_ref, l_ref, acc_ref, g, logits_t, v_t):
    m_old = m_ref[g]
    m_new = jnp.maximum(m_old, jnp.max(logits_t, axis=0, keepdims=True))
    alpha = jnp.exp(m_old - m_new)
    p = jnp.exp(logits_t - m_new)
    l_ref[g] = alpha * l_ref[g] + jnp.sum(p, axis=0, keepdims=True)
    acc_ref[g] = alpha * acc_ref[g] + jnp.dot(v_t, p.astype(_MXU), preferred_element_type=jnp.float32)
    m_ref[g] = m_new


def _flash_store_t(o_ref, l_ref, acc_ref, n_kv, n_grp, rows):
    for g in range(n_kv):
        o = acc_ref[g] / l_ref[g]
        for hh in range(n_grp):
            h = g * n_grp + hh
            o_ref[:, h * DH:(h + 1) * DH] = o[:, hh * rows:(hh + 1) * rows].T.astype(o_ref.dtype)


def _kth_largest_key(count_ge, n_sel, shape):
    int_min = jnp.int32(-2147483648)
    prefix = jnp.full(shape, int_min, jnp.int32)
    zero = jnp.zeros(shape, jnp.int32)
    prefix = jnp.where(count_ge(zero) >= n_sel, zero, prefix)

    def step(it, prefix):
        cand = prefix | (jnp.int32(1) << (30 - it))
        return jnp.where(count_ge(cand) >= n_sel, cand, prefix)

    return lax.fori_loop(0, 31, step, prefix)


def _first_index_cut(count_lt, need, shape, n_idx):
    nbits = max(1, int(n_idx).bit_length())
    cut = jnp.zeros(shape, jnp.int32)

    def step(it, cut):
        cand = cut + (jnp.int32(1) << (nbits - 1 - it))
        return jnp.where(count_lt(cand) < need, cand, cut)

    return lax.fori_loop(0, nbits, step, cut)


def _stage_kv(k_in_ref, v_in_ref, kb_ref, vt_ref, tq):
    for j in range(kb_ref.shape[0]):
        kb_ref[j] = k_in_ref[j * tq:(j + 1) * tq, :].astype(_MXU)
        vt_ref[j] = v_in_ref[j * tq:(j + 1) * tq, :].T.astype(_MXU)


def _attend_t(j, qs, kb_ref, vt_ref, m_ref, l_ref, acc_ref, extra_of_group):
    for g in range(len(qs)):
        logits_t = lax.dot_general(kb_ref[j, :, g * DH:(g + 1) * DH], qs[g], (((1,), (1,)), ((), ())),
                                   preferred_element_type=jnp.float32)
        _flash_update_t(m_ref, l_ref, acc_ref, g, logits_t + extra_of_group(g), vt_ref[j, g * DH:(g + 1) * DH, :])


def _dsa_p_body(qa_ref, qi_ref, kiwq_ref, kiw_ref, ka_ref, va_ref, bias_ref, o_ref,
                kx_ref, kb_ref, vt_ref, key_ref, neg_ref, m_ref, l_ref, acc_ref, *, n_sel, t):
    i = pl.program_id(1)
    tq = _TQ
    n_grp = H_A // HKV_A
    krow = lax.broadcasted_iota(jnp.int32, (tq, tq), 0)
    qcol = lax.broadcasted_iota(jnp.int32, (tq, tq), 1)

    @pl.when(i == 0)
    def _():
        _stage_kv(ka_ref, va_ref, kb_ref, vt_ref, tq)
        for j in range(kx_ref.shape[0]):
            kx_ref[j] = kiw_ref[j * tq:(j + 1) * tq, 0:D_IDX].astype(_MXU)

    qi = [qi_ref[:, h * D_IDX:(h + 1) * D_IDX].astype(_MXU) for h in range(H_IDX)]
    wt = kiwq_ref[...].T[D_IDX:D_IDX + H_IDX, :] * (D_IDX ** -0.5 * H_IDX ** -0.5)

    def score_tile(j, carry):
        kx = kx_ref[j]
        sc = jnp.zeros((tq, tq), jnp.float32)
        for h in range(H_IDX):
            s = lax.dot_general(kx, qi[h], (((1,), (1,)), ((), ())), preferred_element_type=jnp.float32)
            sc = sc + wt[h:h + 1, :] * jnp.maximum(s, 0.0)
        causal = jnp.logical_or(j < i, krow <= qcol)
        key_ref[j] = _sortable(jnp.where(causal, sc, -jnp.inf))
        return carry

    lax.fori_loop(0, i + 1, score_tile, 0)

    def count_tiles(pred):
        def body(j, acc):
            return acc + jnp.sum(pred(key_ref[j], j).astype(jnp.int32).reshape(tq // 8, 8, tq), axis=0)
        acc = lax.fori_loop(0, i + 1, body, jnp.zeros((8, tq), jnp.int32))
        return jnp.sum(acc, axis=0, keepdims=True)

    thr = _kth_largest_key(lambda c: count_tiles(lambda k, j: k >= c), n_sel, (1, tq))
    n_ge = count_tiles(lambda k, j: k >= thr)
    finite = thr > _KEY_NEG_INF
    tie = jnp.max(jnp.where(jnp.logical_and(finite, n_ge > n_sel), 1, 0)) > 0

    def write_neg(cut):
        def body(j, carry):
            k = key_ref[j]
            kpos = j * tq + krow
            keep = jnp.logical_or(k > thr, jnp.logical_and(k == thr, kpos <= cut))
            keep = jnp.logical_and(keep, jnp.logical_or(j < i, krow <= qcol))
            neg_ref[j] = jnp.where(keep, 0.0, _NEG)
            return carry
        lax.fori_loop(0, i + 1, body, 0)

    def with_ties():
        n_gt = count_tiles(lambda k, j: k > thr)
        need = n_sel - n_gt
        cut = _first_index_cut(
            lambda p: count_tiles(lambda k, j: jnp.logical_and(k == thr, j * tq + krow < p)),
            need, (1, tq), t)
        write_neg(jnp.where(finite, cut, jnp.int32(2 ** 30)))

    def without_ties():
        write_neg(jnp.full((1, tq), 2 ** 30, jnp.int32))

    lax.cond(tie, with_ties, without_ties)

    _flash_init(m_ref, l_ref, acc_ref)
    scale = DH ** -0.5
    qs = [_stack_heads(qa_ref, g, n_grp, scale) for g in range(HKV_A)]

    def attend(j, bias_of_head):
        neg = neg_ref[j]
        _attend_t(j, qs, kb_ref, vt_ref, m_ref, l_ref, acc_ref,
                  lambda g: jnp.concatenate([bias_of_head(g * n_grp + hh) + neg for hh in range(n_grp)], axis=1))

    def far(j, carry):
        attend(j, lambda h: bias_ref[h, 0:1, tq - 1:tq])
        return carry

    lax.fori_loop(0, jnp.maximum(i - 1, 0), far, 0)

    @pl.when(i >= 1)
    def _():
        attend(i - 1, lambda h: bias_ref[h, 0:tq, :])

    attend(i, lambda h: bias_ref[h, tq:2 * tq, :])
    _flash_store_t(o_ref, l_ref, acc_ref, HKV_A, n_grp, tq)


def _kv_scratch(nk, tq, n_kv, n_grp):
    return [
        pltpu.VMEM((nk, tq, n_kv * DH), _MXU),
        pltpu.VMEM((nk, n_kv * DH, tq), _MXU),
    ], [
        pltpu.VMEM((n_kv, 1, n_grp * tq), jnp.float32),
        pltpu.VMEM((n_kv, 1, n_grp * tq), jnp.float32),
        pltpu.VMEM((n_kv, DH, n_grp * tq), jnp.float32),
    ]


def _dsa_prompt(z, kiw, bias, n_batch, t):
    tq = _TQ
    nq = t // tq
    n_sel = min(TOPK_IDX, t // 4)
    assert n_sel <= tq and t % tq == 0
    n_grp = H_A // HKV_A
    kv_scratch, flash_scratch = _kv_scratch(nq, tq, HKV_A, n_grp)
    return pl.pallas_call(
        functools.partial(_dsa_p_body, n_sel=n_sel, t=t),
        grid=(n_batch, nq),
        in_specs=[
            pl.BlockSpec((tq, H_A * DH), lambda b, i: (b * nq + i, _C_QA // (H_A * DH))),
            pl.BlockSpec((tq, H_IDX * D_IDX), lambda b, i: (b * nq + i, _C_QI // (H_IDX * D_IDX))),
            pl.BlockSpec((tq, _LANES), lambda b, i: (b * nq + i, 0)),
            pl.BlockSpec((t, _LANES), lambda b, i: (b, 0)),
            pl.BlockSpec((t, HKV_A * DH), lambda b, i: (b, _C_KA // (HKV_A * DH))),
            pl.BlockSpec((t, HKV_A * DH), lambda b, i: (b, _C_VA // (HKV_A * DH))),
            pl.BlockSpec((H_A, 2 * tq, tq), lambda b, i: (0, 0, 0)),
        ],
        out_specs=pl.BlockSpec((tq, H_A * DH), lambda b, i: (b * nq + i, 0)),
        out_shape=jax.ShapeDtypeStruct((n_batch * t, H_A * DH), _MXU),
        scratch_shapes=[pltpu.VMEM((nq, tq, D_IDX), _MXU)] + kv_scratch + [
            pltpu.VMEM((nq, tq, tq), jnp.int32),
            pltpu.VMEM((nq, tq, tq), jnp.float32),
        ] + flash_scratch,
        compiler_params=_params(("parallel", "arbitrary")),
        name="dsa_prompt",
    )(z, z, kiw, kiw, z, z, bias)


def _topk_lanes(gate, n_cand, n_sel):
    lane = lax.broadcasted_iota(jnp.int32, gate.shape, 1)
    live = lane < n_cand
    chosen = jnp.zeros(gate.shape, jnp.bool_)
    big = jnp.int32(2 ** 30)
    for _ in range(n_sel):
        cand = jnp.logical_and(live, jnp.logical_not(chosen))
        best = jnp.max(jnp.where(cand, gate, -jnp.inf), axis=1, keepdims=True)
        first = jnp.min(jnp.where(jnp.logical_and(cand, gate == best), lane, big), axis=1, keepdims=True)
        chosen = jnp.logical_or(chosen, lane == first)
    return chosen


def _topk_rows(gate, n_cand, n_sel):
    row = lax.broadcasted_iota(jnp.int32, gate.shape, 0)
    live = row < n_cand
    chosen = jnp.zeros(gate.shape, jnp.bool_)
    big = jnp.int32(2 ** 30)
    for _ in range(n_sel):
        cand = jnp.logical_and(live, jnp.logical_not(chosen))
        best = jnp.max(jnp.where(cand, gate, -jnp.inf), axis=0, keepdims=True)
        first = jnp.min(jnp.where(jnp.logical_and(cand, gate == best), row, big), axis=0, keepdims=True)
        chosen = jnp.logical_or(chosen, row == first)
    return chosen


def _moba_p_body(qb_ref, k_in_ref, v_in_ref, bias_ref, o_ref, kmean_ref, kb_ref, vt_ref, allow_ref,
                 m_ref, l_ref, acc_ref, *, n_sel):
    i = pl.program_id(1)
    tq = _TQ
    nb = kb_ref.shape[0]
    n_grp = H_B // HKV_B
    krow = lax.broadcasted_iota(jnp.int32, (tq, tq), 0)
    qcol = lax.broadcasted_iota(jnp.int32, (tq, tq), 1)

    @pl.when(i == 0)
    def _():
        _stage_kv(k_in_ref, v_in_ref, kb_ref, vt_ref, tq)
        kmean_ref[...] = jnp.zeros(kmean_ref.shape, jnp.float32)
        for j in range(nb):
            kmean_ref[j:j + 1, :] = jnp.mean(k_in_ref[j * tq:(j + 1) * tq, :], axis=0, keepdims=True)

    for g in range(HKV_B):
        qg = jnp.concatenate([qb_ref[:, (g * n_grp + hh) * DH:(g * n_grp + hh + 1) * DH] for hh in range(n_grp)],
                             axis=0)
        gate_t = lax.dot_general(kmean_ref[:, g * DH:(g + 1) * DH], qg, (((1,), (1,)), ((), ())),
                                 preferred_element_type=jnp.float32, precision=lax.Precision.HIGHEST)
        allow_ref[g] = jnp.where(_topk_rows(gate_t, i, n_sel), 0.0, _NEG)

    _flash_init(m_ref, l_ref, acc_ref)
    scale = DH ** -0.5
    qs = [_stack_heads(qb_ref, g, n_grp, scale) for g in range(HKV_B)]
    causal_neg = jnp.where(krow <= qcol, 0.0, _NEG)

    def attend_past(j, bias_of_head):
        _attend_t(j, qs, kb_ref, vt_ref, m_ref, l_ref, acc_ref,
                  lambda g: jnp.concatenate([jnp.broadcast_to(bias_of_head(g * n_grp + hh), (tq, tq))
                                             for hh in range(n_grp)], axis=1) + allow_ref[g, pl.ds(j, 1), :])

    def far(j, carry):
        attend_past(j, lambda h: bias_ref[h, 0:1, tq - 1:tq])
        return carry

    lax.fori_loop(0, jnp.maximum(i - 1, 0), far, 0)

    @pl.when(i >= 1)
    def _():
        attend_past(i - 1, lambda h: bias_ref[h, 0:tq, :])

    _attend_t(i, qs, kb_ref, vt_ref, m_ref, l_ref, acc_ref,
              lambda g: jnp.concatenate([bias_ref[g * n_grp + hh, tq:2 * tq, :] + causal_neg
                                         for hh in range(n_grp)], axis=1))
    _flash_store_t(o_ref, l_ref, acc_ref, HKV_B, n_grp, tq)


def _moba_prompt(z, bias, n_batch, t):
    tq = _TQ
    assert t % tq == 0 and MOBA_BLOCK == tq
    nb = t // tq
    n_sel = min(MOBA_TOPK, (t - 1) // MOBA_BLOCK)
    n_grp = H_B // HKV_B
    nb_pad = -(-nb // 8) * 8
    kv_scratch, flash_scratch = _kv_scratch(nb, tq, HKV_B, n_grp)
    return pl.pallas_call(
        functools.partial(_moba_p_body, n_sel=n_sel),
        grid=(n_batch, nb),
        in_specs=[
            pl.BlockSpec((tq, H_B * DH), lambda b, i: (b * nb + i, _C_QB // (H_B * DH))),
            pl.BlockSpec((t, HKV_B * DH), lambda b, i: (b, _C_KB // (HKV_B * DH))),
            pl.BlockSpec((t, HKV_B * DH), lambda b, i: (b, _C_VB // (HKV_B * DH))),
            pl.BlockSpec((H_B, 2 * tq, tq), lambda b, i: (1, 0, 0)),
        ],
        out_specs=pl.BlockSpec((tq, H_B * DH), lambda b, i: (b * nb + i, 0)),
        out_shape=jax.ShapeDtypeStruct((n_batch * t, H_B * DH), _MXU),
        scratch_shapes=[pltpu.VMEM((nb_pad, HKV_B * DH), jnp.float32)] + kv_scratch
        + [pltpu.VMEM((HKV_B, nb_pad, n_grp * tq), jnp.float32)] + flash_scratch,
        compiler_params=_params(("parallel", "arbitrary")),
        name="moba_prompt",
    )(z, z, z, bias)


def _ret_body(q_ref, k_ref, v_ref, g_ref, cos_ref, sin_ref, intra_ref, cross_ref, tail_ref, decay_ref, gn_ref,
              s0_ref, o_ref, s_out_ref, s_ref):
    t = pl.program_id(1)

    @pl.when(t == 0)
    def _():
        s_ref[...] = s0_ref[...]

    cos = cos_ref[...]
    sin = sin_ref[...]
    for h in range(H_C):
        qh = q_ref[:, h * DK_C:(h + 1) * DK_C]
        kh = k_ref[:, h * DK_C:(h + 1) * DK_C]
        q = qh * cos + pltpu.roll(qh, DK_C // 2, axis=1) * sin
        k = (kh * cos + pltpu.roll(kh, DK_C // 2, axis=1) * sin) * (DK_C ** -0.5)
        v = v_ref[:, h * DV_C:(h + 1) * DV_C].astype(_MXU)
        s = s_ref[h]
        qm = q.astype(_MXU)
        att = lax.dot_general(qm, k.astype(_MXU), (((1,), (1,)), ((), ())),
                              preferred_element_type=jnp.float32) * intra_ref[h]
        o = (jnp.dot(att.astype(_MXU), v, preferred_element_type=jnp.float32)
             + jnp.dot(qm, s.astype(_MXU), preferred_element_type=jnp.float32) * cross_ref[h])
        kt = (k * tail_ref[h]).astype(_MXU)
        s_ref[h] = s * decay_ref[h] + lax.dot_general(kt, v, (((0,), (0,)), ((), ())),
                                                      preferred_element_type=jnp.float32)
        o = o * lax.rsqrt(jnp.mean(o * o, axis=-1, keepdims=True) + EPS)
        gate = g_ref[:, h * DV_C:(h + 1) * DV_C]
        o = o * gn_ref[:, h * DV_C:(h + 1) * DV_C] * (gate * jax.nn.sigmoid(gate))
        o_ref[:, h * DV_C:(h + 1) * DV_C] = o.astype(o_ref.dtype)

    @pl.when(t == pl.num_programs(1) - 1)
    def _():
        s_out_ref[...] = s_ref[...]


def _retention(z, s0, ret_gn, pos, n_batch, t):
    c = math.gcd(t, RET_CHUNK)
    nc = t // c
    half = DK_C // 2
    inv = ROPE_BASE ** (-jnp.arange(0, DK_C, 2, dtype=jnp.float32) / DK_C)
    ang = pos.astype(jnp.float32)[:, None] * inv[None, :]
    cos = jnp.concatenate([jnp.cos(ang), jnp.cos(ang)], axis=1)
    sin = jnp.concatenate([-jnp.sin(ang), jnp.sin(ang)], axis=1)
    assert cos.shape == (t, 2 * half)
    log_g = jnp.log1p(-jnp.exp2(-5.0 - jnp.arange(H_C, dtype=jnp.float32)))
    idx = jnp.arange(c, dtype=jnp.float32)
    diff = idx[:, None] - idx[None, :]
    intra = jnp.where(diff[None] >= 0, jnp.exp(jnp.maximum(diff, 0.0)[None] * log_g[:, None, None]), 0.0)
    cross = jnp.exp((idx[None, :] + 1.0) * log_g[:, None])[:, :, None]
    tail = jnp.exp((c - 1.0 - idx)[None, :] * log_g[:, None])[:, :, None]
    decay = jnp.exp(c * log_g)[:, None, None]
    wq = H_C * DK_C
    wv = H_C * DV_C
    return pl.pallas_call(
        _ret_body,
        grid=(n_batch, nc),
        in_specs=[
            pl.BlockSpec((c, wq), lambda b, i: (b * nc + i, _C_QC // wq)),
            pl.BlockSpec((c, wq), lambda b, i: (b * nc + i, _C_KC // wq)),
            pl.BlockSpec((c, wv), lambda b, i: (b * nc + i, _C_VC // wv)),
            pl.BlockSpec((c, wv), lambda b, i: (b * nc + i, _C_GC // wv)),
            pl.BlockSpec((c, DK_C), lambda b, i: (i, 0)),
            pl.BlockSpec((c, DK_C), lambda b, i: (i, 0)),
            pl.BlockSpec((H_C, c, c), lambda b, i: (0, 0, 0)),
            pl.BlockSpec((H_C, c, 1), lambda b, i: (0, 0, 0)),
            pl.BlockSpec((H_C, c, 1), lambda b, i: (0, 0, 0)),
            pl.BlockSpec((H_C, 1, 1), lambda b, i: (0, 0, 0)),
            pl.BlockSpec((1, wv), lambda b, i: (0, 0)),
            pl.BlockSpec((None, H_C, DK_C, DV_C), lambda b, i: (b, 0, 0, 0)),
        ],
        out_specs=[
            pl.BlockSpec((c, wv), lambda b, i: (b * nc + i, 0)),
            pl.BlockSpec((None, H_C, DK_C, DV_C), lambda b, i: (b, 0, 0, 0)),
        ],
        out_shape=[
            jax.ShapeDtypeStruct((n_batch * t, wv), _MXU),
            jax.ShapeDtypeStruct((n_batch, H_C, DK_C, DV_C), jnp.float32),
        ],
        scratch_shapes=[pltpu.VMEM((H_C, DK_C, DV_C), jnp.float32)],
        compiler_params=_params(("parallel", "arbitrary")),
        name="retention",
    )(z, z, z, z, cos, sin, intra, cross, tail, decay, ret_gn.reshape(1, wv), s0)


def _merge_body(oa_ref, ob_ref, oc_ref, wa_ref, wb_ref, wc_ref, ga_ref, gb_ref, gc_ref, o_ref):
    def term(o, w, g):
        return jax.nn.sigmoid(g[...]) * jnp.dot(o[...], w[...], preferred_element_type=jnp.float32)
    o_ref[...] = (term(oa_ref, wa_ref, ga_ref) + term(ob_ref, wb_ref, gb_ref)
                  + term(oc_ref, wc_ref, gc_ref)).astype(o_ref.dtype)


def _merge(o_a, o_b, o_c, z, w_pa, w_pb, w_pc, layer):
    m = o_a.shape[0]
    d = w_pa.shape[2]
    tm = _tile(m, 1024, 8)
    tn = _tile(d, 512)
    gate_blk = [(_C_GATES + k * d) // tn for k in range(3)]
    assert all((_C_GATES + k * d) % tn == 0 for k in range(3))
    row = lambda w: pl.BlockSpec((tm, w), lambda i, j: (i, 0))
    col = lambda w: pl.BlockSpec((None, w, tn), lambda i, j: (layer, 0, j))
    gate = lambda k: pl.BlockSpec((tm, tn), lambda i, j: (i, gate_blk[k] + j))
    return pl.pallas_call(
        _merge_body,
        grid=(m // tm, d // tn),
        in_specs=[row(o_a.shape[1]), row(o_b.shape[1]), row(o_c.shape[1]),
                  col(w_pa.shape[1]), col(w_pb.shape[1]), col(w_pc.shape[1]), gate(0), gate(1), gate(2)],
        out_specs=pl.BlockSpec((tm, tn), lambda i, j: (i, j)),
        out_shape=jax.ShapeDtypeStruct((m, d), _MXU),
        compiler_params=_params(("parallel", "parallel")),
        name="merge",
    )(o_a, o_b, o_c, w_pa, w_pb, w_pc, z, z, z)


def _conv_gate(u, u1, u2, g, cw_ref, cb_ref):
    c = cb_ref[...] + cw_ref[0:1, :] * u2 + cw_ref[1:2, :] * u1 + cw_ref[2:3, :] * u
    return 0.5 * c * (1.0 + lax.erf(c * (2.0 ** -0.5))) * g


def _ffn_p_body(h_ref, wu_ref, wg_ref, cw_ref, cb_ref, a_ref, cs_ref, tail_ref, *, seq):
    i = pl.program_id(1)
    tm = h_ref.shape[0]
    h = h_ref[...]
    u = jnp.dot(h, wu_ref[...], preferred_element_type=jnp.float32)
    g = jnp.dot(h, wg_ref[...], preferred_element_type=jnp.float32)

    @pl.when((i * tm) % seq == 0)
    def _():
        tail_ref[...] = jnp.zeros_like(tail_ref)

    prev = tail_ref[...]
    row = lax.broadcasted_iota(jnp.int32, (tm, 1), 0)
    u1 = jnp.where(row == 0, prev[7:8, :], pltpu.roll(u, 1, axis=0))
    u2 = jnp.where(row == 0, prev[6:7, :], jnp.where(row == 1, prev[7:8, :], pltpu.roll(u, 2, axis=0)))
    a_ref[...] = _conv_gate(u, u1, u2, g, cw_ref, cb_ref).astype(a_ref.dtype)
    tail_ref[...] = u[tm - 8:tm, :]

    @pl.when((i * tm + tm) % seq == 0)
    def _():
        cs_ref[...] = u[tm - (CONV_W - 1):tm, :]


def _ffn_in_prompt(h, w_up, w_gate, conv_w, conv_b, layer, n_batch, t):
    m, d = h.shape
    f = w_up.shape[2]
    tm = _tile(t, 1024, 8)
    tn = _tile(f, 512)
    per_seq = t // tm
    return pl.pallas_call(
        functools.partial(_ffn_p_body, seq=t),
        grid=(f // tn, m // tm),
        in_specs=[
            pl.BlockSpec((tm, d), lambda j, i: (i, 0)),
            pl.BlockSpec((None, d, tn), lambda j, i: (layer, 0, j)),
            pl.BlockSpec((None, d, tn), lambda j, i: (layer, 0, j)),
            pl.BlockSpec((None, CONV_W, tn), lambda j, i: (layer, 0, j)),
            pl.BlockSpec((None, 1, tn), lambda j, i: (layer, 0, j)),
        ],
        out_specs=[
            pl.BlockSpec((tm, tn), lambda j, i: (i, j)),
            pl.BlockSpec((None, CONV_W - 1, tn), lambda j, i: (i // per_seq, 0, j)),
        ],
        out_shape=[
            jax.ShapeDtypeStruct((m, f), _MXU),
            jax.ShapeDtypeStruct((n_batch, CONV_W - 1, f), jnp.float32),
        ],
        scratch_shapes=[pltpu.VMEM((8, tn), jnp.float32)],
        compiler_params=_params(("parallel", "arbitrary")),
        name="ffn_in_prompt",
    )(h, w_up, w_gate, conv_w, conv_b)


def _ffn_s_body(h_ref, wu_ref, wg_ref, cw_ref, cb_ref, p1_ref, p2_ref, a_ref, u_ref, *, seq):
    tm = h_ref.shape[0]
    h = h_ref[...]
    u = jnp.dot(h, wu_ref[...], preferred_element_type=jnp.float32)
    g = jnp.dot(h, wg_ref[...], preferred_element_type=jnp.float32)
    pos = lax.broadcasted_iota(jnp.int32, (tm, 1), 0) % seq
    u1 = jnp.where(pos >= 1, pltpu.roll(u, 1, axis=0), p1_ref[...])
    u2 = jnp.where(pos >= 2, pltpu.roll(u, 2, axis=0), p2_ref[...])
    a_ref[...] = _conv_gate(u, u1, u2, g, cw_ref, cb_ref).astype(a_ref.dtype)
    u_ref[...] = u


def _ffn_in_sample(h, w_up, w_gate, conv_w, conv_b, layer, conv_prev, n_batch, t):
    m, d = h.shape
    f = w_up.shape[2]
    assert t >= CONV_W - 1
    tn = _tile(f, 512)
    zeros = jnp.zeros((n_batch, t, f), jnp.float32)
    p1 = zeros.at[:, 0].set(conv_prev[:, 1]).reshape(m, f)
    p2 = zeros.at[:, 0].set(conv_prev[:, 0]).at[:, 1].set(conv_prev[:, 1]).reshape(m, f)
    full = lambda w: pl.BlockSpec((m, w), lambda j: (0, 0))
    colf = lambda r: pl.BlockSpec((r, tn), lambda j: (0, j))
    colw = lambda r: pl.BlockSpec((None, r, tn), lambda j: (layer, 0, j))
    a, u = pl.pallas_call(
        functools.partial(_ffn_s_body, seq=t),
        grid=(f // tn,),
        in_specs=[full(d), colw(d), colw(d), colw(CONV_W), colw(1), colf(m), colf(m)],
        out_specs=[colf(m), colf(m)],
        out_shape=[jax.ShapeDtypeStruct((m, f), _MXU), jax.ShapeDtypeStruct((m, f), jnp.float32)],
        compiler_params=_params(("parallel",)),
        name="ffn_in_sample",
    )(h, w_up, w_gate, conv_w, conv_b, p1, p2)
    return a, u.reshape(n_batch, t, f)[:, t - (CONV_W - 1):]


def _page_specs(n_pages_per_step, layer, page_shape):
    zeros = (0,) * len(page_shape)
    return [pl.BlockSpec((None, None) + tuple(page_shape),
                         functools.partial(lambda b, s, pt, g: (layer, pt[b, s * n_pages_per_step + g]) + zeros, g=g))
            for g in range(n_pages_per_step)]


def _dsa_s_score_body(pt_ref, qi_ref, kiw_ref, *refs, n_pg):
    pages = refs[:n_pg]
    o_ref = refs[n_pg]
    ts = qi_ref.shape[0]
    ps = pages[0].shape[1]
    qst = jnp.concatenate([qi_ref[:, h * D_IDX:(h + 1) * D_IDX] for h in range(H_IDX)], axis=0).astype(_MXU)
    wi = kiw_ref[:, D_IDX:D_IDX + H_IDX] * (D_IDX ** -0.5 * H_IDX ** -0.5)
    for g in range(n_pg):
        s = jnp.dot(qst, pages[g][...].astype(_MXU), preferred_element_type=jnp.float32)
        sc = jnp.zeros((ts, ps), jnp.float32)
        for h in range(H_IDX):
            sc = sc + wi[:, h:h + 1] * jnp.maximum(s[h * ts:(h + 1) * ts], 0.0)
        o_ref[:, g * ps:(g + 1) * ps] = sc


def _dsa_s_select_body(sc_ref, qi_ref, kiw_ref, o_ref, on_ref, key_ref, *, n_sel, n_kv, ts):
    rows = qi_ref.shape[0]
    lp = sc_ref.shape[1]
    qi = qi_ref[...]
    kx = kiw_ref[:, 0:D_IDX].astype(_MXU)
    kx = jnp.concatenate([kx, jnp.zeros((_LANES - rows, D_IDX), _MXU)], axis=0)
    wi = kiw_ref[:, D_IDX:D_IDX + H_IDX] * (D_IDX ** -0.5 * H_IDX ** -0.5)
    sc = jnp.zeros((rows, _LANES), jnp.float32)
    for h in range(H_IDX):
        s = lax.dot_general(qi[:, h * D_IDX:(h + 1) * D_IDX].astype(_MXU), kx, (((1,), (1,)), ((), ())),
                            preferred_element_type=jnp.float32)
        sc = sc + wi[:, h:h + 1] * jnp.maximum(s, 0.0)
    sc = jnp.concatenate([pltpu.roll(sc[b * ts:(b + 1) * ts], (_LANES - b * ts) % _LANES, axis=1)
                          for b in range(rows // ts)], axis=0)
    r = lax.broadcasted_iota(jnp.int32, (rows, _LANES), 0) % ts
    c = lax.broadcasted_iota(jnp.int32, (rows, _LANES), 1)
    key_ref[:, 0:lp] = _sortable(sc_ref[...])
    key_ref[:, lp:lp + _LANES] = _sortable(jnp.where(c <= r, sc, -jnp.inf))

    width = lp + _LANES
    pos = lambda: lax.broadcasted_iota(jnp.int32, (rows, width), 1)
    count = lambda m: jnp.sum(m.astype(jnp.int32), axis=1, keepdims=True)
    thr = _kth_largest_key(lambda t: count(key_ref[...] >= t), n_sel, (rows, 1))
    finite = thr > _KEY_NEG_INF
    n_ge = count(key_ref[...] >= thr)
    tie = jnp.max(jnp.where(jnp.logical_and(finite, n_ge > n_sel), 1, 0)) > 0

    def with_ties():
        need = n_sel - count(key_ref[...] > thr)
        cut = _first_index_cut(lambda p: count(jnp.logical_and(key_ref[...] == thr, pos() < p)), need, (rows, 1),
                               width)
        return jnp.where(finite, cut, jnp.int32(2 ** 30))

    cut = lax.cond(tie, with_ties, lambda: jnp.full((rows, 1), 2 ** 30, jnp.int32))
    key = key_ref[...]
    keep = jnp.logical_or(key > thr, jnp.logical_and(key == thr, pos() <= cut))
    own = keep[:, lp:width]
    on_ref[...] = jnp.where(jnp.logical_and(own, c <= r), 0.0, _NEG)
    keep_f = jnp.where(keep, 1.0, 0.0).astype(_MXU)
    spread = (lax.broadcasted_iota(jnp.int32, (_LANES, n_kv * _LANES), 1) // n_kv
              == lax.broadcasted_iota(jnp.int32, (_LANES, n_kv * _LANES), 0))
    spread = jnp.where(spread, 1.0, 0.0).astype(_MXU)
    for cc in range(lp // _LANES):
        dup = jnp.dot(keep_f[:, cc * _LANES:(cc + 1) * _LANES], spread, preferred_element_type=jnp.float32)
        o_ref[:, cc * n_kv * _LANES:(cc + 1) * n_kv * _LANES] = jnp.where(dup > 0.5, 0.0, _NEG)


def _head_rows(pieces):
    return jnp.concatenate(pieces, axis=0)


def _parity_neg(n_heads, n_kv, ts, width):
    row_kv = lax.broadcasted_iota(jnp.int32, (n_heads * ts, width), 0) // ((n_heads // n_kv) * ts)
    col_kv = lax.broadcasted_iota(jnp.int32, (n_heads * ts, width), 1) % n_kv
    return jnp.where(row_kv == col_kv, 0.0, _NEG)


def _own_block(q_of_group, kn_ref, vn_ref, extra_of_head, n_kv, n_grp, ts):
    pad = jnp.zeros((_LANES - ts, DH), _MXU)
    ms, ls, os_ = [], [], []
    for g in range(n_kv):
        kt = jnp.concatenate([kn_ref[:, g * DH:(g + 1) * DH].astype(_MXU), pad], axis=0)
        vt = jnp.concatenate([vn_ref[:, g * DH:(g + 1) * DH].astype(_MXU), pad], axis=0)
        logits = lax.dot_general(q_of_group(g), kt, (((1,), (1,)), ((), ())), preferred_element_type=jnp.float32)
        logits = logits + _head_rows([extra_of_head(g * n_grp + hh) for hh in range(n_grp)])
        m = jnp.max(logits, axis=1, keepdims=True)
        p = jnp.exp(logits - m)
        ms.append(m)
        ls.append(jnp.sum(p, axis=1, keepdims=True))
        os_.append(jnp.dot(p.astype(_MXU), vt, preferred_element_type=jnp.float32))
    return _head_rows(ms), _head_rows(ls), _head_rows(os_)


def _dsa_s_attn_body(pt_ref, qa_ref, kn_ref, vn_ref, negn_ref, neg_ref, bias_ref, *refs, n_pg):
    kpages = refs[:n_pg]
    vpages = refs[n_pg:2 * n_pg]
    o_ref, m_ref, l_ref, acc_ref = refs[2 * n_pg:]
    s = pl.program_id(1)
    n_steps = pl.num_programs(1)
    ts = qa_ref.shape[0]
    rows_pg = kpages[0].shape[0]
    n_grp = H_A // HKV_A
    scale = DH ** -0.5
    qs = [_stack_heads(qa_ref, g, n_grp, scale) for g in range(HKV_A)]
    q_all = _head_rows(qs)
    wb = bias_ref.shape[2]
    own_w = HKV_A * _LANES

    @pl.when(s == 0)
    def _():
        m, l, o = _own_block(lambda g: qs[g], kn_ref, vn_ref,
                             lambda h: bias_ref[h, :, wb - own_w:wb - own_w + _LANES] + negn_ref[...],
                             HKV_A, n_grp, ts)
        m_ref[0], l_ref[0], acc_ref[0] = m, l, o

    last = s == n_steps - 1
    w = n_pg * rows_pg
    kt = jnp.concatenate([kpages[p][...] for p in range(n_pg)], axis=0).astype(_MXU)
    vt = jnp.concatenate([vpages[p][...] for p in range(n_pg)], axis=0).astype(_MXU)
    logits = lax.dot_general(q_all, kt, (((1,), (1,)), ((), ())), preferred_element_type=jnp.float32)
    neg = neg_ref[...]

    def extra_of_head(h):
        far = bias_ref[h, :, 0:1]
        base = neg + far
        near = jnp.where(last, bias_ref[h, :, wb - own_w - rows_pg:wb - own_w] - far, 0.0)
        return jnp.concatenate([base[:, 0:w - rows_pg], base[:, w - rows_pg:w] + near], axis=1)

    extra = _head_rows([extra_of_head(h) for h in range(H_A)]) + _parity_neg(H_A, HKV_A, ts, w)
    _flash_update(m_ref, l_ref, acc_ref, 0, logits + extra, vt)

    @pl.when(last)
    def _():
        o = acc_ref[0] / l_ref[0]
        for h in range(H_A):
            o_ref[:, h * DH:(h + 1) * DH] = o[h * ts:(h + 1) * ts].astype(o_ref.dtype)


def _dsa_sample(zs, kiws, cache_k, cache_v, cache_kidx, page_table, bias, layer, n_batch, ts):
    n_pages = page_table.shape[1]
    ps = cache_kidx.shape[3]
    past = n_pages * ps
    assert ps == _LANES and ts <= _LANES and ts % 8 == 0 and cache_k.shape[2] == ps * HKV_A
    n_sel = min(TOPK_IDX, (past + ts) // 4)
    n_grp = H_A // HKV_A

    gi = _tile(n_pages, 16, 1)
    scores = pl.pallas_call(
        functools.partial(_dsa_s_score_body, n_pg=gi),
        grid_spec=pltpu.PrefetchScalarGridSpec(
            num_scalar_prefetch=1,
            grid=(n_batch, n_pages // gi),
            in_specs=[pl.BlockSpec((ts, H_IDX * D_IDX), lambda b, s, pt: (b, _C_QI // (H_IDX * D_IDX))),
                      pl.BlockSpec((ts, _LANES), lambda b, s, pt: (b, 0))]
            + _page_specs(gi, layer, (D_IDX, ps)),
            out_specs=pl.BlockSpec((None, ts, gi * ps), lambda b, s, pt: (b, 0, s)),
        ),
        out_shape=jax.ShapeDtypeStruct((n_batch, ts, past), jnp.float32),
        compiler_params=_params(("parallel", "arbitrary")),
        name="dsa_sample_scores",
    )(page_table, zs, kiws, *([cache_kidx] * gi))

    rows = ts * _tile(n_batch, max(1, 32 // ts), 1)
    assert rows <= _LANES
    neg, neg_own = pl.pallas_call(
        functools.partial(_dsa_s_select_body, n_sel=n_sel, n_kv=HKV_A, ts=ts),
        grid=(n_batch * ts // rows,),
        in_specs=[pl.BlockSpec((rows, past), lambda i: (i, 0)),
                  pl.BlockSpec((rows, H_IDX * D_IDX), lambda i: (i, _C_QI // (H_IDX * D_IDX))),
                  pl.BlockSpec((rows, _LANES), lambda i: (i, 0))],
        out_specs=[pl.BlockSpec((rows, HKV_A * past), lambda i: (i, 0)),
                   pl.BlockSpec((rows, _LANES), lambda i: (i, 0))],
        out_shape=[jax.ShapeDtypeStruct((n_batch * ts, HKV_A * past), jnp.float32),
                   jax.ShapeDtypeStruct((n_batch * ts, _LANES), jnp.float32)],
        scratch_shapes=[pltpu.VMEM((rows, past + _LANES), jnp.int32)],
        compiler_params=_params(("parallel",)),
        name="dsa_sample_select",
    )(scores.reshape(n_batch * ts, past), zs, kiws)

    ga = _tile(n_pages, 8, 1)
    wkv = HKV_A * DH
    rows_pg = ps * HKV_A
    return pl.pallas_call(
        functools.partial(_dsa_s_attn_body, n_pg=ga),
        grid_spec=pltpu.PrefetchScalarGridSpec(
            num_scalar_prefetch=1,
            grid=(n_batch, n_pages // ga),
            in_specs=[pl.BlockSpec((ts, H_A * DH), lambda b, s, pt: (b, _C_QA // (H_A * DH))),
                      pl.BlockSpec((ts, wkv), lambda b, s, pt: (b, _C_KA // wkv)),
                      pl.BlockSpec((ts, wkv), lambda b, s, pt: (b, _C_VA // wkv)),
                      pl.BlockSpec((ts, _LANES), lambda b, s, pt: (b, 0)),
                      pl.BlockSpec((ts, ga * rows_pg), lambda b, s, pt: (b, s)),
                      pl.BlockSpec((H_A, ts, bias.shape[2]), lambda b, s, pt: (0, 0, 0))]
            + _page_specs(ga, layer, (rows_pg, DH)) + _page_specs(ga, layer, (rows_pg, DH)),
            out_specs=pl.BlockSpec((ts, H_A * DH), lambda b, s, pt: (b, 0)),
            scratch_shapes=[pltpu.VMEM((1, H_A * ts, 1), jnp.float32),
                            pltpu.VMEM((1, H_A * ts, 1), jnp.float32),
                            pltpu.VMEM((1, H_A * ts, DH), jnp.float32)],
        ),
        out_shape=jax.ShapeDtypeStruct((n_batch * ts, H_A * DH), _MXU),
        compiler_params=_params(("parallel", "arbitrary")),
        name="dsa_sample_attn",
    )(page_table, zs, zs, zs, neg_own, neg, bias, *([cache_k] * ga), *([cache_v] * ga))


def _moba_s_body(pt_ref, qb_ref, kn_ref, vn_ref, bias_ref, *refs, n_pg, n_sel, n_blocks):
    kpages = refs[:n_pg]
    vpages = refs[n_pg:2 * n_pg]
    o_ref, gate_ref, mb_ref, lb_ref, ob_ref = refs[2 * n_pg:]
    s = pl.program_id(1)
    n_steps = pl.num_programs(1)
    ts = qb_ref.shape[0]
    rows_pg = kpages[0].shape[0]
    n_grp = H_B // HKV_B
    rows = H_B * ts
    blk = MOBA_BLOCK * HKV_B
    bps = n_pg * rows_pg // blk
    scale = DH ** -0.5
    wb = bias_ref.shape[2]
    own_w = HKV_B * _LANES
    qf = [jnp.concatenate([qb_ref[:, (g * n_grp + hh) * DH:(g * n_grp + hh + 1) * DH] for hh in range(n_grp)], axis=0)
          for g in range(HKV_B)]
    qs = [(q * scale).astype(_MXU) for q in qf]
    q_all = _head_rows(qs)
    lane = lax.broadcasted_iota(jnp.int32, (rows, gate_ref.shape[1]), 1)
    last = s == n_steps - 1

    @pl.when(s == 0)
    def _():
        gate_ref[...] = jnp.zeros(gate_ref.shape, jnp.float32)
        mb_ref[...] = jnp.full(mb_ref.shape, _NEG, jnp.float32)
        lb_ref[...] = jnp.zeros(lb_ref.shape, jnp.float32)

    kf = jnp.concatenate([kpages[p][...] for p in range(n_pg)], axis=0)
    vt = jnp.concatenate([vpages[p][...] for p in range(n_pg)], axis=0).astype(_MXU)
    logits = lax.dot_general(q_all, kf.astype(_MXU), (((1,), (1,)), ((), ())), preferred_element_type=jnp.float32)
    far = _head_rows([bias_ref[h, :, 0:1] for h in range(H_B)])
    near = _head_rows([bias_ref[h, :, wb - own_w - blk:wb - own_w] for h in range(H_B)])
    parity = _parity_neg(H_B, HKV_B, ts, blk)
    sub_kv = lax.broadcasted_iota(jnp.int32, (8, DH), 0) % HKV_B
    gates, ms, ls, ps_ = gate_ref[...], mb_ref[...], lb_ref[...], []
    for jb in range(bps):
        seg = logits[:, jb * blk:(jb + 1) * blk] + (far + parity)
        if jb == bps - 1:
            seg = seg + jnp.where(last, near - far, 0.0)
        ksum = jnp.sum(kf[jb * blk:(jb + 1) * blk].reshape(blk // 8, 8, DH), axis=0)
        gate = _head_rows([
            jnp.sum(qf[g] * (jnp.sum(jnp.where(sub_kv == g, ksum, 0.0), axis=0, keepdims=True) / MOBA_BLOCK),
                    axis=1, keepdims=True) for g in range(HKV_B)])
        mj = jnp.max(seg, axis=1, keepdims=True)
        p = jnp.exp(seg - mj)
        lj = jnp.sum(p, axis=1, keepdims=True)
        here = lane == s * bps + jb
        gates = jnp.where(here, gate, gates)
        ms = jnp.where(here, mj, ms)
        ls = jnp.where(here, lj, ls)
        ps_.append(jnp.concatenate([p if k == jb else jnp.zeros_like(p) for k in range(bps)], axis=1))
    gate_ref[...], mb_ref[...], lb_ref[...] = gates, ms, ls
    o_blocks = jnp.dot(jnp.concatenate(ps_, axis=0).astype(_MXU), vt, preferred_element_type=jnp.float32)
    for jb in range(bps):
        ob_ref[s * bps + jb] = o_blocks[jb * rows:(jb + 1) * rows]

    @pl.when(last)
    def _():
        r = lax.broadcasted_iota(jnp.int32, (ts, _LANES), 0)
        c = lax.broadcasted_iota(jnp.int32, (ts, _LANES), 1)
        causal_neg = jnp.where(c <= r, 0.0, _NEG)
        m_own, l_own, o_own = _own_block(lambda g: qs[g], kn_ref, vn_ref,
                                         lambda h: bias_ref[h, :, wb - own_w:wb - own_w + _LANES] + causal_neg,
                                         HKV_B, n_grp, ts)
        chosen = _topk_lanes(gate_ref[...], n_blocks, n_sel)
        mb = jnp.where(chosen, mb_ref[...], _NEG)
        m_all = jnp.maximum(m_own, jnp.max(mb, axis=1, keepdims=True))
        w = jnp.where(chosen, jnp.exp(mb - m_all), 0.0)
        w_own = jnp.exp(m_own - m_all)
        den = w_own * l_own + jnp.sum(w * lb_ref[...], axis=1, keepdims=True)

        def add_block(j, acc):
            wj = jnp.sum(jnp.where(lane == j, w, 0.0), axis=1, keepdims=True)
            return acc + wj * ob_ref[j]

        num = lax.fori_loop(0, n_blocks, add_block, w_own * o_own)
        o = num / den
        for h in range(H_B):
            o_ref[:, h * DH:(h + 1) * DH] = o[h * ts:(h + 1) * ts].astype(o_ref.dtype)


def _moba_sample(zs, cache_k, cache_v, page_table, bias, layer, n_batch, ts):
    n_pages = page_table.shape[1]
    rows_pg = cache_k.shape[2]
    ps = rows_pg // HKV_B
    past = n_pages * ps
    assert past % MOBA_BLOCK == 0 and MOBA_BLOCK % ps == 0 and ts <= _LANES and ts % 8 == 0
    n_blocks = past // MOBA_BLOCK
    n_sel = min(MOBA_TOPK, (past + ts - 1) // MOBA_BLOCK)
    ppb = MOBA_BLOCK // ps
    g_pg = ppb * _tile(n_blocks, 4, 1)
    lane_w = -(-n_blocks // _LANES) * _LANES
    wkv = HKV_B * DH
    return pl.pallas_call(
        functools.partial(_moba_s_body, n_pg=g_pg, n_sel=n_sel, n_blocks=n_blocks),
        grid_spec=pltpu.PrefetchScalarGridSpec(
            num_scalar_prefetch=1,
            grid=(n_batch, n_pages // g_pg),
            in_specs=[pl.BlockSpec((ts, H_B * DH), lambda b, s, pt: (b, _C_QB // (H_B * DH))),
                      pl.BlockSpec((ts, wkv), lambda b, s, pt: (b, _C_KB // wkv)),
                      pl.BlockSpec((ts, wkv), lambda b, s, pt: (b, _C_VB // wkv)),
                      pl.BlockSpec((H_B, ts, bias.shape[2]), lambda b, s, pt: (1, 0, 0))]
            + _page_specs(g_pg, layer, (rows_pg, DH)) + _page_specs(g_pg, layer, (rows_pg, DH)),
            out_specs=pl.BlockSpec((ts, H_B * DH), lambda b, s, pt: (b, 0)),
            scratch_shapes=[pltpu.VMEM((H_B * ts, lane_w), jnp.float32),
                            pltpu.VMEM((H_B * ts, lane_w), jnp.float32),
                            pltpu.VMEM((H_B * ts, lane_w), jnp.float32),
                            pltpu.VMEM((n_blocks, H_B * ts, DH), jnp.float32)],
        ),
        out_shape=jax.ShapeDtypeStruct((n_batch * ts, H_B * DH), _MXU),
        compiler_params=_params(("parallel", "arbitrary")),
        name="moba_sample",
    )(page_table, zs, zs, zs, bias, *([cache_k] * g_pg), *([cache_v] * g_pg))


def _relayout_w_in(w_in, d):
    w_t = jnp.swapaxes(w_in, 1, 2)
    sizes = dict(qa=H_A * DH, ka=HKV_A * DH, va=HKV_A * DH, qi=H_IDX * D_IDX, ki=D_IDX, wi=H_IDX,
                 qb=H_B * DH, kb=HKV_B * DH, vb=HKV_B * DH, qc=H_C * DK_C, kc=H_C * DK_C, vc=H_C * DV_C,
                 gc=H_C * DV_C, gates=3 * d)
    off, parts = 0, {}
    for name in ("qa", "ka", "va", "qi", "ki", "wi", "qb", "kb", "vb", "qc", "kc", "vc", "gc", "gates"):
        parts[name] = w_t[:, off:off + sizes[name]]
        off += sizes[name]
    assert off == w_t.shape[1]
    main = jnp.concatenate([parts[n] for n in ("vc", "gc", "qa", "qi", "qb", "qc", "kc", "ka", "va", "kb", "vb",
                                               "gates")], axis=1).astype(_MXU)
    kiw = jnp.concatenate([parts["ki"], parts["wi"],
                           jnp.zeros((w_t.shape[0], _LANES - D_IDX - H_IDX, d), w_in.dtype)], axis=1).astype(_MXU)
    return main, kiw


def _pad_last(w, n):
    return jnp.pad(w, [(0, 0)] * (w.ndim - 1) + [(0, n - w.shape[-1])])


def _mix_and_ffn(x, z, o_a, o_b, o_c, wts, layer, ffn_in):
    m = _merge(o_a, o_b, o_c, z, wts["w_pa"], wts["w_pb"], wts["w_pc"], layer)
    x = _matmul(m, wts["w_out"], layer, jnp.float32, residual=x)
    h2 = _rmsnorm(x, wts["norm_ffn"][layer], _MXU)
    a, conv_new = ffn_in(h2)
    x = _matmul(a, wts["ffn_down"], layer, jnp.float32, residual=x, tk_target=3072)
    return x, conv_new


def _state_slices(z, kiw, n_batch, t):
    ka = z[:, _C_KA:_C_KA + HKV_A * DH].reshape(n_batch, t, HKV_A, DH)
    va = z[:, _C_VA:_C_VA + HKV_A * DH].reshape(n_batch, t, HKV_A, DH)
    ki = kiw[:, :D_IDX].reshape(n_batch, t, D_IDX)
    kb = z[:, _C_KB:_C_KB + HKV_B * DH].reshape(n_batch, t, HKV_B, DH)
    vb = z[:, _C_VB:_C_VB + HKV_B * DH].reshape(n_batch, t, HKV_B, DH)
    return ka, va, ki, kb, vb


def kernel(x_prompt, x_sample, cache_a_k, cache_a_v, cache_a_kidx, cache_b_k, cache_b_v, state_ret, state_conv,
           page_table, rel_bias, norm_mix, w_in, ret_gn, w_pa, w_pb, w_pc, w_out, norm_ffn, ffn_up, ffn_gate,
           conv_w, conv_b, ffn_down, norm_final):
    bp, tp, d = x_prompt.shape
    bs, ts, _ = x_sample.shape
    depth = w_in.shape[0]
    n_pool, ps = cache_a_k.shape[1], cache_a_k.shape[2]
    past = page_table.shape[1] * ps
    f = ffn_up.shape[2]
    fp = -(-f // 512) * 512
    tq = _TQ
    assert d % _LANES == 0 and tp % tq == 0

    ar = jnp.arange
    bias_p = _bias_table(rel_bias, (ar(tq)[None, :] - ar(2 * tq)[:, None] + tq).astype(jnp.int32))
    assert H_A == H_B and HKV_A == HKV_B
    rel_cached = (ar(ts)[:, None] + MOBA_BLOCK - ar(MOBA_BLOCK)[None, :]).astype(jnp.int32)
    rel_own = (ar(ts)[:, None] - ar(_LANES)[None, :]).astype(jnp.int32)
    bias_s = _bias_table(rel_bias, jnp.concatenate([jnp.repeat(rel_cached, HKV_A, axis=1), rel_own, rel_own], axis=1))
    cache_a_k = cache_a_k.reshape(depth, n_pool, ps * HKV_A, DH)
    cache_a_v = cache_a_v.reshape(depth, n_pool, ps * HKV_A, DH)
    cache_b_k = cache_b_k.reshape(depth, n_pool, ps * HKV_B, DH)
    cache_b_v = cache_b_v.reshape(depth, n_pool, ps * HKV_B, DH)
    cache_a_kidx = jnp.swapaxes(cache_a_kidx, 2, 3)

    xp = x_prompt.reshape(bp * tp, d)
    xs = x_sample.reshape(bs * ts, d)
    pos_p = jnp.arange(tp, dtype=jnp.int32)
    pos_s = past + jnp.arange(ts, dtype=jnp.int32)
    w_main, w_kiw = _relayout_w_in(w_in, d)
    wts = dict(w_pa=w_pa.astype(_MXU), w_pb=w_pb.astype(_MXU), w_pc=w_pc.astype(_MXU), w_out=w_out.astype(_MXU),
               norm_ffn=norm_ffn, ffn_down=jnp.pad(ffn_down, ((0, 0), (0, fp - f), (0, 0))).astype(_MXU))
    up = _pad_last(ffn_up, fp).astype(_MXU)
    gate = _pad_last(ffn_gate, fp).astype(_MXU)
    cw = _pad_last(conv_w, fp)
    cb = _pad_last(conv_b.reshape(depth, 1, f), fp)
    conv_prev_all = _pad_last(state_conv, fp)

    st_p, st_s, kv_p = [], [], None
    for l in range(depth):
        z, kiw, *kv_p = _in_proj(xp, norm_mix[l], w_main, w_kiw, l, depth, kv_p)
        o_a = _dsa_prompt(z, kiw, bias_p, bp, tp)
        o_b = _moba_prompt(z, bias_p, bp, tp)
        o_c, ret_new = _retention(z, jnp.zeros((bp, H_C, DK_C, DV_C), jnp.float32), ret_gn[l], pos_p, bp, tp)
        xp, conv_new = _mix_and_ffn(xp, z, o_a, o_b, o_c, wts, l,
                                    lambda h2: _ffn_in_prompt(h2, up, gate, cw, cb, l, bp, tp))
        st_p.append((kiw[:, :D_IDX].reshape(bp, tp, D_IDX), ret_new, conv_new[:, :, :f]))

        h = _rmsnorm(xs, norm_mix[l], _MXU)
        z = _matmul_few_rows(h, w_main, l, jnp.float32)
        kiw = _matmul_few_rows(h, w_kiw, l, jnp.float32)
        o_a = _dsa_sample(z, kiw, cache_a_k, cache_a_v, cache_a_kidx, page_table, bias_s, l, bs, ts)
        o_b = _moba_sample(z, cache_b_k, cache_b_v, page_table, bias_s, l, bs, ts)
        o_c, ret_new = _retention(z, state_ret[l], ret_gn[l], pos_s, bs, ts)
        xs, conv_new = _mix_and_ffn(xs, z, o_a, o_b, o_c, wts, l,
                                    lambda h2: _ffn_in_sample(h2, up, gate, cw, cb, l, conv_prev_all[l], bs, ts))
        st_s.append(_state_slices(z, kiw, bs, ts) + (ret_new, conv_new[:, :, :f]))

    y_prompt = _rmsnorm(xp, norm_final, jnp.float32).reshape(bp, tp, d)
    y_sample = _rmsnorm(xs, norm_final, jnp.float32).reshape(bs, ts, d)
    ka_p, va_p, kb_p, vb_p = kv_p
    ki_p, ret_p, conv_p = [jnp.stack(v) for v in zip(*st_p)]
    outs_s = [jnp.stack(v) for v in zip(*st_s)]
    return (y_prompt, y_sample,
            ka_p.reshape(depth, bp, tp, HKV_A, DH), va_p.reshape(depth, bp, tp, HKV_A, DH), ki_p,
            kb_p.reshape(depth, bp, tp, HKV_B, DH), vb_p.reshape(depth, bp, tp, HKV_B, DH), ret_p, conv_p, *outs_s)
```

```python
import functools
import math

import jax
import jax.numpy as jnp
from jax import lax
from jax.experimental import pallas as pl
from jax.experimental.pallas import tpu as pltpu

DH = 128
H_A, HKV_A = 8, 2
H_IDX, D_IDX = 16, 64
TOPK_IDX = 256
H_B, HKV_B = 8, 2
MOBA_BLOCK = 256
MOBA_TOPK = 3
H_C, DK_C, DV_C = 8, 128, 256
N_BUCKETS = 32
MAX_DIST = 128
CONV_W = 3
EPS = 1e-6
ROPE_BASE = 10000.0
RET_CHUNK = 128

_MXU = jnp.bfloat16
_NEG = -1e30
_TQ = 256
_LANES = 128
_VMEM_LIMIT = 56 * 1024 * 1024

_C_VC = 0
_C_GC = _C_VC + H_C * DV_C
_C_QA = _C_GC + H_C * DV_C
_C_QI = _C_QA + H_A * DH
_C_QB = _C_QI + H_IDX * D_IDX
_C_QC = _C_QB + H_B * DH
_C_KC = _C_QC + H_C * DK_C
_C_KA = _C_KC + H_C * DK_C
_C_VA = _C_KA + HKV_A * DH
_C_KB = _C_VA + HKV_A * DH
_C_VB = _C_KB + HKV_B * DH
_C_GATES = _C_VB + HKV_B * DH


def _params(sem):
    return pltpu.CompilerParams(dimension_semantics=sem, vmem_limit_bytes=_VMEM_LIMIT)


def _tile(n, target, mult=_LANES):
    best = None
    t = mult
    while t <= min(n, target):
        if n % t == 0:
            best = t
        t += mult
    return n if best is None else best


def _rmsnorm_body(x_ref, g_ref, o_ref):
    x = x_ref[...]
    y = x * lax.rsqrt(jnp.mean(x * x, axis=-1, keepdims=True) + EPS)
    o_ref[...] = (y * g_ref[...]).astype(o_ref.dtype)


def _rmsnorm(x, g, out_dtype):
    m, d = x.shape
    tm = _tile(m, 512, 8)
    return pl.pallas_call(
        _rmsnorm_body,
        grid=(m // tm,),
        in_specs=[pl.BlockSpec((tm, d), lambda i: (i, 0)), pl.BlockSpec((1, d), lambda i: (0, 0))],
        out_specs=pl.BlockSpec((tm, d), lambda i: (i, 0)),
        out_shape=jax.ShapeDtypeStruct((m, d), out_dtype),
        compiler_params=_params(("parallel",)),
        name="rmsnorm",
    )(x, g.reshape(1, d))


def _dot_w(x, w, w_t):
    if w_t:
        return lax.dot_general(x, w, (((1,), (1,)), ((), ())), preferred_element_type=jnp.float32)
    return jnp.dot(x, w, preferred_element_type=jnp.float32)


def _mm_body(*refs, nk, has_res, w_t):
    if has_res:
        x_ref, w_ref, r_ref, o_ref, acc_ref = refs
    else:
        x_ref, w_ref, o_ref, acc_ref = refs
    k = pl.program_id(2)

    @pl.when(k == 0)
    def _():
        acc_ref[...] = jnp.zeros_like(acc_ref)

    acc_ref[...] += _dot_w(x_ref[...], w_ref[...], w_t)

    @pl.when(k == nk - 1)
    def _():
        r = acc_ref[...]
        if has_res:
            r = r + r_ref[...]
        o_ref[...] = r.astype(o_ref.dtype)


def _mm_few_rows_body(x_ref, w_ref, o_ref):
    x = x_ref[...]
    x = jnp.concatenate([x, jnp.zeros((_LANES - x.shape[0], x.shape[1]), x.dtype)], axis=0)
    r_t = lax.dot_general(w_ref[...], x, (((1,), (1,)), ((), ())), preferred_element_type=jnp.float32)
    o_ref[...] = r_t.T[0:o_ref.shape[0]].astype(o_ref.dtype)


def _matmul_few_rows(x, w, layer, out_dtype, tn_target=1024):
    m, kdim = x.shape
    n = w.shape[1]
    tn = _tile(n, tn_target)
    assert m <= _LANES and m % 8 == 0
    return pl.pallas_call(
        _mm_few_rows_body,
        grid=(n // tn,),
        in_specs=[pl.BlockSpec((m, kdim), lambda j: (0, 0)), pl.BlockSpec((None, tn, kdim), lambda j: (layer, j, 0))],
        out_specs=pl.BlockSpec((m, tn), lambda j: (0, j)),
        out_shape=jax.ShapeDtypeStruct((m, n), out_dtype),
        compiler_params=_params(("parallel",)),
        name="matmul_few_rows",
    )(x, w)


def _matmul(x, w, layer, out_dtype, residual=None, w_t=False, tm_target=1024, tn_target=1024, tk_target=2048):
    m, kdim = x.shape
    n = w.shape[1] if w_t else w.shape[2]
    tm = _tile(m, tm_target, 8)
    tn = _tile(n, tn_target)
    tk = _tile(kdim, tk_target)
    nk = kdim // tk
    if w_t:
        w_spec = pl.BlockSpec((None, tn, tk), lambda i, j, k: (layer, j, k))
    else:
        w_spec = pl.BlockSpec((None, tk, tn), lambda i, j, k: (layer, k, j))
    in_specs = [pl.BlockSpec((tm, tk), lambda i, j, k: (i, k)), w_spec]
    args = [x, w]
    if residual is not None:
        in_specs.append(pl.BlockSpec((tm, tn), lambda i, j, k: (i, j)))
        args.append(residual)
    return pl.pallas_call(
        functools.partial(_mm_body, nk=nk, has_res=residual is not None, w_t=w_t),
        grid=(m // tm, n // tn, nk),
        in_specs=in_specs,
        out_specs=pl.BlockSpec((tm, tn), lambda i, j, k: (i, j)),
        out_shape=jax.ShapeDtypeStruct((m, n), out_dtype),
        scratch_shapes=[pltpu.VMEM((tm, tn), jnp.float32)],
        compiler_params=_params(("parallel", "parallel", "arbitrary")),
        name="matmul",
    )(*args)


def _in_proj_body(*refs, n_prev, j_state):
    x_ref, g_ref, w_ref, wk_ref = refs[:4]
    z_ref, kiw_ref, ka_ref, va_ref, kb_ref, vb_ref, h_ref = refs[4 + n_prev:]

    @pl.when(pl.program_id(1) == 0)
    def _():
        x = x_ref[...]
        y = x * lax.rsqrt(jnp.mean(x * x, axis=-1, keepdims=True) + EPS)
        h_ref[...] = (y * g_ref[...]).astype(h_ref.dtype)
        kiw_ref[...] = _dot_w(h_ref[...], wk_ref[...], True)

    r = _dot_w(h_ref[...], w_ref[...], True)
    z_ref[...] = r

    @pl.when(pl.program_id(1) == j_state)
    def _():
        col = 0
        for ref in (ka_ref, va_ref, kb_ref, vb_ref):
            for g in range(ref.shape[1]):
                ref[:, g, :] = r[:, col:col + DH]
                col += DH


def _in_proj(x, g, w_main, w_kiw, layer, depth, prev_states):
    m, kdim = x.shape
    n = w_main.shape[1]
    tm = _tile(m, 1024, 8)
    tn = 2 * HKV_A * DH + 2 * HKV_B * DH
    assert n % tn == 0 and _C_KA % tn == 0 and _C_VB + HKV_B * DH == _C_KA + tn
    kv_heads = (HKV_A, HKV_A, HKV_B, HKV_B)
    prev = [] if prev_states is None else list(prev_states)
    return pl.pallas_call(
        functools.partial(_in_proj_body, n_prev=len(prev), j_state=_C_KA // tn),
        grid=(m // tm, n // tn),
        in_specs=[pl.BlockSpec((tm, kdim), lambda i, j: (i, 0)),
                  pl.BlockSpec((1, kdim), lambda i, j: (0, 0)),
                  pl.BlockSpec((None, tn, kdim), lambda i, j: (layer, j, 0)),
                  pl.BlockSpec((None, _LANES, kdim), lambda i, j: (layer, 0, 0))]
        + [pl.BlockSpec(memory_space=pl.ANY)] * len(prev),
        out_specs=[pl.BlockSpec((tm, tn), lambda i, j: (i, j)), pl.BlockSpec((tm, _LANES), lambda i, j: (i, 0))]
        + [pl.BlockSpec((None, tm, nh, DH), lambda i, j: (layer, i, 0, 0)) for nh in kv_heads],
        out_shape=[jax.ShapeDtypeStruct((m, n), jnp.float32), jax.ShapeDtypeStruct((m, _LANES), jnp.float32)]
        + [jax.ShapeDtypeStruct((depth, m, nh, DH), jnp.float32) for nh in kv_heads],
        scratch_shapes=[pltpu.VMEM((tm, kdim), _MXU)],
        input_output_aliases={4 + k: 2 + k for k in range(len(prev))},
        compiler_params=_params(("parallel", "arbitrary")),
        name="in_proj",
    )(x, g.reshape(1, kdim), w_main, w_kiw, *prev)


def _t5_bucket(rel):
    n = jnp.maximum(rel, 0)
    max_exact = N_BUCKETS // 2
    nf = jnp.maximum(n, 1).astype(jnp.float32)
    large = max_exact + (jnp.log(nf / max_exact) / math.log(MAX_DIST / max_exact)
                         * (N_BUCKETS - max_exact)).astype(jnp.int32)
    large = jnp.minimum(large, N_BUCKETS - 1)
    return jnp.where(n < max_exact, n, large)


def _bias_body(tab_ref, bucket_ref, o_ref):
    h = pl.program_id(0)
    bucket = bucket_ref[...]
    acc = jnp.zeros(bucket.shape, jnp.float32)
    for b in range(N_BUCKETS):
        acc = jnp.where(bucket == b, tab_ref[b, h], acc)
    o_ref[...] = acc


def _bias_table(rel_bias, rel):
    bucket = _t5_bucket(rel)
    r, c = rel.shape
    nh = rel_bias.shape[1]
    return pl.pallas_call(
        _bias_body,
        grid=(nh,),
        in_specs=[pl.BlockSpec(memory_space=pltpu.SMEM), pl.BlockSpec((r, c), lambda h: (0, 0))],
        out_specs=pl.BlockSpec((None, r, c), lambda h: (h, 0, 0)),
        out_shape=jax.ShapeDtypeStruct((nh, r, c), jnp.float32),
        compiler_params=_params(("parallel",)),
        name="bias_table",
    )(rel_bias, bucket)


def _sortable(x):
    b = lax.bitcast_convert_type(x, jnp.int32)
    return b ^ ((b >> 31) & jnp.int32(0x7FFFFFFF))


_KEY_NEG_INF = -2147483648 + 0x7FFFFF


def _stack_heads(q_ref, g, n_grp, scale):
    parts = [q_ref[:, (g * n_grp + hh) * DH:(g * n_grp + hh + 1) * DH] for hh in range(n_grp)]
    return (jnp.concatenate(parts, axis=0) * scale).astype(_MXU)


def _flash_update(m_ref, l_ref, acc_ref, g, logits, v):
    m_old = m_ref[g]
    m_new = jnp.maximum(m_old, jnp.max(logits, axis=1, keepdims=True))
    alpha = jnp.exp(m_old - m_new)
    p = jnp.exp(logits - m_new)
    l_ref[g] = alpha * l_ref[g] + jnp.sum(p, axis=1, keepdims=True)
    acc_ref[g] = alpha * acc_ref[g] + jnp.dot(p.astype(_MXU), v, preferred_element_type=jnp.float32)
    m_ref[g] = m_new


def _flash_init(m_ref, l_ref, acc_ref):
    m_ref[...] = jnp.full(m_ref.shape, _NEG, jnp.float32)
    l_ref[...] = jnp.zeros(l_ref.shape, jnp.float32)
    acc_ref[...] = jnp.zeros(acc_ref.shape, jnp.float32)


def _flash_store(o_ref, l_ref, acc_ref, n_kv, n_grp, rows):
    for g in range(n_kv):
        o = acc_ref[g] / l_ref[g]
        for hh in range(n_grp):
            h = g * n_grp + hh
            o_ref[:, h * DH:(h + 1) * DH] = o[hh * rows:(hh + 1) * rows].astype(o_ref.dtype)


def _flash_update_t(m_ref, l_ref, acc_ref, g, logits_t, v_t, extra_of_chunk):
    ps, alphas = [], []
    for c in range(logits_t.shape[1] // _LANES):
        sl = slice(c * _LANES, (c + 1) * _LANES)
        x = logits_t[:, sl] + extra_of_chunk(c)
        m_old = m_ref[g, :, sl]
        m_new = jnp.maximum(m_old, jnp.max(x, axis=0, keepdims=True))
        alpha = jnp.exp(m_old - m_new)
        p = jnp.exp(x - m_new)
        l_ref[g, :, sl] = alpha * l_ref[g, :, sl] + jnp.sum(p, axis=0, keepdims=True)
        m_ref[g, :, sl] = m_new
        ps.append(p.astype(_MXU))
        alphas.append(alpha)
    p_all = jnp.concatenate(ps, axis=1)
    acc_ref[g] = (jnp.concatenate(alphas, axis=1) * acc_ref[g]
                  + jnp.dot(v_t, p_all, preferred_element_type=jnp.float32))


def _flash_store_t(o_ref, l_ref, acc_ref, n_kv, n_grp, rows):
    for g in range(n_kv):
        o = acc_ref[g] / l_ref[g]
        for hh in range(n_grp):
            h = g * n_grp + hh
            o_ref[:, h * DH:(h + 1) * DH] = o[:, hh * rows:(hh + 1) * rows].T.astype(o_ref.dtype)


def _kth_largest_key(count_ge, n_sel, shape):
    int_min = jnp.int32(-2147483648)
    prefix = jnp.full(shape, int_min, jnp.int32)
    zero = jnp.zeros(shape, jnp.int32)
    prefix = jnp.where(count_ge(zero) >= n_sel, zero, prefix)

    def step(it, prefix):
        cand = prefix | (jnp.int32(1) << (30 - it))
        return jnp.where(count_ge(cand) >= n_sel, cand, prefix)

    return lax.fori_loop(0, 31, step, prefix)


def _first_index_cut(count_lt, need, shape, n_idx):
    nbits = max(1, int(n_idx).bit_length())
    cut = jnp.zeros(shape, jnp.int32)

    def step(it, cut):
        cand = cut + (jnp.int32(1) << (nbits - 1 - it))
        return jnp.where(count_lt(cand) < need, cand, cut)

    return lax.fori_loop(0, nbits, step, cut)


def _stage_kv(k_in_ref, v_in_ref, kb_ref, vt_ref, tq):
    for j in range(kb_ref.shape[0]):
        kb_ref[j] = k_in_ref[j * tq:(j + 1) * tq, :].astype(_MXU)
        vt_ref[j] = v_in_ref[j * tq:(j + 1) * tq, :].T.astype(_MXU)


def _attend_t(j, qs, kb_ref, vt_ref, m_ref, l_ref, acc_ref, extra_of_chunk):
    for g in range(len(qs)):
        logits_t = lax.dot_general(kb_ref[j, :, g * DH:(g + 1) * DH], qs[g], (((1,), (1,)), ((), ())),
                                   preferred_element_type=jnp.float32)
        _flash_update_t(m_ref, l_ref, acc_ref, g, logits_t, vt_ref[j, g * DH:(g + 1) * DH, :],
                        functools.partial(extra_of_chunk, g))


def _dsa_p_body(qa_ref, qi_ref, kiwq_ref, kiw_ref, ka_ref, va_ref, bias_ref, o_ref,
                kx_ref, kb_ref, vt_ref, key_ref, neg_ref, m_ref, l_ref, acc_ref, *, n_sel, t):
    i = pl.program_id(1)
    tq = _TQ
    n_grp = H_A // HKV_A
    krow = lax.broadcasted_iota(jnp.int32, (tq, tq), 0)
    qcol = lax.broadcasted_iota(jnp.int32, (tq, tq), 1)

    @pl.when(i == 0)
    def _():
        _stage_kv(ka_ref, va_ref, kb_ref, vt_ref, tq)
        for j in range(kx_ref.shape[0]):
            kx_ref[j] = kiw_ref[j * tq:(j + 1) * tq, 0:D_IDX].astype(_MXU)

    qi = [qi_ref[:, h * D_IDX:(h + 1) * D_IDX].astype(_MXU) for h in range(H_IDX)]
    wt = kiwq_ref[...].T[D_IDX:D_IDX + H_IDX, :] * (D_IDX ** -0.5 * H_IDX ** -0.5)

    def score_tile(j, carry):
        kx = kx_ref[j]
        sc = jnp.zeros((tq, tq), jnp.float32)
        for h in range(H_IDX):
            s = lax.dot_general(kx, qi[h], (((1,), (1,)), ((), ())), preferred_element_type=jnp.float32)
            sc = sc + wt[h:h + 1, :] * jnp.maximum(s, 0.0)
        causal = jnp.logical_or(j < i, krow <= qcol)
        key_ref[j] = _sortable(jnp.where(causal, sc, -jnp.inf))
        return carry

    lax.fori_loop(0, i + 1, score_tile, 0)

    def count_tiles(pred):
        def body(j, acc):
            return acc + jnp.sum(pred(key_ref[j], j).astype(jnp.int32).reshape(tq // 8, 8, tq), axis=0)
        acc = lax.fori_loop(0, i + 1, body, jnp.zeros((8, tq), jnp.int32))
        return jnp.sum(acc, axis=0, keepdims=True)

    thr = _kth_largest_key(lambda c: count_tiles(lambda k, j: k >= c), n_sel, (1, tq))
    n_ge = count_tiles(lambda k, j: k >= thr)
    finite = thr > _KEY_NEG_INF
    tie = jnp.max(jnp.where(jnp.logical_and(finite, n_ge > n_sel), 1, 0)) > 0

    def write_neg(cut):
        def body(j, carry):
            k = key_ref[j]
            kpos = j * tq + krow
            keep = jnp.logical_or(k > thr, jnp.logical_and(k == thr, kpos <= cut))
            keep = jnp.logical_and(keep, jnp.logical_or(j < i, krow <= qcol))
            neg_ref[j] = jnp.where(keep, 0.0, _NEG)
            return carry
        lax.fori_loop(0, i + 1, body, 0)

    def with_ties():
        n_gt = count_tiles(lambda k, j: k > thr)
        need = n_sel - n_gt
        cut = _first_index_cut(
            lambda p: count_tiles(lambda k, j: jnp.logical_and(k == thr, j * tq + krow < p)),
            need, (1, tq), t)
        write_neg(jnp.where(finite, cut, jnp.int32(2 ** 30)))

    def without_ties():
        write_neg(jnp.full((1, tq), 2 ** 30, jnp.int32))

    lax.cond(tie, with_ties, without_ties)

    _flash_init(m_ref, l_ref, acc_ref)
    scale = DH ** -0.5
    qs = [_stack_heads(qa_ref, g, n_grp, scale) for g in range(HKV_A)]

    per_head = tq // _LANES

    def attend(j, bias_of_chunk):
        neg = neg_ref[j]

        def extra(g, c):
            half = slice((c % per_head) * _LANES, (c % per_head + 1) * _LANES)
            return bias_of_chunk(g * n_grp + c // per_head, half) + neg[:, half]
        _attend_t(j, qs, kb_ref, vt_ref, m_ref, l_ref, acc_ref, extra)

    def far(j, carry):
        attend(j, lambda h, half: bias_ref[h, 0:1, tq - 1:tq])
        return carry

    lax.fori_loop(0, jnp.maximum(i - 1, 0), far, 0)

    @pl.when(i >= 1)
    def _():
        attend(i - 1, lambda h, half: bias_ref[h, 0:tq, half])

    attend(i, lambda h, half: bias_ref[h, tq:2 * tq, half])
    _flash_store_t(o_ref, l_ref, acc_ref, HKV_A, n_grp, tq)


def _kv_scratch(nk, tq, n_kv, n_grp):
    return [
        pltpu.VMEM((nk, tq, n_kv * DH), _MXU),
        pltpu.VMEM((nk, n_kv * DH, tq), _MXU),
    ], [
        pltpu.VMEM((n_kv, 1, n_grp * tq), jnp.float32),
        pltpu.VMEM((n_kv, 1, n_grp * tq), jnp.float32),
        pltpu.VMEM((n_kv, DH, n_grp * tq), jnp.float32),
    ]


def _dsa_prompt(z, kiw, bias, n_batch, t):
    tq = _TQ
    nq = t // tq
    n_sel = min(TOPK_IDX, t // 4)
    assert n_sel <= tq and t % tq == 0
    n_grp = H_A // HKV_A
    kv_scratch, flash_scratch = _kv_scratch(nq, tq, HKV_A, n_grp)
    return pl.pallas_call(
        functools.partial(_dsa_p_body, n_sel=n_sel, t=t),
        grid=(n_batch, nq),
        in_specs=[
            pl.BlockSpec((tq, H_A * DH), lambda b, i: (b * nq + i, _C_QA // (H_A * DH))),
            pl.BlockSpec((tq, H_IDX * D_IDX), lambda b, i: (b * nq + i, _C_QI // (H_IDX * D_IDX))),
            pl.BlockSpec((tq, _LANES), lambda b, i: (b * nq + i, 0)),
            pl.BlockSpec((t, _LANES), lambda b, i: (b, 0)),
            pl.BlockSpec((t, HKV_A * DH), lambda b, i: (b, _C_KA // (HKV_A * DH))),
            pl.BlockSpec((t, HKV_A * DH), lambda b, i: (b, _C_VA // (HKV_A * DH))),
            pl.BlockSpec((H_A, 2 * tq, tq), lambda b, i: (0, 0, 0)),
        ],
        out_specs=pl.BlockSpec((tq, H_A * DH), lambda b, i: (b * nq + i, 0)),
        out_shape=jax.ShapeDtypeStruct((n_batch * t, H_A * DH), _MXU),
        scratch_shapes=[pltpu.VMEM((nq, tq, D_IDX), _MXU)] + kv_scratch + [
            pltpu.VMEM((nq, tq, tq), jnp.int32),
            pltpu.VMEM((nq, tq, tq), jnp.float32),
        ] + flash_scratch,
        compiler_params=_params(("parallel", "arbitrary")),
        name="dsa_prompt",
    )(z, z, kiw, kiw, z, z, bias)


def _topk_lanes(gate, n_cand, n_sel):
    lane = lax.broadcasted_iota(jnp.int32, gate.shape, 1)
    live = lane < n_cand
    chosen = jnp.zeros(gate.shape, jnp.bool_)
    big = jnp.int32(2 ** 30)
    for _ in range(n_sel):
        cand = jnp.logical_and(live, jnp.logical_not(chosen))
        best = jnp.max(jnp.where(cand, gate, -jnp.inf), axis=1, keepdims=True)
        first = jnp.min(jnp.where(jnp.logical_and(cand, gate == best), lane, big), axis=1, keepdims=True)
        chosen = jnp.logical_or(chosen, lane == first)
    return chosen


def _topk_rows(gate, n_cand, n_sel):
    row = lax.broadcasted_iota(jnp.int32, gate.shape, 0)
    live = row < n_cand
    chosen = jnp.zeros(gate.shape, jnp.bool_)
    big = jnp.int32(2 ** 30)
    for _ in range(n_sel):
        cand = jnp.logical_and(live, jnp.logical_not(chosen))
        best = jnp.max(jnp.where(cand, gate, -jnp.inf), axis=0, keepdims=True)
        first = jnp.min(jnp.where(jnp.logical_and(cand, gate == best), row, big), axis=0, keepdims=True)
        chosen = jnp.logical_or(chosen, row == first)
    return chosen


def _moba_p_body(qb_ref, k_in_ref, v_in_ref, bias_ref, o_ref, kmean_ref, kb_ref, vt_ref, allow_ref,
                 m_ref, l_ref, acc_ref, *, n_sel):
    i = pl.program_id(1)
    tq = _TQ
    nb = kb_ref.shape[0]
    n_grp = H_B // HKV_B
    krow = lax.broadcasted_iota(jnp.int32, (tq, tq), 0)
    qcol = lax.broadcasted_iota(jnp.int32, (tq, tq), 1)

    @pl.when(i == 0)
    def _():
        _stage_kv(k_in_ref, v_in_ref, kb_ref, vt_ref, tq)
        kmean_ref[...] = jnp.zeros(kmean_ref.shape, jnp.float32)
        for j in range(nb):
            kmean_ref[j:j + 1, :] = jnp.mean(k_in_ref[j * tq:(j + 1) * tq, :], axis=0, keepdims=True)

    for g in range(HKV_B):
        qg = jnp.concatenate([qb_ref[:, (g * n_grp + hh) * DH:(g * n_grp + hh + 1) * DH] for hh in range(n_grp)],
                             axis=0)
        gate_t = lax.dot_general(kmean_ref[:, g * DH:(g + 1) * DH], qg, (((1,), (1,)), ((), ())),
                                 preferred_element_type=jnp.float32, precision=lax.Precision.HIGHEST)
        allow_ref[g] = jnp.where(_topk_rows(gate_t, i, n_sel), 0.0, _NEG)

    _flash_init(m_ref, l_ref, acc_ref)
    scale = DH ** -0.5
    qs = [_stack_heads(qb_ref, g, n_grp, scale) for g in range(HKV_B)]
    causal_neg = jnp.where(krow <= qcol, 0.0, _NEG)
    per_head = tq // _LANES

    def head_half(g, c):
        return g * n_grp + c // per_head, slice((c % per_head) * _LANES, (c % per_head + 1) * _LANES)

    def attend_past(j, bias_of_chunk):
        allow = [allow_ref[g, pl.ds(j, 1), :] for g in range(HKV_B)]

        def extra(g, c):
            return bias_of_chunk(*head_half(g, c)) + allow[g][:, c * _LANES:(c + 1) * _LANES]
        _attend_t(j, qs, kb_ref, vt_ref, m_ref, l_ref, acc_ref, extra)

    def far(j, carry):
        attend_past(j, lambda h, half: bias_ref[h, 0:1, tq - 1:tq])
        return carry

    lax.fori_loop(0, jnp.maximum(i - 1, 0), far, 0)

    @pl.when(i >= 1)
    def _():
        attend_past(i - 1, lambda h, half: bias_ref[h, 0:tq, half])

    def extra_own(g, c):
        h, half = head_half(g, c)
        return bias_ref[h, tq:2 * tq, half] + causal_neg[:, half]

    _attend_t(i, qs, kb_ref, vt_ref, m_ref, l_ref, acc_ref, extra_own)
    _flash_store_t(o_ref, l_ref, acc_ref, HKV_B, n_grp, tq)


def _moba_prompt(z, bias, n_batch, t):
    tq = _TQ
    assert t % tq == 0 and MOBA_BLOCK == tq
    nb = t // tq
    n_sel = min(MOBA_TOPK, (t - 1) // MOBA_BLOCK)
    n_grp = H_B // HKV_B
    nb_pad = -(-nb // 8) * 8
    kv_scratch, flash_scratch = _kv_scratch(nb, tq, HKV_B, n_grp)
    return pl.pallas_call(
        functools.partial(_moba_p_body, n_sel=n_sel),
        grid=(n_batch, nb),
        in_specs=[
            pl.BlockSpec((tq, H_B * DH), lambda b, i: (b * nb + i, _C_QB // (H_B * DH))),
            pl.BlockSpec((t, HKV_B * DH), lambda b, i: (b, _C_KB // (HKV_B * DH))),
            pl.BlockSpec((t, HKV_B * DH), lambda b, i: (b, _C_VB // (HKV_B * DH))),
            pl.BlockSpec((H_B, 2 * tq, tq), lambda b, i: (1, 0, 0)),
        ],
        out_specs=pl.BlockSpec((tq, H_B * DH), lambda b, i: (b * nb + i, 0)),
        out_shape=jax.ShapeDtypeStruct((n_batch * t, H_B * DH), _MXU),
        scratch_shapes=[pltpu.VMEM((nb_pad, HKV_B * DH), jnp.float32)] + kv_scratch
        + [pltpu.VMEM((HKV_B, nb_pad, n_grp * tq), jnp.float32)] + flash_scratch,
        compiler_params=_params(("parallel", "arbitrary")),
        name="moba_prompt",
    )(z, z, z, bias)


def _ret_body(q_ref, k_ref, v_ref, g_ref, cos_ref, sin_ref, intra_ref, cross_ref, tail_ref, decay_ref, gn_ref,
              s0_ref, o_ref, s_out_ref, s_ref):
    t = pl.program_id(1)

    @pl.when(t == 0)
    def _():
        s_ref[...] = s0_ref[...]

    cos = cos_ref[...]
    sin = sin_ref[...]
    for h in range(H_C):
        qh = q_ref[:, h * DK_C:(h + 1) * DK_C]
        kh = k_ref[:, h * DK_C:(h + 1) * DK_C]
        q = qh * cos + pltpu.roll(qh, DK_C // 2, axis=1) * sin
        k = (kh * cos + pltpu.roll(kh, DK_C // 2, axis=1) * sin) * (DK_C ** -0.5)
        v = v_ref[:, h * DV_C:(h + 1) * DV_C].astype(_MXU)
        s = s_ref[h]
        qm = q.astype(_MXU)
        att = lax.dot_general(qm, k.astype(_MXU), (((1,), (1,)), ((), ())),
                              preferred_element_type=jnp.float32) * intra_ref[h]
        o = (jnp.dot(att.astype(_MXU), v, preferred_element_type=jnp.float32)
             + jnp.dot(qm, s.astype(_MXU), preferred_element_type=jnp.float32) * cross_ref[h])
        kt = (k * tail_ref[h]).astype(_MXU)
        s_ref[h] = s * decay_ref[h] + lax.dot_general(kt, v, (((0,), (0,)), ((), ())),
                                                      preferred_element_type=jnp.float32)
        o = o * lax.rsqrt(jnp.mean(o * o, axis=-1, keepdims=True) + EPS)
        gate = g_ref[:, h * DV_C:(h + 1) * DV_C]
        o = o * gn_ref[:, h * DV_C:(h + 1) * DV_C] * (gate * jax.nn.sigmoid(gate))
        o_ref[:, h * DV_C:(h + 1) * DV_C] = o.astype(o_ref.dtype)

    @pl.when(t == pl.num_programs(1) - 1)
    def _():
        s_out_ref[...] = s_ref[...]


def _retention(z, s0, ret_gn, pos, n_batch, t):
    c = math.gcd(t, RET_CHUNK)
    nc = t // c
    half = DK_C // 2
    inv = ROPE_BASE ** (-jnp.arange(0, DK_C, 2, dtype=jnp.float32) / DK_C)
    ang = pos.astype(jnp.float32)[:, None] * inv[None, :]
    cos = jnp.concatenate([jnp.cos(ang), jnp.cos(ang)], axis=1)
    sin = jnp.concatenate([-jnp.sin(ang), jnp.sin(ang)], axis=1)
    assert cos.shape == (t, 2 * half)
    log_g = jnp.log1p(-jnp.exp2(-5.0 - jnp.arange(H_C, dtype=jnp.float32)))
    idx = jnp.arange(c, dtype=jnp.float32)
    diff = idx[:, None] - idx[None, :]
    intra = jnp.where(diff[None] >= 0, jnp.exp(jnp.maximum(diff, 0.0)[None] * log_g[:, None, None]), 0.0)
    cross = jnp.exp((idx[None, :] + 1.0) * log_g[:, None])[:, :, None]
    tail = jnp.exp((c - 1.0 - idx)[None, :] * log_g[:, None])[:, :, None]
    decay = jnp.exp(c * log_g)[:, None, None]
    wq = H_C * DK_C
    wv = H_C * DV_C
    return pl.pallas_call(
        _ret_body,
        grid=(n_batch, nc),
        in_specs=[
            pl.BlockSpec((c, wq), lambda b, i: (b * nc + i, _C_QC // wq)),
            pl.BlockSpec((c, wq), lambda b, i: (b * nc + i, _C_KC // wq)),
            pl.BlockSpec((c, wv), lambda b, i: (b * nc + i, _C_VC // wv)),
            pl.BlockSpec((c, wv), lambda b, i: (b * nc + i, _C_GC // wv)),
            pl.BlockSpec((c, DK_C), lambda b, i: (i, 0)),
            pl.BlockSpec((c, DK_C), lambda b, i: (i, 0)),
            pl.BlockSpec((H_C, c, c), lambda b, i: (0, 0, 0)),
            pl.BlockSpec((H_C, c, 1), lambda b, i: (0, 0, 0)),
            pl.BlockSpec((H_C, c, 1), lambda b, i: (0, 0, 0)),
            pl.BlockSpec((H_C, 1, 1), lambda b, i: (0, 0, 0)),
            pl.BlockSpec((1, wv), lambda b, i: (0, 0)),
            pl.BlockSpec((None, H_C, DK_C, DV_C), lambda b, i: (b, 0, 0, 0)),
        ],
        out_specs=[
            pl.BlockSpec((c, wv), lambda b, i: (b * nc + i, 0)),
            pl.BlockSpec((None, H_C, DK_C, DV_C), lambda b, i: (b, 0, 0, 0)),
        ],
        out_shape=[
            jax.ShapeDtypeStruct((n_batch * t, wv), _MXU),
            jax.ShapeDtypeStruct((n_batch, H_C, DK_C, DV_C), jnp.float32),
        ],
        scratch_shapes=[pltpu.VMEM((H_C, DK_C, DV_C), jnp.float32)],
        compiler_params=_params(("parallel", "arbitrary")),
        name="retention",
    )(z, z, z, z, cos, sin, intra, cross, tail, decay, ret_gn.reshape(1, wv), s0)


def _merge_body(oa_ref, ob_ref, oc_ref, wa_ref, wb_ref, wc_ref, ga_ref, gb_ref, gc_ref, o_ref):
    def term(o, w, g):
        return jax.nn.sigmoid(g[...]) * jnp.dot(o[...], w[...], preferred_element_type=jnp.float32)
    o_ref[...] = (term(oa_ref, wa_ref, ga_ref) + term(ob_ref, wb_ref, gb_ref)
                  + term(oc_ref, wc_ref, gc_ref)).astype(o_ref.dtype)


def _merge(o_a, o_b, o_c, z, w_pa, w_pb, w_pc, layer):
    m = o_a.shape[0]
    d = w_pa.shape[2]
    tm = _tile(m, 1024, 8)
    tn = _tile(d, 512)
    gate_blk = [(_C_GATES + k * d) // tn for k in range(3)]
    assert all((_C_GATES + k * d) % tn == 0 for k in range(3))
    row = lambda w: pl.BlockSpec((tm, w), lambda i, j: (i, 0))
    col = lambda w: pl.BlockSpec((None, w, tn), lambda i, j: (layer, 0, j))
    gate = lambda k: pl.BlockSpec((tm, tn), lambda i, j: (i, gate_blk[k] + j))
    return pl.pallas_call(
        _merge_body,
        grid=(m // tm, d // tn),
        in_specs=[row(o_a.shape[1]), row(o_b.shape[1]), row(o_c.shape[1]),
                  col(w_pa.shape[1]), col(w_pb.shape[1]), col(w_pc.shape[1]), gate(0), gate(1), gate(2)],
        out_specs=pl.BlockSpec((tm, tn), lambda i, j: (i, j)),
        out_shape=jax.ShapeDtypeStruct((m, d), _MXU),
        compiler_params=_params(("parallel", "parallel")),
        name="merge",
    )(o_a, o_b, o_c, w_pa, w_pb, w_pc, z, z, z)


def _conv_gate(u, u1, u2, g, cw_ref, cb_ref):
    c = cb_ref[...] + cw_ref[0:1, :] * u2 + cw_ref[1:2, :] * u1 + cw_ref[2:3, :] * u
    return 0.5 * c * (1.0 + lax.erf(c * (2.0 ** -0.5))) * g


def _ffn_p_body(h_ref, wu_ref, wg_ref, cw_ref, cb_ref, a_ref, cs_ref, tail_ref, *, seq, n_sub):
    i = pl.program_id(1)
    tm = h_ref.shape[0]
    sm = tm // n_sub

    @pl.when((i * tm) % seq == 0)
    def _():
        tail_ref[...] = jnp.zeros_like(tail_ref)

    prev = tail_ref[...]
    row = lax.broadcasted_iota(jnp.int32, (sm, 1), 0)
    for r in range(n_sub):
        h = h_ref[r * sm:(r + 1) * sm, :]
        u = jnp.dot(h, wu_ref[...], preferred_element_type=jnp.float32)
        g = jnp.dot(h, wg_ref[...], preferred_element_type=jnp.float32)
        u1 = jnp.where(row == 0, prev[7:8, :], pltpu.roll(u, 1, axis=0))
        u2 = jnp.where(row == 0, prev[6:7, :], jnp.where(row == 1, prev[7:8, :], pltpu.roll(u, 2, axis=0)))
        a_ref[r * sm:(r + 1) * sm, :] = _conv_gate(u, u1, u2, g, cw_ref, cb_ref).astype(a_ref.dtype)
        prev = u[sm - 8:sm, :]
    tail_ref[...] = prev

    @pl.when((i * tm + tm) % seq == 0)
    def _():
        cs_ref[...] = prev[8 - (CONV_W - 1):8, :]


def _ffn_in_prompt(h, w_up, w_gate, conv_w, conv_b, layer, n_batch, t):
    m, d = h.shape
    f = w_up.shape[2]
    tm = _tile(t, 1024, 8)
    tn = _tile(f, 512)
    per_seq = t // tm
    return pl.pallas_call(
        functools.partial(_ffn_p_body, seq=t, n_sub=4 if tm % 32 == 0 else 1),
        grid=(f // tn, m // tm),
        in_specs=[
            pl.BlockSpec((tm, d), lambda j, i: (i, 0)),
            pl.BlockSpec((None, d, tn), lambda j, i: (layer, 0, j)),
            pl.BlockSpec((None, d, tn), lambda j, i: (layer, 0, j)),
            pl.BlockSpec((None, CONV_W, tn), lambda j, i: (layer, 0, j)),
            pl.BlockSpec((None, 1, tn), lambda j, i: (layer, 0, j)),
        ],
        out_specs=[
            pl.BlockSpec((tm, tn), lambda j, i: (i, j)),
            pl.BlockSpec((None, CONV_W - 1, tn), lambda j, i: (i // per_seq, 0, j)),
        ],
        out_shape=[
            jax.ShapeDtypeStruct((m, f), _MXU),
            jax.ShapeDtypeStruct((n_batch, CONV_W - 1, f), jnp.float32),
        ],
        scratch_shapes=[pltpu.VMEM((8, tn), jnp.float32)],
        compiler_params=_params(("parallel", "arbitrary")),
        name="ffn_in_prompt",
    )(h, w_up, w_gate, conv_w, conv_b)


def _ffn_s_body(h_ref, wu_ref, wg_ref, cw_ref, cb_ref, p1_ref, p2_ref, a_ref, u_ref, *, seq):
    tm = h_ref.shape[0]
    h = h_ref[...]
    u = jnp.dot(h, wu_ref[...], preferred_element_type=jnp.float32)
    g = jnp.dot(h, wg_ref[...], preferred_element_type=jnp.float32)
    pos = lax.broadcasted_iota(jnp.int32, (tm, 1), 0) % seq
    u1 = jnp.where(pos >= 1, pltpu.roll(u, 1, axis=0), p1_ref[...])
    u2 = jnp.where(pos >= 2, pltpu.roll(u, 2, axis=0), p2_ref[...])
    a_ref[...] = _conv_gate(u, u1, u2, g, cw_ref, cb_ref).astype(a_ref.dtype)
    u_ref[...] = u


def _ffn_in_sample(h, w_up, w_gate, conv_w, conv_b, layer, conv_prev, n_batch, t):
    m, d = h.shape
    f = w_up.shape[2]
    assert t >= CONV_W - 1
    tn = _tile(f, 512)
    zeros = jnp.zeros((n_batch, t, f), jnp.float32)
    p1 = zeros.at[:, 0].set(conv_prev[:, 1]).reshape(m, f)
    p2 = zeros.at[:, 0].set(conv_prev[:, 0]).at[:, 1].set(conv_prev[:, 1]).reshape(m, f)
    full = lambda w: pl.BlockSpec((m, w), lambda j: (0, 0))
    colf = lambda r: pl.BlockSpec((r, tn), lambda j: (0, j))
    colw = lambda r: pl.BlockSpec((None, r, tn), lambda j: (layer, 0, j))
    a, u = pl.pallas_call(
        functools.partial(_ffn_s_body, seq=t),
        grid=(f // tn,),
        in_specs=[full(d), colw(d), colw(d), colw(CONV_W), colw(1), colf(m), colf(m)],
        out_specs=[colf(m), colf(m)],
        out_shape=[jax.ShapeDtypeStruct((m, f), _MXU), jax.ShapeDtypeStruct((m, f), jnp.float32)],
        compiler_params=_params(("parallel",)),
        name="ffn_in_sample",
    )(h, w_up, w_gate, conv_w, conv_b, p1, p2)
    return a, u.reshape(n_batch, t, f)[:, t - (CONV_W - 1):]


def _page_specs(n_pages_per_step, layer, page_shape):
    zeros = (0,) * len(page_shape)
    return [pl.BlockSpec((None, None) + tuple(page_shape),
                         functools.partial(lambda b, s, pt, g: (layer, pt[b, s * n_pages_per_step + g]) + zeros, g=g))
            for g in range(n_pages_per_step)]


def _dsa_s_score_body(pt_ref, qi_ref, kiw_ref, *refs, n_pg):
    pages = refs[:n_pg]
    o_ref = refs[n_pg]
    ts = qi_ref.shape[0]
    ps = pages[0].shape[1]
    qst = jnp.concatenate([qi_ref[:, h * D_IDX:(h + 1) * D_IDX] for h in range(H_IDX)], axis=0).astype(_MXU)
    wi = kiw_ref[:, D_IDX:D_IDX + H_IDX] * (D_IDX ** -0.5 * H_IDX ** -0.5)
    for g in range(n_pg):
        s = jnp.dot(qst, pages[g][...].astype(_MXU), preferred_element_type=jnp.float32)
        sc = jnp.zeros((ts, ps), jnp.float32)
        for h in range(H_IDX):
            sc = sc + wi[:, h:h + 1] * jnp.maximum(s[h * ts:(h + 1) * ts], 0.0)
        o_ref[:, g * ps:(g + 1) * ps] = sc


def _dsa_s_select_body(sc_ref, qi_ref, kiw_ref, o_ref, on_ref, key_ref, *, n_sel, n_kv, ts):
    rows = qi_ref.shape[0]
    lp = sc_ref.shape[1]
    qi = qi_ref[...]
    kx = kiw_ref[:, 0:D_IDX].astype(_MXU)
    kx = jnp.concatenate([kx, jnp.zeros((_LANES - rows, D_IDX), _MXU)], axis=0)
    wi = kiw_ref[:, D_IDX:D_IDX + H_IDX] * (D_IDX ** -0.5 * H_IDX ** -0.5)
    sc = jnp.zeros((rows, _LANES), jnp.float32)
    for h in range(H_IDX):
        s = lax.dot_general(qi[:, h * D_IDX:(h + 1) * D_IDX].astype(_MXU), kx, (((1,), (1,)), ((), ())),
                            preferred_element_type=jnp.float32)
        sc = sc + wi[:, h:h + 1] * jnp.maximum(s, 0.0)
    sc = jnp.concatenate([pltpu.roll(sc[b * ts:(b + 1) * ts], (_LANES - b * ts) % _LANES, axis=1)
                          for b in range(rows // ts)], axis=0)
    r = lax.broadcasted_iota(jnp.int32, (rows, _LANES), 0) % ts
    c = lax.broadcasted_iota(jnp.int32, (rows, _LANES), 1)
    key_ref[:, 0:lp] = _sortable(sc_ref[...])
    key_ref[:, lp:lp + _LANES] = _sortable(jnp.where(c <= r, sc, -jnp.inf))

    width = lp + _LANES
    pos = lambda: lax.broadcasted_iota(jnp.int32, (rows, width), 1)
    count = lambda m: jnp.sum(m.astype(jnp.int32), axis=1, keepdims=True)
    thr = _kth_largest_key(lambda t: count(key_ref[...] >= t), n_sel, (rows, 1))
    finite = thr > _KEY_NEG_INF
    n_ge = count(key_ref[...] >= thr)
    tie = jnp.max(jnp.where(jnp.logical_and(finite, n_ge > n_sel), 1, 0)) > 0

    def with_ties():
        need = n_sel - count(key_ref[...] > thr)
        cut = _first_index_cut(lambda p: count(jnp.logical_and(key_ref[...] == thr, pos() < p)), need, (rows, 1),
                               width)
        return jnp.where(finite, cut, jnp.int32(2 ** 30))

    cut = lax.cond(tie, with_ties, lambda: jnp.full((rows, 1), 2 ** 30, jnp.int32))
    key = key_ref[...]
    keep = jnp.logical_or(key > thr, jnp.logical_and(key == thr, pos() <= cut))
    own = keep[:, lp:width]
    on_ref[...] = jnp.where(jnp.logical_and(own, c <= r), 0.0, _NEG)
    keep_f = jnp.where(keep, 1.0, 0.0).astype(_MXU)
    spread = (lax.broadcasted_iota(jnp.int32, (_LANES, n_kv * _LANES), 1) // n_kv
              == lax.broadcasted_iota(jnp.int32, (_LANES, n_kv * _LANES), 0))
    spread = jnp.where(spread, 1.0, 0.0).astype(_MXU)
    for cc in range(lp // _LANES):
        dup = jnp.dot(keep_f[:, cc * _LANES:(cc + 1) * _LANES], spread, preferred_element_type=jnp.float32)
        o_ref[:, cc * n_kv * _LANES:(cc + 1) * n_kv * _LANES] = jnp.where(dup > 0.5, 0.0, _NEG)


def _head_rows(pieces):
    return jnp.concatenate(pieces, axis=0)


def _parity_neg(n_heads, n_kv, ts, width):
    row_kv = lax.broadcasted_iota(jnp.int32, (n_heads * ts, width), 0) // ((n_heads // n_kv) * ts)
    col_kv = lax.broadcasted_iota(jnp.int32, (n_heads * ts, width), 1) % n_kv
    return jnp.where(row_kv == col_kv, 0.0, _NEG)


def _own_block(q_of_group, kn_ref, vn_ref, extra_of_head, n_kv, n_grp, ts):
    pad = jnp.zeros((_LANES - ts, DH), _MXU)
    ms, ls, os_ = [], [], []
    for g in range(n_kv):
        kt = jnp.concatenate([kn_ref[:, g * DH:(g + 1) * DH].astype(_MXU), pad], axis=0)
        vt = jnp.concatenate([vn_ref[:, g * DH:(g + 1) * DH].astype(_MXU), pad], axis=0)
        logits = lax.dot_general(q_of_group(g), kt, (((1,), (1,)), ((), ())), preferred_element_type=jnp.float32)
        logits = logits + _head_rows([extra_of_head(g * n_grp + hh) for hh in range(n_grp)])
        m = jnp.max(logits, axis=1, keepdims=True)
        p = jnp.exp(logits - m)
        ms.append(m)
        ls.append(jnp.sum(p, axis=1, keepdims=True))
        os_.append(jnp.dot(p.astype(_MXU), vt, preferred_element_type=jnp.float32))
    return _head_rows(ms), _head_rows(ls), _head_rows(os_)


def _dsa_s_attn_body(pt_ref, qa_ref, kn_ref, vn_ref, negn_ref, neg_ref, bias_ref, *refs, n_pg):
    kpages = refs[:n_pg]
    vpages = refs[n_pg:2 * n_pg]
    o_ref, m_ref, l_ref, acc_ref = refs[2 * n_pg:]
    s = pl.program_id(1)
    n_steps = pl.num_programs(1)
    ts = qa_ref.shape[0]
    rows_pg = kpages[0].shape[0]
    n_grp = H_A // HKV_A
    scale = DH ** -0.5
    qs = [_stack_heads(qa_ref, g, n_grp, scale) for g in range(HKV_A)]
    q_all = _head_rows(qs)
    wb = bias_ref.shape[2]
    own_w = HKV_A * _LANES

    @pl.when(s == 0)
    def _():
        m, l, o = _own_block(lambda g: qs[g], kn_ref, vn_ref,
                             lambda h: bias_ref[h, :, wb - own_w:wb - own_w + _LANES] + negn_ref[...],
                             HKV_A, n_grp, ts)
        m_ref[0], l_ref[0], acc_ref[0] = m, l, o

    last = s == n_steps - 1
    w = n_pg * rows_pg
    kt = jnp.concatenate([kpages[p][...] for p in range(n_pg)], axis=0).astype(_MXU)
    vt = jnp.concatenate([vpages[p][...] for p in range(n_pg)], axis=0).astype(_MXU)
    logits = lax.dot_general(q_all, kt, (((1,), (1,)), ((), ())), preferred_element_type=jnp.float32)
    neg = neg_ref[...]

    def extra_of_head(h):
        far = bias_ref[h, :, 0:1]
        base = neg + far
        near = jnp.where(last, bias_ref[h, :, wb - own_w - rows_pg:wb - own_w] - far, 0.0)
        return jnp.concatenate([base[:, 0:w - rows_pg], base[:, w - rows_pg:w] + near], axis=1)

    extra = _head_rows([extra_of_head(h) for h in range(H_A)]) + _parity_neg(H_A, HKV_A, ts, w)
    _flash_update(m_ref, l_ref, acc_ref, 0, logits + extra, vt)

    @pl.when(last)
    def _():
        o = acc_ref[0] / l_ref[0]
        for h in range(H_A):
            o_ref[:, h * DH:(h + 1) * DH] = o[h * ts:(h + 1) * ts].astype(o_ref.dtype)


def _dsa_sample(zs, kiws, cache_k, cache_v, cache_kidx, page_table, bias, layer, n_batch, ts):
    n_pages = page_table.shape[1]
    ps = cache_kidx.shape[3]
    past = n_pages * ps
    assert ps == _LANES and ts <= _LANES and ts % 8 == 0 and cache_k.shape[2] == ps * HKV_A
    n_sel = min(TOPK_IDX, (past + ts) // 4)
    n_grp = H_A // HKV_A

    gi = _tile(n_pages, 16, 1)
    scores = pl.pallas_call(
        functools.partial(_dsa_s_score_body, n_pg=gi),
        grid_spec=pltpu.PrefetchScalarGridSpec(
            num_scalar_prefetch=1,
            grid=(n_batch, n_pages // gi),
            in_specs=[pl.BlockSpec((ts, H_IDX * D_IDX), lambda b, s, pt: (b, _C_QI // (H_IDX * D_IDX))),
                      pl.BlockSpec((ts, _LANES), lambda b, s, pt: (b, 0))]
            + _page_specs(gi, layer, (D_IDX, ps)),
            out_specs=pl.BlockSpec((None, ts, gi * ps), lambda b, s, pt: (b, 0, s)),
        ),
        out_shape=jax.ShapeDtypeStruct((n_batch, ts, past), jnp.float32),
        compiler_params=_params(("parallel", "arbitrary")),
        name="dsa_sample_scores",
    )(page_table, zs, kiws, *([cache_kidx] * gi))

    rows = ts * _tile(n_batch, max(1, 32 // ts), 1)
    assert rows <= _LANES
    neg, neg_own = pl.pallas_call(
        functools.partial(_dsa_s_select_body, n_sel=n_sel, n_kv=HKV_A, ts=ts),
        grid=(n_batch * ts // rows,),
        in_specs=[pl.BlockSpec((rows, past), lambda i: (i, 0)),
                  pl.BlockSpec((rows, H_IDX * D_IDX), lambda i: (i, _C_QI // (H_IDX * D_IDX))),
                  pl.BlockSpec((rows, _LANES), lambda i: (i, 0))],
        out_specs=[pl.BlockSpec((rows, HKV_A * past), lambda i: (i, 0)),
                   pl.BlockSpec((rows, _LANES), lambda i: (i, 0))],
        out_shape=[jax.ShapeDtypeStruct((n_batch * ts, HKV_A * past), jnp.float32),
                   jax.ShapeDtypeStruct((n_batch * ts, _LANES), jnp.float32)],
        scratch_shapes=[pltpu.VMEM((rows, past + _LANES), jnp.int32)],
        compiler_params=_params(("parallel",)),
        name="dsa_sample_select",
    )(scores.reshape(n_batch * ts, past), zs, kiws)

    ga = _tile(n_pages, 8, 1)
    wkv = HKV_A * DH
    rows_pg = ps * HKV_A
    return pl.pallas_call(
        functools.partial(_dsa_s_attn_body, n_pg=ga),
        grid_spec=pltpu.PrefetchScalarGridSpec(
            num_scalar_prefetch=1,
            grid=(n_batch, n_pages // ga),
            in_specs=[pl.BlockSpec((ts, H_A * DH), lambda b, s, pt: (b, _C_QA // (H_A * DH))),
                      pl.BlockSpec((ts, wkv), lambda b, s, pt: (b, _C_KA // wkv)),
                      pl.BlockSpec((ts, wkv), lambda b, s, pt: (b, _C_VA // wkv)),
                      pl.BlockSpec((ts, _LANES), lambda b, s, pt: (b, 0)),
                      pl.BlockSpec((ts, ga * rows_pg), lambda b, s, pt: (b, s)),
                      pl.BlockSpec((H_A, ts, bias.shape[2]), lambda b, s, pt: (0, 0, 0))]
            + _page_specs(ga, layer, (rows_pg, DH)) + _page_specs(ga, layer, (rows_pg, DH)),
            out_specs=pl.BlockSpec((ts, H_A * DH), lambda b, s, pt: (b, 0)),
            scratch_shapes=[pltpu.VMEM((1, H_A * ts, 1), jnp.float32),
                            pltpu.VMEM((1, H_A * ts, 1), jnp.float32),
                            pltpu.VMEM((1, H_A * ts, DH), jnp.float32)],
        ),
        out_shape=jax.ShapeDtypeStruct((n_batch * ts, H_A * DH), _MXU),
        compiler_params=_params(("parallel", "arbitrary")),
        name="dsa_sample_attn",
    )(page_table, zs, zs, zs, neg_own, neg, bias, *([cache_k] * ga), *([cache_v] * ga))


def _moba_s_body(pt_ref, qb_ref, kn_ref, vn_ref, bias_ref, *refs, n_pg, n_sel, n_blocks):
    kpages = refs[:n_pg]
    vpages = refs[n_pg:2 * n_pg]
    o_ref, gate_ref, mb_ref, lb_ref, ob_ref = refs[2 * n_pg:]
    s = pl.program_id(1)
    n_steps = pl.num_programs(1)
    ts = qb_ref.shape[0]
    rows_pg = kpages[0].shape[0]
    n_grp = H_B // HKV_B
    rows = H_B * ts
    blk = MOBA_BLOCK * HKV_B
    bps = n_pg * rows_pg // blk
    scale = DH ** -0.5
    wb = bias_ref.shape[2]
    own_w = HKV_B * _LANES
    qf = [jnp.concatenate([qb_ref[:, (g * n_grp + hh) * DH:(g * n_grp + hh + 1) * DH] for hh in range(n_grp)], axis=0)
          for g in range(HKV_B)]
    qs = [(q * scale).astype(_MXU) for q in qf]
    q_all = _head_rows(qs)
    lane = lax.broadcasted_iota(jnp.int32, (rows, gate_ref.shape[1]), 1)
    last = s == n_steps - 1

    @pl.when(s == 0)
    def _():
        gate_ref[...] = jnp.zeros(gate_ref.shape, jnp.float32)
        mb_ref[...] = jnp.full(mb_ref.shape, _NEG, jnp.float32)
        lb_ref[...] = jnp.zeros(lb_ref.shape, jnp.float32)

    kf = jnp.concatenate([kpages[p][...] for p in range(n_pg)], axis=0)
    vt = jnp.concatenate([vpages[p][...] for p in range(n_pg)], axis=0).astype(_MXU)
    logits = lax.dot_general(q_all, kf.astype(_MXU), (((1,), (1,)), ((), ())), preferred_element_type=jnp.float32)
    far = _head_rows([bias_ref[h, :, 0:1] for h in range(H_B)])
    near = _head_rows([bias_ref[h, :, wb - own_w - blk:wb - own_w] for h in range(H_B)])
    parity = _parity_neg(H_B, HKV_B, ts, blk)
    sub_kv = lax.broadcasted_iota(jnp.int32, (8, DH), 0) % HKV_B
    gates, ms, ls, ps_ = gate_ref[...], mb_ref[...], lb_ref[...], []
    for jb in range(bps):
        seg = logits[:, jb * blk:(jb + 1) * blk] + (far + parity)
        if jb == bps - 1:
            seg = seg + jnp.where(last, near - far, 0.0)
        ksum = jnp.sum(kf[jb * blk:(jb + 1) * blk].reshape(blk // 8, 8, DH), axis=0)
        gate = _head_rows([
            jnp.sum(qf[g] * (jnp.sum(jnp.where(sub_kv == g, ksum, 0.0), axis=0, keepdims=True) / MOBA_BLOCK),
                    axis=1, keepdims=True) for g in range(HKV_B)])
        mj = jnp.max(seg, axis=1, keepdims=True)
        p = jnp.exp(seg - mj)
        lj = jnp.sum(p, axis=1, keepdims=True)
        here = lane == s * bps + jb
        gates = jnp.where(here, gate, gates)
        ms = jnp.where(here, mj, ms)
        ls = jnp.where(here, lj, ls)
        ps_.append(jnp.concatenate([p if k == jb else jnp.zeros_like(p) for k in range(bps)], axis=1))
    gate_ref[...], mb_ref[...], lb_ref[...] = gates, ms, ls
    o_blocks = jnp.dot(jnp.concatenate(ps_, axis=0).astype(_MXU), vt, preferred_element_type=jnp.float32)
    for jb in range(bps):
        ob_ref[s * bps + jb] = o_blocks[jb * rows:(jb + 1) * rows]

    @pl.when(last)
    def _():
        r = lax.broadcasted_iota(jnp.int32, (ts, _LANES), 0)
        c = lax.broadcasted_iota(jnp.int32, (ts, _LANES), 1)
        causal_neg = jnp.where(c <= r, 0.0, _NEG)
        m_own, l_own, o_own = _own_block(lambda g: qs[g], kn_ref, vn_ref,
                                         lambda h: bias_ref[h, :, wb - own_w:wb - own_w + _LANES] + causal_neg,
                                         HKV_B, n_grp, ts)
        chosen = _topk_lanes(gate_ref[...], n_blocks, n_sel)
        mb = jnp.where(chosen, mb_ref[...], _NEG)
        m_all = jnp.maximum(m_own, jnp.max(mb, axis=1, keepdims=True))
        w = jnp.where(chosen, jnp.exp(mb - m_all), 0.0)
        w_own = jnp.exp(m_own - m_all)
        den = w_own * l_own + jnp.sum(w * lb_ref[...], axis=1, keepdims=True)

        def add_block(j, acc):
            wj = jnp.sum(jnp.where(lane == j, w, 0.0), axis=1, keepdims=True)
            return acc + wj * ob_ref[j]

        num = lax.fori_loop(0, n_blocks, add_block, w_own * o_own)
        o = num / den
        for h in range(H_B):
            o_ref[:, h * DH:(h + 1) * DH] = o[h * ts:(h + 1) * ts].astype(o_ref.dtype)


def _moba_sample(zs, cache_k, cache_v, page_table, bias, layer, n_batch, ts):
    n_pages = page_table.shape[1]
    rows_pg = cache_k.shape[2]
    ps = rows_pg // HKV_B
    past = n_pages * ps
    assert past % MOBA_BLOCK == 0 and MOBA_BLOCK % ps == 0 and ts <= _LANES and ts % 8 == 0
    n_blocks = past // MOBA_BLOCK
    n_sel = min(MOBA_TOPK, (past + ts - 1) // MOBA_BLOCK)
    ppb = MOBA_BLOCK // ps
    g_pg = ppb * _tile(n_blocks, 4, 1)
    lane_w = -(-n_blocks // _LANES) * _LANES
    wkv = HKV_B * DH
    return pl.pallas_call(
        functools.partial(_moba_s_body, n_pg=g_pg, n_sel=n_sel, n_blocks=n_blocks),
        grid_spec=pltpu.PrefetchScalarGridSpec(
            num_scalar_prefetch=1,
            grid=(n_batch, n_pages // g_pg),
            in_specs=[pl.BlockSpec((ts, H_B * DH), lambda b, s, pt: (b, _C_QB // (H_B * DH))),
                      pl.BlockSpec((ts, wkv), lambda b, s, pt: (b, _C_KB // wkv)),
                      pl.BlockSpec((ts, wkv), lambda b, s, pt: (b, _C_VB // wkv)),
                      pl.BlockSpec((H_B, ts, bias.shape[2]), lambda b, s, pt: (1, 0, 0))]
            + _page_specs(g_pg, layer, (rows_pg, DH)) + _page_specs(g_pg, layer, (rows_pg, DH)),
            out_specs=pl.BlockSpec((ts, H_B * DH), lambda b, s, pt: (b, 0)),
            scratch_shapes=[pltpu.VMEM((H_B * ts, lane_w), jnp.float32),
                            pltpu.VMEM((H_B * ts, lane_w), jnp.float32),
                            pltpu.VMEM((H_B * ts, lane_w), jnp.float32),
                            pltpu.VMEM((n_blocks, H_B * ts, DH), jnp.float32)],
        ),
        out_shape=jax.ShapeDtypeStruct((n_batch * ts, H_B * DH), _MXU),
        compiler_params=_params(("parallel", "arbitrary")),
        name="moba_sample",
    )(page_table, zs, zs, zs, bias, *([cache_k] * g_pg), *([cache_v] * g_pg))


def _relayout_w_in(w_in, d):
    w_t = jnp.swapaxes(w_in, 1, 2)
    sizes = dict(qa=H_A * DH, ka=HKV_A * DH, va=HKV_A * DH, qi=H_IDX * D_IDX, ki=D_IDX, wi=H_IDX,
                 qb=H_B * DH, kb=HKV_B * DH, vb=HKV_B * DH, qc=H_C * DK_C, kc=H_C * DK_C, vc=H_C * DV_C,
                 gc=H_C * DV_C, gates=3 * d)
    off, parts = 0, {}
    for name in ("qa", "ka", "va", "qi", "ki", "wi", "qb", "kb", "vb", "qc", "kc", "vc", "gc", "gates"):
        parts[name] = w_t[:, off:off + sizes[name]]
        off += sizes[name]
    assert off == w_t.shape[1]
    main = jnp.concatenate([parts[n] for n in ("vc", "gc", "qa", "qi", "qb", "qc", "kc", "ka", "va", "kb", "vb",
                                               "gates")], axis=1).astype(_MXU)
    kiw = jnp.concatenate([parts["ki"], parts["wi"],
                           jnp.zeros((w_t.shape[0], _LANES - D_IDX - H_IDX, d), w_in.dtype)], axis=1).astype(_MXU)
    return main, kiw


def _pad_last(w, n):
    return jnp.pad(w, [(0, 0)] * (w.ndim - 1) + [(0, n - w.shape[-1])])


def _mix_and_ffn(x, z, o_a, o_b, o_c, wts, layer, ffn_in):
    m = _merge(o_a, o_b, o_c, z, wts["w_pa"], wts["w_pb"], wts["w_pc"], layer)
    x = _matmul(m, wts["w_out"], layer, jnp.float32, residual=x)
    h2 = _rmsnorm(x, wts["norm_ffn"][layer], _MXU)
    a, conv_new = ffn_in(h2)
    x = _matmul(a, wts["ffn_down"], layer, jnp.float32, residual=x, tk_target=3072)
    return x, conv_new


def _state_slices(z, kiw, n_batch, t):
    ka = z[:, _C_KA:_C_KA + HKV_A * DH].reshape(n_batch, t, HKV_A, DH)
    va = z[:, _C_VA:_C_VA + HKV_A * DH].reshape(n_batch, t, HKV_A, DH)
    ki = kiw[:, :D_IDX].reshape(n_batch, t, D_IDX)
    kb = z[:, _C_KB:_C_KB + HKV_B * DH].reshape(n_batch, t, HKV_B, DH)
    vb = z[:, _C_VB:_C_VB + HKV_B * DH].reshape(n_batch, t, HKV_B, DH)
    return ka, va, ki, kb, vb


def kernel(x_prompt, x_sample, cache_a_k, cache_a_v, cache_a_kidx, cache_b_k, cache_b_v, state_ret, state_conv,
           page_table, rel_bias, norm_mix, w_in, ret_gn, w_pa, w_pb, w_pc, w_out, norm_ffn, ffn_up, ffn_gate,
           conv_w, conv_b, ffn_down, norm_final):
    bp, tp, d = x_prompt.shape
    bs, ts, _ = x_sample.shape
    depth = w_in.shape[0]
    n_pool, ps = cache_a_k.shape[1], cache_a_k.shape[2]
    past = page_table.shape[1] * ps
    f = ffn_up.shape[2]
    fp = -(-f // 512) * 512
    tq = _TQ
    assert d % _LANES == 0 and tp % tq == 0

    ar = jnp.arange
    bias_p = _bias_table(rel_bias, (ar(tq)[None, :] - ar(2 * tq)[:, None] + tq).astype(jnp.int32))
    assert H_A == H_B and HKV_A == HKV_B
    rel_cached = (ar(ts)[:, None] + MOBA_BLOCK - ar(MOBA_BLOCK)[None, :]).astype(jnp.int32)
    rel_own = (ar(ts)[:, None] - ar(_LANES)[None, :]).astype(jnp.int32)
    bias_s = _bias_table(rel_bias, jnp.concatenate([jnp.repeat(rel_cached, HKV_A, axis=1), rel_own, rel_own], axis=1))
    cache_a_k = cache_a_k.reshape(depth, n_pool, ps * HKV_A, DH)
    cache_a_v = cache_a_v.reshape(depth, n_pool, ps * HKV_A, DH)
    cache_b_k = cache_b_k.reshape(depth, n_pool, ps * HKV_B, DH)
    cache_b_v = cache_b_v.reshape(depth, n_pool, ps * HKV_B, DH)
    cache_a_kidx = jnp.swapaxes(cache_a_kidx, 2, 3)

    xp = x_prompt.reshape(bp * tp, d)
    xs = x_sample.reshape(bs * ts, d)
    pos_p = jnp.arange(tp, dtype=jnp.int32)
    pos_s = past + jnp.arange(ts, dtype=jnp.int32)
    w_main, w_kiw = _relayout_w_in(w_in, d)
    wts = dict(w_pa=w_pa.astype(_MXU), w_pb=w_pb.astype(_MXU), w_pc=w_pc.astype(_MXU), w_out=w_out.astype(_MXU),
               norm_ffn=norm_ffn, ffn_down=jnp.pad(ffn_down, ((0, 0), (0, fp - f), (0, 0))).astype(_MXU))
    up = _pad_last(ffn_up, fp).astype(_MXU)
    gate = _pad_last(ffn_gate, fp).astype(_MXU)
    cw = _pad_last(conv_w, fp)
    cb = _pad_last(conv_b.reshape(depth, 1, f), fp)
    conv_prev_all = _pad_last(state_conv, fp)

    st_p, st_s, kv_p = [], [], None
    for l in range(depth):
        z, kiw, *kv_p = _in_proj(xp, norm_mix[l], w_main, w_kiw, l, depth, kv_p)
        o_a = _dsa_prompt(z, kiw, bias_p, bp, tp)
        o_b = _moba_prompt(z, bias_p, bp, tp)
        o_c, ret_new = _retention(z, jnp.zeros((bp, H_C, DK_C, DV_C), jnp.float32), ret_gn[l], pos_p, bp, tp)
        xp, conv_new = _mix_and_ffn(xp, z, o_a, o_b, o_c, wts, l,
                                    lambda h2: _ffn_in_prompt(h2, up, gate, cw, cb, l, bp, tp))
        st_p.append((kiw[:, :D_IDX].reshape(bp, tp, D_IDX), ret_new, conv_new[:, :, :f]))

        h = _rmsnorm(xs, norm_mix[l], _MXU)
        z = _matmul_few_rows(h, w_main, l, jnp.float32)
        kiw = _matmul_few_rows(h, w_kiw, l, jnp.float32)
        o_a = _dsa_sample(z, kiw, cache_a_k, cache_a_v, cache_a_kidx, page_table, bias_s, l, bs, ts)
        o_b = _moba_sample(z, cache_b_k, cache_b_v, page_table, bias_s, l, bs, ts)
        o_c, ret_new = _retention(z, state_ret[l], ret_gn[l], pos_s, bs, ts)
        xs, conv_new = _mix_and_ffn(xs, z, o_a, o_b, o_c, wts, l,
                                    lambda h2: _ffn_in_sample(h2, up, gate, cw, cb, l, conv_prev_all[l], bs, ts))
        st_s.append(_state_slices(z, kiw, bs, ts) + (ret_new, conv_new[:, :, :f]))

    y_prompt = _rmsnorm(xp, norm_final, jnp.float32).reshape(bp, tp, d)
    y_sample = _rmsnorm(xs, norm_final, jnp.float32).reshape(bs, ts, d)
    ka_p, va_p, kb_p, vb_p = kv_p
    ki_p, ret_p, conv_p = [jnp.stack(v) for v in zip(*st_p)]
    outs_s = [jnp.stack(v) for v in zip(*st_s)]
    return (y_prompt, y_sample,
            ka_p.reshape(depth, bp, tp, HKV_A, DH), va_p.reshape(depth, bp, tp, HKV_A, DH), ki_p,
            kb_p.reshape(depth, bp, tp, HKV_B, DH), vb_p.reshape(depth, bp, tp, HKV_B, DH), ret_p, conv_p, *outs_s)
```

```python
import functools
import math

import jax
import jax.numpy as jnp
from jax import lax
from jax.experimental import pallas as pl
from jax.experimental.pallas import tpu as pltpu

DH = 128
H_A, HKV_A = 8, 2
H_IDX, D_IDX = 16, 64
TOPK_IDX = 256
H_B, HKV_B = 8, 2
MOBA_BLOCK = 256
MOBA_TOPK = 3
H_C, DK_C, DV_C = 8, 128, 256
N_BUCKETS = 32
MAX_DIST = 128
CONV_W = 3
EPS = 1e-6
ROPE_BASE = 10000.0
RET_CHUNK = 128

_MXU = jnp.bfloat16
_NEG = -1e30
_TQ = 256
_LANES = 128
_VMEM_LIMIT = 56 * 1024 * 1024

_C_VC = 0
_C_GC = _C_VC + H_C * DV_C
_C_QA = _C_GC + H_C * DV_C
_C_QI = _C_QA + H_A * DH
_C_QB = _C_QI + H_IDX * D_IDX
_C_QC = _C_QB + H_B * DH
_C_KC = _C_QC + H_C * DK_C
_C_KA = _C_KC + H_C * DK_C
_C_VA = _C_KA + HKV_A * DH
_C_KB = _C_VA + HKV_A * DH
_C_VB = _C_KB + HKV_B * DH
_C_GATES = _C_VB + HKV_B * DH


def _params(sem):
    return pltpu.CompilerParams(dimension_semantics=sem, vmem_limit_bytes=_VMEM_LIMIT)


def _tile(n, target, mult=_LANES):
    best = None
    t = mult
    while t <= min(n, target):
        if n % t == 0:
            best = t
        t += mult
    return n if best is None else best


def _rmsnorm_body(x_ref, g_ref, o_ref):
    x = x_ref[...]
    y = x * lax.rsqrt(jnp.mean(x * x, axis=-1, keepdims=True) + EPS)
    o_ref[...] = (y * g_ref[...]).astype(o_ref.dtype)


def _rmsnorm(x, g, out_dtype):
    m, d = x.shape
    tm = _tile(m, 512, 8)
    return pl.pallas_call(
        _rmsnorm_body,
        grid=(m // tm,),
        in_specs=[pl.BlockSpec((tm, d), lambda i: (i, 0)), pl.BlockSpec((1, d), lambda i: (0, 0))],
        out_specs=pl.BlockSpec((tm, d), lambda i: (i, 0)),
        out_shape=jax.ShapeDtypeStruct((m, d), out_dtype),
        compiler_params=_params(("parallel",)),
        name="rmsnorm",
    )(x, g.reshape(1, d))


def _dot_nt(x, w):
    return lax.dot_general(x, w, (((1,), (1,)), ((), ())), preferred_element_type=jnp.float32)


def _mm_body(*refs, nk, has_res):
    x_ref, w_ref = refs[0], refs[1]
    r_ref = refs[2] if has_res else None
    o_ref = refs[3] if has_res else refs[2]
    acc_ref = refs[-1]
    k = pl.program_id(2)
    d = jnp.dot(x_ref[...], w_ref[...], preferred_element_type=jnp.float32)

    def finish(r):
        if has_res:
            r = r + r_ref[...]
        o_ref[...] = r.astype(o_ref.dtype)

    if nk == 1:
        finish(d)
        return

    @pl.when(k == 0)
    def _():
        acc_ref[...] = d

    @pl.when(jnp.logical_and(k > 0, k < nk - 1))
    def _():
        acc_ref[...] += d

    @pl.when(k == nk - 1)
    def _():
        finish(acc_ref[...] + d)


def _mm_few_rows_body(x_ref, w_ref, o_ref):
    x = x_ref[...]
    x = jnp.concatenate([x, jnp.zeros((_LANES - x.shape[0], x.shape[1]), x.dtype)], axis=0)
    r_t = lax.dot_general(w_ref[...], x, (((1,), (1,)), ((), ())), preferred_element_type=jnp.float32)
    o_ref[...] = r_t.T[0:o_ref.shape[0]].astype(o_ref.dtype)


def _matmul_few_rows(x, w, layer, out_dtype, tn_target=1024):
    m, kdim = x.shape
    n = w.shape[1]
    tn = _tile(n, tn_target)
    assert m <= _LANES and m % 8 == 0
    return pl.pallas_call(
        _mm_few_rows_body,
        grid=(n // tn,),
        in_specs=[pl.BlockSpec((m, kdim), lambda j: (0, 0)), pl.BlockSpec((None, tn, kdim), lambda j: (layer, j, 0))],
        out_specs=pl.BlockSpec((m, tn), lambda j: (0, j)),
        out_shape=jax.ShapeDtypeStruct((m, n), out_dtype),
        compiler_params=_params(("parallel",)),
        name="matmul_few_rows",
    )(x, w)


def _matmul(x, w, layer, out_dtype, residual=None, tm_target=1024, tn_target=1024, tk_target=2048):
    m, kdim = x.shape
    n = w.shape[2]
    tm = _tile(m, tm_target, 8)
    tn = _tile(n, tn_target)
    tk = _tile(kdim, tk_target)
    nk = kdim // tk
    in_specs = [pl.BlockSpec((tm, tk), lambda i, j, k: (i, k)),
                pl.BlockSpec((None, tk, tn), lambda i, j, k: (layer, k, j))]
    args = [x, w]
    if residual is not None:
        in_specs.append(pl.BlockSpec((tm, tn), lambda i, j, k: (i, j)))
        args.append(residual)
    return pl.pallas_call(
        functools.partial(_mm_body, nk=nk, has_res=residual is not None),
        grid=(m // tm, n // tn, nk),
        in_specs=in_specs,
        out_specs=pl.BlockSpec((tm, tn), lambda i, j, k: (i, j)),
        out_shape=jax.ShapeDtypeStruct((m, n), out_dtype),
        scratch_shapes=[pltpu.VMEM((tm, tn), jnp.float32)] if nk > 1 else [],
        compiler_params=_params(("parallel", "parallel", "arbitrary")),
        name="matmul",
    )(*args)


def _in_proj_body(*refs, n_prev, j_state):
    x_ref, g_ref, w_ref, wk_ref = refs[:4]
    z_ref, kiw_ref, ka_ref, va_ref, kb_ref, vb_ref, h_ref = refs[4 + n_prev:]

    @pl.when(pl.program_id(1) == 0)
    def _():
        x = x_ref[...]
        y = x * lax.rsqrt(jnp.mean(x * x, axis=-1, keepdims=True) + EPS)
        h_ref[...] = (y * g_ref[...]).astype(h_ref.dtype)
        kiw_ref[...] = _dot_nt(h_ref[...], wk_ref[...])

    r = _dot_nt(h_ref[...], w_ref[...])
    z_ref[...] = r

    @pl.when(pl.program_id(1) == j_state)
    def _():
        col = 0
        for ref in (ka_ref, va_ref, kb_ref, vb_ref):
            for g in range(ref.shape[1]):
                ref[:, g, :] = r[:, col:col + DH]
                col += DH


def _in_proj(x, g, w_main, w_kiw, layer, depth, prev_states):
    m, kdim = x.shape
    n = w_main.shape[1]
    tm = _tile(m, 1024, 8)
    tn = 2 * HKV_A * DH + 2 * HKV_B * DH
    assert n % tn == 0 and _C_KA % tn == 0 and _C_VB + HKV_B * DH == _C_KA + tn
    kv_heads = (HKV_A, HKV_A, HKV_B, HKV_B)
    prev = list(prev_states)
    state_spec = lambda nh: pl.BlockSpec((None, tm, nh, DH), lambda i, j: (layer, i, 0, 0))
    return pl.pallas_call(
        functools.partial(_in_proj_body, n_prev=len(prev), j_state=_C_KA // tn),
        grid=(m // tm, n // tn),
        in_specs=[pl.BlockSpec((tm, kdim), lambda i, j: (i, 0)),
                  pl.BlockSpec((1, kdim), lambda i, j: (0, 0)),
                  pl.BlockSpec((None, tn, kdim), lambda i, j: (layer, j, 0)),
                  pl.BlockSpec((None, _LANES, kdim), lambda i, j: (layer, 0, 0))]
        + [pl.BlockSpec(memory_space=pl.ANY)] * len(prev),
        out_specs=[pl.BlockSpec((tm, tn), lambda i, j: (i, j)), pl.BlockSpec((tm, _LANES), lambda i, j: (i, 0))]
        + [state_spec(nh) for nh in kv_heads],
        out_shape=[jax.ShapeDtypeStruct((m, n), jnp.float32), jax.ShapeDtypeStruct((m, _LANES), jnp.float32)]
        + [jax.ShapeDtypeStruct((depth, m, nh, DH), jnp.float32) for nh in kv_heads],
        scratch_shapes=[pltpu.VMEM((tm, kdim), _MXU)],
        input_output_aliases={4 + k: 2 + k for k in range(len(prev))},
        compiler_params=_params(("parallel", "arbitrary")),
        name="in_proj",
    )(x, g.reshape(1, kdim), w_main, w_kiw, *prev)


def _t5_bucket(rel):
    n = jnp.maximum(rel, 0)
    max_exact = N_BUCKETS // 2
    nf = jnp.maximum(n, 1).astype(jnp.float32)
    large = max_exact + (jnp.log(nf / max_exact) / math.log(MAX_DIST / max_exact)
                         * (N_BUCKETS - max_exact)).astype(jnp.int32)
    large = jnp.minimum(large, N_BUCKETS - 1)
    return jnp.where(n < max_exact, n, large)


def _bias_body(tab_ref, bucket_ref, o_ref):
    h = pl.program_id(0)
    bucket = bucket_ref[...]
    acc = jnp.zeros(bucket.shape, jnp.float32)
    for b in range(N_BUCKETS):
        acc = jnp.where(bucket == b, tab_ref[b, h], acc)
    o_ref[...] = acc


def _bias_table(rel_bias, rel):
    bucket = _t5_bucket(rel)
    r, c = rel.shape
    nh = rel_bias.shape[1]
    return pl.pallas_call(
        _bias_body,
        grid=(nh,),
        in_specs=[pl.BlockSpec(memory_space=pltpu.SMEM), pl.BlockSpec((r, c), lambda h: (0, 0))],
        out_specs=pl.BlockSpec((None, r, c), lambda h: (h, 0, 0)),
        out_shape=jax.ShapeDtypeStruct((nh, r, c), jnp.float32),
        compiler_params=_params(("parallel",)),
        name="bias_table",
    )(rel_bias, bucket)


def _sortable(x):
    b = lax.bitcast_convert_type(x, jnp.int32)
    return b ^ ((b >> 31) & jnp.int32(0x7FFFFFFF))


_KEY_NEG_INF = -2147483648 + 0x7FFFFF


def _stack_heads(q_ref, g, n_grp, scale):
    parts = [q_ref[:, (g * n_grp + hh) * DH:(g * n_grp + hh + 1) * DH] for hh in range(n_grp)]
    return (jnp.concatenate(parts, axis=0) * scale).astype(_MXU)


def _flash_update(m_ref, l_ref, acc_ref, g, logits, v):
    m_old = m_ref[g]
    m_new = jnp.maximum(m_old, jnp.max(logits, axis=1, keepdims=True))
    alpha = jnp.exp(m_old - m_new)
    p = jnp.exp(logits - m_new)
    l_ref[g] = alpha * l_ref[g] + jnp.sum(p, axis=1, keepdims=True)
    acc_ref[g] = alpha * acc_ref[g] + jnp.dot(p.astype(_MXU), v, preferred_element_type=jnp.float32)
    m_ref[g] = m_new


def _flash_init(m_ref, l_ref, acc_ref):
    m_ref[...] = jnp.full(m_ref.shape, _NEG, jnp.float32)
    l_ref[...] = jnp.zeros(l_ref.shape, jnp.float32)
    acc_ref[...] = jnp.zeros(acc_ref.shape, jnp.float32)


def _flash_store(o_ref, l_ref, acc_ref, n_kv, n_grp, rows):
    for g in range(n_kv):
        o = acc_ref[g] / l_ref[g]
        for hh in range(n_grp):
            h = g * n_grp + hh
            o_ref[:, h * DH:(h + 1) * DH] = o[hh * rows:(hh + 1) * rows].astype(o_ref.dtype)


def _flash_update_t(m_ref, l_ref, acc_ref, g, logits_t, v_t, extra_of_chunk):
    ps, alphas = [], []
    for c in range(logits_t.shape[1] // _LANES):
        sl = slice(c * _LANES, (c + 1) * _LANES)
        x = logits_t[:, sl] + extra_of_chunk(c)
        m_old = m_ref[g, :, sl]
        m_new = jnp.maximum(m_old, jnp.max(x, axis=0, keepdims=True))
        alpha = jnp.exp(m_old - m_new)
        p = jnp.exp(x - m_new)
        l_ref[g, :, sl] = alpha * l_ref[g, :, sl] + jnp.sum(p, axis=0, keepdims=True)
        m_ref[g, :, sl] = m_new
        ps.append(p.astype(_MXU))
        alphas.append(alpha)
    p_all = jnp.concatenate(ps, axis=1)
    acc_ref[g] = (jnp.concatenate(alphas, axis=1) * acc_ref[g]
                  + jnp.dot(v_t, p_all, preferred_element_type=jnp.float32))


def _flash_store_t(o_ref, l_ref, acc_ref, n_kv, n_grp, rows):
    for g in range(n_kv):
        o = acc_ref[g] / l_ref[g]
        for hh in range(n_grp):
            h = g * n_grp + hh
            o_ref[:, h * DH:(h + 1) * DH] = o[:, hh * rows:(hh + 1) * rows].T.astype(o_ref.dtype)


def _kth_largest_key(count_ge, n_sel, shape):
    int_min = jnp.int32(-2147483648)
    prefix = jnp.full(shape, int_min, jnp.int32)
    zero = jnp.zeros(shape, jnp.int32)
    prefix = jnp.where(count_ge(zero) >= n_sel, zero, prefix)

    def step(it, prefix):
        cand = prefix | (jnp.int32(1) << (30 - it))
        return jnp.where(count_ge(cand) >= n_sel, cand, prefix)

    return lax.fori_loop(0, 31, step, prefix)


def _first_index_cut(count_lt, need, shape, n_idx):
    nbits = max(1, int(n_idx).bit_length())
    cut = jnp.zeros(shape, jnp.int32)

    def step(it, cut):
        cand = cut + (jnp.int32(1) << (nbits - 1 - it))
        return jnp.where(count_lt(cand) < need, cand, cut)

    return lax.fori_loop(0, nbits, step, cut)


def _stage_kv(k_in_ref, v_in_ref, kb_ref, vt_ref, tq):
    for j in range(kb_ref.shape[0]):
        kb_ref[j] = k_in_ref[j * tq:(j + 1) * tq, :].astype(_MXU)
        vt_ref[j] = v_in_ref[j * tq:(j + 1) * tq, :].T.astype(_MXU)


def _attend_t(j, qs, kb_ref, vt_ref, m_ref, l_ref, acc_ref, extra_of_chunk):
    for g in range(len(qs)):
        logits_t = lax.dot_general(kb_ref[j, :, g * DH:(g + 1) * DH], qs[g], (((1,), (1,)), ((), ())),
                                   preferred_element_type=jnp.float32)
        _flash_update_t(m_ref, l_ref, acc_ref, g, logits_t, vt_ref[j, g * DH:(g + 1) * DH, :],
                        functools.partial(extra_of_chunk, g))


def _dsa_p_body(qa_ref, qi_ref, kiwq_ref, kiw_ref, ka_ref, va_ref, bias_ref, o_ref,
                kx_ref, kb_ref, vt_ref, key_ref, neg_ref, m_ref, l_ref, acc_ref, *, n_sel, t):
    i = pl.program_id(1)
    tq = _TQ
    n_grp = H_A // HKV_A
    krow = lax.broadcasted_iota(jnp.int32, (tq, tq), 0)
    qcol = lax.broadcasted_iota(jnp.int32, (tq, tq), 1)

    @pl.when(i == 0)
    def _():
        _stage_kv(ka_ref, va_ref, kb_ref, vt_ref, tq)
        for j in range(kx_ref.shape[0]):
            kx_ref[j] = kiw_ref[j * tq:(j + 1) * tq, 0:D_IDX].astype(_MXU)

    qi = [qi_ref[:, h * D_IDX:(h + 1) * D_IDX].astype(_MXU) for h in range(H_IDX)]
    wt = kiwq_ref[...].T[D_IDX:D_IDX + H_IDX, :] * (D_IDX ** -0.5 * H_IDX ** -0.5)

    def score_tile(j, carry):
        kx = kx_ref[j]
        sc = jnp.zeros((tq, tq), jnp.float32)
        for h in range(H_IDX):
            s = lax.dot_general(kx, qi[h], (((1,), (1,)), ((), ())), preferred_element_type=jnp.float32)
            sc = sc + wt[h:h + 1, :] * jnp.maximum(s, 0.0)
        causal = jnp.logical_or(j < i, krow <= qcol)
        key_ref[j] = _sortable(jnp.where(causal, sc, -jnp.inf))
        return carry

    lax.fori_loop(0, i + 1, score_tile, 0)

    def count_tiles(pred):
        def body(j, acc):
            return acc + jnp.sum(pred(key_ref[j], j).astype(jnp.int32).reshape(tq // 8, 8, tq), axis=0)
        acc = lax.fori_loop(0, i + 1, body, jnp.zeros((8, tq), jnp.int32))
        return jnp.sum(acc, axis=0, keepdims=True)

    thr = _kth_largest_key(lambda c: count_tiles(lambda k, j: k >= c), n_sel, (1, tq))
    n_ge = count_tiles(lambda k, j: k >= thr)
    finite = thr > _KEY_NEG_INF
    tie = jnp.max(jnp.where(jnp.logical_and(finite, n_ge > n_sel), 1, 0)) > 0

    def write_neg(cut):
        def body(j, carry):
            k = key_ref[j]
            kpos = j * tq + krow
            keep = jnp.logical_or(k > thr, jnp.logical_and(k == thr, kpos <= cut))
            keep = jnp.logical_and(keep, jnp.logical_or(j < i, krow <= qcol))
            neg_ref[j] = jnp.where(keep, 0.0, _NEG)
            return carry
        lax.fori_loop(0, i + 1, body, 0)

    def with_ties():
        n_gt = count_tiles(lambda k, j: k > thr)
        need = n_sel - n_gt
        cut = _first_index_cut(
            lambda p: count_tiles(lambda k, j: jnp.logical_and(k == thr, j * tq + krow < p)),
            need, (1, tq), t)
        write_neg(jnp.where(finite, cut, jnp.int32(2 ** 30)))

    def without_ties():
        write_neg(jnp.full((1, tq), 2 ** 30, jnp.int32))

    lax.cond(tie, with_ties, without_ties)

    _flash_init(m_ref, l_ref, acc_ref)
    scale = DH ** -0.5
    qs = [_stack_heads(qa_ref, g, n_grp, scale) for g in range(HKV_A)]

    per_head = tq // _LANES

    def attend(j, bias_of_chunk):
        neg = neg_ref[j]

        def extra(g, c):
            half = slice((c % per_head) * _LANES, (c % per_head + 1) * _LANES)
            return bias_of_chunk(g * n_grp + c // per_head, half) + neg[:, half]
        _attend_t(j, qs, kb_ref, vt_ref, m_ref, l_ref, acc_ref, extra)

    def far(j, carry):
        attend(j, lambda h, half: bias_ref[h, 0:1, tq - 1:tq])
        return carry

    lax.fori_loop(0, jnp.maximum(i - 1, 0), far, 0)

    @pl.when(i >= 1)
    def _():
        attend(i - 1, lambda h, half: bias_ref[h, 0:tq, half])

    attend(i, lambda h, half: bias_ref[h, tq:2 * tq, half])
    _flash_store_t(o_ref, l_ref, acc_ref, HKV_A, n_grp, tq)


def _kv_scratch(nk, tq, n_kv, n_grp):
    return [
        pltpu.VMEM((nk, tq, n_kv * DH), _MXU),
        pltpu.VMEM((nk, n_kv * DH, tq), _MXU),
    ], [
        pltpu.VMEM((n_kv, 1, n_grp * tq), jnp.float32),
        pltpu.VMEM((n_kv, 1, n_grp * tq), jnp.float32),
        pltpu.VMEM((n_kv, DH, n_grp * tq), jnp.float32),
    ]


def _dsa_prompt(z, kiw, bias, n_batch, t):
    tq = _TQ
    nq = t // tq
    n_sel = min(TOPK_IDX, t // 4)
    assert n_sel <= tq and t % tq == 0
    n_grp = H_A // HKV_A
    kv_scratch, flash_scratch = _kv_scratch(nq, tq, HKV_A, n_grp)
    return pl.pallas_call(
        functools.partial(_dsa_p_body, n_sel=n_sel, t=t),
        grid=(n_batch, nq),
        in_specs=[
            pl.BlockSpec((tq, H_A * DH), lambda b, i: (b * nq + i, _C_QA // (H_A * DH))),
            pl.BlockSpec((tq, H_IDX * D_IDX), lambda b, i: (b * nq + i, _C_QI // (H_IDX * D_IDX))),
            pl.BlockSpec((tq, _LANES), lambda b, i: (b * nq + i, 0)),
            pl.BlockSpec((t, _LANES), lambda b, i: (b, 0)),
            pl.BlockSpec((t, HKV_A * DH), lambda b, i: (b, _C_KA // (HKV_A * DH))),
            pl.BlockSpec((t, HKV_A * DH), lambda b, i: (b, _C_VA // (HKV_A * DH))),
            pl.BlockSpec((H_A, 2 * tq, tq), lambda b, i: (0, 0, 0)),
        ],
        out_specs=pl.BlockSpec((tq, H_A * DH), lambda b, i: (b * nq + i, 0)),
        out_shape=jax.ShapeDtypeStruct((n_batch * t, H_A * DH), _MXU),
        scratch_shapes=[pltpu.VMEM((nq, tq, D_IDX), _MXU)] + kv_scratch + [
            pltpu.VMEM((nq, tq, tq), jnp.int32),
            pltpu.VMEM((nq, tq, tq), jnp.float32),
        ] + flash_scratch,
        compiler_params=_params(("parallel", "arbitrary")),
        name="dsa_prompt",
    )(z, z, kiw, kiw, z, z, bias)


def _topk_lanes(gate, n_cand, n_sel):
    lane = lax.broadcasted_iota(jnp.int32, gate.shape, 1)
    live = lane < n_cand
    chosen = jnp.zeros(gate.shape, jnp.bool_)
    big = jnp.int32(2 ** 30)
    for _ in range(n_sel):
        cand = jnp.logical_and(live, jnp.logical_not(chosen))
        best = jnp.max(jnp.where(cand, gate, -jnp.inf), axis=1, keepdims=True)
        first = jnp.min(jnp.where(jnp.logical_and(cand, gate == best), lane, big), axis=1, keepdims=True)
        chosen = jnp.logical_or(chosen, lane == first)
    return chosen


def _topk_rows(gate, n_cand, n_sel):
    row = lax.broadcasted_iota(jnp.int32, gate.shape, 0)
    live = row < n_cand
    chosen = jnp.zeros(gate.shape, jnp.bool_)
    big = jnp.int32(2 ** 30)
    for _ in range(n_sel):
        cand = jnp.logical_and(live, jnp.logical_not(chosen))
        best = jnp.max(jnp.where(cand, gate, -jnp.inf), axis=0, keepdims=True)
        first = jnp.min(jnp.where(jnp.logical_and(cand, gate == best), row, big), axis=0, keepdims=True)
        chosen = jnp.logical_or(chosen, row == first)
    return chosen


def _moba_p_body(qb_ref, k_in_ref, v_in_ref, bias_ref, o_ref, kmean_ref, kb_ref, vt_ref, allow_ref,
                 m_ref, l_ref, acc_ref, *, n_sel):
    i = pl.program_id(1)
    tq = _TQ
    nb = kb_ref.shape[0]
    n_grp = H_B // HKV_B
    krow = lax.broadcasted_iota(jnp.int32, (tq, tq), 0)
    qcol = lax.broadcasted_iota(jnp.int32, (tq, tq), 1)

    @pl.when(i == 0)
    def _():
        _stage_kv(k_in_ref, v_in_ref, kb_ref, vt_ref, tq)
        kmean_ref[...] = jnp.zeros(kmean_ref.shape, jnp.float32)
        for j in range(nb):
            kmean_ref[j:j + 1, :] = jnp.mean(k_in_ref[j * tq:(j + 1) * tq, :], axis=0, keepdims=True)

    for g in range(HKV_B):
        qg = jnp.concatenate([qb_ref[:, (g * n_grp + hh) * DH:(g * n_grp + hh + 1) * DH] for hh in range(n_grp)],
                             axis=0)
        gate_t = lax.dot_general(kmean_ref[:, g * DH:(g + 1) * DH], qg, (((1,), (1,)), ((), ())),
                                 preferred_element_type=jnp.float32, precision=lax.Precision.HIGHEST)
        allow_ref[g] = jnp.where(_topk_rows(gate_t, i, n_sel), 0.0, _NEG)

    _flash_init(m_ref, l_ref, acc_ref)
    scale = DH ** -0.5
    qs = [_stack_heads(qb_ref, g, n_grp, scale) for g in range(HKV_B)]
    causal_neg = jnp.where(krow <= qcol, 0.0, _NEG)
    per_head = tq // _LANES

    def head_half(g, c):
        return g * n_grp + c // per_head, slice((c % per_head) * _LANES, (c % per_head + 1) * _LANES)

    def attend_past(j, bias_of_chunk):
        allow = [allow_ref[g, pl.ds(j, 1), :] for g in range(HKV_B)]

        def extra(g, c):
            return bias_of_chunk(*head_half(g, c)) + allow[g][:, c * _LANES:(c + 1) * _LANES]
        _attend_t(j, qs, kb_ref, vt_ref, m_ref, l_ref, acc_ref, extra)

    def far(j, carry):
        attend_past(j, lambda h, half: bias_ref[h, 0:1, tq - 1:tq])
        return carry

    lax.fori_loop(0, jnp.maximum(i - 1, 0), far, 0)

    @pl.when(i >= 1)
    def _():
        attend_past(i - 1, lambda h, half: bias_ref[h, 0:tq, half])

    def extra_own(g, c):
        h, half = head_half(g, c)
        return bias_ref[h, tq:2 * tq, half] + causal_neg[:, half]

    _attend_t(i, qs, kb_ref, vt_ref, m_ref, l_ref, acc_ref, extra_own)
    _flash_store_t(o_ref, l_ref, acc_ref, HKV_B, n_grp, tq)


def _moba_prompt(z, bias, n_batch, t):
    tq = _TQ
    assert t % tq == 0 and MOBA_BLOCK == tq
    nb = t // tq
    n_sel = min(MOBA_TOPK, (t - 1) // MOBA_BLOCK)
    n_grp = H_B // HKV_B
    nb_pad = -(-nb // 8) * 8
    kv_scratch, flash_scratch = _kv_scratch(nb, tq, HKV_B, n_grp)
    return pl.pallas_call(
        functools.partial(_moba_p_body, n_sel=n_sel),
        grid=(n_batch, nb),
        in_specs=[
            pl.BlockSpec((tq, H_B * DH), lambda b, i: (b * nb + i, _C_QB // (H_B * DH))),
            pl.BlockSpec((t, HKV_B * DH), lambda b, i: (b, _C_KB // (HKV_B * DH))),
            pl.BlockSpec((t, HKV_B * DH), lambda b, i: (b, _C_VB // (HKV_B * DH))),
            pl.BlockSpec((H_B, 2 * tq, tq), lambda b, i: (1, 0, 0)),
        ],
        out_specs=pl.BlockSpec((tq, H_B * DH), lambda b, i: (b * nb + i, 0)),
        out_shape=jax.ShapeDtypeStruct((n_batch * t, H_B * DH), _MXU),
        scratch_shapes=[pltpu.VMEM((nb_pad, HKV_B * DH), jnp.float32)] + kv_scratch
        + [pltpu.VMEM((HKV_B, nb_pad, n_grp * tq), jnp.float32)] + flash_scratch,
        compiler_params=_params(("parallel", "arbitrary")),
        name="moba_prompt",
    )(z, z, z, bias)


def _ret_body(q_ref, k_ref, v_ref, g_ref, cos_ref, sin_ref, intra_ref, cross_ref, tail_ref, decay_ref, gn_ref,
              s0_ref, o_ref, s_out_ref, s_ref):
    t = pl.program_id(1)

    @pl.when(t == 0)
    def _():
        s_ref[...] = s0_ref[...]

    cos = cos_ref[...]
    sin = sin_ref[...]
    for h in range(H_C):
        qh = q_ref[:, h * DK_C:(h + 1) * DK_C]
        kh = k_ref[:, h * DK_C:(h + 1) * DK_C]
        q = qh * cos + pltpu.roll(qh, DK_C // 2, axis=1) * sin
        k = (kh * cos + pltpu.roll(kh, DK_C // 2, axis=1) * sin) * (DK_C ** -0.5)
        v = v_ref[:, h * DV_C:(h + 1) * DV_C].astype(_MXU)
        s = s_ref[h]
        qm = q.astype(_MXU)
        att = lax.dot_general(qm, k.astype(_MXU), (((1,), (1,)), ((), ())),
                              preferred_element_type=jnp.float32) * intra_ref[h]
        o = (jnp.dot(att.astype(_MXU), v, preferred_element_type=jnp.float32)
             + jnp.dot(qm, s.astype(_MXU), preferred_element_type=jnp.float32) * cross_ref[h])
        kt = (k * tail_ref[h]).astype(_MXU)
        s_ref[h] = s * decay_ref[h] + lax.dot_general(kt, v, (((0,), (0,)), ((), ())),
                                                      preferred_element_type=jnp.float32)
        o = o * lax.rsqrt(jnp.mean(o * o, axis=-1, keepdims=True) + EPS)
        gate = g_ref[:, h * DV_C:(h + 1) * DV_C]
        o = o * gn_ref[:, h * DV_C:(h + 1) * DV_C] * (gate * jax.nn.sigmoid(gate))
        o_ref[:, h * DV_C:(h + 1) * DV_C] = o.astype(o_ref.dtype)

    @pl.when(t == pl.num_programs(1) - 1)
    def _():
        s_out_ref[...] = s_ref[...]


def _retention(z, s0, ret_gn, pos, n_batch, t):
    c = math.gcd(t, RET_CHUNK)
    nc = t // c
    half = DK_C // 2
    inv = ROPE_BASE ** (-jnp.arange(0, DK_C, 2, dtype=jnp.float32) / DK_C)
    ang = pos.astype(jnp.float32)[:, None] * inv[None, :]
    cos = jnp.concatenate([jnp.cos(ang), jnp.cos(ang)], axis=1)
    sin = jnp.concatenate([-jnp.sin(ang), jnp.sin(ang)], axis=1)
    assert cos.shape == (t, 2 * half)
    log_g = jnp.log1p(-jnp.exp2(-5.0 - jnp.arange(H_C, dtype=jnp.float32)))
    idx = jnp.arange(c, dtype=jnp.float32)
    diff = idx[:, None] - idx[None, :]
    intra = jnp.where(diff[None] >= 0, jnp.exp(jnp.maximum(diff, 0.0)[None] * log_g[:, None, None]), 0.0)
    cross = jnp.exp((idx[None, :] + 1.0) * log_g[:, None])[:, :, None]
    tail = jnp.exp((c - 1.0 - idx)[None, :] * log_g[:, None])[:, :, None]
    decay = jnp.exp(c * log_g)[:, None, None]
    wq = H_C * DK_C
    wv = H_C * DV_C
    return pl.pallas_call(
        _ret_body,
        grid=(n_batch, nc),
        in_specs=[
            pl.BlockSpec((c, wq), lambda b, i: (b * nc + i, _C_QC // wq)),
            pl.BlockSpec((c, wq), lambda b, i: (b * nc + i, _C_KC // wq)),
            pl.BlockSpec((c, wv), lambda b, i: (b * nc + i, _C_VC // wv)),
            pl.BlockSpec((c, wv), lambda b, i: (b * nc + i, _C_GC // wv)),
            pl.BlockSpec((c, DK_C), lambda b, i: (i, 0)),
            pl.BlockSpec((c, DK_C), lambda b, i: (i, 0)),
            pl.BlockSpec((H_C, c, c), lambda b, i: (0, 0, 0)),
            pl.BlockSpec((H_C, c, 1), lambda b, i: (0, 0, 0)),
            pl.BlockSpec((H_C, c, 1), lambda b, i: (0, 0, 0)),
            pl.BlockSpec((H_C, 1, 1), lambda b, i: (0, 0, 0)),
            pl.BlockSpec((1, wv), lambda b, i: (0, 0)),
            pl.BlockSpec((None, H_C, DK_C, DV_C), lambda b, i: (b, 0, 0, 0)),
        ],
        out_specs=[
            pl.BlockSpec((c, wv), lambda b, i: (b * nc + i, 0)),
            pl.BlockSpec((None, H_C, DK_C, DV_C), lambda b, i: (b, 0, 0, 0)),
        ],
        out_shape=[
            jax.ShapeDtypeStruct((n_batch * t, wv), _MXU),
            jax.ShapeDtypeStruct((n_batch, H_C, DK_C, DV_C), jnp.float32),
        ],
        scratch_shapes=[pltpu.VMEM((H_C, DK_C, DV_C), jnp.float32)],
        compiler_params=_params(("parallel", "arbitrary")),
        name="retention",
    )(z, z, z, z, cos, sin, intra, cross, tail, decay, ret_gn.reshape(1, wv), s0)


def _merge_body(oa_ref, ob_ref, oc_ref, wa_ref, wb_ref, wc_ref, ga_ref, gb_ref, gc_ref, o_ref):
    def term(o, w, g):
        return jax.nn.sigmoid(g[...]) * jnp.dot(o[...], w[...], preferred_element_type=jnp.float32)
    o_ref[...] = (term(oa_ref, wa_ref, ga_ref) + term(ob_ref, wb_ref, gb_ref)
                  + term(oc_ref, wc_ref, gc_ref)).astype(o_ref.dtype)


def _merge(o_a, o_b, o_c, z, w_pa, w_pb, w_pc, layer):
    m = o_a.shape[0]
    d = w_pa.shape[2]
    tm = _tile(m, 1024, 8)
    tn = _tile(d, 512)
    gate_blk = [(_C_GATES + k * d) // tn for k in range(3)]
    assert all((_C_GATES + k * d) % tn == 0 for k in range(3))
    row = lambda w: pl.BlockSpec((tm, w), lambda i, j: (i, 0))
    col = lambda w: pl.BlockSpec((None, w, tn), lambda i, j: (layer, 0, j))
    gate = lambda k: pl.BlockSpec((tm, tn), lambda i, j: (i, gate_blk[k] + j))
    return pl.pallas_call(
        _merge_body,
        grid=(m // tm, d // tn),
        in_specs=[row(o_a.shape[1]), row(o_b.shape[1]), row(o_c.shape[1]),
                  col(w_pa.shape[1]), col(w_pb.shape[1]), col(w_pc.shape[1]), gate(0), gate(1), gate(2)],
        out_specs=pl.BlockSpec((tm, tn), lambda i, j: (i, j)),
        out_shape=jax.ShapeDtypeStruct((m, d), _MXU),
        compiler_params=_params(("parallel", "parallel")),
        name="merge",
    )(o_a, o_b, o_c, w_pa, w_pb, w_pc, z, z, z)


def _conv_gate(u, u1, u2, g, cw_ref, cb_ref):
    c = cb_ref[...] + cw_ref[0:1, :] * u2 + cw_ref[1:2, :] * u1 + cw_ref[2:3, :] * u
    return 0.5 * c * (1.0 + lax.erf(c * (2.0 ** -0.5))) * g


def _ffn_p_body(h_ref, wu_ref, wg_ref, cw_ref, cb_ref, a_ref, cs_ref, tail_ref, *, seq, n_sub):
    i = pl.program_id(1)
    tm = h_ref.shape[0]
    sm = tm // n_sub

    @pl.when((i * tm) % seq == 0)
    def _():
        tail_ref[...] = jnp.zeros_like(tail_ref)

    prev = tail_ref[...]
    row = lax.broadcasted_iota(jnp.int32, (sm, 1), 0)
    for r in range(n_sub):
        h = h_ref[r * sm:(r + 1) * sm, :]
        u = jnp.dot(h, wu_ref[...], preferred_element_type=jnp.float32)
        g = jnp.dot(h, wg_ref[...], preferred_element_type=jnp.float32)
        u1 = jnp.where(row == 0, prev[7:8, :], pltpu.roll(u, 1, axis=0))
        u2 = jnp.where(row == 0, prev[6:7, :], jnp.where(row == 1, prev[7:8, :], pltpu.roll(u, 2, axis=0)))
        a_ref[r * sm:(r + 1) * sm, :] = _conv_gate(u, u1, u2, g, cw_ref, cb_ref).astype(a_ref.dtype)
        prev = u[sm - 8:sm, :]
    tail_ref[...] = prev

    @pl.when((i * tm + tm) % seq == 0)
    def _():
        cs_ref[...] = prev[8 - (CONV_W - 1):8, :]


def _ffn_in_prompt(h, w_up, w_gate, conv_w, conv_b, layer, n_batch, t):
    m, d = h.shape
    f = w_up.shape[2]
    tm = _tile(t, 1024, 8)
    tn = _tile(f, 512)
    per_seq = t // tm
    return pl.pallas_call(
        functools.partial(_ffn_p_body, seq=t, n_sub=4 if tm % 32 == 0 else 1),
        grid=(f // tn, m // tm),
        in_specs=[
            pl.BlockSpec((tm, d), lambda j, i: (i, 0)),
            pl.BlockSpec((None, d, tn), lambda j, i: (layer, 0, j)),
            pl.BlockSpec((None, d, tn), lambda j, i: (layer, 0, j)),
            pl.BlockSpec((None, CONV_W, tn), lambda j, i: (layer, 0, j)),
            pl.BlockSpec((None, 1, tn), lambda j, i: (layer, 0, j)),
        ],
        out_specs=[
            pl.BlockSpec((tm, tn), lambda j, i: (i, j)),
            pl.BlockSpec((None, CONV_W - 1, tn), lambda j, i: (i // per_seq, 0, j)),
        ],
        out_shape=[
            jax.ShapeDtypeStruct((m, f), _MXU),
            jax.ShapeDtypeStruct((n_batch, CONV_W - 1, f), jnp.float32),
        ],
        scratch_shapes=[pltpu.VMEM((8, tn), jnp.float32)],
        compiler_params=_params(("parallel", "arbitrary")),
        name="ffn_in_prompt",
    )(h, w_up, w_gate, conv_w, conv_b)


def _ffn_s_body(h_ref, wu_ref, wg_ref, cw_ref, cb_ref, p1_ref, p2_ref, a_ref, u_ref, *, seq):
    tm = h_ref.shape[0]
    h = h_ref[...]
    u = jnp.dot(h, wu_ref[...], preferred_element_type=jnp.float32)
    g = jnp.dot(h, wg_ref[...], preferred_element_type=jnp.float32)
    pos = lax.broadcasted_iota(jnp.int32, (tm, 1), 0) % seq
    u1 = jnp.where(pos >= 1, pltpu.roll(u, 1, axis=0), p1_ref[...])
    u2 = jnp.where(pos >= 2, pltpu.roll(u, 2, axis=0), p2_ref[...])
    a_ref[...] = _conv_gate(u, u1, u2, g, cw_ref, cb_ref).astype(a_ref.dtype)
    u_ref[...] = u


def _ffn_in_sample(h, w_up, w_gate, conv_w, conv_b, layer, conv_prev, n_batch, t):
    m, d = h.shape
    f = w_up.shape[2]
    assert t >= CONV_W - 1
    tn = _tile(f, 512)
    zeros = jnp.zeros((n_batch, t, f), jnp.float32)
    p1 = zeros.at[:, 0].set(conv_prev[:, 1]).reshape(m, f)
    p2 = zeros.at[:, 0].set(conv_prev[:, 0]).at[:, 1].set(conv_prev[:, 1]).reshape(m, f)
    full = lambda w: pl.BlockSpec((m, w), lambda j: (0, 0))
    colf = lambda r: pl.BlockSpec((r, tn), lambda j: (0, j))
    colw = lambda r: pl.BlockSpec((None, r, tn), lambda j: (layer, 0, j))
    a, u = pl.pallas_call(
        functools.partial(_ffn_s_body, seq=t),
        grid=(f // tn,),
        in_specs=[full(d), colw(d), colw(d), colw(CONV_W), colw(1), colf(m), colf(m)],
        out_specs=[colf(m), colf(m)],
        out_shape=[jax.ShapeDtypeStruct((m, f), _MXU), jax.ShapeDtypeStruct((m, f), jnp.float32)],
        compiler_params=_params(("parallel",)),
        name="ffn_in_sample",
    )(h, w_up, w_gate, conv_w, conv_b, p1, p2)
    return a, u.reshape(n_batch, t, f)[:, t - (CONV_W - 1):]


def _page_specs(n_pages_per_step, layer, page_shape):
    zeros = (0,) * len(page_shape)
    return [pl.BlockSpec((None, None) + tuple(page_shape),
                         functools.partial(lambda b, s, pt, g: (layer, pt[b, s * n_pages_per_step + g]) + zeros, g=g))
            for g in range(n_pages_per_step)]


def _dsa_s_score_body(pt_ref, qi_ref, kiw_ref, *refs, n_pg):
    pages = refs[:n_pg]
    o_ref = refs[n_pg]
    ts = qi_ref.shape[0]
    ps = pages[0].shape[1]
    qst = jnp.concatenate([qi_ref[:, h * D_IDX:(h + 1) * D_IDX] for h in range(H_IDX)], axis=0).astype(_MXU)
    wi = kiw_ref[:, D_IDX:D_IDX + H_IDX] * (D_IDX ** -0.5 * H_IDX ** -0.5)
    for g in range(n_pg):
        s = jnp.dot(qst, pages[g][...].astype(_MXU), preferred_element_type=jnp.float32)
        sc = jnp.zeros((ts, ps), jnp.float32)
        for h in range(H_IDX):
            sc = sc + wi[:, h:h + 1] * jnp.maximum(s[h * ts:(h + 1) * ts], 0.0)
        o_ref[:, g * ps:(g + 1) * ps] = sc


def _dsa_s_select_body(sc_ref, qi_ref, kiw_ref, o_ref, on_ref, key_ref, *, n_sel, n_kv, ts):
    rows = qi_ref.shape[0]
    lp = sc_ref.shape[1]
    qi = qi_ref[...]
    kx = kiw_ref[:, 0:D_IDX].astype(_MXU)
    kx = jnp.concatenate([kx, jnp.zeros((_LANES - rows, D_IDX), _MXU)], axis=0)
    wi = kiw_ref[:, D_IDX:D_IDX + H_IDX] * (D_IDX ** -0.5 * H_IDX ** -0.5)
    sc = jnp.zeros((rows, _LANES), jnp.float32)
    for h in range(H_IDX):
        s = lax.dot_general(qi[:, h * D_IDX:(h + 1) * D_IDX].astype(_MXU), kx, (((1,), (1,)), ((), ())),
                            preferred_element_type=jnp.float32)
        sc = sc + wi[:, h:h + 1] * jnp.maximum(s, 0.0)
    sc = jnp.concatenate([pltpu.roll(sc[b * ts:(b + 1) * ts], (_LANES - b * ts) % _LANES, axis=1)
                          for b in range(rows // ts)], axis=0)
    r = lax.broadcasted_iota(jnp.int32, (rows, _LANES), 0) % ts
    c = lax.broadcasted_iota(jnp.int32, (rows, _LANES), 1)
    key_ref[:, 0:lp] = _sortable(sc_ref[...])
    key_ref[:, lp:lp + _LANES] = _sortable(jnp.where(c <= r, sc, -jnp.inf))

    width = lp + _LANES
    pos = lambda: lax.broadcasted_iota(jnp.int32, (rows, width), 1)
    count = lambda m: jnp.sum(m.astype(jnp.int32), axis=1, keepdims=True)
    thr = _kth_largest_key(lambda t: count(key_ref[...] >= t), n_sel, (rows, 1))
    finite = thr > _KEY_NEG_INF
    n_ge = count(key_ref[...] >= thr)
    tie = jnp.max(jnp.where(jnp.logical_and(finite, n_ge > n_sel), 1, 0)) > 0

    def with_ties():
        need = n_sel - count(key_ref[...] > thr)
        cut = _first_index_cut(lambda p: count(jnp.logical_and(key_ref[...] == thr, pos() < p)), need, (rows, 1),
                               width)
        return jnp.where(finite, cut, jnp.int32(2 ** 30))

    cut = lax.cond(tie, with_ties, lambda: jnp.full((rows, 1), 2 ** 30, jnp.int32))
    key = key_ref[...]
    keep = jnp.logical_or(key > thr, jnp.logical_and(key == thr, pos() <= cut))
    own = keep[:, lp:width]
    on_ref[...] = jnp.where(jnp.logical_and(own, c <= r), 0.0, _NEG)
    keep_f = jnp.where(keep, 1.0, 0.0).astype(_MXU)
    spread = (lax.broadcasted_iota(jnp.int32, (_LANES, n_kv * _LANES), 1) // n_kv
              == lax.broadcasted_iota(jnp.int32, (_LANES, n_kv * _LANES), 0))
    spread = jnp.where(spread, 1.0, 0.0).astype(_MXU)
    for cc in range(lp // _LANES):
        dup = jnp.dot(keep_f[:, cc * _LANES:(cc + 1) * _LANES], spread, preferred_element_type=jnp.float32)
        o_ref[:, cc * n_kv * _LANES:(cc + 1) * n_kv * _LANES] = jnp.where(dup > 0.5, 0.0, _NEG)


def _head_rows(pieces):
    return jnp.concatenate(pieces, axis=0)


def _parity_neg(n_heads, n_kv, ts, width):
    row_kv = lax.broadcasted_iota(jnp.int32, (n_heads * ts, width), 0) // ((n_heads // n_kv) * ts)
    col_kv = lax.broadcasted_iota(jnp.int32, (n_heads * ts, width), 1) % n_kv
    return jnp.where(row_kv == col_kv, 0.0, _NEG)


def _own_block(q_of_group, kn_ref, vn_ref, extra_of_head, n_kv, n_grp, ts):
    pad = jnp.zeros((_LANES - ts, DH), _MXU)
    ms, ls, os_ = [], [], []
    for g in range(n_kv):
        kt = jnp.concatenate([kn_ref[:, g * DH:(g + 1) * DH].astype(_MXU), pad], axis=0)
        vt = jnp.concatenate([vn_ref[:, g * DH:(g + 1) * DH].astype(_MXU), pad], axis=0)
        logits = lax.dot_general(q_of_group(g), kt, (((1,), (1,)), ((), ())), preferred_element_type=jnp.float32)
        logits = logits + _head_rows([extra_of_head(g * n_grp + hh) for hh in range(n_grp)])
        m = jnp.max(logits, axis=1, keepdims=True)
        p = jnp.exp(logits - m)
        ms.append(m)
        ls.append(jnp.sum(p, axis=1, keepdims=True))
        os_.append(jnp.dot(p.astype(_MXU), vt, preferred_element_type=jnp.float32))
    return _head_rows(ms), _head_rows(ls), _head_rows(os_)


def _dsa_s_attn_body(pt_ref, qa_ref, kn_ref, vn_ref, negn_ref, neg_ref, bias_ref, *refs, n_pg):
    kpages = refs[:n_pg]
    vpages = refs[n_pg:2 * n_pg]
    o_ref, m_ref, l_ref, acc_ref = refs[2 * n_pg:]
    s = pl.program_id(1)
    n_steps = pl.num_programs(1)
    ts = qa_ref.shape[0]
    rows_pg = kpages[0].shape[0]
    n_grp = H_A // HKV_A
    scale = DH ** -0.5
    qs = [_stack_heads(qa_ref, g, n_grp, scale) for g in range(HKV_A)]
    q_all = _head_rows(qs)
    wb = bias_ref.shape[2]
    own_w = HKV_A * _LANES

    @pl.when(s == 0)
    def _():
        m, l, o = _own_block(lambda g: qs[g], kn_ref, vn_ref,
                             lambda h: bias_ref[h, :, wb - own_w:wb - own_w + _LANES] + negn_ref[...],
                             HKV_A, n_grp, ts)
        m_ref[0], l_ref[0], acc_ref[0] = m, l, o

    last = s == n_steps - 1
    w = n_pg * rows_pg
    kt = jnp.concatenate([kpages[p][...] for p in range(n_pg)], axis=0).astype(_MXU)
    vt = jnp.concatenate([vpages[p][...] for p in range(n_pg)], axis=0).astype(_MXU)
    logits = lax.dot_general(q_all, kt, (((1,), (1,)), ((), ())), preferred_element_type=jnp.float32)
    neg = neg_ref[...]

    def extra_of_head(h):
        far = bias_ref[h, :, 0:1]
        base = neg + far
        near = jnp.where(last, bias_ref[h, :, wb - own_w - rows_pg:wb - own_w] - far, 0.0)
        return jnp.concatenate([base[:, 0:w - rows_pg], base[:, w - rows_pg:w] + near], axis=1)

    extra = _head_rows([extra_of_head(h) for h in range(H_A)]) + _parity_neg(H_A, HKV_A, ts, w)
    _flash_update(m_ref, l_ref, acc_ref, 0, logits + extra, vt)

    @pl.when(last)
    def _():
        o = acc_ref[0] / l_ref[0]
        for h in range(H_A):
            o_ref[:, h * DH:(h + 1) * DH] = o[h * ts:(h + 1) * ts].astype(o_ref.dtype)


def _dsa_sample(zs, kiws, cache_k, cache_v, cache_kidx, page_table, bias, layer, n_batch, ts):
    n_pages = page_table.shape[1]
    ps = cache_kidx.shape[3]
    past = n_pages * ps
    assert ps == _LANES and ts <= _LANES and ts % 8 == 0 and cache_k.shape[2] == ps * HKV_A
    n_sel = min(TOPK_IDX, (past + ts) // 4)
    n_grp = H_A // HKV_A

    gi = _tile(n_pages, 32, 1)
    scores = pl.pallas_call(
        functools.partial(_dsa_s_score_body, n_pg=gi),
        grid_spec=pltpu.PrefetchScalarGridSpec(
            num_scalar_prefetch=1,
            grid=(n_batch, n_pages // gi),
            in_specs=[pl.BlockSpec((ts, H_IDX * D_IDX), lambda b, s, pt: (b, _C_QI // (H_IDX * D_IDX))),
                      pl.BlockSpec((ts, _LANES), lambda b, s, pt: (b, 0))]
            + _page_specs(gi, layer, (D_IDX, ps)),
            out_specs=pl.BlockSpec((None, ts, gi * ps), lambda b, s, pt: (b, 0, s)),
        ),
        out_shape=jax.ShapeDtypeStruct((n_batch, ts, past), jnp.float32),
        compiler_params=_params(("parallel", "arbitrary")),
        name="dsa_sample_scores",
    )(page_table, zs, kiws, *([cache_kidx] * gi))

    rows = ts * _tile(n_batch, max(1, 32 // ts), 1)
    assert rows <= _LANES
    neg, neg_own = pl.pallas_call(
        functools.partial(_dsa_s_select_body, n_sel=n_sel, n_kv=HKV_A, ts=ts),
        grid=(n_batch * ts // rows,),
        in_specs=[pl.BlockSpec((rows, past), lambda i: (i, 0)),
                  pl.BlockSpec((rows, H_IDX * D_IDX), lambda i: (i, _C_QI // (H_IDX * D_IDX))),
                  pl.BlockSpec((rows, _LANES), lambda i: (i, 0))],
        out_specs=[pl.BlockSpec((rows, HKV_A * past), lambda i: (i, 0)),
                   pl.BlockSpec((rows, _LANES), lambda i: (i, 0))],
        out_shape=[jax.ShapeDtypeStruct((n_batch * ts, HKV_A * past), jnp.float32),
                   jax.ShapeDtypeStruct((n_batch * ts, _LANES), jnp.float32)],
        scratch_shapes=[pltpu.VMEM((rows, past + _LANES), jnp.int32)],
        compiler_params=_params(("parallel",)),
        name="dsa_sample_select",
    )(scores.reshape(n_batch * ts, past), zs, kiws)

    ga = _tile(n_pages, 16, 1)
    wkv = HKV_A * DH
    rows_pg = ps * HKV_A
    return pl.pallas_call(
        functools.partial(_dsa_s_attn_body, n_pg=ga),
        grid_spec=pltpu.PrefetchScalarGridSpec(
            num_scalar_prefetch=1,
            grid=(n_batch, n_pages // ga),
            in_specs=[pl.BlockSpec((ts, H_A * DH), lambda b, s, pt: (b, _C_QA // (H_A * DH))),
                      pl.BlockSpec((ts, wkv), lambda b, s, pt: (b, _C_KA // wkv)),
                      pl.BlockSpec((ts, wkv), lambda b, s, pt: (b, _C_VA // wkv)),
                      pl.BlockSpec((ts, _LANES), lambda b, s, pt: (b, 0)),
                      pl.BlockSpec((ts, ga * rows_pg), lambda b, s, pt: (b, s)),
                      pl.BlockSpec((H_A, ts, bias.shape[2]), lambda b, s, pt: (0, 0, 0))]
            + _page_specs(ga, layer, (rows_pg, DH)) + _page_specs(ga, layer, (rows_pg, DH)),
            out_specs=pl.BlockSpec((ts, H_A * DH), lambda b, s, pt: (b, 0)),
            scratch_shapes=[pltpu.VMEM((1, H_A * ts, 1), jnp.float32),
                            pltpu.VMEM((1, H_A * ts, 1), jnp.float32),
                            pltpu.VMEM((1, H_A * ts, DH), jnp.float32)],
        ),
        out_shape=jax.ShapeDtypeStruct((n_batch * ts, H_A * DH), _MXU),
        compiler_params=_params(("parallel", "arbitrary")),
        name="dsa_sample_attn",
    )(page_table, zs, zs, zs, neg_own, neg, bias, *([cache_k] * ga), *([cache_v] * ga))


def _moba_s_body(pt_ref, qb_ref, kn_ref, vn_ref, bias_ref, *refs, n_pg, n_sel, n_blocks):
    kpages = refs[:n_pg]
    vpages = refs[n_pg:2 * n_pg]
    o_ref, gate_ref, mb_ref, lb_ref, ob_ref = refs[2 * n_pg:]
    s = pl.program_id(1)
    n_steps = pl.num_programs(1)
    ts = qb_ref.shape[0]
    rows_pg = kpages[0].shape[0]
    n_grp = H_B // HKV_B
    rows = H_B * ts
    blk = MOBA_BLOCK * HKV_B
    bps = n_pg * rows_pg // blk
    scale = DH ** -0.5
    wb = bias_ref.shape[2]
    own_w = HKV_B * _LANES
    qf = [jnp.concatenate([qb_ref[:, (g * n_grp + hh) * DH:(g * n_grp + hh + 1) * DH] for hh in range(n_grp)], axis=0)
          for g in range(HKV_B)]
    qs = [(q * scale).astype(_MXU) for q in qf]
    q_all = _head_rows(qs)
    lane = lax.broadcasted_iota(jnp.int32, (rows, gate_ref.shape[1]), 1)
    last = s == n_steps - 1

    @pl.when(s == 0)
    def _():
        gate_ref[...] = jnp.zeros(gate_ref.shape, jnp.float32)
        mb_ref[...] = jnp.full(mb_ref.shape, _NEG, jnp.float32)
        lb_ref[...] = jnp.zeros(lb_ref.shape, jnp.float32)

    kf = jnp.concatenate([kpages[p][...] for p in range(n_pg)], axis=0)
    vt = jnp.concatenate([vpages[p][...] for p in range(n_pg)], axis=0).astype(_MXU)
    logits = lax.dot_general(q_all, kf.astype(_MXU), (((1,), (1,)), ((), ())), preferred_element_type=jnp.float32)
    far = _head_rows([bias_ref[h, :, 0:1] for h in range(H_B)])
    near = _head_rows([bias_ref[h, :, wb - own_w - blk:wb - own_w] for h in range(H_B)])
    parity = _parity_neg(H_B, HKV_B, ts, blk)
    sub_kv = lax.broadcasted_iota(jnp.int32, (8, DH), 0) % HKV_B
    gates, ms, ls = gate_ref[...], mb_ref[...], lb_ref[...]
    per_dot = 4 if bps % 4 == 0 else bps
    for j0 in range(0, bps, per_dot):
        ps_ = []
        for jb in range(j0, j0 + per_dot):
            seg = logits[:, jb * blk:(jb + 1) * blk] + (far + parity)
            if jb == bps - 1:
                seg = seg + jnp.where(last, near - far, 0.0)
            ksum = jnp.sum(kf[jb * blk:(jb + 1) * blk].reshape(blk // 8, 8, DH), axis=0)
            gate = _head_rows([
                jnp.sum(qf[g] * (jnp.sum(jnp.where(sub_kv == g, ksum, 0.0), axis=0, keepdims=True) / MOBA_BLOCK),
                        axis=1, keepdims=True) for g in range(HKV_B)])
            mj = jnp.max(seg, axis=1, keepdims=True)
            p = jnp.exp(seg - mj)
            lj = jnp.sum(p, axis=1, keepdims=True)
            here = lane == s * bps + jb
            gates = jnp.where(here, gate, gates)
            ms = jnp.where(here, mj, ms)
            ls = jnp.where(here, lj, ls)
            ps_.append(jnp.concatenate([p if k == jb else jnp.zeros_like(p) for k in range(j0, j0 + per_dot)],
                                       axis=1))
        o_blocks = jnp.dot(jnp.concatenate(ps_, axis=0).astype(_MXU), vt[j0 * blk:(j0 + per_dot) * blk],
                           preferred_element_type=jnp.float32)
        for jb in range(j0, j0 + per_dot):
            ob_ref[s * bps + jb] = o_blocks[(jb - j0) * rows:(jb - j0 + 1) * rows]
    gate_ref[...], mb_ref[...], lb_ref[...] = gates, ms, ls

    @pl.when(last)
    def _():
        r = lax.broadcasted_iota(jnp.int32, (ts, _LANES), 0)
        c = lax.broadcasted_iota(jnp.int32, (ts, _LANES), 1)
        causal_neg = jnp.where(c <= r, 0.0, _NEG)
        m_own, l_own, o_own = _own_block(lambda g: qs[g], kn_ref, vn_ref,
                                         lambda h: bias_ref[h, :, wb - own_w:wb - own_w + _LANES] + causal_neg,
                                         HKV_B, n_grp, ts)
        chosen = _topk_lanes(gate_ref[...], n_blocks, n_sel)
        mb = jnp.where(chosen, mb_ref[...], _NEG)
        m_all = jnp.maximum(m_own, jnp.max(mb, axis=1, keepdims=True))
        w = jnp.where(chosen, jnp.exp(mb - m_all), 0.0)
        w_own = jnp.exp(m_own - m_all)
        den = w_own * l_own + jnp.sum(w * lb_ref[...], axis=1, keepdims=True)

        def add_block(j, acc):
            wj = jnp.sum(jnp.where(lane == j, w, 0.0), axis=1, keepdims=True)
            return acc + wj * ob_ref[j]

        num = lax.fori_loop(0, n_blocks, add_block, w_own * o_own)
        o = num / den
        for h in range(H_B):
            o_ref[:, h * DH:(h + 1) * DH] = o[h * ts:(h + 1) * ts].astype(o_ref.dtype)


def _moba_sample(zs, cache_k, cache_v, page_table, bias, layer, n_batch, ts):
    n_pages = page_table.shape[1]
    rows_pg = cache_k.shape[2]
    ps = rows_pg // HKV_B
    past = n_pages * ps
    assert past % MOBA_BLOCK == 0 and MOBA_BLOCK % ps == 0 and ts <= _LANES and ts % 8 == 0
    n_blocks = past // MOBA_BLOCK
    n_sel = min(MOBA_TOPK, (past + ts - 1) // MOBA_BLOCK)
    ppb = MOBA_BLOCK // ps
    g_pg = ppb * _tile(n_blocks, 8, 1)
    lane_w = -(-n_blocks // _LANES) * _LANES
    wkv = HKV_B * DH
    return pl.pallas_call(
        functools.partial(_moba_s_body, n_pg=g_pg, n_sel=n_sel, n_blocks=n_blocks),
        grid_spec=pltpu.PrefetchScalarGridSpec(
            num_scalar_prefetch=1,
            grid=(n_batch, n_pages // g_pg),
            in_specs=[pl.BlockSpec((ts, H_B * DH), lambda b, s, pt: (b, _C_QB // (H_B * DH))),
                      pl.BlockSpec((ts, wkv), lambda b, s, pt: (b, _C_KB // wkv)),
                      pl.BlockSpec((ts, wkv), lambda b, s, pt: (b, _C_VB // wkv)),
                      pl.BlockSpec((H_B, ts, bias.shape[2]), lambda b, s, pt: (1, 0, 0))]
            + _page_specs(g_pg, layer, (rows_pg, DH)) + _page_specs(g_pg, layer, (rows_pg, DH)),
            out_specs=pl.BlockSpec((ts, H_B * DH), lambda b, s, pt: (b, 0)),
            scratch_shapes=[pltpu.VMEM((H_B * ts, lane_w), jnp.float32),
                            pltpu.VMEM((H_B * ts, lane_w), jnp.float32),
                            pltpu.VMEM((H_B * ts, lane_w), jnp.float32),
                            pltpu.VMEM((n_blocks, H_B * ts, DH), jnp.float32)],
        ),
        out_shape=jax.ShapeDtypeStruct((n_batch * ts, H_B * DH), _MXU),
        compiler_params=_params(("parallel", "arbitrary")),
        name="moba_sample",
    )(page_table, zs, zs, zs, bias, *([cache_k] * g_pg), *([cache_v] * g_pg))


def _relayout_w_in(w_in, d):
    w_t = jnp.swapaxes(w_in, 1, 2)
    sizes = dict(qa=H_A * DH, ka=HKV_A * DH, va=HKV_A * DH, qi=H_IDX * D_IDX, ki=D_IDX, wi=H_IDX,
                 qb=H_B * DH, kb=HKV_B * DH, vb=HKV_B * DH, qc=H_C * DK_C, kc=H_C * DK_C, vc=H_C * DV_C,
                 gc=H_C * DV_C, gates=3 * d)
    off, parts = 0, {}
    for name in ("qa", "ka", "va", "qi", "ki", "wi", "qb", "kb", "vb", "qc", "kc", "vc", "gc", "gates"):
        parts[name] = w_t[:, off:off + sizes[name]]
        off += sizes[name]
    assert off == w_t.shape[1]
    main = jnp.concatenate([parts[n] for n in ("vc", "gc", "qa", "qi", "qb", "qc", "kc", "ka", "va", "kb", "vb",
                                               "gates")], axis=1).astype(_MXU)
    kiw = jnp.concatenate([parts["ki"], parts["wi"],
                           jnp.zeros((w_t.shape[0], _LANES - D_IDX - H_IDX, d), w_in.dtype)], axis=1).astype(_MXU)
    return main, kiw


def _pad_last(w, n):
    return jnp.pad(w, [(0, 0)] * (w.ndim - 1) + [(0, n - w.shape[-1])])


def _mix_and_ffn(x, z, o_a, o_b, o_c, wts, layer, ffn_in):
    m = _merge(o_a, o_b, o_c, z, wts["w_pa"], wts["w_pb"], wts["w_pc"], layer)
    x = _matmul(m, wts["w_out"], layer, jnp.float32, residual=x)
    h2 = _rmsnorm(x, wts["norm_ffn"][layer], _MXU)
    a, conv_new = ffn_in(h2)
    x = _matmul(a, wts["ffn_down"], layer, jnp.float32, residual=x, tk_target=3072)
    return x, conv_new


def _state_slices(z, kiw, n_batch, t):
    ka = z[:, _C_KA:_C_KA + HKV_A * DH].reshape(n_batch, t, HKV_A, DH)
    va = z[:, _C_VA:_C_VA + HKV_A * DH].reshape(n_batch, t, HKV_A, DH)
    ki = kiw[:, :D_IDX].reshape(n_batch, t, D_IDX)
    kb = z[:, _C_KB:_C_KB + HKV_B * DH].reshape(n_batch, t, HKV_B, DH)
    vb = z[:, _C_VB:_C_VB + HKV_B * DH].reshape(n_batch, t, HKV_B, DH)
    return ka, va, ki, kb, vb


def kernel(x_prompt, x_sample, cache_a_k, cache_a_v, cache_a_kidx, cache_b_k, cache_b_v, state_ret, state_conv,
           page_table, rel_bias, norm_mix, w_in, ret_gn, w_pa, w_pb, w_pc, w_out, norm_ffn, ffn_up, ffn_gate,
           conv_w, conv_b, ffn_down, norm_final):
    bp, tp, d = x_prompt.shape
    bs, ts, _ = x_sample.shape
    depth = w_in.shape[0]
    n_pool, ps = cache_a_k.shape[1], cache_a_k.shape[2]
    past = page_table.shape[1] * ps
    f = ffn_up.shape[2]
    fp = -(-f // 512) * 512
    tq = _TQ
    assert d % _LANES == 0 and tp % tq == 0

    ar = jnp.arange
    bias_p = _bias_table(rel_bias, (ar(tq)[None, :] - ar(2 * tq)[:, None] + tq).astype(jnp.int32))
    assert H_A == H_B and HKV_A == HKV_B
    rel_cached = (ar(ts)[:, None] + MOBA_BLOCK - ar(MOBA_BLOCK)[None, :]).astype(jnp.int32)
    rel_own = (ar(ts)[:, None] - ar(_LANES)[None, :]).astype(jnp.int32)
    bias_s = _bias_table(rel_bias, jnp.concatenate([jnp.repeat(rel_cached, HKV_A, axis=1), rel_own, rel_own], axis=1))
    cache_a_k = cache_a_k.reshape(depth, n_pool, ps * HKV_A, DH)
    cache_a_v = cache_a_v.reshape(depth, n_pool, ps * HKV_A, DH)
    cache_b_k = cache_b_k.reshape(depth, n_pool, ps * HKV_B, DH)
    cache_b_v = cache_b_v.reshape(depth, n_pool, ps * HKV_B, DH)
    cache_a_kidx = jnp.swapaxes(cache_a_kidx, 2, 3)

    xp = x_prompt.reshape(bp * tp, d)
    xs = x_sample.reshape(bs * ts, d)
    pos_p = jnp.arange(tp, dtype=jnp.int32)
    pos_s = past + jnp.arange(ts, dtype=jnp.int32)
    w_main, w_kiw = _relayout_w_in(w_in, d)
    wts = dict(w_pa=w_pa.astype(_MXU), w_pb=w_pb.astype(_MXU), w_pc=w_pc.astype(_MXU), w_out=w_out.astype(_MXU),
               norm_ffn=norm_ffn, ffn_down=jnp.pad(ffn_down, ((0, 0), (0, fp - f), (0, 0))).astype(_MXU))
    up = _pad_last(ffn_up, fp).astype(_MXU)
    gate = _pad_last(ffn_gate, fp).astype(_MXU)
    cw = _pad_last(conv_w, fp)
    cb = _pad_last(conv_b.reshape(depth, 1, f), fp)
    conv_prev_all = _pad_last(state_conv, fp)

    st_p, st_s = [], []
    kv_p = [jnp.zeros((depth, bp * tp, nh, DH), jnp.float32) for nh in (HKV_A, HKV_A, HKV_B, HKV_B)]
    for l in range(depth):
        z, kiw, *kv_p = _in_proj(xp, norm_mix[l], w_main, w_kiw, l, depth, kv_p)
        o_a = _dsa_prompt(z, kiw, bias_p, bp, tp)
        o_b = _moba_prompt(z, bias_p, bp, tp)
        o_c, ret_new = _retention(z, jnp.zeros((bp, H_C, DK_C, DV_C), jnp.float32), ret_gn[l], pos_p, bp, tp)
        xp, conv_new = _mix_and_ffn(xp, z, o_a, o_b, o_c, wts, l,
                                    lambda h2: _ffn_in_prompt(h2, up, gate, cw, cb, l, bp, tp))
        st_p.append((kiw[:, :D_IDX].reshape(bp, tp, D_IDX), ret_new, conv_new[:, :, :f]))

        h = _rmsnorm(xs, norm_mix[l], _MXU)
        z = _matmul_few_rows(h, w_main, l, jnp.float32)
        kiw = _matmul_few_rows(h, w_kiw, l, jnp.float32)
        o_a = _dsa_sample(z, kiw, cache_a_k, cache_a_v, cache_a_kidx, page_table, bias_s, l, bs, ts)
        o_b = _moba_sample(z, cache_b_k, cache_b_v, page_table, bias_s, l, bs, ts)
        o_c, ret_new = _retention(z, state_ret[l], ret_gn[l], pos_s, bs, ts)
        xs, conv_new = _mix_and_ffn(xs, z, o_a, o_b, o_c, wts, l,
                                    lambda h2: _ffn_in_sample(h2, up, gate, cw, cb, l, conv_prev_all[l], bs, ts))
        st_s.append(_state_slices(z, kiw, bs, ts) + (ret_new, conv_new[:, :, :f]))

    y_prompt = _rmsnorm(xp, norm_final, jnp.float32).reshape(bp, tp, d)
    y_sample = _rmsnorm(xs, norm_final, jnp.float32).reshape(bs, ts, d)
    ka_p, va_p, kb_p, vb_p = kv_p
    ki_p, ret_p, conv_p = [jnp.stack(v) for v in zip(*st_p)]
    outs_s = [jnp.stack(v) for v in zip(*st_s)]
    return (y_prompt, y_sample,
            ka_p.reshape(depth, bp, tp, HKV_A, DH), va_p.reshape(depth, bp, tp, HKV_A, DH), ki_p,
            kb_p.reshape(depth, bp, tp, HKV_B, DH), vb_p.reshape(depth, bp, tp, HKV_B, DH), ret_p, conv_p, *outs_s)
```

```python
import functools
import math

import jax
import jax.numpy as jnp
from jax import lax
from jax.experimental import pallas as pl
from jax.experimental.pallas import tpu as pltpu

DH = 128
H_A, HKV_A = 8, 2
H_IDX, D_IDX = 16, 64
TOPK_IDX = 256
H_B, HKV_B = 8, 2
MOBA_BLOCK = 256
MOBA_TOPK = 3
H_C, DK_C, DV_C = 8, 128, 256
N_BUCKETS = 32
MAX_DIST = 128
CONV_W = 3
EPS = 1e-6
ROPE_BASE = 10000.0
RET_CHUNK = 128

_MXU = jnp.bfloat16
_NEG = -1e30
_TQ = 256
_LANES = 128
_VMEM_LIMIT = 56 * 1024 * 1024

_C_VC = 0
_C_GC = _C_VC + H_C * DV_C
_C_QA = _C_GC + H_C * DV_C
_C_QI = _C_QA + H_A * DH
_C_QB = _C_QI + H_IDX * D_IDX
_C_QC = _C_QB + H_B * DH
_C_KC = _C_QC + H_C * DK_C
_C_KA = _C_KC + H_C * DK_C
_C_VA = _C_KA + HKV_A * DH
_C_KB = _C_VA + HKV_A * DH
_C_VB = _C_KB + HKV_B * DH
_C_GATES = _C_VB + HKV_B * DH


def _params(sem):
    return pltpu.CompilerParams(dimension_semantics=sem, vmem_limit_bytes=_VMEM_LIMIT)


def _tile(n, target, mult=_LANES):
    best = None
    t = mult
    while t <= min(n, target):
        if n % t == 0:
            best = t
        t += mult
    return n if best is None else best


def _rmsnorm_body(x_ref, g_ref, o_ref):
    x = x_ref[...]
    y = x * lax.rsqrt(jnp.mean(x * x, axis=-1, keepdims=True) + EPS)
    o_ref[...] = (y * g_ref[...]).astype(o_ref.dtype)


def _rmsnorm(x, g, out_dtype):
    m, d = x.shape
    tm = _tile(m, 512, 8)
    return pl.pallas_call(
        _rmsnorm_body,
        grid=(m // tm,),
        in_specs=[pl.BlockSpec((tm, d), lambda i: (i, 0)), pl.BlockSpec((1, d), lambda i: (0, 0))],
        out_specs=pl.BlockSpec((tm, d), lambda i: (i, 0)),
        out_shape=jax.ShapeDtypeStruct((m, d), out_dtype),
        compiler_params=_params(("parallel",)),
        name="rmsnorm",
    )(x, g.reshape(1, d))


def _dot_nt(x, w):
    return lax.dot_general(x, w, (((1,), (1,)), ((), ())), preferred_element_type=jnp.float32)


def _mm_body(*refs, nk, has_res):
    x_ref, w_ref = refs[0], refs[1]
    r_ref = refs[2] if has_res else None
    o_ref = refs[3] if has_res else refs[2]
    acc_ref = refs[-1]
    k = pl.program_id(2)
    d = jnp.dot(x_ref[...], w_ref[...], preferred_element_type=jnp.float32)

    def finish(r):
        if has_res:
            r = r + r_ref[...]
        o_ref[...] = r.astype(o_ref.dtype)

    if nk == 1:
        finish(d)
        return

    @pl.when(k == 0)
    def _():
        acc_ref[...] = d

    @pl.when(jnp.logical_and(k > 0, k < nk - 1))
    def _():
        acc_ref[...] += d

    @pl.when(k == nk - 1)
    def _():
        finish(acc_ref[...] + d)


def _mm_few_rows_body(x_ref, w_ref, o_ref):
    x = x_ref[...]
    x = jnp.concatenate([x, jnp.zeros((_LANES - x.shape[0], x.shape[1]), x.dtype)], axis=0)
    r_t = lax.dot_general(w_ref[...], x, (((1,), (1,)), ((), ())), preferred_element_type=jnp.float32)
    o_ref[...] = r_t.T[0:o_ref.shape[0]].astype(o_ref.dtype)


def _matmul_few_rows(x, w, layer, out_dtype, tn_target=1024):
    m, kdim = x.shape
    n = w.shape[1]
    tn = _tile(n, tn_target)
    assert m <= _LANES and m % 8 == 0
    return pl.pallas_call(
        _mm_few_rows_body,
        grid=(n // tn,),
        in_specs=[pl.BlockSpec((m, kdim), lambda j: (0, 0)), pl.BlockSpec((None, tn, kdim), lambda j: (layer, j, 0))],
        out_specs=pl.BlockSpec((m, tn), lambda j: (0, j)),
        out_shape=jax.ShapeDtypeStruct((m, n), out_dtype),
        compiler_params=_params(("parallel",)),
        name="matmul_few_rows",
    )(x, w)


def _matmul(x, w, layer, out_dtype, residual=None, tm_target=1024, tn_target=1024, tk_target=2048):
    m, kdim = x.shape
    n = w.shape[2]
    tm = _tile(m, tm_target, 8)
    tn = _tile(n, tn_target)
    tk = _tile(kdim, tk_target)
    nk = kdim // tk
    in_specs = [pl.BlockSpec((tm, tk), lambda i, j, k: (i, k)),
                pl.BlockSpec((None, tk, tn), lambda i, j, k: (layer, k, j))]
    args = [x, w]
    if residual is not None:
        in_specs.append(pl.BlockSpec((tm, tn), lambda i, j, k: (i, j)))
        args.append(residual)
    return pl.pallas_call(
        functools.partial(_mm_body, nk=nk, has_res=residual is not None),
        grid=(m // tm, n // tn, nk),
        in_specs=in_specs,
        out_specs=pl.BlockSpec((tm, tn), lambda i, j, k: (i, j)),
        out_shape=jax.ShapeDtypeStruct((m, n), out_dtype),
        scratch_shapes=[pltpu.VMEM((tm, tn), jnp.float32)] if nk > 1 else [],
        compiler_params=_params(("parallel", "parallel", "arbitrary")),
        name="matmul",
    )(*args)


def _out_proj_body(m_ref, w_ref, x_ref, g_ref, o_ref, h_ref):
    r = x_ref[...] + jnp.dot(m_ref[...], w_ref[...], preferred_element_type=jnp.float32)
    o_ref[...] = r
    y = r * lax.rsqrt(jnp.mean(r * r, axis=-1, keepdims=True) + EPS)
    h_ref[...] = (y * g_ref[...]).astype(h_ref.dtype)


def _out_proj(m_in, w, layer, x, g):
    m, kdim = m_in.shape
    d = w.shape[2]
    tm = _tile(m, 512, 8)
    return pl.pallas_call(
        _out_proj_body,
        grid=(m // tm,),
        in_specs=[pl.BlockSpec((tm, kdim), lambda i: (i, 0)),
                  pl.BlockSpec((None, kdim, d), lambda i: (layer, 0, 0)),
                  pl.BlockSpec((tm, d), lambda i: (i, 0)),
                  pl.BlockSpec((1, d), lambda i: (0, 0))],
        out_specs=[pl.BlockSpec((tm, d), lambda i: (i, 0)), pl.BlockSpec((tm, d), lambda i: (i, 0))],
        out_shape=[jax.ShapeDtypeStruct((m, d), jnp.float32), jax.ShapeDtypeStruct((m, d), _MXU)],
        compiler_params=_params(("parallel",)),
        name="out_proj",
    )(m_in, w, x, g.reshape(1, d))


def _in_proj_body(*refs, n_prev, j_state):
    x_ref, g_ref, w_ref, wk_ref = refs[:4]
    z_ref, kiw_ref, ka_ref, va_ref, kb_ref, vb_ref, h_ref = refs[4 + n_prev:]

    @pl.when(pl.program_id(1) == 0)
    def _():
        x = x_ref[...]
        y = x * lax.rsqrt(jnp.mean(x * x, axis=-1, keepdims=True) + EPS)
        h_ref[...] = (y * g_ref[...]).astype(h_ref.dtype)
        kiw_ref[...] = _dot_nt(h_ref[...], wk_ref[...])

    r = _dot_nt(h_ref[...], w_ref[...])
    z_ref[...] = r

    @pl.when(pl.program_id(1) == j_state)
    def _():
        col = 0
        for ref in (ka_ref, va_ref, kb_ref, vb_ref):
            for g in range(ref.shape[1]):
                ref[:, g, :] = r[:, col:col + DH]
                col += DH


def _in_proj(x, g, w_main, w_kiw, layer, depth, prev_states):
    m, kdim = x.shape
    n = w_main.shape[1]
    tm = _tile(m, 1024, 8)
    tn = 2 * HKV_A * DH + 2 * HKV_B * DH
    assert n % tn == 0 and _C_KA % tn == 0 and _C_VB + HKV_B * DH == _C_KA + tn
    kv_heads = (HKV_A, HKV_A, HKV_B, HKV_B)
    prev = list(prev_states)
    state_spec = lambda nh: pl.BlockSpec((None, tm, nh, DH), lambda i, j: (layer, i, 0, 0))
    return pl.pallas_call(
        functools.partial(_in_proj_body, n_prev=len(prev), j_state=_C_KA // tn),
        grid=(m // tm, n // tn),
        in_specs=[pl.BlockSpec((tm, kdim), lambda i, j: (i, 0)),
                  pl.BlockSpec((1, kdim), lambda i, j: (0, 0)),
                  pl.BlockSpec((None, tn, kdim), lambda i, j: (layer, j, 0)),
                  pl.BlockSpec((None, _LANES, kdim), lambda i, j: (layer, 0, 0))]
        + [pl.BlockSpec(memory_space=pl.ANY)] * len(prev),
        out_specs=[pl.BlockSpec((tm, tn), lambda i, j: (i, j)), pl.BlockSpec((tm, _LANES), lambda i, j: (i, 0))]
        + [state_spec(nh) for nh in kv_heads],
        out_shape=[jax.ShapeDtypeStruct((m, n), jnp.float32), jax.ShapeDtypeStruct((m, _LANES), jnp.float32)]
        + [jax.ShapeDtypeStruct((depth, m, nh, DH), jnp.float32) for nh in kv_heads],
        scratch_shapes=[pltpu.VMEM((tm, kdim), _MXU)],
        input_output_aliases={4 + k: 2 + k for k in range(len(prev))},
        compiler_params=_params(("parallel", "arbitrary")),
        name="in_proj",
    )(x, g.reshape(1, kdim), w_main, w_kiw, *prev)


def _t5_bucket(rel):
    n = jnp.maximum(rel, 0)
    max_exact = N_BUCKETS // 2
    nf = jnp.maximum(n, 1).astype(jnp.float32)
    large = max_exact + (jnp.log(nf / max_exact) / math.log(MAX_DIST / max_exact)
                         * (N_BUCKETS - max_exact)).astype(jnp.int32)
    large = jnp.minimum(large, N_BUCKETS - 1)
    return jnp.where(n < max_exact, n, large)


def _bias_body(tab_ref, bucket_ref, o_ref):
    h = pl.program_id(0)
    bucket = bucket_ref[...]
    acc = jnp.zeros(bucket.shape, jnp.float32)
    for b in range(N_BUCKETS):
        acc = jnp.where(bucket == b, tab_ref[b, h], acc)
    o_ref[...] = acc


def _bias_table(rel_bias, rel):
    bucket = _t5_bucket(rel)
    r, c = rel.shape
    nh = rel_bias.shape[1]
    return pl.pallas_call(
        _bias_body,
        grid=(nh,),
        in_specs=[pl.BlockSpec(memory_space=pltpu.SMEM), pl.BlockSpec((r, c), lambda h: (0, 0))],
        out_specs=pl.BlockSpec((None, r, c), lambda h: (h, 0, 0)),
        out_shape=jax.ShapeDtypeStruct((nh, r, c), jnp.float32),
        compiler_params=_params(("parallel",)),
        name="bias_table",
    )(rel_bias, bucket)


def _sortable(x):
    b = lax.bitcast_convert_type(x, jnp.int32)
    return b ^ ((b >> 31) & jnp.int32(0x7FFFFFFF))


_KEY_NEG_INF = -2147483648 + 0x7FFFFF


def _stack_heads(q_ref, g, n_grp, scale):
    parts = [q_ref[:, (g * n_grp + hh) * DH:(g * n_grp + hh + 1) * DH] for hh in range(n_grp)]
    return (jnp.concatenate(parts, axis=0) * scale).astype(_MXU)


def _flash_update(m_ref, l_ref, acc_ref, g, logits, v):
    m_old = m_ref[g]
    m_new = jnp.maximum(m_old, jnp.max(logits, axis=1, keepdims=True))
    alpha = jnp.exp(m_old - m_new)
    p = jnp.exp(logits - m_new)
    l_ref[g] = alpha * l_ref[g] + jnp.sum(p, axis=1, keepdims=True)
    acc_ref[g] = alpha * acc_ref[g] + jnp.dot(p.astype(_MXU), v, preferred_element_type=jnp.float32)
    m_ref[g] = m_new


def _flash_init(m_ref, l_ref, acc_ref):
    m_ref[...] = jnp.full(m_ref.shape, _NEG, jnp.float32)
    l_ref[...] = jnp.zeros(l_ref.shape, jnp.float32)
    acc_ref[...] = jnp.zeros(acc_ref.shape, jnp.float32)


def _flash_store(o_ref, l_ref, acc_ref, n_kv, n_grp, rows):
    for g in range(n_kv):
        o = acc_ref[g] / l_ref[g]
        for hh in range(n_grp):
            h = g * n_grp + hh
            o_ref[:, h * DH:(h + 1) * DH] = o[hh * rows:(hh + 1) * rows].astype(o_ref.dtype)


def _flash_update_t(m_ref, l_ref, acc_ref, g, logits_t, v_t, extra_of_chunk):
    ps, alphas = [], []
    for c in range(logits_t.shape[1] // _LANES):
        sl = slice(c * _LANES, (c + 1) * _LANES)
        x = logits_t[:, sl] + extra_of_chunk(c)
        m_old = m_ref[g, :, sl]
        m_new = jnp.maximum(m_old, jnp.max(x, axis=0, keepdims=True))
        alpha = jnp.exp(m_old - m_new)
        p = jnp.exp(x - m_new)
        l_ref[g, :, sl] = alpha * l_ref[g, :, sl] + jnp.sum(p, axis=0, keepdims=True)
        m_ref[g, :, sl] = m_new
        ps.append(p.astype(_MXU))
        alphas.append(alpha)
    p_all = jnp.concatenate(ps, axis=1)
    acc_ref[g] = (jnp.concatenate(alphas, axis=1) * acc_ref[g]
                  + jnp.dot(v_t, p_all, preferred_element_type=jnp.float32))


def _flash_store_t(o_ref, l_ref, acc_ref, n_kv, n_grp, rows):
    for g in range(n_kv):
        o = acc_ref[g] / l_ref[g]
        for hh in range(n_grp):
            h = g * n_grp + hh
            o_ref[:, h * DH:(h + 1) * DH] = o[:, hh * rows:(hh + 1) * rows].T.astype(o_ref.dtype)


def _kth_largest_key(count_ge, n_sel, shape):
    int_min = jnp.int32(-2147483648)
    prefix = jnp.full(shape, int_min, jnp.int32)
    zero = jnp.zeros(shape, jnp.int32)
    prefix = jnp.where(count_ge(zero) >= n_sel, zero, prefix)

    def step(it, prefix):
        cand = prefix | (jnp.int32(1) << (30 - it))
        return jnp.where(count_ge(cand) >= n_sel, cand, prefix)

    return lax.fori_loop(0, 31, step, prefix)


def _first_index_cut(count_lt, need, shape, n_idx):
    nbits = max(1, int(n_idx).bit_length())
    cut = jnp.zeros(shape, jnp.int32)

    def step(it, cut):
        cand = cut + (jnp.int32(1) << (nbits - 1 - it))
        return jnp.where(count_lt(cand) < need, cand, cut)

    return lax.fori_loop(0, nbits, step, cut)


def _stage_kv(k_in_ref, v_in_ref, kb_ref, vt_ref, tq):
    for j in range(kb_ref.shape[0]):
        kb_ref[j] = k_in_ref[j * tq:(j + 1) * tq, :].astype(_MXU)
        vt_ref[j] = v_in_ref[j * tq:(j + 1) * tq, :].T.astype(_MXU)


def _attend_t(j, qs, kb_ref, vt_ref, m_ref, l_ref, acc_ref, extra_of_chunk):
    for g in range(len(qs)):
        logits_t = lax.dot_general(kb_ref[j, :, g * DH:(g + 1) * DH], qs[g], (((1,), (1,)), ((), ())),
                                   preferred_element_type=jnp.float32)
        _flash_update_t(m_ref, l_ref, acc_ref, g, logits_t, vt_ref[j, g * DH:(g + 1) * DH, :],
                        functools.partial(extra_of_chunk, g))


def _dsa_p_body(qa_ref, qi_ref, kiwq_ref, kiw_ref, ka_ref, va_ref, bias_ref, o_ref,
                kx_ref, kb_ref, vt_ref, key_ref, neg_ref, m_ref, l_ref, acc_ref, *, n_sel, t):
    i = pl.program_id(1)
    tq = _TQ
    n_grp = H_A // HKV_A
    krow = lax.broadcasted_iota(jnp.int32, (tq, tq), 0)
    qcol = lax.broadcasted_iota(jnp.int32, (tq, tq), 1)

    @pl.when(i == 0)
    def _():
        _stage_kv(ka_ref, va_ref, kb_ref, vt_ref, tq)
        for j in range(kx_ref.shape[0]):
            kx_ref[j] = kiw_ref[j * tq:(j + 1) * tq, 0:D_IDX].astype(_MXU)

    qi = [qi_ref[:, h * D_IDX:(h + 1) * D_IDX].astype(_MXU) for h in range(H_IDX)]
    wt = kiwq_ref[...].T[D_IDX:D_IDX + H_IDX, :] * (D_IDX ** -0.5 * H_IDX ** -0.5)

    def score_tile(j, carry):
        kx = kx_ref[j]
        sc = jnp.zeros((tq, tq), jnp.float32)
        for h in range(H_IDX):
            s = lax.dot_general(kx, qi[h], (((1,), (1,)), ((), ())), preferred_element_type=jnp.float32)
            sc = sc + wt[h:h + 1, :] * jnp.maximum(s, 0.0)
        causal = jnp.logical_or(j < i, krow <= qcol)
        key_ref[j] = _sortable(jnp.where(causal, sc, -jnp.inf))
        return carry

    lax.fori_loop(0, i + 1, score_tile, 0)

    def count_tiles(pred):
        def body(j, acc):
            return acc + jnp.sum(pred(key_ref[j], j).astype(jnp.int32).reshape(tq // 8, 8, tq), axis=0)
        acc = lax.fori_loop(0, i + 1, body, jnp.zeros((8, tq), jnp.int32))
        return jnp.sum(acc, axis=0, keepdims=True)

    thr = _kth_largest_key(lambda c: count_tiles(lambda k, j: k >= c), n_sel, (1, tq))
    n_ge = count_tiles(lambda k, j: k >= thr)
    finite = thr > _KEY_NEG_INF
    tie = jnp.max(jnp.where(jnp.logical_and(finite, n_ge > n_sel), 1, 0)) > 0

    def write_neg(cut):
        def body(j, carry):
            k = key_ref[j]
            kpos = j * tq + krow
            keep = jnp.logical_or(k > thr, jnp.logical_and(k == thr, kpos <= cut))
            keep = jnp.logical_and(keep, jnp.logical_or(j < i, krow <= qcol))
            neg_ref[j] = jnp.where(keep, 0.0, _NEG)
            return carry
        lax.fori_loop(0, i + 1, body, 0)

    def with_ties():
        n_gt = count_tiles(lambda k, j: k > thr)
        need = n_sel - n_gt
        cut = _first_index_cut(
            lambda p: count_tiles(lambda k, j: jnp.logical_and(k == thr, j * tq + krow < p)),
            need, (1, tq), t)
        write_neg(jnp.where(finite, cut, jnp.int32(2 ** 30)))

    def without_ties():
        write_neg(jnp.full((1, tq), 2 ** 30, jnp.int32))

    lax.cond(tie, with_ties, without_ties)

    _flash_init(m_ref, l_ref, acc_ref)
    scale = DH ** -0.5
    qs = [_stack_heads(qa_ref, g, n_grp, scale) for g in range(HKV_A)]

    per_head = tq // _LANES

    def attend(j, bias_of_chunk):
        neg = neg_ref[j]

        def extra(g, c):
            half = slice((c % per_head) * _LANES, (c % per_head + 1) * _LANES)
            return bias_of_chunk(g * n_grp + c // per_head, half) + neg[:, half]
        _attend_t(j, qs, kb_ref, vt_ref, m_ref, l_ref, acc_ref, extra)

    def far(j, carry):
        attend(j, lambda h, half: bias_ref[h, 0:1, tq - 1:tq])
        return carry

    lax.fori_loop(0, jnp.maximum(i - 1, 0), far, 0)

    @pl.when(i >= 1)
    def _():
        attend(i - 1, lambda h, half: bias_ref[h, 0:tq, half])

    attend(i, lambda h, half: bias_ref[h, tq:2 * tq, half])
    _flash_store_t(o_ref, l_ref, acc_ref, HKV_A, n_grp, tq)


def _kv_scratch(nk, tq, n_kv, n_grp):
    return [
        pltpu.VMEM((nk, tq, n_kv * DH), _MXU),
        pltpu.VMEM((nk, n_kv * DH, tq), _MXU),
    ], [
        pltpu.VMEM((n_kv, 1, n_grp * tq), jnp.float32),
        pltpu.VMEM((n_kv, 1, n_grp * tq), jnp.float32),
        pltpu.VMEM((n_kv, DH, n_grp * tq), jnp.float32),
    ]


def _dsa_prompt(z, kiw, bias, n_batch, t):
    tq = _TQ
    nq = t // tq
    n_sel = min(TOPK_IDX, t // 4)
    assert n_sel <= tq and t % tq == 0
    n_grp = H_A // HKV_A
    kv_scratch, flash_scratch = _kv_scratch(nq, tq, HKV_A, n_grp)
    return pl.pallas_call(
        functools.partial(_dsa_p_body, n_sel=n_sel, t=t),
        grid=(n_batch, nq),
        in_specs=[
            pl.BlockSpec((tq, H_A * DH), lambda b, i: (b * nq + i, _C_QA // (H_A * DH))),
            pl.BlockSpec((tq, H_IDX * D_IDX), lambda b, i: (b * nq + i, _C_QI // (H_IDX * D_IDX))),
            pl.BlockSpec((tq, _LANES), lambda b, i: (b * nq + i, 0)),
            pl.BlockSpec((t, _LANES), lambda b, i: (b, 0)),
            pl.BlockSpec((t, HKV_A * DH), lambda b, i: (b, _C_KA // (HKV_A * DH))),
            pl.BlockSpec((t, HKV_A * DH), lambda b, i: (b, _C_VA // (HKV_A * DH))),
            pl.BlockSpec((H_A, 2 * tq, tq), lambda b, i: (0, 0, 0)),
        ],
        out_specs=pl.BlockSpec((tq, H_A * DH), lambda b, i: (b * nq + i, 0)),
        out_shape=jax.ShapeDtypeStruct((n_batch * t, H_A * DH), _MXU),
        scratch_shapes=[pltpu.VMEM((nq, tq, D_IDX), _MXU)] + kv_scratch + [
            pltpu.VMEM((nq, tq, tq), jnp.int32),
            pltpu.VMEM((nq, tq, tq), jnp.float32),
        ] + flash_scratch,
        compiler_params=_params(("parallel", "arbitrary")),
        name="dsa_prompt",
    )(z, z, kiw, kiw, z, z, bias)


def _topk_lanes(gate, n_cand, n_sel):
    lane = lax.broadcasted_iota(jnp.int32, gate.shape, 1)
    live = lane < n_cand
    chosen = jnp.zeros(gate.shape, jnp.bool_)
    big = jnp.int32(2 ** 30)
    for _ in range(n_sel):
        cand = jnp.logical_and(live, jnp.logical_not(chosen))
        best = jnp.max(jnp.where(cand, gate, -jnp.inf), axis=1, keepdims=True)
        first = jnp.min(jnp.where(jnp.logical_and(cand, gate == best), lane, big), axis=1, keepdims=True)
        chosen = jnp.logical_or(chosen, lane == first)
    return chosen


def _topk_rows(gate, n_cand, n_sel):
    row = lax.broadcasted_iota(jnp.int32, gate.shape, 0)
    live = row < n_cand
    chosen = jnp.zeros(gate.shape, jnp.bool_)
    big = jnp.int32(2 ** 30)
    for _ in range(n_sel):
        cand = jnp.logical_and(live, jnp.logical_not(chosen))
        best = jnp.max(jnp.where(cand, gate, -jnp.inf), axis=0, keepdims=True)
        first = jnp.min(jnp.where(jnp.logical_and(cand, gate == best), row, big), axis=0, keepdims=True)
        chosen = jnp.logical_or(chosen, row == first)
    return chosen


def _moba_p_body(qb_ref, k_in_ref, v_in_ref, bias_ref, o_ref, kmean_ref, kb_ref, vt_ref, allow_ref,
                 m_ref, l_ref, acc_ref, *, n_sel):
    i = pl.program_id(1)
    tq = _TQ
    nb = kb_ref.shape[0]
    n_grp = H_B // HKV_B
    krow = lax.broadcasted_iota(jnp.int32, (tq, tq), 0)
    qcol = lax.broadcasted_iota(jnp.int32, (tq, tq), 1)

    @pl.when(i == 0)
    def _():
        _stage_kv(k_in_ref, v_in_ref, kb_ref, vt_ref, tq)
        kmean_ref[...] = jnp.zeros(kmean_ref.shape, jnp.float32)
        for j in range(nb):
            kmean_ref[j:j + 1, :] = jnp.mean(k_in_ref[j * tq:(j + 1) * tq, :], axis=0, keepdims=True)

    for g in range(HKV_B):
        qg = jnp.concatenate([qb_ref[:, (g * n_grp + hh) * DH:(g * n_grp + hh + 1) * DH] for hh in range(n_grp)],
                             axis=0)
        gate_t = lax.dot_general(kmean_ref[:, g * DH:(g + 1) * DH], qg, (((1,), (1,)), ((), ())),
                                 preferred_element_type=jnp.float32, precision=lax.Precision.HIGHEST)
        allow_ref[g] = jnp.where(_topk_rows(gate_t, i, n_sel), 0.0, _NEG)

    _flash_init(m_ref, l_ref, acc_ref)
    scale = DH ** -0.5
    qs = [_stack_heads(qb_ref, g, n_grp, scale) for g in range(HKV_B)]
    causal_neg = jnp.where(krow <= qcol, 0.0, _NEG)
    per_head = tq // _LANES

    def head_half(g, c):
        return g * n_grp + c // per_head, slice((c % per_head) * _LANES, (c % per_head + 1) * _LANES)

    def attend_past(j, bias_of_chunk):
        allow = [allow_ref[g, pl.ds(j, 1), :] for g in range(HKV_B)]

        def extra(g, c):
            return bias_of_chunk(*head_half(g, c)) + allow[g][:, c * _LANES:(c + 1) * _LANES]
        _attend_t(j, qs, kb_ref, vt_ref, m_ref, l_ref, acc_ref, extra)

    def far(j, carry):
        attend_past(j, lambda h, half: bias_ref[h, 0:1, tq - 1:tq])
        return carry

    lax.fori_loop(0, jnp.maximum(i - 1, 0), far, 0)

    @pl.when(i >= 1)
    def _():
        attend_past(i - 1, lambda h, half: bias_ref[h, 0:tq, half])

    def extra_own(g, c):
        h, half = head_half(g, c)
        return bias_ref[h, tq:2 * tq, half] + causal_neg[:, half]

    _attend_t(i, qs, kb_ref, vt_ref, m_ref, l_ref, acc_ref, extra_own)
    _flash_store_t(o_ref, l_ref, acc_ref, HKV_B, n_grp, tq)


def _moba_prompt(z, bias, n_batch, t):
    tq = _TQ
    assert t % tq == 0 and MOBA_BLOCK == tq
    nb = t // tq
    n_sel = min(MOBA_TOPK, (t - 1) // MOBA_BLOCK)
    n_grp = H_B // HKV_B
    nb_pad = -(-nb // 8) * 8
    kv_scratch, flash_scratch = _kv_scratch(nb, tq, HKV_B, n_grp)
    return pl.pallas_call(
        functools.partial(_moba_p_body, n_sel=n_sel),
        grid=(n_batch, nb),
        in_specs=[
            pl.BlockSpec((tq, H_B * DH), lambda b, i: (b * nb + i, _C_QB // (H_B * DH))),
            pl.BlockSpec((t, HKV_B * DH), lambda b, i: (b, _C_KB // (HKV_B * DH))),
            pl.BlockSpec((t, HKV_B * DH), lambda b, i: (b, _C_VB // (HKV_B * DH))),
            pl.BlockSpec((H_B, 2 * tq, tq), lambda b, i: (1, 0, 0)),
        ],
        out_specs=pl.BlockSpec((tq, H_B * DH), lambda b, i: (b * nb + i, 0)),
        out_shape=jax.ShapeDtypeStruct((n_batch * t, H_B * DH), _MXU),
        scratch_shapes=[pltpu.VMEM((nb_pad, HKV_B * DH), jnp.float32)] + kv_scratch
        + [pltpu.VMEM((HKV_B, nb_pad, n_grp * tq), jnp.float32)] + flash_scratch,
        compiler_params=_params(("parallel", "arbitrary")),
        name="moba_prompt",
    )(z, z, z, bias)


def _ret_body(q_ref, k_ref, v_ref, g_ref, cos_ref, sin_ref, intra_ref, cross_ref, tail_ref, decay_ref, gn_ref,
              s0_ref, o_ref, s_out_ref, s_ref):
    t = pl.program_id(1)

    @pl.when(t == 0)
    def _():
        s_ref[...] = s0_ref[...]

    cos = cos_ref[...]
    sin = sin_ref[...]
    for h in range(H_C):
        qh = q_ref[:, h * DK_C:(h + 1) * DK_C]
        kh = k_ref[:, h * DK_C:(h + 1) * DK_C]
        q = qh * cos + pltpu.roll(qh, DK_C // 2, axis=1) * sin
        k = (kh * cos + pltpu.roll(kh, DK_C // 2, axis=1) * sin) * (DK_C ** -0.5)
        v = v_ref[:, h * DV_C:(h + 1) * DV_C].astype(_MXU)
        s = s_ref[h]
        qm = q.astype(_MXU)
        att = lax.dot_general(qm, k.astype(_MXU), (((1,), (1,)), ((), ())),
                              preferred_element_type=jnp.float32) * intra_ref[h]
        o = (jnp.dot(att.astype(_MXU), v, preferred_element_type=jnp.float32)
             + jnp.dot(qm, s.astype(_MXU), preferred_element_type=jnp.float32) * cross_ref[h])
        kt = (k * tail_ref[h]).astype(_MXU)
        s_ref[h] = s * decay_ref[h] + lax.dot_general(kt, v, (((0,), (0,)), ((), ())),
                                                      preferred_element_type=jnp.float32)
        o = o * lax.rsqrt(jnp.mean(o * o, axis=-1, keepdims=True) + EPS)
        gate = g_ref[:, h * DV_C:(h + 1) * DV_C]
        o = o * gn_ref[:, h * DV_C:(h + 1) * DV_C] * (gate * jax.nn.sigmoid(gate))
        o_ref[:, h * DV_C:(h + 1) * DV_C] = o.astype(o_ref.dtype)

    @pl.when(t == pl.num_programs(1) - 1)
    def _():
        s_out_ref[...] = s_ref[...]


def _retention(z, s0, ret_gn, pos, n_batch, t):
    c = math.gcd(t, RET_CHUNK)
    nc = t // c
    half = DK_C // 2
    inv = ROPE_BASE ** (-jnp.arange(0, DK_C, 2, dtype=jnp.float32) / DK_C)
    ang = pos.astype(jnp.float32)[:, None] * inv[None, :]
    cos = jnp.concatenate([jnp.cos(ang), jnp.cos(ang)], axis=1)
    sin = jnp.concatenate([-jnp.sin(ang), jnp.sin(ang)], axis=1)
    assert cos.shape == (t, 2 * half)
    log_g = jnp.log1p(-jnp.exp2(-5.0 - jnp.arange(H_C, dtype=jnp.float32)))
    idx = jnp.arange(c, dtype=jnp.float32)
    diff = idx[:, None] - idx[None, :]
    intra = jnp.where(diff[None] >= 0, jnp.exp(jnp.maximum(diff, 0.0)[None] * log_g[:, None, None]), 0.0)
    cross = jnp.exp((idx[None, :] + 1.0) * log_g[:, None])[:, :, None]
    tail = jnp.exp((c - 1.0 - idx)[None, :] * log_g[:, None])[:, :, None]
    decay = jnp.exp(c * log_g)[:, None, None]
    wq = H_C * DK_C
    wv = H_C * DV_C
    return pl.pallas_call(
        _ret_body,
        grid=(n_batch, nc),
        in_specs=[
            pl.BlockSpec((c, wq), lambda b, i: (b * nc + i, _C_QC // wq)),
            pl.BlockSpec((c, wq), lambda b, i: (b * nc + i, _C_KC // wq)),
            pl.BlockSpec((c, wv), lambda b, i: (b * nc + i, _C_VC // wv)),
            pl.BlockSpec((c, wv), lambda b, i: (b * nc + i, _C_GC // wv)),
            pl.BlockSpec((c, DK_C), lambda b, i: (i, 0)),
            pl.BlockSpec((c, DK_C), lambda b, i: (i, 0)),
            pl.BlockSpec((H_C, c, c), lambda b, i: (0, 0, 0)),
            pl.BlockSpec((H_C, c, 1), lambda b, i: (0, 0, 0)),
            pl.BlockSpec((H_C, c, 1), lambda b, i: (0, 0, 0)),
            pl.BlockSpec((H_C, 1, 1), lambda b, i: (0, 0, 0)),
            pl.BlockSpec((1, wv), lambda b, i: (0, 0)),
            pl.BlockSpec((None, H_C, DK_C, DV_C), lambda b, i: (b, 0, 0, 0)),
        ],
        out_specs=[
            pl.BlockSpec((c, wv), lambda b, i: (b * nc + i, 0)),
            pl.BlockSpec((None, H_C, DK_C, DV_C), lambda b, i: (b, 0, 0, 0)),
        ],
        out_shape=[
            jax.ShapeDtypeStruct((n_batch * t, wv), _MXU),
            jax.ShapeDtypeStruct((n_batch, H_C, DK_C, DV_C), jnp.float32),
        ],
        scratch_shapes=[pltpu.VMEM((H_C, DK_C, DV_C), jnp.float32)],
        compiler_params=_params(("parallel", "arbitrary")),
        name="retention",
    )(z, z, z, z, cos, sin, intra, cross, tail, decay, ret_gn.reshape(1, wv), s0)


def _merge_body(oa_ref, ob_ref, oc_ref, wa_ref, wb_ref, wc_ref, ga_ref, gb_ref, gc_ref, o_ref):
    def term(o, w, g):
        return jax.nn.sigmoid(g[...]) * jnp.dot(o[...], w[...], preferred_element_type=jnp.float32)
    o_ref[...] = (term(oa_ref, wa_ref, ga_ref) + term(ob_ref, wb_ref, gb_ref)
                  + term(oc_ref, wc_ref, gc_ref)).astype(o_ref.dtype)


def _merge(o_a, o_b, o_c, z, w_pa, w_pb, w_pc, layer):
    m = o_a.shape[0]
    d = w_pa.shape[2]
    tm = _tile(m, 1024, 8)
    tn = _tile(d, 512)
    gate_blk = [(_C_GATES + k * d) // tn for k in range(3)]
    assert all((_C_GATES + k * d) % tn == 0 for k in range(3))
    row = lambda w: pl.BlockSpec((tm, w), lambda i, j: (i, 0))
    col = lambda w: pl.BlockSpec((None, w, tn), lambda i, j: (layer, 0, j))
    gate = lambda k: pl.BlockSpec((tm, tn), lambda i, j: (i, gate_blk[k] + j))
    return pl.pallas_call(
        _merge_body,
        grid=(m // tm, d // tn),
        in_specs=[row(o_a.shape[1]), row(o_b.shape[1]), row(o_c.shape[1]),
                  col(w_pa.shape[1]), col(w_pb.shape[1]), col(w_pc.shape[1]), gate(0), gate(1), gate(2)],
        out_specs=pl.BlockSpec((tm, tn), lambda i, j: (i, j)),
        out_shape=jax.ShapeDtypeStruct((m, d), _MXU),
        compiler_params=_params(("parallel", "parallel")),
        name="merge",
    )(o_a, o_b, o_c, w_pa, w_pb, w_pc, z, z, z)


def _conv_gate(u, u1, u2, g, cw_ref, cb_ref):
    c = cb_ref[...] + cw_ref[0:1, :] * u2 + cw_ref[1:2, :] * u1 + cw_ref[2:3, :] * u
    return 0.5 * c * (1.0 + lax.erf(c * (2.0 ** -0.5))) * g


def _ffn_p_body(h_ref, wu_ref, wg_ref, cw_ref, cb_ref, a_ref, cs_ref, tail_ref, *, seq, n_sub):
    i = pl.program_id(1)
    tm = h_ref.shape[0]
    sm = tm // n_sub

    @pl.when((i * tm) % seq == 0)
    def _():
        tail_ref[...] = jnp.zeros_like(tail_ref)

    prev = tail_ref[...]
    row = lax.broadcasted_iota(jnp.int32, (sm, 1), 0)
    for r in range(n_sub):
        h = h_ref[r * sm:(r + 1) * sm, :]
        u = jnp.dot(h, wu_ref[...], preferred_element_type=jnp.float32)
        g = jnp.dot(h, wg_ref[...], preferred_element_type=jnp.float32)
        u1 = jnp.where(row == 0, prev[7:8, :], pltpu.roll(u, 1, axis=0))
        u2 = jnp.where(row == 0, prev[6:7, :], jnp.where(row == 1, prev[7:8, :], pltpu.roll(u, 2, axis=0)))
        a_ref[r * sm:(r + 1) * sm, :] = _conv_gate(u, u1, u2, g, cw_ref, cb_ref).astype(a_ref.dtype)
        prev = u[sm - 8:sm, :]
    tail_ref[...] = prev

    @pl.when((i * tm + tm) % seq == 0)
    def _():
        cs_ref[...] = prev[8 - (CONV_W - 1):8, :]


def _ffn_in_prompt(h, w_up, w_gate, conv_w, conv_b, layer, n_batch, t):
    m, d = h.shape
    f = w_up.shape[2]
    tm = _tile(t, 1024, 8)
    tn = _tile(f, 512)
    per_seq = t // tm
    return pl.pallas_call(
        functools.partial(_ffn_p_body, seq=t, n_sub=4 if tm % 32 == 0 else 1),
        grid=(f // tn, m // tm),
        in_specs=[
            pl.BlockSpec((tm, d), lambda j, i: (i, 0)),
            pl.BlockSpec((None, d, tn), lambda j, i: (layer, 0, j)),
            pl.BlockSpec((None, d, tn), lambda j, i: (layer, 0, j)),
            pl.BlockSpec((None, CONV_W, tn), lambda j, i: (layer, 0, j)),
            pl.BlockSpec((None, 1, tn), lambda j, i: (layer, 0, j)),
        ],
        out_specs=[
            pl.BlockSpec((tm, tn), lambda j, i: (i, j)),
            pl.BlockSpec((None, CONV_W - 1, tn), lambda j, i: (i // per_seq, 0, j)),
        ],
        out_shape=[
            jax.ShapeDtypeStruct((m, f), _MXU),
            jax.ShapeDtypeStruct((n_batch, CONV_W - 1, f), jnp.float32),
        ],
        scratch_shapes=[pltpu.VMEM((8, tn), jnp.float32)],
        compiler_params=_params(("parallel", "arbitrary")),
        name="ffn_in_prompt",
    )(h, w_up, w_gate, conv_w, conv_b)


def _ffn_s_body(h_ref, wu_ref, wg_ref, cw_ref, cb_ref, p1_ref, p2_ref, a_ref, u_ref, *, seq):
    tm = h_ref.shape[0]
    h = h_ref[...]
    u = jnp.dot(h, wu_ref[...], preferred_element_type=jnp.float32)
    g = jnp.dot(h, wg_ref[...], preferred_element_type=jnp.float32)
    pos = lax.broadcasted_iota(jnp.int32, (tm, 1), 0) % seq
    u1 = jnp.where(pos >= 1, pltpu.roll(u, 1, axis=0), p1_ref[...])
    u2 = jnp.where(pos >= 2, pltpu.roll(u, 2, axis=0), p2_ref[...])
    a_ref[...] = _conv_gate(u, u1, u2, g, cw_ref, cb_ref).astype(a_ref.dtype)
    u_ref[...] = u


def _ffn_in_sample(h, w_up, w_gate, conv_w, conv_b, layer, conv_prev, n_batch, t):
    m, d = h.shape
    f = w_up.shape[2]
    assert t >= CONV_W - 1
    tn = _tile(f, 512)
    zeros = jnp.zeros((n_batch, t, f), jnp.float32)
    p1 = zeros.at[:, 0].set(conv_prev[:, 1]).reshape(m, f)
    p2 = zeros.at[:, 0].set(conv_prev[:, 0]).at[:, 1].set(conv_prev[:, 1]).reshape(m, f)
    full = lambda w: pl.BlockSpec((m, w), lambda j: (0, 0))
    colf = lambda r: pl.BlockSpec((r, tn), lambda j: (0, j))
    colw = lambda r: pl.BlockSpec((None, r, tn), lambda j: (layer, 0, j))
    a, u = pl.pallas_call(
        functools.partial(_ffn_s_body, seq=t),
        grid=(f // tn,),
        in_specs=[full(d), colw(d), colw(d), colw(CONV_W), colw(1), colf(m), colf(m)],
        out_specs=[colf(m), colf(m)],
        out_shape=[jax.ShapeDtypeStruct((m, f), _MXU), jax.ShapeDtypeStruct((m, f), jnp.float32)],
        compiler_params=_params(("parallel",)),
        name="ffn_in_sample",
    )(h, w_up, w_gate, conv_w, conv_b, p1, p2)
    return a, u.reshape(n_batch, t, f)[:, t - (CONV_W - 1):]


def _page_specs(n_pages_per_step, layer, page_shape):
    zeros = (0,) * len(page_shape)
    return [pl.BlockSpec((None, None) + tuple(page_shape),
                         functools.partial(lambda b, s, pt, g: (layer, pt[b, s * n_pages_per_step + g]) + zeros, g=g))
            for g in range(n_pages_per_step)]


def _dsa_s_score_body(pt_ref, qi_ref, kiw_ref, *refs, n_pg):
    pages = refs[:n_pg]
    o_ref = refs[n_pg]
    ts = qi_ref.shape[0]
    ps = pages[0].shape[1]
    qst = jnp.concatenate([qi_ref[:, h * D_IDX:(h + 1) * D_IDX] for h in range(H_IDX)], axis=0).astype(_MXU)
    wi = kiw_ref[:, D_IDX:D_IDX + H_IDX] * (D_IDX ** -0.5 * H_IDX ** -0.5)
    for g in range(n_pg):
        s = jnp.dot(qst, pages[g][...].astype(_MXU), preferred_element_type=jnp.float32)
        sc = jnp.zeros((ts, ps), jnp.float32)
        for h in range(H_IDX):
            sc = sc + wi[:, h:h + 1] * jnp.maximum(s[h * ts:(h + 1) * ts], 0.0)
        o_ref[:, g * ps:(g + 1) * ps] = sc


def _dsa_s_select_body(sc_ref, qi_ref, kiw_ref, o_ref, on_ref, key_ref, *, n_sel, n_kv, ts):
    rows = qi_ref.shape[0]
    lp = sc_ref.shape[1]
    qi = qi_ref[...]
    kx = kiw_ref[:, 0:D_IDX].astype(_MXU)
    kx = jnp.concatenate([kx, jnp.zeros((_LANES - rows, D_IDX), _MXU)], axis=0)
    wi = kiw_ref[:, D_IDX:D_IDX + H_IDX] * (D_IDX ** -0.5 * H_IDX ** -0.5)
    sc = jnp.zeros((rows, _LANES), jnp.float32)
    for h in range(H_IDX):
        s = lax.dot_general(qi[:, h * D_IDX:(h + 1) * D_IDX].astype(_MXU), kx, (((1,), (1,)), ((), ())),
                            preferred_element_type=jnp.float32)
        sc = sc + wi[:, h:h + 1] * jnp.maximum(s, 0.0)
    sc = jnp.concatenate([pltpu.roll(sc[b * ts:(b + 1) * ts], (_LANES - b * ts) % _LANES, axis=1)
                          for b in range(rows // ts)], axis=0)
    r = lax.broadcasted_iota(jnp.int32, (rows, _LANES), 0) % ts
    c = lax.broadcasted_iota(jnp.int32, (rows, _LANES), 1)
    key_ref[:, 0:lp] = _sortable(sc_ref[...])
    key_ref[:, lp:lp + _LANES] = _sortable(jnp.where(c <= r, sc, -jnp.inf))

    width = lp + _LANES
    pos = lambda: lax.broadcasted_iota(jnp.int32, (rows, width), 1)
    count = lambda m: jnp.sum(m.astype(jnp.int32), axis=1, keepdims=True)
    thr = _kth_largest_key(lambda t: count(key_ref[...] >= t), n_sel, (rows, 1))
    finite = thr > _KEY_NEG_INF
    n_ge = count(key_ref[...] >= thr)
    tie = jnp.max(jnp.where(jnp.logical_and(finite, n_ge > n_sel), 1, 0)) > 0

    def with_ties():
        need = n_sel - count(key_ref[...] > thr)
        cut = _first_index_cut(lambda p: count(jnp.logical_and(key_ref[...] == thr, pos() < p)), need, (rows, 1),
                               width)
        return jnp.where(finite, cut, jnp.int32(2 ** 30))

    cut = lax.cond(tie, with_ties, lambda: jnp.full((rows, 1), 2 ** 30, jnp.int32))
    key = key_ref[...]
    keep = jnp.logical_or(key > thr, jnp.logical_and(key == thr, pos() <= cut))
    own = keep[:, lp:width]
    on_ref[...] = jnp.where(jnp.logical_and(own, c <= r), 0.0, _NEG)
    keep_f = jnp.where(keep, 1.0, 0.0).astype(_MXU)
    spread = (lax.broadcasted_iota(jnp.int32, (_LANES, n_kv * _LANES), 1) // n_kv
              == lax.broadcasted_iota(jnp.int32, (_LANES, n_kv * _LANES), 0))
    spread = jnp.where(spread, 1.0, 0.0).astype(_MXU)
    for cc in range(lp // _LANES):
        dup = jnp.dot(keep_f[:, cc * _LANES:(cc + 1) * _LANES], spread, preferred_element_type=jnp.float32)
        o_ref[:, cc * n_kv * _LANES:(cc + 1) * n_kv * _LANES] = jnp.where(dup > 0.5, 0.0, _NEG)


def _head_rows(pieces):
    return jnp.concatenate(pieces, axis=0)


def _parity_neg(n_heads, n_kv, ts, width):
    row_kv = lax.broadcasted_iota(jnp.int32, (n_heads * ts, width), 0) // ((n_heads // n_kv) * ts)
    col_kv = lax.broadcasted_iota(jnp.int32, (n_heads * ts, width), 1) % n_kv
    return jnp.where(row_kv == col_kv, 0.0, _NEG)


def _own_block(q_of_group, kn_ref, vn_ref, extra_of_head, n_kv, n_grp, ts):
    pad = jnp.zeros((_LANES - ts, DH), _MXU)
    ms, ls, os_ = [], [], []
    for g in range(n_kv):
        kt = jnp.concatenate([kn_ref[:, g * DH:(g + 1) * DH].astype(_MXU), pad], axis=0)
        vt = jnp.concatenate([vn_ref[:, g * DH:(g + 1) * DH].astype(_MXU), pad], axis=0)
        logits = lax.dot_general(q_of_group(g), kt, (((1,), (1,)), ((), ())), preferred_element_type=jnp.float32)
        logits = logits + _head_rows([extra_of_head(g * n_grp + hh) for hh in range(n_grp)])
        m = jnp.max(logits, axis=1, keepdims=True)
        p = jnp.exp(logits - m)
        ms.append(m)
        ls.append(jnp.sum(p, axis=1, keepdims=True))
        os_.append(jnp.dot(p.astype(_MXU), vt, preferred_element_type=jnp.float32))
    return _head_rows(ms), _head_rows(ls), _head_rows(os_)


def _dsa_s_attn_body(pt_ref, qa_ref, kn_ref, vn_ref, negn_ref, neg_ref, bias_ref, *refs, n_pg):
    kpages = refs[:n_pg]
    vpages = refs[n_pg:2 * n_pg]
    o_ref, m_ref, l_ref, acc_ref = refs[2 * n_pg:]
    s = pl.program_id(1)
    n_steps = pl.num_programs(1)
    ts = qa_ref.shape[0]
    rows_pg = kpages[0].shape[0]
    n_grp = H_A // HKV_A
    scale = DH ** -0.5
    qs = [_stack_heads(qa_ref, g, n_grp, scale) for g in range(HKV_A)]
    q_all = _head_rows(qs)
    wb = bias_ref.shape[2]
    own_w = HKV_A * _LANES

    @pl.when(s == 0)
    def _():
        m, l, o = _own_block(lambda g: qs[g], kn_ref, vn_ref,
                             lambda h: bias_ref[h, :, wb - own_w:wb - own_w + _LANES] + negn_ref[...],
                             HKV_A, n_grp, ts)
        m_ref[0], l_ref[0], acc_ref[0] = m, l, o

    last = s == n_steps - 1
    w = n_pg * rows_pg
    kt = jnp.concatenate([kpages[p][...] for p in range(n_pg)], axis=0).astype(_MXU)
    vt = jnp.concatenate([vpages[p][...] for p in range(n_pg)], axis=0).astype(_MXU)
    logits = lax.dot_general(q_all, kt, (((1,), (1,)), ((), ())), preferred_element_type=jnp.float32)
    neg = neg_ref[...]

    def extra_of_head(h):
        far = bias_ref[h, :, 0:1]
        base = neg + far
        near = jnp.where(last, bias_ref[h, :, wb - own_w - rows_pg:wb - own_w] - far, 0.0)
        return jnp.concatenate([base[:, 0:w - rows_pg], base[:, w - rows_pg:w] + near], axis=1)

    extra = _head_rows([extra_of_head(h) for h in range(H_A)]) + _parity_neg(H_A, HKV_A, ts, w)
    _flash_update(m_ref, l_ref, acc_ref, 0, logits + extra, vt)

    @pl.when(last)
    def _():
        o = acc_ref[0] / l_ref[0]
        for h in range(H_A):
            o_ref[:, h * DH:(h + 1) * DH] = o[h * ts:(h + 1) * ts].astype(o_ref.dtype)


def _dsa_sample(zs, kiws, cache_k, cache_v, cache_kidx, page_table, bias, layer, n_batch, ts):
    n_pages = page_table.shape[1]
    ps = cache_kidx.shape[3]
    past = n_pages * ps
    assert ps == _LANES and ts <= _LANES and ts % 8 == 0 and cache_k.shape[2] == ps * HKV_A
    n_sel = min(TOPK_IDX, (past + ts) // 4)
    n_grp = H_A // HKV_A

    gi = _tile(n_pages, 32, 1)
    scores = pl.pallas_call(
        functools.partial(_dsa_s_score_body, n_pg=gi),
        grid_spec=pltpu.PrefetchScalarGridSpec(
            num_scalar_prefetch=1,
            grid=(n_batch, n_pages // gi),
            in_specs=[pl.BlockSpec((ts, H_IDX * D_IDX), lambda b, s, pt: (b, _C_QI // (H_IDX * D_IDX))),
                      pl.BlockSpec((ts, _LANES), lambda b, s, pt: (b, 0))]
            + _page_specs(gi, layer, (D_IDX, ps)),
            out_specs=pl.BlockSpec((None, ts, gi * ps), lambda b, s, pt: (b, 0, s)),
        ),
        out_shape=jax.ShapeDtypeStruct((n_batch, ts, past), jnp.float32),
        compiler_params=_params(("parallel", "arbitrary")),
        name="dsa_sample_scores",
    )(page_table, zs, kiws, *([cache_kidx] * gi))

    rows = ts * _tile(n_batch, max(1, 32 // ts), 1)
    assert rows <= _LANES
    neg, neg_own = pl.pallas_call(
        functools.partial(_dsa_s_select_body, n_sel=n_sel, n_kv=HKV_A, ts=ts),
        grid=(n_batch * ts // rows,),
        in_specs=[pl.BlockSpec((rows, past), lambda i: (i, 0)),
                  pl.BlockSpec((rows, H_IDX * D_IDX), lambda i: (i, _C_QI // (H_IDX * D_IDX))),
                  pl.BlockSpec((rows, _LANES), lambda i: (i, 0))],
        out_specs=[pl.BlockSpec((rows, HKV_A * past), lambda i: (i, 0)),
                   pl.BlockSpec((rows, _LANES), lambda i: (i, 0))],
        out_shape=[jax.ShapeDtypeStruct((n_batch * ts, HKV_A * past), jnp.float32),
                   jax.ShapeDtypeStruct((n_batch * ts, _LANES), jnp.float32)],
        scratch_shapes=[pltpu.VMEM((rows, past + _LANES), jnp.int32)],
        compiler_params=_params(("parallel",)),
        name="dsa_sample_select",
    )(scores.reshape(n_batch * ts, past), zs, kiws)

    ga = _tile(n_pages, 32, 1)
    wkv = HKV_A * DH
    rows_pg = ps * HKV_A
    return pl.pallas_call(
        functools.partial(_dsa_s_attn_body, n_pg=ga),
        grid_spec=pltpu.PrefetchScalarGridSpec(
            num_scalar_prefetch=1,
            grid=(n_batch, n_pages // ga),
            in_specs=[pl.BlockSpec((ts, H_A * DH), lambda b, s, pt: (b, _C_QA // (H_A * DH))),
                      pl.BlockSpec((ts, wkv), lambda b, s, pt: (b, _C_KA // wkv)),
                      pl.BlockSpec((ts, wkv), lambda b, s, pt: (b, _C_VA // wkv)),
                      pl.BlockSpec((ts, _LANES), lambda b, s, pt: (b, 0)),
                      pl.BlockSpec((ts, ga * rows_pg), lambda b, s, pt: (b, s)),
                      pl.BlockSpec((H_A, ts, bias.shape[2]), lambda b, s, pt: (0, 0, 0))]
            + _page_specs(ga, layer, (rows_pg, DH)) + _page_specs(ga, layer, (rows_pg, DH)),
            out_specs=pl.BlockSpec((ts, H_A * DH), lambda b, s, pt: (b, 0)),
            scratch_shapes=[pltpu.VMEM((1, H_A * ts, 1), jnp.float32),
                            pltpu.VMEM((1, H_A * ts, 1), jnp.float32),
                            pltpu.VMEM((1, H_A * ts, DH), jnp.float32)],
        ),
        out_shape=jax.ShapeDtypeStruct((n_batch * ts, H_A * DH), _MXU),
        compiler_params=_params(("parallel", "arbitrary")),
        name="dsa_sample_attn",
    )(page_table, zs, zs, zs, neg_own, neg, bias, *([cache_k] * ga), *([cache_v] * ga))


def _moba_s_body(pt_ref, qb_ref, kn_ref, vn_ref, bias_ref, *refs, n_pg, n_sel, n_blocks):
    kpages = refs[:n_pg]
    vpages = refs[n_pg:2 * n_pg]
    o_ref, gate_ref, mb_ref, lb_ref, ob_ref = refs[2 * n_pg:]
    s = pl.program_id(1)
    n_steps = pl.num_programs(1)
    ts = qb_ref.shape[0]
    rows_pg = kpages[0].shape[0]
    n_grp = H_B // HKV_B
    rows = H_B * ts
    blk = MOBA_BLOCK * HKV_B
    bps = n_pg * rows_pg // blk
    scale = DH ** -0.5
    wb = bias_ref.shape[2]
    own_w = HKV_B * _LANES
    qf = [jnp.concatenate([qb_ref[:, (g * n_grp + hh) * DH:(g * n_grp + hh + 1) * DH] for hh in range(n_grp)], axis=0)
          for g in range(HKV_B)]
    qs = [(q * scale).astype(_MXU) for q in qf]
    q_all = _head_rows(qs)
    lane = lax.broadcasted_iota(jnp.int32, (rows, gate_ref.shape[1]), 1)
    last = s == n_steps - 1

    @pl.when(s == 0)
    def _():
        gate_ref[...] = jnp.zeros(gate_ref.shape, jnp.float32)
        mb_ref[...] = jnp.full(mb_ref.shape, _NEG, jnp.float32)
        lb_ref[...] = jnp.zeros(lb_ref.shape, jnp.float32)

    kf = jnp.concatenate([kpages[p][...] for p in range(n_pg)], axis=0)
    vt = jnp.concatenate([vpages[p][...] for p in range(n_pg)], axis=0).astype(_MXU)
    logits = lax.dot_general(q_all, kf.astype(_MXU), (((1,), (1,)), ((), ())), preferred_element_type=jnp.float32)
    far = _head_rows([bias_ref[h, :, 0:1] for h in range(H_B)])
    near = _head_rows([bias_ref[h, :, wb - own_w - blk:wb - own_w] for h in range(H_B)])
    parity = _parity_neg(H_B, HKV_B, ts, blk)
    sub_kv = lax.broadcasted_iota(jnp.int32, (8, DH), 0) % HKV_B
    gates, ms, ls = gate_ref[...], mb_ref[...], lb_ref[...]
    per_dot = 4 if bps % 4 == 0 else bps
    for j0 in range(0, bps, per_dot):
        ps_ = []
        for jb in range(j0, j0 + per_dot):
            seg = logits[:, jb * blk:(jb + 1) * blk] + (far + parity)
            if jb == bps - 1:
                seg = seg + jnp.where(last, near - far, 0.0)
            ksum = jnp.sum(kf[jb * blk:(jb + 1) * blk].reshape(blk // 8, 8, DH), axis=0)
            gate = _head_rows([
                jnp.sum(qf[g] * (jnp.sum(jnp.where(sub_kv == g, ksum, 0.0), axis=0, keepdims=True) / MOBA_BLOCK),
                        axis=1, keepdims=True) for g in range(HKV_B)])
            mj = jnp.max(seg, axis=1, keepdims=True)
            p = jnp.exp(seg - mj)
            lj = jnp.sum(p, axis=1, keepdims=True)
            here = lane == s * bps + jb
            gates = jnp.where(here, gate, gates)
            ms = jnp.where(here, mj, ms)
            ls = jnp.where(here, lj, ls)
            ps_.append(jnp.concatenate([p if k == jb else jnp.zeros_like(p) for k in range(j0, j0 + per_dot)],
                                       axis=1))
        o_blocks = jnp.dot(jnp.concatenate(ps_, axis=0).astype(_MXU), vt[j0 * blk:(j0 + per_dot) * blk],
                           preferred_element_type=jnp.float32)
        for jb in range(j0, j0 + per_dot):
            ob_ref[s * bps + jb] = o_blocks[(jb - j0) * rows:(jb - j0 + 1) * rows]
    gate_ref[...], mb_ref[...], lb_ref[...] = gates, ms, ls

    @pl.when(last)
    def _():
        r = lax.broadcasted_iota(jnp.int32, (ts, _LANES), 0)
        c = lax.broadcasted_iota(jnp.int32, (ts, _LANES), 1)
        causal_neg = jnp.where(c <= r, 0.0, _NEG)
        m_own, l_own, o_own = _own_block(lambda g: qs[g], kn_ref, vn_ref,
                                         lambda h: bias_ref[h, :, wb - own_w:wb - own_w + _LANES] + causal_neg,
                                         HKV_B, n_grp, ts)
        chosen = _topk_lanes(gate_ref[...], n_blocks, n_sel)
        mb = jnp.where(chosen, mb_ref[...], _NEG)
        m_all = jnp.maximum(m_own, jnp.max(mb, axis=1, keepdims=True))
        w = jnp.where(chosen, jnp.exp(mb - m_all), 0.0)
        w_own = jnp.exp(m_own - m_all)
        den = w_own * l_own + jnp.sum(w * lb_ref[...], axis=1, keepdims=True)

        def add_block(j, acc):
            wj = jnp.sum(jnp.where(lane == j, w, 0.0), axis=1, keepdims=True)
            return acc + wj * ob_ref[j]

        num = lax.fori_loop(0, n_blocks, add_block, w_own * o_own)
        o = num / den
        for h in range(H_B):
            o_ref[:, h * DH:(h + 1) * DH] = o[h * ts:(h + 1) * ts].astype(o_ref.dtype)


def _moba_sample(zs, cache_k, cache_v, page_table, bias, layer, n_batch, ts):
    n_pages = page_table.shape[1]
    rows_pg = cache_k.shape[2]
    ps = rows_pg // HKV_B
    past = n_pages * ps
    assert past % MOBA_BLOCK == 0 and MOBA_BLOCK % ps == 0 and ts <= _LANES and ts % 8 == 0
    n_blocks = past // MOBA_BLOCK
    n_sel = min(MOBA_TOPK, (past + ts - 1) // MOBA_BLOCK)
    ppb = MOBA_BLOCK // ps
    g_pg = ppb * _tile(n_blocks, 16, 1)
    lane_w = -(-n_blocks // _LANES) * _LANES
    wkv = HKV_B * DH
    return pl.pallas_call(
        functools.partial(_moba_s_body, n_pg=g_pg, n_sel=n_sel, n_blocks=n_blocks),
        grid_spec=pltpu.PrefetchScalarGridSpec(
            num_scalar_prefetch=1,
            grid=(n_batch, n_pages // g_pg),
            in_specs=[pl.BlockSpec((ts, H_B * DH), lambda b, s, pt: (b, _C_QB // (H_B * DH))),
                      pl.BlockSpec((ts, wkv), lambda b, s, pt: (b, _C_KB // wkv)),
                      pl.BlockSpec((ts, wkv), lambda b, s, pt: (b, _C_VB // wkv)),
                      pl.BlockSpec((H_B, ts, bias.shape[2]), lambda b, s, pt: (1, 0, 0))]
            + _page_specs(g_pg, layer, (rows_pg, DH)) + _page_specs(g_pg, layer, (rows_pg, DH)),
            out_specs=pl.BlockSpec((ts, H_B * DH), lambda b, s, pt: (b, 0)),
            scratch_shapes=[pltpu.VMEM((H_B * ts, lane_w), jnp.float32),
                            pltpu.VMEM((H_B * ts, lane_w), jnp.float32),
                            pltpu.VMEM((H_B * ts, lane_w), jnp.float32),
                            pltpu.VMEM((n_blocks, H_B * ts, DH), jnp.float32)],
        ),
        out_shape=jax.ShapeDtypeStruct((n_batch * ts, H_B * DH), _MXU),
        compiler_params=_params(("parallel", "arbitrary")),
        name="moba_sample",
    )(page_table, zs, zs, zs, bias, *([cache_k] * g_pg), *([cache_v] * g_pg))


def _relayout_w_in(w_in, d):
    w_t = jnp.swapaxes(w_in, 1, 2)
    sizes = dict(qa=H_A * DH, ka=HKV_A * DH, va=HKV_A * DH, qi=H_IDX * D_IDX, ki=D_IDX, wi=H_IDX,
                 qb=H_B * DH, kb=HKV_B * DH, vb=HKV_B * DH, qc=H_C * DK_C, kc=H_C * DK_C, vc=H_C * DV_C,
                 gc=H_C * DV_C, gates=3 * d)
    off, parts = 0, {}
    for name in ("qa", "ka", "va", "qi", "ki", "wi", "qb", "kb", "vb", "qc", "kc", "vc", "gc", "gates"):
        parts[name] = w_t[:, off:off + sizes[name]]
        off += sizes[name]
    assert off == w_t.shape[1]
    main = jnp.concatenate([parts[n] for n in ("vc", "gc", "qa", "qi", "qb", "qc", "kc", "ka", "va", "kb", "vb",
                                               "gates")], axis=1).astype(_MXU)
    kiw = jnp.concatenate([parts["ki"], parts["wi"],
                           jnp.zeros((w_t.shape[0], _LANES - D_IDX - H_IDX, d), w_in.dtype)], axis=1).astype(_MXU)
    return main, kiw


def _pad_last(w, n):
    return jnp.pad(w, [(0, 0)] * (w.ndim - 1) + [(0, n - w.shape[-1])])


def _mix_and_ffn(x, z, o_a, o_b, o_c, wts, layer, ffn_in):
    m = _merge(o_a, o_b, o_c, z, wts["w_pa"], wts["w_pb"], wts["w_pc"], layer)
    x, h2 = _out_proj(m, wts["w_out"], layer, x, wts["norm_ffn"][layer])
    a, conv_new = ffn_in(h2)
    x = _matmul(a, wts["ffn_down"], layer, jnp.float32, residual=x, tk_target=3072)
    return x, conv_new


def _state_slices(z, kiw, n_batch, t):
    ka = z[:, _C_KA:_C_KA + HKV_A * DH].reshape(n_batch, t, HKV_A, DH)
    va = z[:, _C_VA:_C_VA + HKV_A * DH].reshape(n_batch, t, HKV_A, DH)
    ki = kiw[:, :D_IDX].reshape(n_batch, t, D_IDX)
    kb = z[:, _C_KB:_C_KB + HKV_B * DH].reshape(n_batch, t, HKV_B, DH)
    vb = z[:, _C_VB:_C_VB + HKV_B * DH].reshape(n_batch, t, HKV_B, DH)
    return ka, va, ki, kb, vb


def kernel(x_prompt, x_sample, cache_a_k, cache_a_v, cache_a_kidx, cache_b_k, cache_b_v, state_ret, state_conv,
           page_table, rel_bias, norm_mix, w_in, ret_gn, w_pa, w_pb, w_pc, w_out, norm_ffn, ffn_up, ffn_gate,
           conv_w, conv_b, ffn_down, norm_final):
    bp, tp, d = x_prompt.shape
    bs, ts, _ = x_sample.shape
    depth = w_in.shape[0]
    n_pool, ps = cache_a_k.shape[1], cache_a_k.shape[2]
    past = page_table.shape[1] * ps
    f = ffn_up.shape[2]
    fp = -(-f // 512) * 512
    tq = _TQ
    assert d % _LANES == 0 and tp % tq == 0

    ar = jnp.arange
    bias_p = _bias_table(rel_bias, (ar(tq)[None, :] - ar(2 * tq)[:, None] + tq).astype(jnp.int32))
    assert H_A == H_B and HKV_A == HKV_B
    rel_cached = (ar(ts)[:, None] + MOBA_BLOCK - ar(MOBA_BLOCK)[None, :]).astype(jnp.int32)
    rel_own = (ar(ts)[:, None] - ar(_LANES)[None, :]).astype(jnp.int32)
    bias_s = _bias_table(rel_bias, jnp.concatenate([jnp.repeat(rel_cached, HKV_A, axis=1), rel_own, rel_own], axis=1))
    cache_a_k = cache_a_k.reshape(depth, n_pool, ps * HKV_A, DH)
    cache_a_v = cache_a_v.reshape(depth, n_pool, ps * HKV_A, DH)
    cache_b_k = cache_b_k.reshape(depth, n_pool, ps * HKV_B, DH)
    cache_b_v = cache_b_v.reshape(depth, n_pool, ps * HKV_B, DH)
    cache_a_kidx = jnp.swapaxes(cache_a_kidx, 2, 3)

    xp = x_prompt.reshape(bp * tp, d)
    xs = x_sample.reshape(bs * ts, d)
    pos_p = jnp.arange(tp, dtype=jnp.int32)
    pos_s = past + jnp.arange(ts, dtype=jnp.int32)
    w_main, w_kiw = _relayout_w_in(w_in, d)
    wts = dict(w_pa=w_pa.astype(_MXU), w_pb=w_pb.astype(_MXU), w_pc=w_pc.astype(_MXU), w_out=w_out.astype(_MXU),
               norm_ffn=norm_ffn, ffn_down=jnp.pad(ffn_down, ((0, 0), (0, fp - f), (0, 0))).astype(_MXU))
    up = _pad_last(ffn_up, fp).astype(_MXU)
    gate = _pad_last(ffn_gate, fp).astype(_MXU)
    cw = _pad_last(conv_w, fp)
    cb = _pad_last(conv_b.reshape(depth, 1, f), fp)
    conv_prev_all = _pad_last(state_conv, fp)

    st_p, st_s = [], []
    kv_p = [jnp.zeros((depth, bp * tp, nh, DH), jnp.float32) for nh in (HKV_A, HKV_A, HKV_B, HKV_B)]
    for l in range(depth):
        z, kiw, *kv_p = _in_proj(xp, norm_mix[l], w_main, w_kiw, l, depth, kv_p)
        o_a = _dsa_prompt(z, kiw, bias_p, bp, tp)
        o_b = _moba_prompt(z, bias_p, bp, tp)
        o_c, ret_new = _retention(z, jnp.zeros((bp, H_C, DK_C, DV_C), jnp.float32), ret_gn[l], pos_p, bp, tp)
        xp, conv_new = _mix_and_ffn(xp, z, o_a, o_b, o_c, wts, l,
                                    lambda h2: _ffn_in_prompt(h2, up, gate, cw, cb, l, bp, tp))
        st_p.append((kiw[:, :D_IDX].reshape(bp, tp, D_IDX), ret_new, conv_new[:, :, :f]))

        h = _rmsnorm(xs, norm_mix[l], _MXU)
        z = _matmul_few_rows(h, w_main, l, jnp.float32)
        kiw = _matmul_few_rows(h, w_kiw, l, jnp.float32)
        o_a = _dsa_sample(z, kiw, cache_a_k, cache_a_v, cache_a_kidx, page_table, bias_s, l, bs, ts)
        o_b = _moba_sample(z, cache_b_k, cache_b_v, page_table, bias_s, l, bs, ts)
        o_c, ret_new = _retention(z, state_ret[l], ret_gn[l], pos_s, bs, ts)
        xs, conv_new = _mix_and_ffn(xs, z, o_a, o_b, o_c, wts, l,
                                    lambda h2: _ffn_in_sample(h2, up, gate, cw, cb, l, conv_prev_all[l], bs, ts))
        st_s.append(_state_slices(z, kiw, bs, ts) + (ret_new, conv_new[:, :, :f]))

    y_prompt = _rmsnorm(xp, norm_final, jnp.float32).reshape(bp, tp, d)
    y_sample = _rmsnorm(xs, norm_final, jnp.float32).reshape(bs, ts, d)
    ka_p, va_p, kb_p, vb_p = kv_p
    ki_p, ret_p, conv_p = [jnp.stack(v) for v in zip(*st_p)]
    outs_s = [jnp.stack(v) for v in zip(*st_s)]
    return (y_prompt, y_sample,
            ka_p.reshape(depth, bp, tp, HKV_A, DH), va_p.reshape(depth, bp, tp, HKV_A, DH), ki_p,
            kb_p.reshape(depth, bp, tp, HKV_B, DH), vb_p.reshape(depth, bp, tp, HKV_B, DH), ret_p, conv_p, *outs_s)
```

```python
import functools
import math

import jax
import jax.numpy as jnp
from jax import lax
from jax.experimental import pallas as pl
from jax.experimental.pallas import tpu as pltpu

DH = 128
H_A, HKV_A = 8, 2
H_IDX, D_IDX = 16, 64
TOPK_IDX = 256
H_B, HKV_B = 8, 2
MOBA_BLOCK = 256
MOBA_TOPK = 3
H_C, DK_C, DV_C = 8, 128, 256
N_BUCKETS = 32
MAX_DIST = 128
CONV_W = 3
EPS = 1e-6
ROPE_BASE = 10000.0
RET_CHUNK = 128

_MXU = jnp.bfloat16
_NEG = -1e30
_TQ = 256
_LANES = 128
_VMEM_LIMIT = 56 * 1024 * 1024

_C_VC = 0
_C_GC = _C_VC + H_C * DV_C
_C_QA = _C_GC + H_C * DV_C
_C_QI = _C_QA + H_A * DH
_C_QB = _C_QI + H_IDX * D_IDX
_C_QC = _C_QB + H_B * DH
_C_KC = _C_QC + H_C * DK_C
_C_KA = _C_KC + H_C * DK_C
_C_VA = _C_KA + HKV_A * DH
_C_KB = _C_VA + HKV_A * DH
_C_VB = _C_KB + HKV_B * DH
_C_GATES = _C_VB + HKV_B * DH


def _params(sem):
    return pltpu.CompilerParams(dimension_semantics=sem, vmem_limit_bytes=_VMEM_LIMIT)


def _tile(n, target, mult=_LANES):
    best = None
    t = mult
    while t <= min(n, target):
        if n % t == 0:
            best = t
        t += mult
    return n if best is None else best


def _rmsnorm_body(x_ref, g_ref, o_ref):
    x = x_ref[...]
    y = x * lax.rsqrt(jnp.mean(x * x, axis=-1, keepdims=True) + EPS)
    o_ref[...] = (y * g_ref[...]).astype(o_ref.dtype)


def _rmsnorm(x, g, out_dtype):
    m, d = x.shape
    tm = _tile(m, 512, 8)
    return pl.pallas_call(
        _rmsnorm_body,
        grid=(m // tm,),
        in_specs=[pl.BlockSpec((tm, d), lambda i: (i, 0)), pl.BlockSpec((1, d), lambda i: (0, 0))],
        out_specs=pl.BlockSpec((tm, d), lambda i: (i, 0)),
        out_shape=jax.ShapeDtypeStruct((m, d), out_dtype),
        compiler_params=_params(("parallel",)),
        name="rmsnorm",
    )(x, g.reshape(1, d))


def _dot_nt(x, w):
    return lax.dot_general(x, w, (((1,), (1,)), ((), ())), preferred_element_type=jnp.float32)


def _mm_body(*refs, nk, has_res):
    x_ref, w_ref = refs[0], refs[1]
    r_ref = refs[2] if has_res else None
    o_ref = refs[3] if has_res else refs[2]
    acc_ref = refs[-1]
    k = pl.program_id(2)
    d = jnp.dot(x_ref[...], w_ref[...], preferred_element_type=jnp.float32)

    def finish(r):
        if has_res:
            r = r + r_ref[...]
        o_ref[...] = r.astype(o_ref.dtype)

    if nk == 1:
        finish(d)
        return

    @pl.when(k == 0)
    def _():
        acc_ref[...] = d

    @pl.when(jnp.logical_and(k > 0, k < nk - 1))
    def _():
        acc_ref[...] += d

    @pl.when(k == nk - 1)
    def _():
        finish(acc_ref[...] + d)


def _mm_few_rows_body(x_ref, w_ref, o_ref):
    x = x_ref[...]
    x = jnp.concatenate([x, jnp.zeros((_LANES - x.shape[0], x.shape[1]), x.dtype)], axis=0)
    r_t = lax.dot_general(w_ref[...], x, (((1,), (1,)), ((), ())), preferred_element_type=jnp.float32)
    o_ref[...] = r_t.T[0:o_ref.shape[0]].astype(o_ref.dtype)


def _matmul_few_rows(x, w, layer, out_dtype, tn_target=1024):
    m, kdim = x.shape
    n = w.shape[1]
    tn = _tile(n, tn_target)
    assert m <= _LANES and m % 8 == 0
    return pl.pallas_call(
        _mm_few_rows_body,
        grid=(n // tn,),
        in_specs=[pl.BlockSpec((m, kdim), lambda j: (0, 0)), pl.BlockSpec((None, tn, kdim), lambda j: (layer, j, 0))],
        out_specs=pl.BlockSpec((m, tn), lambda j: (0, j)),
        out_shape=jax.ShapeDtypeStruct((m, n), out_dtype),
        compiler_params=_params(("parallel",)),
        name="matmul_few_rows",
    )(x, w)


def _matmul(x, w, layer, out_dtype, residual=None, tm_target=1024, tn_target=1024, tk_target=2048):
    m, kdim = x.shape
    n = w.shape[2]
    tm = _tile(m, tm_target, 8)
    tn = _tile(n, tn_target)
    tk = _tile(kdim, tk_target)
    nk = kdim // tk
    in_specs = [pl.BlockSpec((tm, tk), lambda i, j, k: (i, k)),
                pl.BlockSpec((None, tk, tn), lambda i, j, k: (layer, k, j))]
    args = [x, w]
    if residual is not None:
        in_specs.append(pl.BlockSpec((tm, tn), lambda i, j, k: (i, j)))
        args.append(residual)
    return pl.pallas_call(
        functools.partial(_mm_body, nk=nk, has_res=residual is not None),
        grid=(m // tm, n // tn, nk),
        in_specs=in_specs,
        out_specs=pl.BlockSpec((tm, tn), lambda i, j, k: (i, j)),
        out_shape=jax.ShapeDtypeStruct((m, n), out_dtype),
        scratch_shapes=[pltpu.VMEM((tm, tn), jnp.float32)] if nk > 1 else [],
        compiler_params=_params(("parallel", "parallel", "arbitrary")),
        name="matmul",
    )(*args)


def _out_proj_body(m_ref, w_ref, x_ref, g_ref, o_ref, h_ref):
    r = x_ref[...] + jnp.dot(m_ref[...], w_ref[...], preferred_element_type=jnp.float32)
    o_ref[...] = r
    y = r * lax.rsqrt(jnp.mean(r * r, axis=-1, keepdims=True) + EPS)
    h_ref[...] = (y * g_ref[...]).astype(h_ref.dtype)


def _out_proj(m_in, w, layer, x, g):
    m, kdim = m_in.shape
    d = w.shape[2]
    tm = _tile(m, 512, 8)
    return pl.pallas_call(
        _out_proj_body,
        grid=(m // tm,),
        in_specs=[pl.BlockSpec((tm, kdim), lambda i: (i, 0)),
                  pl.BlockSpec((None, kdim, d), lambda i: (layer, 0, 0)),
                  pl.BlockSpec((tm, d), lambda i: (i, 0)),
                  pl.BlockSpec((1, d), lambda i: (0, 0))],
        out_specs=[pl.BlockSpec((tm, d), lambda i: (i, 0)), pl.BlockSpec((tm, d), lambda i: (i, 0))],
        out_shape=[jax.ShapeDtypeStruct((m, d), jnp.float32), jax.ShapeDtypeStruct((m, d), _MXU)],
        compiler_params=_params(("parallel",)),
        name="out_proj",
    )(m_in, w, x, g.reshape(1, d))


def _in_proj_body(*refs, n_prev, j_state):
    x_ref, g_ref, w_ref, wk_ref = refs[:4]
    z_ref, kiw_ref, ka_ref, va_ref, kb_ref, vb_ref, h_ref = refs[4 + n_prev:]

    @pl.when(pl.program_id(1) == 0)
    def _():
        x = x_ref[...]
        y = x * lax.rsqrt(jnp.mean(x * x, axis=-1, keepdims=True) + EPS)
        h_ref[...] = (y * g_ref[...]).astype(h_ref.dtype)
        kiw_ref[...] = _dot_nt(h_ref[...], wk_ref[...])

    r = _dot_nt(h_ref[...], w_ref[...])
    z_ref[...] = r

    @pl.when(pl.program_id(1) == j_state)
    def _():
        col = 0
        for ref in (ka_ref, va_ref, kb_ref, vb_ref):
            for g in range(ref.shape[1]):
                ref[:, g, :] = r[:, col:col + DH]
                col += DH


def _in_proj(x, g, w_main, w_kiw, layer, depth, prev_states):
    m, kdim = x.shape
    n = w_main.shape[1]
    tm = _tile(m, 1024, 8)
    tn = 2 * HKV_A * DH + 2 * HKV_B * DH
    assert n % tn == 0 and _C_KA % tn == 0 and _C_VB + HKV_B * DH == _C_KA + tn
    kv_heads = (HKV_A, HKV_A, HKV_B, HKV_B)
    prev = list(prev_states)
    state_spec = lambda nh: pl.BlockSpec((None, tm, nh, DH), lambda i, j: (layer, i, 0, 0))
    return pl.pallas_call(
        functools.partial(_in_proj_body, n_prev=len(prev), j_state=_C_KA // tn),
        grid=(m // tm, n // tn),
        in_specs=[pl.BlockSpec((tm, kdim), lambda i, j: (i, 0)),
                  pl.BlockSpec((1, kdim), lambda i, j: (0, 0)),
                  pl.BlockSpec((None, tn, kdim), lambda i, j: (layer, j, 0)),
                  pl.BlockSpec((None, _LANES, kdim), lambda i, j: (layer, 0, 0))]
        + [pl.BlockSpec(memory_space=pl.ANY)] * len(prev),
        out_specs=[pl.BlockSpec((tm, tn), lambda i, j: (i, j)), pl.BlockSpec((tm, _LANES), lambda i, j: (i, 0))]
        + [state_spec(nh) for nh in kv_heads],
        out_shape=[jax.ShapeDtypeStruct((m, n), jnp.float32), jax.ShapeDtypeStruct((m, _LANES), jnp.float32)]
        + [jax.ShapeDtypeStruct((depth, m, nh, DH), jnp.float32) for nh in kv_heads],
        scratch_shapes=[pltpu.VMEM((tm, kdim), _MXU)],
        input_output_aliases={4 + k: 2 + k for k in range(len(prev))},
        compiler_params=_params(("parallel", "arbitrary")),
        name="in_proj",
    )(x, g.reshape(1, kdim), w_main, w_kiw, *prev)


def _t5_bucket(rel):
    n = jnp.maximum(rel, 0)
    max_exact = N_BUCKETS // 2
    nf = jnp.maximum(n, 1).astype(jnp.float32)
    large = max_exact + (jnp.log(nf / max_exact) / math.log(MAX_DIST / max_exact)
                         * (N_BUCKETS - max_exact)).astype(jnp.int32)
    large = jnp.minimum(large, N_BUCKETS - 1)
    return jnp.where(n < max_exact, n, large)


def _bias_body(tab_ref, bucket_ref, o_ref, *, log2_rel_to):
    h = pl.program_id(0)
    bucket = bucket_ref[...]
    acc = jnp.zeros(bucket.shape, jnp.float32)
    for b in range(N_BUCKETS):
        acc = jnp.where(bucket == b, tab_ref[b, h], acc)
    if log2_rel_to is not None:
        r0, c0 = log2_rel_to
        acc = (acc - acc[r0:r0 + 1, c0:c0 + 1]) * math.log2(math.e)
    o_ref[...] = acc


def _bias_table(rel_bias, rel, log2_rel_to=None):
    bucket = _t5_bucket(rel)
    r, c = rel.shape
    nh = rel_bias.shape[1]
    return pl.pallas_call(
        functools.partial(_bias_body, log2_rel_to=log2_rel_to),
        grid=(nh,),
        in_specs=[pl.BlockSpec(memory_space=pltpu.SMEM), pl.BlockSpec((r, c), lambda h: (0, 0))],
        out_specs=pl.BlockSpec((None, r, c), lambda h: (h, 0, 0)),
        out_shape=jax.ShapeDtypeStruct((nh, r, c), jnp.float32),
        compiler_params=_params(("parallel",)),
        name="bias_table",
    )(rel_bias, bucket)


def _sortable(x):
    b = lax.bitcast_convert_type(x, jnp.int32)
    return b ^ ((b >> 31) & jnp.int32(0x7FFFFFFF))


_KEY_NEG_INF = -2147483648 + 0x7FFFFF


def _stack_heads(q_ref, g, n_grp, scale):
    parts = [q_ref[:, (g * n_grp + hh) * DH:(g * n_grp + hh + 1) * DH] for hh in range(n_grp)]
    return (jnp.concatenate(parts, axis=0) * scale).astype(_MXU)


def _flash_update(m_ref, l_ref, acc_ref, g, logits, v):
    m_old = m_ref[g]
    m_new = jnp.maximum(m_old, jnp.max(logits, axis=1, keepdims=True))
    alpha = jnp.exp(m_old - m_new)
    p = jnp.exp(logits - m_new)
    l_ref[g] = alpha * l_ref[g] + jnp.sum(p, axis=1, keepdims=True)
    acc_ref[g] = alpha * acc_ref[g] + jnp.dot(p.astype(_MXU), v, preferred_element_type=jnp.float32)
    m_ref[g] = m_new


def _flash_init(m_ref, l_ref, acc_ref):
    m_ref[...] = jnp.full(m_ref.shape, _NEG, jnp.float32)
    l_ref[...] = jnp.zeros(l_ref.shape, jnp.float32)
    acc_ref[...] = jnp.zeros(acc_ref.shape, jnp.float32)


def _flash_store(o_ref, l_ref, acc_ref, n_kv, n_grp, rows):
    for g in range(n_kv):
        o = acc_ref[g] / l_ref[g]
        for hh in range(n_grp):
            h = g * n_grp + hh
            o_ref[:, h * DH:(h + 1) * DH] = o[hh * rows:(hh + 1) * rows].astype(o_ref.dtype)


def _flash_update_t(m_ref, l_ref, acc_ref, g, logits_t, v_t, extra_of_chunk):
    ps, alphas = [], []
    for c in range(logits_t.shape[1] // _LANES):
        sl = slice(c * _LANES, (c + 1) * _LANES)
        x = logits_t[:, sl] + extra_of_chunk(c)
        m_old = m_ref[g, :, sl]
        m_new = jnp.maximum(m_old, jnp.max(x, axis=0, keepdims=True))
        alpha = jnp.exp2(m_old - m_new)
        p = jnp.exp2(x - m_new)
        l_ref[g, :, sl] = alpha * l_ref[g, :, sl] + jnp.sum(p, axis=0, keepdims=True)
        m_ref[g, :, sl] = m_new
        ps.append(p.astype(_MXU))
        alphas.append(alpha)
    p_all = jnp.concatenate(ps, axis=1)
    acc_ref[g] = (jnp.concatenate(alphas, axis=1) * acc_ref[g]
                  + jnp.dot(v_t, p_all, preferred_element_type=jnp.float32))


def _flash_store_t(o_ref, l_ref, acc_ref, n_kv, n_grp, rows):
    for g in range(n_kv):
        o = acc_ref[g] / l_ref[g]
        for hh in range(n_grp):
            h = g * n_grp + hh
            o_ref[:, h * DH:(h + 1) * DH] = o[:, hh * rows:(hh + 1) * rows].T.astype(o_ref.dtype)


def _kth_largest_key(count_ge, n_sel, shape):
    int_min = jnp.int32(-2147483648)
    prefix = jnp.full(shape, int_min, jnp.int32)
    zero = jnp.zeros(shape, jnp.int32)
    prefix = jnp.where(count_ge(zero) >= n_sel, zero, prefix)

    def step(it, prefix):
        cand = prefix | (jnp.int32(1) << (30 - it))
        return jnp.where(count_ge(cand) >= n_sel, cand, prefix)

    return lax.fori_loop(0, 31, step, prefix)


def _first_index_cut(count_lt, need, shape, n_idx):
    nbits = max(1, int(n_idx).bit_length())
    cut = jnp.zeros(shape, jnp.int32)

    def step(it, cut):
        cand = cut + (jnp.int32(1) << (nbits - 1 - it))
        return jnp.where(count_lt(cand) < need, cand, cut)

    return lax.fori_loop(0, nbits, step, cut)


def _stage_kv(k_in_ref, v_in_ref, kb_ref, vt_ref, tq):
    for j in range(kb_ref.shape[0]):
        kb_ref[j] = k_in_ref[j * tq:(j + 1) * tq, :].astype(_MXU)
        vt_ref[j] = v_in_ref[j * tq:(j + 1) * tq, :].T.astype(_MXU)


def _attend_t(j, qs, kb_ref, vt_ref, m_ref, l_ref, acc_ref, extra_of_chunk):
    for g in range(len(qs)):
        logits_t = lax.dot_general(kb_ref[j, :, g * DH:(g + 1) * DH], qs[g], (((1,), (1,)), ((), ())),
                                   preferred_element_type=jnp.float32)
        _flash_update_t(m_ref, l_ref, acc_ref, g, logits_t, vt_ref[j, g * DH:(g + 1) * DH, :],
                        functools.partial(extra_of_chunk, g))


def _dsa_p_body(qa_ref, qi_ref, kiwq_ref, kiw_ref, ka_ref, va_ref, bias_ref, o_ref,
                kx_ref, kb_ref, vt_ref, key_ref, neg_ref, m_ref, l_ref, acc_ref, *, n_sel, t):
    i = pl.program_id(1)
    tq = _TQ
    n_grp = H_A // HKV_A
    krow = lax.broadcasted_iota(jnp.int32, (tq, tq), 0)
    qcol = lax.broadcasted_iota(jnp.int32, (tq, tq), 1)

    @pl.when(i == 0)
    def _():
        _stage_kv(ka_ref, va_ref, kb_ref, vt_ref, tq)
        for j in range(kx_ref.shape[0]):
            kx_ref[j] = kiw_ref[j * tq:(j + 1) * tq, 0:D_IDX].astype(_MXU)

    qi = [qi_ref[:, h * D_IDX:(h + 1) * D_IDX].astype(_MXU) for h in range(H_IDX)]
    wt = kiwq_ref[...].T[D_IDX:D_IDX + H_IDX, :] * (D_IDX ** -0.5 * H_IDX ** -0.5)

    def score_tile(j, carry):
        kx = kx_ref[j]
        sc = jnp.zeros((tq, tq), jnp.float32)
        for h in range(H_IDX):
            s = lax.dot_general(kx, qi[h], (((1,), (1,)), ((), ())), preferred_element_type=jnp.float32)
            sc = sc + wt[h:h + 1, :] * jnp.maximum(s, 0.0)
        causal = jnp.logical_or(j < i, krow <= qcol)
        key_ref[j] = _sortable(jnp.where(causal, sc, -jnp.inf))
        return carry

    lax.fori_loop(0, i + 1, score_tile, 0)

    def count_tiles(pred):
        def body(j, acc):
            return acc + jnp.sum(pred(key_ref[j], j).astype(jnp.int32).reshape(tq // 8, 8, tq), axis=0)
        acc = lax.fori_loop(0, i + 1, body, jnp.zeros((8, tq), jnp.int32))
        return jnp.sum(acc, axis=0, keepdims=True)

    thr = _kth_largest_key(lambda c: count_tiles(lambda k, j: k >= c), n_sel, (1, tq))
    n_ge = count_tiles(lambda k, j: k >= thr)
    finite = thr > _KEY_NEG_INF
    tie = jnp.max(jnp.where(jnp.logical_and(finite, n_ge > n_sel), 1, 0)) > 0

    def write_neg(cut):
        def body(j, carry):
            k = key_ref[j]
            kpos = j * tq + krow
            keep = jnp.logical_or(k > thr, jnp.logical_and(k == thr, kpos <= cut))
            keep = jnp.logical_and(keep, jnp.logical_or(j < i, krow <= qcol))
            neg_ref[j] = jnp.where(keep, 0.0, _NEG)
            return carry
        lax.fori_loop(0, i + 1, body, 0)

    def with_ties():
        n_gt = count_tiles(lambda k, j: k > thr)
        need = n_sel - n_gt
        cut = _first_index_cut(
            lambda p: count_tiles(lambda k, j: jnp.logical_and(k == thr, j * tq + krow < p)),
            need, (1, tq), t)
        write_neg(jnp.where(finite, cut, jnp.int32(2 ** 30)))

    def without_ties():
        write_neg(jnp.full((1, tq), 2 ** 30, jnp.int32))

    lax.cond(tie, with_ties, without_ties)

    _flash_init(m_ref, l_ref, acc_ref)
    scale = DH ** -0.5 * math.log2(math.e)
    qs = [_stack_heads(qa_ref, g, n_grp, scale) for g in range(HKV_A)]
    per_head = tq // _LANES

    def attend(j, bias_of_chunk):
        neg = neg_ref[j]

        def extra(g, c):
            half = slice((c % per_head) * _LANES, (c % per_head + 1) * _LANES)
            if bias_of_chunk is None:
                return neg[:, half]
            return bias_of_chunk(g * n_grp + c // per_head, half) + neg[:, half]
        _attend_t(j, qs, kb_ref, vt_ref, m_ref, l_ref, acc_ref, extra)

    def far(j, carry):
        attend(j, None)
        return carry

    lax.fori_loop(0, jnp.maximum(i - 1, 0), far, 0)

    @pl.when(i >= 1)
    def _():
        attend(i - 1, lambda h, half: bias_ref[h, 0:tq, half])

    attend(i, lambda h, half: bias_ref[h, tq:2 * tq, half])
    _flash_store_t(o_ref, l_ref, acc_ref, HKV_A, n_grp, tq)


def _kv_scratch(nk, tq, n_kv, n_grp):
    return [
        pltpu.VMEM((nk, tq, n_kv * DH), _MXU),
        pltpu.VMEM((nk, n_kv * DH, tq), _MXU),
    ], [
        pltpu.VMEM((n_kv, 1, n_grp * tq), jnp.float32),
        pltpu.VMEM((n_kv, 1, n_grp * tq), jnp.float32),
        pltpu.VMEM((n_kv, DH, n_grp * tq), jnp.float32),
    ]


def _dsa_prompt(z, kiw, bias, n_batch, t):
    tq = _TQ
    nq = t // tq
    n_sel = min(TOPK_IDX, t // 4)
    assert n_sel <= tq and t % tq == 0
    n_grp = H_A // HKV_A
    kv_scratch, flash_scratch = _kv_scratch(nq, tq, HKV_A, n_grp)
    return pl.pallas_call(
        functools.partial(_dsa_p_body, n_sel=n_sel, t=t),
        grid=(n_batch, nq),
        in_specs=[
            pl.BlockSpec((tq, H_A * DH), lambda b, i: (b * nq + i, _C_QA // (H_A * DH))),
            pl.BlockSpec((tq, H_IDX * D_IDX), lambda b, i: (b * nq + i, _C_QI // (H_IDX * D_IDX))),
            pl.BlockSpec((tq, _LANES), lambda b, i: (b * nq + i, 0)),
            pl.BlockSpec((t, _LANES), lambda b, i: (b, 0)),
            pl.BlockSpec((t, HKV_A * DH), lambda b, i: (b, _C_KA // (HKV_A * DH))),
            pl.BlockSpec((t, HKV_A * DH), lambda b, i: (b, _C_VA // (HKV_A * DH))),
            pl.BlockSpec((H_A, 2 * tq, tq), lambda b, i: (0, 0, 0)),
        ],
        out_specs=pl.BlockSpec((tq, H_A * DH), lambda b, i: (b * nq + i, 0)),
        out_shape=jax.ShapeDtypeStruct((n_batch * t, H_A * DH), _MXU),
        scratch_shapes=[pltpu.VMEM((nq, tq, D_IDX), _MXU)] + kv_scratch + [
            pltpu.VMEM((nq, tq, tq), jnp.int32),
            pltpu.VMEM((nq, tq, tq), jnp.float32),
        ] + flash_scratch,
        compiler_params=_params(("parallel", "arbitrary")),
        name="dsa_prompt",
    )(z, z, kiw, kiw, z, z, bias)


def _topk_lanes(gate, n_cand, n_sel):
    lane = lax.broadcasted_iota(jnp.int32, gate.shape, 1)
    live = lane < n_cand
    chosen = jnp.zeros(gate.shape, jnp.bool_)
    big = jnp.int32(2 ** 30)
    for _ in range(n_sel):
        cand = jnp.logical_and(live, jnp.logical_not(chosen))
        best = jnp.max(jnp.where(cand, gate, -jnp.inf), axis=1, keepdims=True)
        first = jnp.min(jnp.where(jnp.logical_and(cand, gate == best), lane, big), axis=1, keepdims=True)
        chosen = jnp.logical_or(chosen, lane == first)
    return chosen


def _topk_rows(gate, n_cand, n_sel):
    row = lax.broadcasted_iota(jnp.int32, gate.shape, 0)
    live = row < n_cand
    chosen = jnp.zeros(gate.shape, jnp.bool_)
    big = jnp.int32(2 ** 30)
    for _ in range(n_sel):
        cand = jnp.logical_and(live, jnp.logical_not(chosen))
        best = jnp.max(jnp.where(cand, gate, -jnp.inf), axis=0, keepdims=True)
        first = jnp.min(jnp.where(jnp.logical_and(cand, gate == best), row, big), axis=0, keepdims=True)
        chosen = jnp.logical_or(chosen, row == first)
    return chosen


def _moba_p_body(qb_ref, k_in_ref, v_in_ref, bias_ref, o_ref, kmean_ref, kb_ref, vt_ref, allow_ref,
                 m_ref, l_ref, acc_ref, *, n_sel):
    i = pl.program_id(1)
    tq = _TQ
    nb = kb_ref.shape[0]
    n_grp = H_B // HKV_B
    krow = lax.broadcasted_iota(jnp.int32, (tq, tq), 0)
    qcol = lax.broadcasted_iota(jnp.int32, (tq, tq), 1)

    @pl.when(i == 0)
    def _():
        _stage_kv(k_in_ref, v_in_ref, kb_ref, vt_ref, tq)
        kmean_ref[...] = jnp.zeros(kmean_ref.shape, jnp.float32)
        for j in range(nb):
            kmean_ref[j:j + 1, :] = jnp.mean(k_in_ref[j * tq:(j + 1) * tq, :], axis=0, keepdims=True)

    for g in range(HKV_B):
        qg = jnp.concatenate([qb_ref[:, (g * n_grp + hh) * DH:(g * n_grp + hh + 1) * DH] for hh in range(n_grp)],
                             axis=0)
        gate_t = lax.dot_general(kmean_ref[:, g * DH:(g + 1) * DH], qg, (((1,), (1,)), ((), ())),
                                 preferred_element_type=jnp.float32, precision=lax.Precision.HIGHEST)
        allow_ref[g] = jnp.where(_topk_rows(gate_t, i, n_sel), 0.0, _NEG)

    _flash_init(m_ref, l_ref, acc_ref)
    scale = DH ** -0.5 * math.log2(math.e)
    qs = [_stack_heads(qb_ref, g, n_grp, scale) for g in range(HKV_B)]
    causal_neg = jnp.where(krow <= qcol, 0.0, _NEG)
    per_head = tq // _LANES

    def head_half(g, c):
        return g * n_grp + c // per_head, slice((c % per_head) * _LANES, (c % per_head + 1) * _LANES)

    def attend_past(j, bias_of_chunk):
        allow = [allow_ref[g, pl.ds(j, 1), :] for g in range(HKV_B)]

        def extra(g, c):
            picked = allow[g][:, c * _LANES:(c + 1) * _LANES]
            if bias_of_chunk is None:
                return picked
            return bias_of_chunk(*head_half(g, c)) + picked
        _attend_t(j, qs, kb_ref, vt_ref, m_ref, l_ref, acc_ref, extra)

    def far(j, carry):
        attend_past(j, None)
        return carry

    lax.fori_loop(0, jnp.maximum(i - 1, 0), far, 0)

    @pl.when(i >= 1)
    def _():
        attend_past(i - 1, lambda h, half: bias_ref[h, 0:tq, half])

    def extra_own(g, c):
        h, half = head_half(g, c)
        return bias_ref[h, tq:2 * tq, half] + causal_neg[:, half]

    _attend_t(i, qs, kb_ref, vt_ref, m_ref, l_ref, acc_ref, extra_own)
    _flash_store_t(o_ref, l_ref, acc_ref, HKV_B, n_grp, tq)


def _moba_prompt(z, bias, n_batch, t):
    tq = _TQ
    assert t % tq == 0 and MOBA_BLOCK == tq
    nb = t // tq
    n_sel = min(MOBA_TOPK, (t - 1) // MOBA_BLOCK)
    n_grp = H_B // HKV_B
    nb_pad = -(-nb // 8) * 8
    kv_scratch, flash_scratch = _kv_scratch(nb, tq, HKV_B, n_grp)
    return pl.pallas_call(
        functools.partial(_moba_p_body, n_sel=n_sel),
        grid=(n_batch, nb),
        in_specs=[
            pl.BlockSpec((tq, H_B * DH), lambda b, i: (b * nb + i, _C_QB // (H_B * DH))),
            pl.BlockSpec((t, HKV_B * DH), lambda b, i: (b, _C_KB // (HKV_B * DH))),
            pl.BlockSpec((t, HKV_B * DH), lambda b, i: (b, _C_VB // (HKV_B * DH))),
            pl.BlockSpec((H_B, 2 * tq, tq), lambda b, i: (1, 0, 0)),
        ],
        out_specs=pl.BlockSpec((tq, H_B * DH), lambda b, i: (b * nb + i, 0)),
        out_shape=jax.ShapeDtypeStruct((n_batch * t, H_B * DH), _MXU),
        scratch_shapes=[pltpu.VMEM((nb_pad, HKV_B * DH), jnp.float32)] + kv_scratch
        + [pltpu.VMEM((HKV_B, nb_pad, n_grp * tq), jnp.float32)] + flash_scratch,
        compiler_params=_params(("parallel", "arbitrary")),
        name="moba_prompt",
    )(z, z, z, bias)


def _ret_body(q_ref, k_ref, v_ref, g_ref, cos_ref, sin_ref, intra_ref, cross_ref, tail_ref, decay_ref, gn_ref,
              s0_ref, o_ref, s_out_ref, s_ref):
    t = pl.program_id(1)

    @pl.when(t == 0)
    def _():
        s_ref[...] = s0_ref[...]

    cos = cos_ref[...]
    sin = sin_ref[...]
    for h in range(H_C):
        qh = q_ref[:, h * DK_C:(h + 1) * DK_C]
        kh = k_ref[:, h * DK_C:(h + 1) * DK_C]
        q = qh * cos + pltpu.roll(qh, DK_C // 2, axis=1) * sin
        k = (kh * cos + pltpu.roll(kh, DK_C // 2, axis=1) * sin) * (DK_C ** -0.5)
        v = v_ref[:, h * DV_C:(h + 1) * DV_C].astype(_MXU)
        s = s_ref[h]
        qm = q.astype(_MXU)
        att = lax.dot_general(qm, k.astype(_MXU), (((1,), (1,)), ((), ())),
                              preferred_element_type=jnp.float32) * intra_ref[h]
        o = (jnp.dot(att.astype(_MXU), v, preferred_element_type=jnp.float32)
             + jnp.dot(qm, s.astype(_MXU), preferred_element_type=jnp.float32) * cross_ref[h])
        kt = (k * tail_ref[h]).astype(_MXU)
        s_ref[h] = s * decay_ref[h] + lax.dot_general(kt, v, (((0,), (0,)), ((), ())),
                                                      preferred_element_type=jnp.float32)
        o = o * lax.rsqrt(jnp.mean(o * o, axis=-1, keepdims=True) + EPS)
        gate = g_ref[:, h * DV_C:(h + 1) * DV_C]
        o = o * gn_ref[:, h * DV_C:(h + 1) * DV_C] * (gate * jax.nn.sigmoid(gate))
        o_ref[:, h * DV_C:(h + 1) * DV_C] = o.astype(o_ref.dtype)

    @pl.when(t == pl.num_programs(1) - 1)
    def _():
        s_out_ref[...] = s_ref[...]


def _retention(z, s0, ret_gn, pos, n_batch, t):
    c = math.gcd(t, RET_CHUNK)
    nc = t // c
    half = DK_C // 2
    inv = ROPE_BASE ** (-jnp.arange(0, DK_C, 2, dtype=jnp.float32) / DK_C)
    ang = pos.astype(jnp.float32)[:, None] * inv[None, :]
    cos = jnp.concatenate([jnp.cos(ang), jnp.cos(ang)], axis=1)
    sin = jnp.concatenate([-jnp.sin(ang), jnp.sin(ang)], axis=1)
    assert cos.shape == (t, 2 * half)
    log_g = jnp.log1p(-jnp.exp2(-5.0 - jnp.arange(H_C, dtype=jnp.float32)))
    idx = jnp.arange(c, dtype=jnp.float32)
    diff = idx[:, None] - idx[None, :]
    intra = jnp.where(diff[None] >= 0, jnp.exp(jnp.maximum(diff, 0.0)[None] * log_g[:, None, None]), 0.0)
    cross = jnp.exp((idx[None, :] + 1.0) * log_g[:, None])[:, :, None]
    tail = jnp.exp((c - 1.0 - idx)[None, :] * log_g[:, None])[:, :, None]
    decay = jnp.exp(c * log_g)[:, None, None]
    wq = H_C * DK_C
    wv = H_C * DV_C
    return pl.pallas_call(
        _ret_body,
        grid=(n_batch, nc),
        in_specs=[
            pl.BlockSpec((c, wq), lambda b, i: (b * nc + i, _C_QC // wq)),
            pl.BlockSpec((c, wq), lambda b, i: (b * nc + i, _C_KC // wq)),
            pl.BlockSpec((c, wv), lambda b, i: (b * nc + i, _C_VC // wv)),
            pl.BlockSpec((c, wv), lambda b, i: (b * nc + i, _C_GC // wv)),
            pl.BlockSpec((c, DK_C), lambda b, i: (i, 0)),
            pl.BlockSpec((c, DK_C), lambda b, i: (i, 0)),
            pl.BlockSpec((H_C, c, c), lambda b, i: (0, 0, 0)),
            pl.BlockSpec((H_C, c, 1), lambda b, i: (0, 0, 0)),
            pl.BlockSpec((H_C, c, 1), lambda b, i: (0, 0, 0)),
            pl.BlockSpec((H_C, 1, 1), lambda b, i: (0, 0, 0)),
            pl.BlockSpec((1, wv), lambda b, i: (0, 0)),
            pl.BlockSpec((None, H_C, DK_C, DV_C), lambda b, i: (b, 0, 0, 0)),
        ],
        out_specs=[
            pl.BlockSpec((c, wv), lambda b, i: (b * nc + i, 0)),
            pl.BlockSpec((None, H_C, DK_C, DV_C), lambda b, i: (b, 0, 0, 0)),
        ],
        out_shape=[
            jax.ShapeDtypeStruct((n_batch * t, wv), _MXU),
            jax.ShapeDtypeStruct((n_batch, H_C, DK_C, DV_C), jnp.float32),
        ],
        scratch_shapes=[pltpu.VMEM((H_C, DK_C, DV_C), jnp.float32)],
        compiler_params=_params(("parallel", "arbitrary")),
        name="retention",
    )(z, z, z, z, cos, sin, intra, cross, tail, decay, ret_gn.reshape(1, wv), s0)


def _merge_body(oa_ref, ob_ref, oc_ref, wa_ref, wb_ref, wc_ref, ga_ref, gb_ref, gc_ref, o_ref):
    def term(o, w, g):
        return jax.nn.sigmoid(g[...]) * jnp.dot(o[...], w[...], preferred_element_type=jnp.float32)
    o_ref[...] = (term(oa_ref, wa_ref, ga_ref) + term(ob_ref, wb_ref, gb_ref)
                  + term(oc_ref, wc_ref, gc_ref)).astype(o_ref.dtype)


def _merge(o_a, o_b, o_c, z, w_pa, w_pb, w_pc, layer):
    m = o_a.shape[0]
    d = w_pa.shape[2]
    tm = _tile(m, 1024, 8)
    tn = _tile(d, 512)
    gate_blk = [(_C_GATES + k * d) // tn for k in range(3)]
    assert all((_C_GATES + k * d) % tn == 0 for k in range(3))
    row = lambda w: pl.BlockSpec((tm, w), lambda i, j: (i, 0))
    col = lambda w: pl.BlockSpec((None, w, tn), lambda i, j: (layer, 0, j))
    gate = lambda k: pl.BlockSpec((tm, tn), lambda i, j: (i, gate_blk[k] + j))
    return pl.pallas_call(
        _merge_body,
        grid=(m // tm, d // tn),
        in_specs=[row(o_a.shape[1]), row(o_b.shape[1]), row(o_c.shape[1]),
                  col(w_pa.shape[1]), col(w_pb.shape[1]), col(w_pc.shape[1]), gate(0), gate(1), gate(2)],
        out_specs=pl.BlockSpec((tm, tn), lambda i, j: (i, j)),
        out_shape=jax.ShapeDtypeStruct((m, d), _MXU),
        compiler_params=_params(("parallel", "parallel")),
        name="merge",
    )(o_a, o_b, o_c, w_pa, w_pb, w_pc, z, z, z)


def _conv_gate(u, u1, u2, g, cw_ref, cb_ref):
    c = cb_ref[...] + cw_ref[0:1, :] * u2 + cw_ref[1:2, :] * u1 + cw_ref[2:3, :] * u
    return 0.5 * c * (1.0 + lax.erf(c * (2.0 ** -0.5))) * g


def _ffn_p_body(h_ref, wu_ref, wg_ref, cw_ref, cb_ref, a_ref, cs_ref, tail_ref, *, seq, n_sub):
    i = pl.program_id(1)
    tm = h_ref.shape[0]
    sm = tm // n_sub

    @pl.when((i * tm) % seq == 0)
    def _():
        tail_ref[...] = jnp.zeros_like(tail_ref)

    prev = tail_ref[...]
    row = lax.broadcasted_iota(jnp.int32, (sm, 1), 0)
    for r in range(n_sub):
        h = h_ref[r * sm:(r + 1) * sm, :]
        u = jnp.dot(h, wu_ref[...], preferred_element_type=jnp.float32)
        g = jnp.dot(h, wg_ref[...], preferred_element_type=jnp.float32)
        u1 = jnp.where(row == 0, prev[7:8, :], pltpu.roll(u, 1, axis=0))
        u2 = jnp.where(row == 0, prev[6:7, :], jnp.where(row == 1, prev[7:8, :], pltpu.roll(u, 2, axis=0)))
        a_ref[r * sm:(r + 1) * sm, :] = _conv_gate(u, u1, u2, g, cw_ref, cb_ref).astype(a_ref.dtype)
        prev = u[sm - 8:sm, :]
    tail_ref[...] = prev

    @pl.when((i * tm + tm) % seq == 0)
    def _():
        cs_ref[...] = prev[8 - (CONV_W - 1):8, :]


def _ffn_in_prompt(h, w_up, w_gate, conv_w, conv_b, layer, n_batch, t):
    m, d = h.shape
    f = w_up.shape[2]
    tm = _tile(t, 1024, 8)
    tn = _tile(f, 1408)
    per_seq = t // tm
    return pl.pallas_call(
        functools.partial(_ffn_p_body, seq=t, n_sub=4 if tm % 32 == 0 else 1),
        grid=(f // tn, m // tm),
        in_specs=[
            pl.BlockSpec((tm, d), lambda j, i: (i, 0)),
            pl.BlockSpec((None, d, tn), lambda j, i: (layer, 0, j)),
            pl.BlockSpec((None, d, tn), lambda j, i: (layer, 0, j)),
            pl.BlockSpec((None, CONV_W, tn), lambda j, i: (layer, 0, j)),
            pl.BlockSpec((None, 1, tn), lambda j, i: (layer, 0, j)),
        ],
        out_specs=[
            pl.BlockSpec((tm, tn), lambda j, i: (i, j)),
            pl.BlockSpec((None, CONV_W - 1, tn), lambda j, i: (i // per_seq, 0, j)),
        ],
        out_shape=[
            jax.ShapeDtypeStruct((m, f), _MXU),
            jax.ShapeDtypeStruct((n_batch, CONV_W - 1, f), jnp.float32),
        ],
        scratch_shapes=[pltpu.VMEM((8, tn), jnp.float32)],
        compiler_params=_params(("parallel", "arbitrary")),
        name="ffn_in_prompt",
    )(h, w_up, w_gate, conv_w, conv_b)


def _ffn_s_body(h_ref, wu_ref, wg_ref, cw_ref, cb_ref, p1_ref, p2_ref, a_ref, u_ref, *, seq):
    tm = h_ref.shape[0]
    h = h_ref[...]
    u = jnp.dot(h, wu_ref[...], preferred_element_type=jnp.float32)
    g = jnp.dot(h, wg_ref[...], preferred_element_type=jnp.float32)
    pos = lax.broadcasted_iota(jnp.int32, (tm, 1), 0) % seq
    u1 = jnp.where(pos >= 1, pltpu.roll(u, 1, axis=0), p1_ref[...])
    u2 = jnp.where(pos >= 2, pltpu.roll(u, 2, axis=0), p2_ref[...])
    a_ref[...] = _conv_gate(u, u1, u2, g, cw_ref, cb_ref).astype(a_ref.dtype)
    u_ref[...] = u


def _ffn_in_sample(h, w_up, w_gate, conv_w, conv_b, layer, conv_prev, n_batch, t):
    m, d = h.shape
    f = w_up.shape[2]
    assert t >= CONV_W - 1
    tn = _tile(f, 512)
    zeros = jnp.zeros((n_batch, t, f), jnp.float32)
    p1 = zeros.at[:, 0].set(conv_prev[:, 1]).reshape(m, f)
    p2 = zeros.at[:, 0].set(conv_prev[:, 0]).at[:, 1].set(conv_prev[:, 1]).reshape(m, f)
    full = lambda w: pl.BlockSpec((m, w), lambda j: (0, 0))
    colf = lambda r: pl.BlockSpec((r, tn), lambda j: (0, j))
    colw = lambda r: pl.BlockSpec((None, r, tn), lambda j: (layer, 0, j))
    a, u = pl.pallas_call(
        functools.partial(_ffn_s_body, seq=t),
        grid=(f // tn,),
        in_specs=[full(d), colw(d), colw(d), colw(CONV_W), colw(1), colf(m), colf(m)],
        out_specs=[colf(m), colf(m)],
        out_shape=[jax.ShapeDtypeStruct((m, f), _MXU), jax.ShapeDtypeStruct((m, f), jnp.float32)],
        compiler_params=_params(("parallel",)),
        name="ffn_in_sample",
    )(h, w_up, w_gate, conv_w, conv_b, p1, p2)
    return a, u.reshape(n_batch, t, f)[:, t - (CONV_W - 1):]


def _page_specs(n_pages_per_step, layer, page_shape):
    zeros = (0,) * len(page_shape)
    return [pl.BlockSpec((None, None) + tuple(page_shape),
                         functools.partial(lambda b, s, pt, g: (layer, pt[b, s * n_pages_per_step + g]) + zeros, g=g))
            for g in range(n_pages_per_step)]


def _dsa_s_score_body(pt_ref, qi_ref, kiw_ref, *refs, n_pg):
    pages = refs[:n_pg]
    o_ref = refs[n_pg]
    ts = qi_ref.shape[0]
    ps = pages[0].shape[1]
    qst = jnp.concatenate([qi_ref[:, h * D_IDX:(h + 1) * D_IDX] for h in range(H_IDX)], axis=0).astype(_MXU)
    wi = kiw_ref[:, D_IDX:D_IDX + H_IDX] * (D_IDX ** -0.5 * H_IDX ** -0.5)
    for g in range(n_pg):
        s = jnp.dot(qst, pages[g][...].astype(_MXU), preferred_element_type=jnp.float32)
        sc = jnp.zeros((ts, ps), jnp.float32)
        for h in range(H_IDX):
            sc = sc + wi[:, h:h + 1] * jnp.maximum(s[h * ts:(h + 1) * ts], 0.0)
        o_ref[:, g * ps:(g + 1) * ps] = sc


def _dsa_s_select_body(sc_ref, qi_ref, kiw_ref, o_ref, on_ref, key_ref, *, n_sel, n_kv, ts):
    rows = qi_ref.shape[0]
    lp = sc_ref.shape[1]
    qi = qi_ref[...]
    kx = kiw_ref[:, 0:D_IDX].astype(_MXU)
    kx = jnp.concatenate([kx, jnp.zeros((_LANES - rows, D_IDX), _MXU)], axis=0)
    wi = kiw_ref[:, D_IDX:D_IDX + H_IDX] * (D_IDX ** -0.5 * H_IDX ** -0.5)
    sc = jnp.zeros((rows, _LANES), jnp.float32)
    for h in range(H_IDX):
        s = lax.dot_general(qi[:, h * D_IDX:(h + 1) * D_IDX].astype(_MXU), kx, (((1,), (1,)), ((), ())),
                            preferred_element_type=jnp.float32)
        sc = sc + wi[:, h:h + 1] * jnp.maximum(s, 0.0)
    sc = jnp.concatenate([pltpu.roll(sc[b * ts:(b + 1) * ts], (_LANES - b * ts) % _LANES, axis=1)
                          for b in range(rows // ts)], axis=0)
    r = lax.broadcasted_iota(jnp.int32, (rows, _LANES), 0) % ts
    c = lax.broadcasted_iota(jnp.int32, (rows, _LANES), 1)
    key_ref[:, 0:lp] = _sortable(sc_ref[...])
    key_ref[:, lp:lp + _LANES] = _sortable(jnp.where(c <= r, sc, -jnp.inf))

    width = lp + _LANES
    pos = lambda: lax.broadcasted_iota(jnp.int32, (rows, width), 1)
    count = lambda m: jnp.sum(m.astype(jnp.int32), axis=1, keepdims=True)
    thr = _kth_largest_key(lambda t: count(key_ref[...] >= t), n_sel, (rows, 1))
    finite = thr > _KEY_NEG_INF
    n_ge = count(key_ref[...] >= thr)
    tie = jnp.max(jnp.where(jnp.logical_and(finite, n_ge > n_sel), 1, 0)) > 0

    def with_ties():
        need = n_sel - count(key_ref[...] > thr)
        cut = _first_index_cut(lambda p: count(jnp.logical_and(key_ref[...] == thr, pos() < p)), need, (rows, 1),
                               width)
        return jnp.where(finite, cut, jnp.int32(2 ** 30))

    cut = lax.cond(tie, with_ties, lambda: jnp.full((rows, 1), 2 ** 30, jnp.int32))
    key = key_ref[...]
    keep = jnp.logical_or(key > thr, jnp.logical_and(key == thr, pos() <= cut))
    own = keep[:, lp:width]
    on_ref[...] = jnp.where(jnp.logical_and(own, c <= r), 0.0, _NEG)
    keep_f = jnp.where(keep, 1.0, 0.0).astype(_MXU)
    spread = (lax.broadcasted_iota(jnp.int32, (_LANES, n_kv * _LANES), 1) // n_kv
              == lax.broadcasted_iota(jnp.int32, (_LANES, n_kv * _LANES), 0))
    spread = jnp.where(spread, 1.0, 0.0).astype(_MXU)
    for cc in range(lp // _LANES):
        dup = jnp.dot(keep_f[:, cc * _LANES:(cc + 1) * _LANES], spread, preferred_element_type=jnp.float32)
        o_ref[:, cc * n_kv * _LANES:(cc + 1) * n_kv * _LANES] = jnp.where(dup > 0.5, 0.0, _NEG)


def _head_rows(pieces):
    return jnp.concatenate(pieces, axis=0)


def _parity_neg(n_heads, n_kv, ts, width):
    row_kv = lax.broadcasted_iota(jnp.int32, (n_heads * ts, width), 0) // ((n_heads // n_kv) * ts)
    col_kv = lax.broadcasted_iota(jnp.int32, (n_heads * ts, width), 1) % n_kv
    return jnp.where(row_kv == col_kv, 0.0, _NEG)


def _own_block(q_of_group, kn_ref, vn_ref, extra_of_head, n_kv, n_grp, ts):
    pad = jnp.zeros((_LANES - ts, DH), _MXU)
    ms, ls, os_ = [], [], []
    for g in range(n_kv):
        kt = jnp.concatenate([kn_ref[:, g * DH:(g + 1) * DH].astype(_MXU), pad], axis=0)
        vt = jnp.concatenate([vn_ref[:, g * DH:(g + 1) * DH].astype(_MXU), pad], axis=0)
        logits = lax.dot_general(q_of_group(g), kt, (((1,), (1,)), ((), ())), preferred_element_type=jnp.float32)
        logits = logits + _head_rows([extra_of_head(g * n_grp + hh) for hh in range(n_grp)])
        m = jnp.max(logits, axis=1, keepdims=True)
        p = jnp.exp(logits - m)
        ms.append(m)
        ls.append(jnp.sum(p, axis=1, keepdims=True))
        os_.append(jnp.dot(p.astype(_MXU), vt, preferred_element_type=jnp.float32))
    return _head_rows(ms), _head_rows(ls), _head_rows(os_)


def _dsa_s_attn_body(pt_ref, qa_ref, kn_ref, vn_ref, negn_ref, neg_ref, bias_ref, *refs, n_pg):
    kpages = refs[:n_pg]
    vpages = refs[n_pg:2 * n_pg]
    o_ref, m_ref, l_ref, acc_ref = refs[2 * n_pg:]
    s = pl.program_id(1)
    n_steps = pl.num_programs(1)
    ts = qa_ref.shape[0]
    rows_pg = kpages[0].shape[0]
    n_grp = H_A // HKV_A
    scale = DH ** -0.5
    qs = [_stack_heads(qa_ref, g, n_grp, scale) for g in range(HKV_A)]
    q_all = _head_rows(qs)
    wb = bias_ref.shape[2]
    own_w = HKV_A * _LANES

    @pl.when(s == 0)
    def _():
        m, l, o = _own_block(lambda g: qs[g], kn_ref, vn_ref,
                             lambda h: bias_ref[h, :, wb - own_w:wb - own_w + _LANES] + negn_ref[...],
                             HKV_A, n_grp, ts)
        m_ref[0], l_ref[0], acc_ref[0] = m, l, o

    last = s == n_steps - 1
    w = n_pg * rows_pg
    kt = jnp.concatenate([kpages[p][...] for p in range(n_pg)], axis=0).astype(_MXU)
    vt = jnp.concatenate([vpages[p][...] for p in range(n_pg)], axis=0).astype(_MXU)
    logits = lax.dot_general(q_all, kt, (((1,), (1,)), ((), ())), preferred_element_type=jnp.float32)
    neg = neg_ref[...]

    def extra_of_head(h):
        far = bias_ref[h, :, 0:1]
        base = neg + far
        near = jnp.where(last, bias_ref[h, :, wb - own_w - rows_pg:wb - own_w] - far, 0.0)
        return jnp.concatenate([base[:, 0:w - rows_pg], base[:, w - rows_pg:w] + near], axis=1)

    extra = _head_rows([extra_of_head(h) for h in range(H_A)]) + _parity_neg(H_A, HKV_A, ts, w)
    _flash_update(m_ref, l_ref, acc_ref, 0, logits + extra, vt)

    @pl.when(last)
    def _():
        o = acc_ref[0] / l_ref[0]
        for h in range(H_A):
            o_ref[:, h * DH:(h + 1) * DH] = o[h * ts:(h + 1) * ts].astype(o_ref.dtype)


def _dsa_sample(zs, kiws, cache_k, cache_v, cache_kidx, page_table, bias, layer, n_batch, ts):
    n_pages = page_table.shape[1]
    ps = cache_kidx.shape[3]
    past = n_pages * ps
    assert ps == _LANES and ts <= _LANES and ts % 8 == 0 and cache_k.shape[2] == ps * HKV_A
    n_sel = min(TOPK_IDX, (past + ts) // 4)
    n_grp = H_A // HKV_A

    gi = _tile(n_pages, 32, 1)
    scores = pl.pallas_call(
        functools.partial(_dsa_s_score_body, n_pg=gi),
        grid_spec=pltpu.PrefetchScalarGridSpec(
            num_scalar_prefetch=1,
            grid=(n_batch, n_pages // gi),
            in_specs=[pl.BlockSpec((ts, H_IDX * D_IDX), lambda b, s, pt: (b, _C_QI // (H_IDX * D_IDX))),
                      pl.BlockSpec((ts, _LANES), lambda b, s, pt: (b, 0))]
            + _page_specs(gi, layer, (D_IDX, ps)),
            out_specs=pl.BlockSpec((None, ts, gi * ps), lambda b, s, pt: (b, 0, s)),
        ),
        out_shape=jax.ShapeDtypeStruct((n_batch, ts, past), jnp.float32),
        compiler_params=_params(("parallel", "arbitrary")),
        name="dsa_sample_scores",
    )(page_table, zs, kiws, *([cache_kidx] * gi))

    rows = ts * _tile(n_batch, max(1, 32 // ts), 1)
    assert rows <= _LANES
    neg, neg_own = pl.pallas_call(
        functools.partial(_dsa_s_select_body, n_sel=n_sel, n_kv=HKV_A, ts=ts),
        grid=(n_batch * ts // rows,),
        in_specs=[pl.BlockSpec((rows, past), lambda i: (i, 0)),
                  pl.BlockSpec((rows, H_IDX * D_IDX), lambda i: (i, _C_QI // (H_IDX * D_IDX))),
                  pl.BlockSpec((rows, _LANES), lambda i: (i, 0))],
        out_specs=[pl.BlockSpec((rows, HKV_A * past), lambda i: (i, 0)),
                   pl.BlockSpec((rows, _LANES), lambda i: (i, 0))],
        out_shape=[jax.ShapeDtypeStruct((n_batch * ts, HKV_A * past), jnp.float32),
                   jax.ShapeDtypeStruct((n_batch * ts, _LANES), jnp.float32)],
        scratch_shapes=[pltpu.VMEM((rows, past + _LANES), jnp.int32)],
        compiler_params=_params(("parallel",)),
        name="dsa_sample_select",
    )(scores.reshape(n_batch * ts, past), zs, kiws)

    ga = _tile(n_pages, 32, 1)
    wkv = HKV_A * DH
    rows_pg = ps * HKV_A
    return pl.pallas_call(
        functools.partial(_dsa_s_attn_body, n_pg=ga),
        grid_spec=pltpu.PrefetchScalarGridSpec(
            num_scalar_prefetch=1,
            grid=(n_batch, n_pages // ga),
            in_specs=[pl.BlockSpec((ts, H_A * DH), lambda b, s, pt: (b, _C_QA // (H_A * DH))),
                      pl.BlockSpec((ts, wkv), lambda b, s, pt: (b, _C_KA // wkv)),
                      pl.BlockSpec((ts, wkv), lambda b, s, pt: (b, _C_VA // wkv)),
                      pl.BlockSpec((ts, _LANES), lambda b, s, pt: (b, 0)),
                      pl.BlockSpec((ts, ga * rows_pg), lambda b, s, pt: (b, s)),
                      pl.BlockSpec((H_A, ts, bias.shape[2]), lambda b, s, pt: (0, 0, 0))]
            + _page_specs(ga, layer, (rows_pg, DH)) + _page_specs(ga, layer, (rows_pg, DH)),
            out_specs=pl.BlockSpec((ts, H_A * DH), lambda b, s, pt: (b, 0)),
            scratch_shapes=[pltpu.VMEM((1, H_A * ts, 1), jnp.float32),
                            pltpu.VMEM((1, H_A * ts, 1), jnp.float32),
                            pltpu.VMEM((1, H_A * ts, DH), jnp.float32)],
        ),
        out_shape=jax.ShapeDtypeStruct((n_batch * ts, H_A * DH), _MXU),
        compiler_params=_params(("parallel", "arbitrary")),
        name="dsa_sample_attn",
    )(page_table, zs, zs, zs, neg_own, neg, bias, *([cache_k] * ga), *([cache_v] * ga))


def _moba_s_body(pt_ref, qb_ref, kn_ref, vn_ref, bias_ref, *refs, n_pg, n_sel, n_blocks):
    kpages = refs[:n_pg]
    vpages = refs[n_pg:2 * n_pg]
    o_ref, gate_ref, mb_ref, lb_ref, ob_ref = refs[2 * n_pg:]
    s = pl.program_id(1)
    n_steps = pl.num_programs(1)
    ts = qb_ref.shape[0]
    rows_pg = kpages[0].shape[0]
    n_grp = H_B // HKV_B
    rows = H_B * ts
    blk = MOBA_BLOCK * HKV_B
    bps = n_pg * rows_pg // blk
    scale = DH ** -0.5
    wb = bias_ref.shape[2]
    own_w = HKV_B * _LANES
    qf = [jnp.concatenate([qb_ref[:, (g * n_grp + hh) * DH:(g * n_grp + hh + 1) * DH] for hh in range(n_grp)], axis=0)
          for g in range(HKV_B)]
    qs = [(q * scale).astype(_MXU) for q in qf]
    q_all = _head_rows(qs)
    lane = lax.broadcasted_iota(jnp.int32, (rows, gate_ref.shape[1]), 1)
    last = s == n_steps - 1

    @pl.when(s == 0)
    def _():
        gate_ref[...] = jnp.zeros(gate_ref.shape, jnp.float32)
        mb_ref[...] = jnp.full(mb_ref.shape, _NEG, jnp.float32)
        lb_ref[...] = jnp.zeros(lb_ref.shape, jnp.float32)

    kf = jnp.concatenate([kpages[p][...] for p in range(n_pg)], axis=0)
    vt = jnp.concatenate([vpages[p][...] for p in range(n_pg)], axis=0).astype(_MXU)
    logits = lax.dot_general(q_all, kf.astype(_MXU), (((1,), (1,)), ((), ())), preferred_element_type=jnp.float32)
    far = _head_rows([bias_ref[h, :, 0:1] for h in range(H_B)])
    near = _head_rows([bias_ref[h, :, wb - own_w - blk:wb - own_w] for h in range(H_B)])
    parity = _parity_neg(H_B, HKV_B, ts, blk)
    sub_kv = lax.broadcasted_iota(jnp.int32, (8, DH), 0) % HKV_B
    gates, ms, ls = gate_ref[...], mb_ref[...], lb_ref[...]
    per_dot = 4 if bps % 4 == 0 else bps
    for j0 in range(0, bps, per_dot):
        ps_ = []
        for jb in range(j0, j0 + per_dot):
            seg = logits[:, jb * blk:(jb + 1) * blk] + (far + parity)
            if jb == bps - 1:
                seg = seg + jnp.where(last, near - far, 0.0)
            ksum = jnp.sum(kf[jb * blk:(jb + 1) * blk].reshape(blk // 8, 8, DH), axis=0)
            gate = _head_rows([
                jnp.sum(qf[g] * (jnp.sum(jnp.where(sub_kv == g, ksum, 0.0), axis=0, keepdims=True) / MOBA_BLOCK),
                        axis=1, keepdims=True) for g in range(HKV_B)])
            mj = jnp.max(seg, axis=1, keepdims=True)
            p = jnp.exp(seg - mj)
            lj = jnp.sum(p, axis=1, keepdims=True)
            here = lane == s * bps + jb
            gates = jnp.where(here, gate, gates)
            ms = jnp.where(here, mj, ms)
            ls = jnp.where(here, lj, ls)
            ps_.append(jnp.concatenate([p if k == jb else jnp.zeros_like(p) for k in range(j0, j0 + per_dot)],
                                       axis=1))
        o_blocks = jnp.dot(jnp.concatenate(ps_, axis=0).astype(_MXU), vt[j0 * blk:(j0 + per_dot) * blk],
                           preferred_element_type=jnp.float32)
        for jb in range(j0, j0 + per_dot):
            ob_ref[s * bps + jb] = o_blocks[(jb - j0) * rows:(jb - j0 + 1) * rows]
    gate_ref[...], mb_ref[...], lb_ref[...] = gates, ms, ls

    @pl.when(last)
    def _():
        r = lax.broadcasted_iota(jnp.int32, (ts, _LANES), 0)
        c = lax.broadcasted_iota(jnp.int32, (ts, _LANES), 1)
        causal_neg = jnp.where(c <= r, 0.0, _NEG)
        m_own, l_own, o_own = _own_block(lambda g: qs[g], kn_ref, vn_ref,
                                         lambda h: bias_ref[h, :, wb - own_w:wb - own_w + _LANES] + causal_neg,
                                         HKV_B, n_grp, ts)
        chosen = _topk_lanes(gate_ref[...], n_blocks, n_sel)
        mb = jnp.where(chosen, mb_ref[...], _NEG)
        m_all = jnp.maximum(m_own, jnp.max(mb, axis=1, keepdims=True))
        w = jnp.where(chosen, jnp.exp(mb - m_all), 0.0)
        w_own = jnp.exp(m_own - m_all)
        den = w_own * l_own + jnp.sum(w * lb_ref[...], axis=1, keepdims=True)

        def add_block(j, acc):
            wj = jnp.sum(jnp.where(lane == j, w, 0.0), axis=1, keepdims=True)
            return acc + wj * ob_ref[j]

        num = lax.fori_loop(0, n_blocks, add_block, w_own * o_own)
        o = num / den
        for h in range(H_B):
            o_ref[:, h * DH:(h + 1) * DH] = o[h * ts:(h + 1) * ts].astype(o_ref.dtype)


def _moba_sample(zs, cache_k, cache_v, page_table, bias, layer, n_batch, ts):
    n_pages = page_table.shape[1]
    rows_pg = cache_k.shape[2]
    ps = rows_pg // HKV_B
    past = n_pages * ps
    assert past % MOBA_BLOCK == 0 and MOBA_BLOCK % ps == 0 and ts <= _LANES and ts % 8 == 0
    n_blocks = past // MOBA_BLOCK
    n_sel = min(MOBA_TOPK, (past + ts - 1) // MOBA_BLOCK)
    ppb = MOBA_BLOCK // ps
    g_pg = ppb * _tile(n_blocks, 16, 1)
    lane_w = -(-n_blocks // _LANES) * _LANES
    wkv = HKV_B * DH
    return pl.pallas_call(
        functools.partial(_moba_s_body, n_pg=g_pg, n_sel=n_sel, n_blocks=n_blocks),
        grid_spec=pltpu.PrefetchScalarGridSpec(
            num_scalar_prefetch=1,
            grid=(n_batch, n_pages // g_pg),
            in_specs=[pl.BlockSpec((ts, H_B * DH), lambda b, s, pt: (b, _C_QB // (H_B * DH))),
                      pl.BlockSpec((ts, wkv), lambda b, s, pt: (b, _C_KB // wkv)),
                      pl.BlockSpec((ts, wkv), lambda b, s, pt: (b, _C_VB // wkv)),
                      pl.BlockSpec((H_B, ts, bias.shape[2]), lambda b, s, pt: (1, 0, 0))]
            + _page_specs(g_pg, layer, (rows_pg, DH)) + _page_specs(g_pg, layer, (rows_pg, DH)),
            out_specs=pl.BlockSpec((ts, H_B * DH), lambda b, s, pt: (b, 0)),
            scratch_shapes=[pltpu.VMEM((H_B * ts, lane_w), jnp.float32),
                            pltpu.VMEM((H_B * ts, lane_w), jnp.float32),
                            pltpu.VMEM((H_B * ts, lane_w), jnp.float32),
                            pltpu.VMEM((n_blocks, H_B * ts, DH), jnp.float32)],
        ),
        out_shape=jax.ShapeDtypeStruct((n_batch * ts, H_B * DH), _MXU),
        compiler_params=_params(("parallel", "arbitrary")),
        name="moba_sample",
    )(page_table, zs, zs, zs, bias, *([cache_k] * g_pg), *([cache_v] * g_pg))


def _relayout_w_in(w_in, d):
    w_t = jnp.swapaxes(w_in, 1, 2)
    sizes = dict(qa=H_A * DH, ka=HKV_A * DH, va=HKV_A * DH, qi=H_IDX * D_IDX, ki=D_IDX, wi=H_IDX,
                 qb=H_B * DH, kb=HKV_B * DH, vb=HKV_B * DH, qc=H_C * DK_C, kc=H_C * DK_C, vc=H_C * DV_C,
                 gc=H_C * DV_C, gates=3 * d)
    off, parts = 0, {}
    for name in ("qa", "ka", "va", "qi", "ki", "wi", "qb", "kb", "vb", "qc", "kc", "vc", "gc", "gates"):
        parts[name] = w_t[:, off:off + sizes[name]]
        off += sizes[name]
    assert off == w_t.shape[1]
    main = jnp.concatenate([parts[n] for n in ("vc", "gc", "qa", "qi", "qb", "qc", "kc", "ka", "va", "kb", "vb",
                                               "gates")], axis=1).astype(_MXU)
    kiw = jnp.concatenate([parts["ki"], parts["wi"],
                           jnp.zeros((w_t.shape[0], _LANES - D_IDX - H_IDX, d), w_in.dtype)], axis=1).astype(_MXU)
    return main, kiw


def _pad_last(w, n):
    return jnp.pad(w, [(0, 0)] * (w.ndim - 1) + [(0, n - w.shape[-1])])


def _mix_and_ffn(x, z, o_a, o_b, o_c, wts, layer, ffn_in):
    m = _merge(o_a, o_b, o_c, z, wts["w_pa"], wts["w_pb"], wts["w_pc"], layer)
    x, h2 = _out_proj(m, wts["w_out"], layer, x, wts["norm_ffn"][layer])
    a, conv_new = ffn_in(h2)
    x = _matmul(a, wts["ffn_down"], layer, jnp.float32, residual=x, tk_target=3072)
    return x, conv_new


def _state_slices(z, kiw, n_batch, t):
    ka = z[:, _C_KA:_C_KA + HKV_A * DH].reshape(n_batch, t, HKV_A, DH)
    va = z[:, _C_VA:_C_VA + HKV_A * DH].reshape(n_batch, t, HKV_A, DH)
    ki = kiw[:, :D_IDX].reshape(n_batch, t, D_IDX)
    kb = z[:, _C_KB:_C_KB + HKV_B * DH].reshape(n_batch, t, HKV_B, DH)
    vb = z[:, _C_VB:_C_VB + HKV_B * DH].reshape(n_batch, t, HKV_B, DH)
    return ka, va, ki, kb, vb


def kernel(x_prompt, x_sample, cache_a_k, cache_a_v, cache_a_kidx, cache_b_k, cache_b_v, state_ret, state_conv,
           page_table, rel_bias, norm_mix, w_in, ret_gn, w_pa, w_pb, w_pc, w_out, norm_ffn, ffn_up, ffn_gate,
           conv_w, conv_b, ffn_down, norm_final):
    bp, tp, d = x_prompt.shape
    bs, ts, _ = x_sample.shape
    depth = w_in.shape[0]
    n_pool, ps = cache_a_k.shape[1], cache_a_k.shape[2]
    past = page_table.shape[1] * ps
    f = ffn_up.shape[2]
    fp = -(-f // 512) * 512
    tq = _TQ
    assert d % _LANES == 0 and tp % tq == 0

    ar = jnp.arange
    assert tq >= MAX_DIST
    bias_p = _bias_table(rel_bias, (ar(tq)[None, :] - ar(2 * tq)[:, None] + tq).astype(jnp.int32),
                         log2_rel_to=(0, tq - 1))
    assert H_A == H_B and HKV_A == HKV_B
    rel_cached = (ar(ts)[:, None] + MOBA_BLOCK - ar(MOBA_BLOCK)[None, :]).astype(jnp.int32)
    rel_own = (ar(ts)[:, None] - ar(_LANES)[None, :]).astype(jnp.int32)
    bias_s = _bias_table(rel_bias, jnp.concatenate([jnp.repeat(rel_cached, HKV_A, axis=1), rel_own, rel_own], axis=1))
    cache_a_k = cache_a_k.reshape(depth, n_pool, ps * HKV_A, DH)
    cache_a_v = cache_a_v.reshape(depth, n_pool, ps * HKV_A, DH)
    cache_b_k = cache_b_k.reshape(depth, n_pool, ps * HKV_B, DH)
    cache_b_v = cache_b_v.reshape(depth, n_pool, ps * HKV_B, DH)
    cache_a_kidx = jnp.swapaxes(cache_a_kidx, 2, 3)

    xp = x_prompt.reshape(bp * tp, d)
    xs = x_sample.reshape(bs * ts, d)
    pos_p = jnp.arange(tp, dtype=jnp.int32)
    pos_s = past + jnp.arange(ts, dtype=jnp.int32)
    w_main, w_kiw = _relayout_w_in(w_in, d)
    wts = dict(w_pa=w_pa.astype(_MXU), w_pb=w_pb.astype(_MXU), w_pc=w_pc.astype(_MXU), w_out=w_out.astype(_MXU),
               norm_ffn=norm_ffn, ffn_down=jnp.pad(ffn_down, ((0, 0), (0, fp - f), (0, 0))).astype(_MXU))
    up = _pad_last(ffn_up, fp).astype(_MXU)
    gate = _pad_last(ffn_gate, fp).astype(_MXU)
    cw = _pad_last(conv_w, fp)
    cb = _pad_last(conv_b.reshape(depth, 1, f), fp)
    conv_prev_all = _pad_last(state_conv, fp)

    st_p, st_s = [], []
    kv_p = [jnp.zeros((depth, bp * tp, nh, DH), jnp.float32) for nh in (HKV_A, HKV_A, HKV_B, HKV_B)]
    for l in range(depth):
        z, kiw, *kv_p = _in_proj(xp, norm_mix[l], w_main, w_kiw, l, depth, kv_p)
        o_a = _dsa_prompt(z, kiw, bias_p, bp, tp)
        o_b = _moba_prompt(z, bias_p, bp, tp)
        o_c, ret_new = _retention(z, jnp.zeros((bp, H_C, DK_C, DV_C), jnp.float32), ret_gn[l], pos_p, bp, tp)
        xp, conv_new = _mix_and_ffn(xp, z, o_a, o_b, o_c, wts, l,
                                    lambda h2: _ffn_in_prompt(h2, up, gate, cw, cb, l, bp, tp))
        st_p.append((kiw[:, :D_IDX].reshape(bp, tp, D_IDX), ret_new, conv_new[:, :, :f]))

        h = _rmsnorm(xs, norm_mix[l], _MXU)
        z = _matmul_few_rows(h, w_main, l, jnp.float32)
        kiw = _matmul_few_rows(h, w_kiw, l, jnp.float32)
        o_a = _dsa_sample(z, kiw, cache_a_k, cache_a_v, cache_a_kidx, page_table, bias_s, l, bs, ts)
        o_b = _moba_sample(z, cache_b_k, cache_b_v, page_table, bias_s, l, bs, ts)
        o_c, ret_new = _retention(z, state_ret[l], ret_gn[l], pos_s, bs, ts)
        xs, conv_new = _mix_and_ffn(xs, z, o_a, o_b, o_c, wts, l,
                                    lambda h2: _ffn_in_sample(h2, up, gate, cw, cb, l, conv_prev_all[l], bs, ts))
        st_s.append(_state_slices(z, kiw, bs, ts) + (ret_new, conv_new[:, :, :f]))

    y_prompt = _rmsnorm(xp, norm_final, jnp.float32).reshape(bp, tp, d)
    y_sample = _rmsnorm(xs, norm_final, jnp.float32).reshape(bs, ts, d)
    ka_p, va_p, kb_p, vb_p = kv_p
    ki_p, ret_p, conv_p = [jnp.stack(v) for v in zip(*st_p)]
    outs_s = [jnp.stack(v) for v in zip(*st_s)]
    return (y_prompt, y_sample,
            ka_p.reshape(depth, bp, tp, HKV_A, DH), va_p.reshape(depth, bp, tp, HKV_A, DH), ki_p,
            kb_p.reshape(depth, bp, tp, HKV_B, DH), vb_p.reshape(depth, bp, tp, HKV_B, DH), ret_p, conv_p, *outs_s)
```

```python
import functools
import math

import jax
import jax.numpy as jnp
from jax import lax
from jax.experimental import pallas as pl
from jax.experimental.pallas import tpu as pltpu

DH = 128
H_A, HKV_A = 8, 2
H_IDX, D_IDX = 16, 64
TOPK_IDX = 256
H_B, HKV_B = 8, 2
MOBA_BLOCK = 256
MOBA_TOPK = 3
H_C, DK_C, DV_C = 8, 128, 256
N_BUCKETS = 32
MAX_DIST = 128
CONV_W = 3
EPS = 1e-6
ROPE_BASE = 10000.0
RET_CHUNK = 128

_MXU = jnp.bfloat16
_NEG = -1e30
_TQ = 256
_LANES = 128
_VMEM_LIMIT = 56 * 1024 * 1024

_C_VC = 0
_C_GC = _C_VC + H_C * DV_C
_C_QA = _C_GC + H_C * DV_C
_C_QI = _C_QA + H_A * DH
_C_QB = _C_QI + H_IDX * D_IDX
_C_QC = _C_QB + H_B * DH
_C_KC = _C_QC + H_C * DK_C
_C_KA = _C_KC + H_C * DK_C
_C_VA = _C_KA + HKV_A * DH
_C_KB = _C_VA + HKV_A * DH
_C_VB = _C_KB + HKV_B * DH
_C_GATES = _C_VB + HKV_B * DH


def _params(sem):
    return pltpu.CompilerParams(dimension_semantics=sem, vmem_limit_bytes=_VMEM_LIMIT)


def _tile(n, target, mult=_LANES):
    best = None
    t = mult
    while t <= min(n, target):
        if n % t == 0:
            best = t
        t += mult
    return n if best is None else best


def _rmsnorm_body(x_ref, g_ref, o_ref):
    x = x_ref[...]
    y = x * lax.rsqrt(jnp.mean(x * x, axis=-1, keepdims=True) + EPS)
    o_ref[...] = (y * g_ref[...]).astype(o_ref.dtype)


def _rmsnorm(x, g, out_dtype):
    m, d = x.shape
    tm = _tile(m, 512, 8)
    return pl.pallas_call(
        _rmsnorm_body,
        grid=(m // tm,),
        in_specs=[pl.BlockSpec((tm, d), lambda i: (i, 0)), pl.BlockSpec((1, d), lambda i: (0, 0))],
        out_specs=pl.BlockSpec((tm, d), lambda i: (i, 0)),
        out_shape=jax.ShapeDtypeStruct((m, d), out_dtype),
        compiler_params=_params(("parallel",)),
        name="rmsnorm",
    )(x, g.reshape(1, d))


def _dot_nt(x, w):
    return lax.dot_general(x, w, (((1,), (1,)), ((), ())), preferred_element_type=jnp.float32)


def _mm_body(*refs, nk, has_res):
    x_ref, w_ref = refs[0], refs[1]
    r_ref = refs[2] if has_res else None
    o_ref = refs[3] if has_res else refs[2]
    acc_ref = refs[-1]
    k = pl.program_id(2)
    d = jnp.dot(x_ref[...], w_ref[...], preferred_element_type=jnp.float32)

    def finish(r):
        if has_res:
            r = r + r_ref[...]
        o_ref[...] = r.astype(o_ref.dtype)

    if nk == 1:
        finish(d)
        return

    @pl.when(k == 0)
    def _():
        acc_ref[...] = d

    @pl.when(jnp.logical_and(k > 0, k < nk - 1))
    def _():
        acc_ref[...] += d

    @pl.when(k == nk - 1)
    def _():
        finish(acc_ref[...] + d)


def _mm_few_rows_body(x_ref, w_ref, o_ref):
    x = x_ref[...]
    x = jnp.concatenate([x, jnp.zeros((_LANES - x.shape[0], x.shape[1]), x.dtype)], axis=0)
    r_t = lax.dot_general(w_ref[...], x, (((1,), (1,)), ((), ())), preferred_element_type=jnp.float32)
    o_ref[...] = r_t.T[0:o_ref.shape[0]].astype(o_ref.dtype)


def _matmul_few_rows(x, w, layer, out_dtype, tn_target=1024):
    m, kdim = x.shape
    n = w.shape[1]
    tn = _tile(n, tn_target)
    assert m <= _LANES and m % 8 == 0
    return pl.pallas_call(
        _mm_few_rows_body,
        grid=(n // tn,),
        in_specs=[pl.BlockSpec((m, kdim), lambda j: (0, 0)), pl.BlockSpec((None, tn, kdim), lambda j: (layer, j, 0))],
        out_specs=pl.BlockSpec((m, tn), lambda j: (0, j)),
        out_shape=jax.ShapeDtypeStruct((m, n), out_dtype),
        compiler_params=_params(("parallel",)),
        name="matmul_few_rows",
    )(x, w)


def _matmul(x, w, layer, out_dtype, residual=None, tm_target=1024, tn_target=1024, tk_target=2048):
    m, kdim = x.shape
    n = w.shape[2]
    tm = _tile(m, tm_target, 8)
    tn = _tile(n, tn_target)
    tk = _tile(kdim, tk_target)
    nk = kdim // tk
    in_specs = [pl.BlockSpec((tm, tk), lambda i, j, k: (i, k)),
                pl.BlockSpec((None, tk, tn), lambda i, j, k: (layer, k, j))]
    args = [x, w]
    if residual is not None:
        in_specs.append(pl.BlockSpec((tm, tn), lambda i, j, k: (i, j)))
        args.append(residual)
    return pl.pallas_call(
        functools.partial(_mm_body, nk=nk, has_res=residual is not None),
        grid=(m // tm, n // tn, nk),
        in_specs=in_specs,
        out_specs=pl.BlockSpec((tm, tn), lambda i, j, k: (i, j)),
        out_shape=jax.ShapeDtypeStruct((m, n), out_dtype),
        scratch_shapes=[pltpu.VMEM((tm, tn), jnp.float32)] if nk > 1 else [],
        compiler_params=_params(("parallel", "parallel", "arbitrary")),
        name="matmul",
    )(*args)


def _out_proj_body(m_ref, w_ref, x_ref, g_ref, o_ref, h_ref):
    r = x_ref[...] + jnp.dot(m_ref[...], w_ref[...], preferred_element_type=jnp.float32)
    o_ref[...] = r
    y = r * lax.rsqrt(jnp.mean(r * r, axis=-1, keepdims=True) + EPS)
    h_ref[...] = (y * g_ref[...]).astype(h_ref.dtype)


def _out_proj(m_in, w, layer, x, g):
    m, kdim = m_in.shape
    d = w.shape[2]
    tm = _tile(m, 512, 8)
    return pl.pallas_call(
        _out_proj_body,
        grid=(m // tm,),
        in_specs=[pl.BlockSpec((tm, kdim), lambda i: (i, 0)),
                  pl.BlockSpec((None, kdim, d), lambda i: (layer, 0, 0)),
                  pl.BlockSpec((tm, d), lambda i: (i, 0)),
                  pl.BlockSpec((1, d), lambda i: (0, 0))],
        out_specs=[pl.BlockSpec((tm, d), lambda i: (i, 0)), pl.BlockSpec((tm, d), lambda i: (i, 0))],
        out_shape=[jax.ShapeDtypeStruct((m, d), jnp.float32), jax.ShapeDtypeStruct((m, d), _MXU)],
        compiler_params=_params(("parallel",)),
        name="out_proj",
    )(m_in, w, x, g.reshape(1, d))


def _in_proj_body(*refs, n_prev, j_state):
    x_ref, g_ref, w_ref, wk_ref = refs[:4]
    z_ref, kiw_ref, ka_ref, va_ref, kb_ref, vb_ref, h_ref = refs[4 + n_prev:]

    @pl.when(pl.program_id(1) == 0)
    def _():
        x = x_ref[...]
        y = x * lax.rsqrt(jnp.mean(x * x, axis=-1, keepdims=True) + EPS)
        h_ref[...] = (y * g_ref[...]).astype(h_ref.dtype)
        kiw_ref[...] = _dot_nt(h_ref[...], wk_ref[...])

    r = _dot_nt(h_ref[...], w_ref[...])
    z_ref[...] = r

    @pl.when(pl.program_id(1) == j_state)
    def _():
        col = 0
        for ref in (ka_ref, va_ref, kb_ref, vb_ref):
            for g in range(ref.shape[1]):
                ref[:, g, :] = r[:, col:col + DH]
                col += DH


def _in_proj(x, g, w_main, w_kiw, layer, depth, prev_states):
    m, kdim = x.shape
    n = w_main.shape[1]
    tm = _tile(m, 1024, 8)
    tn = 2 * HKV_A * DH + 2 * HKV_B * DH
    assert n % tn == 0 and _C_KA % tn == 0 and _C_VB + HKV_B * DH == _C_KA + tn
    kv_heads = (HKV_A, HKV_A, HKV_B, HKV_B)
    prev = list(prev_states)
    state_spec = lambda nh: pl.BlockSpec((None, tm, nh, DH), lambda i, j: (layer, i, 0, 0))
    return pl.pallas_call(
        functools.partial(_in_proj_body, n_prev=len(prev), j_state=_C_KA // tn),
        grid=(m // tm, n // tn),
        in_specs=[pl.BlockSpec((tm, kdim), lambda i, j: (i, 0)),
                  pl.BlockSpec((1, kdim), lambda i, j: (0, 0)),
                  pl.BlockSpec((None, tn, kdim), lambda i, j: (layer, j, 0)),
                  pl.BlockSpec((None, _LANES, kdim), lambda i, j: (layer, 0, 0))]
        + [pl.BlockSpec(memory_space=pl.ANY)] * len(prev),
        out_specs=[pl.BlockSpec((tm, tn), lambda i, j: (i, j)), pl.BlockSpec((tm, _LANES), lambda i, j: (i, 0))]
        + [state_spec(nh) for nh in kv_heads],
        out_shape=[jax.ShapeDtypeStruct((m, n), jnp.float32), jax.ShapeDtypeStruct((m, _LANES), jnp.float32)]
        + [jax.ShapeDtypeStruct((depth, m, nh, DH), jnp.float32) for nh in kv_heads],
        scratch_shapes=[pltpu.VMEM((tm, kdim), _MXU)],
        input_output_aliases={4 + k: 2 + k for k in range(len(prev))},
        compiler_params=_params(("parallel", "arbitrary")),
        name="in_proj",
    )(x, g.reshape(1, kdim), w_main, w_kiw, *prev)


def _t5_bucket(rel):
    n = jnp.maximum(rel, 0)
    max_exact = N_BUCKETS // 2
    nf = jnp.maximum(n, 1).astype(jnp.float32)
    large = max_exact + (jnp.log(nf / max_exact) / math.log(MAX_DIST / max_exact)
                         * (N_BUCKETS - max_exact)).astype(jnp.int32)
    large = jnp.minimum(large, N_BUCKETS - 1)
    return jnp.where(n < max_exact, n, large)


def _bias_body(tab_ref, bucket_ref, o_ref, *, log2_rel_to):
    h = pl.program_id(0)
    bucket = bucket_ref[...]
    acc = jnp.zeros(bucket.shape, jnp.float32)
    for b in range(N_BUCKETS):
        acc = jnp.where(bucket == b, tab_ref[b, h], acc)
    if log2_rel_to is not None:
        r0, c0 = log2_rel_to
        acc = (acc - acc[r0:r0 + 1, c0:c0 + 1]) * math.log2(math.e)
    o_ref[...] = acc


def _bias_table(rel_bias, rel, log2_rel_to=None):
    bucket = _t5_bucket(rel)
    r, c = rel.shape
    nh = rel_bias.shape[1]
    return pl.pallas_call(
        functools.partial(_bias_body, log2_rel_to=log2_rel_to),
        grid=(nh,),
        in_specs=[pl.BlockSpec(memory_space=pltpu.SMEM), pl.BlockSpec((r, c), lambda h: (0, 0))],
        out_specs=pl.BlockSpec((None, r, c), lambda h: (h, 0, 0)),
        out_shape=jax.ShapeDtypeStruct((nh, r, c), jnp.float32),
        compiler_params=_params(("parallel",)),
        name="bias_table",
    )(rel_bias, bucket)


def _sortable(x):
    b = lax.bitcast_convert_type(x, jnp.int32)
    return b ^ ((b >> 31) & jnp.int32(0x7FFFFFFF))


_KEY_NEG_INF = -2147483648 + 0x7FFFFF


def _stack_heads(q_ref, g, n_grp, scale):
    parts = [q_ref[:, (g * n_grp + hh) * DH:(g * n_grp + hh + 1) * DH] for hh in range(n_grp)]
    return (jnp.concatenate(parts, axis=0) * scale).astype(_MXU)


def _flash_update(m_ref, l_ref, acc_ref, g, logits, v):
    m_old = m_ref[g]
    m_new = jnp.maximum(m_old, jnp.max(logits, axis=1, keepdims=True))
    alpha = jnp.exp(m_old - m_new)
    p = jnp.exp(logits - m_new)
    l_ref[g] = alpha * l_ref[g] + jnp.sum(p, axis=1, keepdims=True)
    acc_ref[g] = alpha * acc_ref[g] + jnp.dot(p.astype(_MXU), v, preferred_element_type=jnp.float32)
    m_ref[g] = m_new


def _flash_init(m_ref, l_ref, acc_ref):
    m_ref[...] = jnp.full(m_ref.shape, _NEG, jnp.float32)
    l_ref[...] = jnp.zeros(l_ref.shape, jnp.float32)
    acc_ref[...] = jnp.zeros(acc_ref.shape, jnp.float32)


def _flash_update_t(m_ref, l_ref, acc_ref, g, logits_t, v_t, extra_of_chunk):
    ps, alphas = [], []
    for c in range(logits_t.shape[1] // _LANES):
        sl = slice(c * _LANES, (c + 1) * _LANES)
        x = logits_t[:, sl] + extra_of_chunk(c)
        m_old = m_ref[g, :, sl]
        m_new = jnp.maximum(m_old, jnp.max(x, axis=0, keepdims=True))
        alpha = jnp.exp2(m_old - m_new)
        p = jnp.exp2(x - m_new)
        l_ref[g, :, sl] = alpha * l_ref[g, :, sl] + jnp.sum(p, axis=0, keepdims=True)
        m_ref[g, :, sl] = m_new
        ps.append(p.astype(_MXU))
        alphas.append(alpha)
    p_all = jnp.concatenate(ps, axis=1)
    acc_ref[g] = (jnp.concatenate(alphas, axis=1) * acc_ref[g]
                  + jnp.dot(v_t, p_all, preferred_element_type=jnp.float32))


def _flash_store_t(o_ref, l_ref, acc_ref, n_kv, n_grp, rows):
    for g in range(n_kv):
        o = acc_ref[g] / l_ref[g]
        for hh in range(n_grp):
            h = g * n_grp + hh
            o_ref[:, h * DH:(h + 1) * DH] = o[:, hh * rows:(hh + 1) * rows].T.astype(o_ref.dtype)


def _kth_largest_key(count_ge, n_sel, shape):
    int_min = jnp.int32(-2147483648)
    prefix = jnp.full(shape, int_min, jnp.int32)
    zero = jnp.zeros(shape, jnp.int32)
    prefix = jnp.where(count_ge(zero) >= n_sel, zero, prefix)

    def step(it, prefix):
        cand = prefix | (jnp.int32(1) << (30 - it))
        return jnp.where(count_ge(cand) >= n_sel, cand, prefix)

    return lax.fori_loop(0, 31, step, prefix)


def _kth_largest_half(count_ge, n_sel, shape):
    prefix = jnp.full(shape, -32768, jnp.int32)
    zero = jnp.zeros(shape, jnp.int32)
    prefix = jnp.where(count_ge(zero) >= n_sel, zero, prefix)

    def step(it, prefix):
        cand = prefix | (jnp.int32(1) << (14 - it))
        return jnp.where(count_ge(cand) >= n_sel, cand, prefix)

    return lax.fori_loop(0, 15, step, prefix)


def _first_index_cut(count_lt, need, shape, n_idx):
    nbits = max(1, int(n_idx).bit_length())
    cut = jnp.zeros(shape, jnp.int32)

    def step(it, cut):
        cand = cut + (jnp.int32(1) << (nbits - 1 - it))
        return jnp.where(count_lt(cand) < need, cand, cut)

    return lax.fori_loop(0, nbits, step, cut)


def _stage_kv(k_in_ref, v_in_ref, kb_ref, vt_ref, tq):
    for j in range(kb_ref.shape[0]):
        kb_ref[j] = k_in_ref[j * tq:(j + 1) * tq, :].astype(_MXU)
        vt_ref[j] = v_in_ref[j * tq:(j + 1) * tq, :].T.astype(_MXU)


def _attend_t(j, qs, kb_ref, vt_ref, m_ref, l_ref, acc_ref, extra_of_chunk):
    for g in range(len(qs)):
        logits_t = lax.dot_general(kb_ref[j, :, g * DH:(g + 1) * DH], qs[g], (((1,), (1,)), ((), ())),
                                   preferred_element_type=jnp.float32)
        _flash_update_t(m_ref, l_ref, acc_ref, g, logits_t, vt_ref[j, g * DH:(g + 1) * DH, :],
                        functools.partial(extra_of_chunk, g))


def _dsa_p_body(qa_ref, qi_ref, kiwq_ref, kiw_ref, ka_ref, va_ref, bias_ref, o_ref,
                kx_ref, kb_ref, vt_ref, key_ref, hi_ref, lo_ref, neg_ref, m_ref, l_ref, acc_ref, *, n_sel, t):
    i = pl.program_id(1)
    tq = _TQ
    n_grp = H_A // HKV_A
    krow = lax.broadcasted_iota(jnp.int32, (tq, tq), 0)
    qcol = lax.broadcasted_iota(jnp.int32, (tq, tq), 1)

    @pl.when(i == 0)
    def _():
        _stage_kv(ka_ref, va_ref, kb_ref, vt_ref, tq)
        for j in range(kx_ref.shape[0]):
            kx_ref[j] = kiw_ref[j * tq:(j + 1) * tq, 0:D_IDX].astype(_MXU)

    qi = [qi_ref[:, h * D_IDX:(h + 1) * D_IDX].astype(_MXU) for h in range(H_IDX)]
    wt = kiwq_ref[...].T[D_IDX:D_IDX + H_IDX, :] * (D_IDX ** -0.5 * H_IDX ** -0.5)

    def score_tile(j, carry):
        kx = kx_ref[j]
        sc = jnp.zeros((tq, tq), jnp.float32)
        for h in range(H_IDX):
            s = lax.dot_general(kx, qi[h], (((1,), (1,)), ((), ())), preferred_element_type=jnp.float32)
            sc = sc + wt[h:h + 1, :] * jnp.maximum(s, 0.0)
        causal = jnp.logical_or(j < i, krow <= qcol)
        key = _sortable(jnp.where(causal, sc, -jnp.inf))
        key_ref[j] = key
        hi_ref[j] = (key >> 16).astype(jnp.int16)
        lo_ref[j] = ((key & 0xFFFF) - 32768).astype(jnp.int16)
        return carry

    lax.fori_loop(0, i + 1, score_tile, 0)

    def count_tiles(pred):
        def body(j, acc):
            return acc + jnp.sum(pred(key_ref[j], j).astype(jnp.int32).reshape(tq // 8, 8, tq), axis=0)
        acc = lax.fori_loop(0, i + 1, body, jnp.zeros((8, tq), jnp.int32))
        return jnp.sum(acc, axis=0, keepdims=True)

    def count_half(ref, pred):
        one, zero = jnp.int16(1), jnp.int16(0)

        def body(j, acc):
            hit = jnp.where(pred(ref[j]), one, zero)
            for r in range(tq // 16):
                acc = acc + hit[r * 16:(r + 1) * 16]
            return acc
        acc = lax.fori_loop(0, i + 1, body, jnp.zeros((16, tq), jnp.int16))
        return jnp.sum(acc.astype(jnp.int32), axis=0, keepdims=True)

    t_hi = _kth_largest_half(lambda c: count_half(hi_ref, lambda k: k >= c.astype(jnp.int16)), n_sel, (1, tq))
    t_hi16 = t_hi.astype(jnp.int16)
    need_lo = n_sel - count_half(hi_ref, lambda k: k > t_hi16)

    def mask_lo(j, carry):
        lo_ref[j] = jnp.where(hi_ref[j] == t_hi16, lo_ref[j], jnp.int16(-32768))
        return carry

    lax.fori_loop(0, i + 1, mask_lo, 0)
    t_lo = _kth_largest_half(lambda c: count_half(lo_ref, lambda k: k >= c.astype(jnp.int16)), need_lo, (1, tq))
    thr = (t_hi << 16) | ((t_lo + 32768) & 0xFFFF)
    n_ge = count_tiles(lambda k, j: k >= thr)
    finite = thr > _KEY_NEG_INF
    tie = jnp.max(jnp.where(jnp.logical_and(finite, n_ge > n_sel), 1, 0)) > 0

    def write_neg(cut):
        def body(j, carry):
            k = key_ref[j]
            kpos = j * tq + krow
            keep = jnp.logical_or(k > thr, jnp.logical_and(k == thr, kpos <= cut))
            keep = jnp.logical_and(keep, jnp.logical_or(j < i, krow <= qcol))
            neg_ref[j] = jnp.where(keep, 0.0, _NEG)
            return carry
        lax.fori_loop(0, i + 1, body, 0)

    def with_ties():
        n_gt = count_tiles(lambda k, j: k > thr)
        need = n_sel - n_gt
        cut = _first_index_cut(
            lambda p: count_tiles(lambda k, j: jnp.logical_and(k == thr, j * tq + krow < p)),
            need, (1, tq), t)
        write_neg(jnp.where(finite, cut, jnp.int32(2 ** 30)))

    def without_ties():
        write_neg(jnp.full((1, tq), 2 ** 30, jnp.int32))

    lax.cond(tie, with_ties, without_ties)

    _flash_init(m_ref, l_ref, acc_ref)
    scale = DH ** -0.5 * math.log2(math.e)
    qs = [_stack_heads(qa_ref, g, n_grp, scale) for g in range(HKV_A)]
    per_head = tq // _LANES

    def attend(j, bias_of_chunk):
        neg = neg_ref[j]

        def extra(g, c):
            half = slice((c % per_head) * _LANES, (c % per_head + 1) * _LANES)
            if bias_of_chunk is None:
                return neg[:, half]
            return bias_of_chunk(g * n_grp + c // per_head, half) + neg[:, half]
        _attend_t(j, qs, kb_ref, vt_ref, m_ref, l_ref, acc_ref, extra)

    def far(j, carry):
        attend(j, None)
        return carry

    lax.fori_loop(0, jnp.maximum(i - 1, 0), far, 0)

    @pl.when(i >= 1)
    def _():
        attend(i - 1, lambda h, half: bias_ref[h, 0:tq, half])

    attend(i, lambda h, half: bias_ref[h, tq:2 * tq, half])
    _flash_store_t(o_ref, l_ref, acc_ref, HKV_A, n_grp, tq)


def _kv_scratch(nk, tq, n_kv, n_grp):
    return [
        pltpu.VMEM((nk, tq, n_kv * DH), _MXU),
        pltpu.VMEM((nk, n_kv * DH, tq), _MXU),
    ], [
        pltpu.VMEM((n_kv, 1, n_grp * tq), jnp.float32),
        pltpu.VMEM((n_kv, 1, n_grp * tq), jnp.float32),
        pltpu.VMEM((n_kv, DH, n_grp * tq), jnp.float32),
    ]


def _dsa_prompt(z, kiw, bias, n_batch, t):
    tq = _TQ
    nq = t // tq
    n_sel = min(TOPK_IDX, t // 4)
    assert n_sel <= tq and t % tq == 0
    n_grp = H_A // HKV_A
    kv_scratch, flash_scratch = _kv_scratch(nq, tq, HKV_A, n_grp)
    return pl.pallas_call(
        functools.partial(_dsa_p_body, n_sel=n_sel, t=t),
        grid=(n_batch, nq),
        in_specs=[
            pl.BlockSpec((tq, H_A * DH), lambda b, i: (b * nq + i, _C_QA // (H_A * DH))),
            pl.BlockSpec((tq, H_IDX * D_IDX), lambda b, i: (b * nq + i, _C_QI // (H_IDX * D_IDX))),
            pl.BlockSpec((tq, _LANES), lambda b, i: (b * nq + i, 0)),
            pl.BlockSpec((t, _LANES), lambda b, i: (b, 0)),
            pl.BlockSpec((t, HKV_A * DH), lambda b, i: (b, _C_KA // (HKV_A * DH))),
            pl.BlockSpec((t, HKV_A * DH), lambda b, i: (b, _C_VA // (HKV_A * DH))),
            pl.BlockSpec((H_A, 2 * tq, tq), lambda b, i: (0, 0, 0)),
        ],
        out_specs=pl.BlockSpec((tq, H_A * DH), lambda b, i: (b * nq + i, 0)),
        out_shape=jax.ShapeDtypeStruct((n_batch * t, H_A * DH), _MXU),
        scratch_shapes=[pltpu.VMEM((nq, tq, D_IDX), _MXU)] + kv_scratch + [
            pltpu.VMEM((nq, tq, tq), jnp.int32),
            pltpu.VMEM((nq, tq, tq), jnp.int16),
            pltpu.VMEM((nq, tq, tq), jnp.int16),
            pltpu.VMEM((nq, tq, tq), jnp.float32),
        ] + flash_scratch,
        compiler_params=_params(("parallel", "arbitrary")),
        name="dsa_prompt",
    )(z, z, kiw, kiw, z, z, bias)


def _topk_lanes(gate, n_cand, n_sel):
    lane = lax.broadcasted_iota(jnp.int32, gate.shape, 1)
    live = lane < n_cand
    chosen = jnp.zeros(gate.shape, jnp.bool_)
    big = jnp.int32(2 ** 30)
    for _ in range(n_sel):
        cand = jnp.logical_and(live, jnp.logical_not(chosen))
        best = jnp.max(jnp.where(cand, gate, -jnp.inf), axis=1, keepdims=True)
        first = jnp.min(jnp.where(jnp.logical_and(cand, gate == best), lane, big), axis=1, keepdims=True)
        chosen = jnp.logical_or(chosen, lane == first)
    return chosen


def _topk_rows(gate, n_cand, n_sel):
    row = lax.broadcasted_iota(jnp.int32, gate.shape, 0)
    live = row < n_cand
    chosen = jnp.zeros(gate.shape, jnp.bool_)
    big = jnp.int32(2 ** 30)
    for _ in range(n_sel):
        cand = jnp.logical_and(live, jnp.logical_not(chosen))
        best = jnp.max(jnp.where(cand, gate, -jnp.inf), axis=0, keepdims=True)
        first = jnp.min(jnp.where(jnp.logical_and(cand, gate == best), row, big), axis=0, keepdims=True)
        chosen = jnp.logical_or(chosen, row == first)
    return chosen


def _moba_p_body(qb_ref, k_in_ref, v_in_ref, bias_ref, o_ref, kmean_ref, kb_ref, vt_ref, allow_ref,
                 m_ref, l_ref, acc_ref, *, n_sel):
    i = pl.program_id(1)
    tq = _TQ
    nb = kb_ref.shape[0]
    n_grp = H_B // HKV_B
    krow = lax.broadcasted_iota(jnp.int32, (tq, tq), 0)
    qcol = lax.broadcasted_iota(jnp.int32, (tq, tq), 1)

    @pl.when(i == 0)
    def _():
        _stage_kv(k_in_ref, v_in_ref, kb_ref, vt_ref, tq)
        kmean_ref[...] = jnp.zeros(kmean_ref.shape, jnp.float32)
        for j in range(nb):
            kmean_ref[j:j + 1, :] = jnp.mean(k_in_ref[j * tq:(j + 1) * tq, :], axis=0, keepdims=True)

    for g in range(HKV_B):
        qg = jnp.concatenate([qb_ref[:, (g * n_grp + hh) * DH:(g * n_grp + hh + 1) * DH] for hh in range(n_grp)],
                             axis=0)
        gate_t = lax.dot_general(kmean_ref[:, g * DH:(g + 1) * DH], qg, (((1,), (1,)), ((), ())),
                                 preferred_element_type=jnp.float32, precision=lax.Precision.HIGHEST)
        allow_ref[g] = jnp.where(_topk_rows(gate_t, i, n_sel), 0.0, _NEG)

    _flash_init(m_ref, l_ref, acc_ref)
    scale = DH ** -0.5 * math.log2(math.e)
    qs = [_stack_heads(qb_ref, g, n_grp, scale) for g in range(HKV_B)]
    causal_neg = jnp.where(krow <= qcol, 0.0, _NEG)
    per_head = tq // _LANES

    def head_half(g, c):
        return g * n_grp + c // per_head, slice((c % per_head) * _LANES, (c % per_head + 1) * _LANES)

    def attend_past(j, bias_of_chunk):
        allow = [allow_ref[g, pl.ds(j, 1), :] for g in range(HKV_B)]

        def extra(g, c):
            picked = allow[g][:, c * _LANES:(c + 1) * _LANES]
            if bias_of_chunk is None:
                return picked
            return bias_of_chunk(*head_half(g, c)) + picked
        _attend_t(j, qs, kb_ref, vt_ref, m_ref, l_ref, acc_ref, extra)

    def far(j, carry):
        attend_past(j, None)
        return carry

    lax.fori_loop(0, jnp.maximum(i - 1, 0), far, 0)

    @pl.when(i >= 1)
    def _():
        attend_past(i - 1, lambda h, half: bias_ref[h, 0:tq, half])

    def extra_own(g, c):
        h, half = head_half(g, c)
        return bias_ref[h, tq:2 * tq, half] + causal_neg[:, half]

    _attend_t(i, qs, kb_ref, vt_ref, m_ref, l_ref, acc_ref, extra_own)
    _flash_store_t(o_ref, l_ref, acc_ref, HKV_B, n_grp, tq)


def _moba_prompt(z, bias, n_batch, t):
    tq = _TQ
    assert t % tq == 0 and MOBA_BLOCK == tq
    nb = t // tq
    n_sel = min(MOBA_TOPK, (t - 1) // MOBA_BLOCK)
    n_grp = H_B // HKV_B
    nb_pad = -(-nb // 8) * 8
    kv_scratch, flash_scratch = _kv_scratch(nb, tq, HKV_B, n_grp)
    return pl.pallas_call(
        functools.partial(_moba_p_body, n_sel=n_sel),
        grid=(n_batch, nb),
        in_specs=[
            pl.BlockSpec((tq, H_B * DH), lambda b, i: (b * nb + i, _C_QB // (H_B * DH))),
            pl.BlockSpec((t, HKV_B * DH), lambda b, i: (b, _C_KB // (HKV_B * DH))),
            pl.BlockSpec((t, HKV_B * DH), lambda b, i: (b, _C_VB // (HKV_B * DH))),
            pl.BlockSpec((H_B, 2 * tq, tq), lambda b, i: (1, 0, 0)),
        ],
        out_specs=pl.BlockSpec((tq, H_B * DH), lambda b, i: (b * nb + i, 0)),
        out_shape=jax.ShapeDtypeStruct((n_batch * t, H_B * DH), _MXU),
        scratch_shapes=[pltpu.VMEM((nb_pad, HKV_B * DH), jnp.float32)] + kv_scratch
        + [pltpu.VMEM((HKV_B, nb_pad, n_grp * tq), jnp.float32)] + flash_scratch,
        compiler_params=_params(("parallel", "arbitrary")),
        name="moba_prompt",
    )(z, z, z, bias)


def _ret_body(q_ref, k_ref, v_ref, g_ref, cos_ref, sin_ref, intra_ref, cross_ref, tail_ref, decay_ref, gn_ref,
              s0_ref, o_ref, s_out_ref, s_ref):
    t = pl.program_id(1)

    @pl.when(t == 0)
    def _():
        s_ref[...] = s0_ref[...]

    cos = cos_ref[...]
    sin = sin_ref[...]
    for h in range(H_C):
        qh = q_ref[:, h * DK_C:(h + 1) * DK_C]
        kh = k_ref[:, h * DK_C:(h + 1) * DK_C]
        q = qh * cos + pltpu.roll(qh, DK_C // 2, axis=1) * sin
        k = (kh * cos + pltpu.roll(kh, DK_C // 2, axis=1) * sin) * (DK_C ** -0.5)
        v = v_ref[:, h * DV_C:(h + 1) * DV_C].astype(_MXU)
        s = s_ref[h]
        qm = q.astype(_MXU)
        att = lax.dot_general(qm, k.astype(_MXU), (((1,), (1,)), ((), ())),
                              preferred_element_type=jnp.float32) * intra_ref[h]
        o = (jnp.dot(att.astype(_MXU), v, preferred_element_type=jnp.float32)
             + jnp.dot(qm, s.astype(_MXU), preferred_element_type=jnp.float32) * cross_ref[h])
        kt = (k * tail_ref[h]).astype(_MXU)
        s_ref[h] = s * decay_ref[h] + lax.dot_general(kt, v, (((0,), (0,)), ((), ())),
                                                      preferred_element_type=jnp.float32)
        o = o * lax.rsqrt(jnp.mean(o * o, axis=-1, keepdims=True) + EPS)
        gate = g_ref[:, h * DV_C:(h + 1) * DV_C]
        o = o * gn_ref[:, h * DV_C:(h + 1) * DV_C] * (gate * jax.nn.sigmoid(gate))
        o_ref[:, h * DV_C:(h + 1) * DV_C] = o.astype(o_ref.dtype)

    @pl.when(t == pl.num_programs(1) - 1)
    def _():
        s_out_ref[...] = s_ref[...]


def _retention(z, s0, ret_gn, pos, n_batch, t):
    c = math.gcd(t, RET_CHUNK)
    nc = t // c
    half = DK_C // 2
    inv = ROPE_BASE ** (-jnp.arange(0, DK_C, 2, dtype=jnp.float32) / DK_C)
    ang = pos.astype(jnp.float32)[:, None] * inv[None, :]
    cos = jnp.concatenate([jnp.cos(ang), jnp.cos(ang)], axis=1)
    sin = jnp.concatenate([-jnp.sin(ang), jnp.sin(ang)], axis=1)
    assert cos.shape == (t, 2 * half)
    log_g = jnp.log1p(-jnp.exp2(-5.0 - jnp.arange(H_C, dtype=jnp.float32)))
    idx = jnp.arange(c, dtype=jnp.float32)
    diff = idx[:, None] - idx[None, :]
    intra = jnp.where(diff[None] >= 0, jnp.exp(jnp.maximum(diff, 0.0)[None] * log_g[:, None, None]), 0.0)
    cross = jnp.exp((idx[None, :] + 1.0) * log_g[:, None])[:, :, None]
    tail = jnp.exp((c - 1.0 - idx)[None, :] * log_g[:, None])[:, :, None]
    decay = jnp.exp(c * log_g)[:, None, None]
    wq = H_C * DK_C
    wv = H_C * DV_C
    return pl.pallas_call(
        _ret_body,
        grid=(n_batch, nc),
        in_specs=[
            pl.BlockSpec((c, wq), lambda b, i: (b * nc + i, _C_QC // wq)),
            pl.BlockSpec((c, wq), lambda b, i: (b * nc + i, _C_KC // wq)),
            pl.BlockSpec((c, wv), lambda b, i: (b * nc + i, _C_VC // wv)),
            pl.BlockSpec((c, wv), lambda b, i: (b * nc + i, _C_GC // wv)),
            pl.BlockSpec((c, DK_C), lambda b, i: (i, 0)),
            pl.BlockSpec((c, DK_C), lambda b, i: (i, 0)),
            pl.BlockSpec((H_C, c, c), lambda b, i: (0, 0, 0)),
            pl.BlockSpec((H_C, c, 1), lambda b, i: (0, 0, 0)),
            pl.BlockSpec((H_C, c, 1), lambda b, i: (0, 0, 0)),
            pl.BlockSpec((H_C, 1, 1), lambda b, i: (0, 0, 0)),
            pl.BlockSpec((1, wv), lambda b, i: (0, 0)),
            pl.BlockSpec((None, H_C, DK_C, DV_C), lambda b, i: (b, 0, 0, 0)),
        ],
        out_specs=[
            pl.BlockSpec((c, wv), lambda b, i: (b * nc + i, 0)),
            pl.BlockSpec((None, H_C, DK_C, DV_C), lambda b, i: (b, 0, 0, 0)),
        ],
        out_shape=[
            jax.ShapeDtypeStruct((n_batch * t, wv), _MXU),
            jax.ShapeDtypeStruct((n_batch, H_C, DK_C, DV_C), jnp.float32),
        ],
        scratch_shapes=[pltpu.VMEM((H_C, DK_C, DV_C), jnp.float32)],
        compiler_params=_params(("parallel", "arbitrary")),
        name="retention",
    )(z, z, z, z, cos, sin, intra, cross, tail, decay, ret_gn.reshape(1, wv), s0)


def _merge_body(oa_ref, ob_ref, oc_ref, wa_ref, wb_ref, wc_ref, ga_ref, gb_ref, gc_ref, o_ref):
    def term(o, w, g):
        return jax.nn.sigmoid(g[...]) * jnp.dot(o[...], w[...], preferred_element_type=jnp.float32)
    o_ref[...] = (term(oa_ref, wa_ref, ga_ref) + term(ob_ref, wb_ref, gb_ref)
                  + term(oc_ref, wc_ref, gc_ref)).astype(o_ref.dtype)


def _merge(o_a, o_b, o_c, z, w_pa, w_pb, w_pc, layer):
    m = o_a.shape[0]
    d = w_pa.shape[2]
    tm = _tile(m, 1024, 8)
    tn = _tile(d, 512)
    gate_blk = [(_C_GATES + k * d) // tn for k in range(3)]
    assert all((_C_GATES + k * d) % tn == 0 for k in range(3))
    row = lambda w: pl.BlockSpec((tm, w), lambda i, j: (i, 0))
    col = lambda w: pl.BlockSpec((None, w, tn), lambda i, j: (layer, 0, j))
    gate = lambda k: pl.BlockSpec((tm, tn), lambda i, j: (i, gate_blk[k] + j))
    return pl.pallas_call(
        _merge_body,
        grid=(m // tm, d // tn),
        in_specs=[row(o_a.shape[1]), row(o_b.shape[1]), row(o_c.shape[1]),
                  col(w_pa.shape[1]), col(w_pb.shape[1]), col(w_pc.shape[1]), gate(0), gate(1), gate(2)],
        out_specs=pl.BlockSpec((tm, tn), lambda i, j: (i, j)),
        out_shape=jax.ShapeDtypeStruct((m, d), _MXU),
        compiler_params=_params(("parallel", "parallel")),
        name="merge",
    )(o_a, o_b, o_c, w_pa, w_pb, w_pc, z, z, z)


def _conv_gate(u, u1, u2, g, cw_ref, cb_ref):
    c = cb_ref[...] + cw_ref[0:1, :] * u2 + cw_ref[1:2, :] * u1 + cw_ref[2:3, :] * u
    return 0.5 * c * (1.0 + lax.erf(c * (2.0 ** -0.5))) * g


def _ffn_p_body(h_ref, wu_ref, wg_ref, cw_ref, cb_ref, a_ref, cs_ref, tail_ref, *, seq, n_sub):
    i = pl.program_id(1)
    tm = h_ref.shape[0]
    sm = tm // n_sub

    @pl.when((i * tm) % seq == 0)
    def _():
        tail_ref[...] = jnp.zeros_like(tail_ref)

    prev = tail_ref[...]
    row = lax.broadcasted_iota(jnp.int32, (sm, 1), 0)
    for r in range(n_sub):
        h = h_ref[r * sm:(r + 1) * sm, :]
        u = jnp.dot(h, wu_ref[...], preferred_element_type=jnp.float32)
        g = jnp.dot(h, wg_ref[...], preferred_element_type=jnp.float32)
        u1 = jnp.where(row == 0, prev[7:8, :], pltpu.roll(u, 1, axis=0))
        u2 = jnp.where(row == 0, prev[6:7, :], jnp.where(row == 1, prev[7:8, :], pltpu.roll(u, 2, axis=0)))
        a_ref[r * sm:(r + 1) * sm, :] = _conv_gate(u, u1, u2, g, cw_ref, cb_ref).astype(a_ref.dtype)
        prev = u[sm - 8:sm, :]
    tail_ref[...] = prev

    @pl.when((i * tm + tm) % seq == 0)
    def _():
        cs_ref[...] = prev[8 - (CONV_W - 1):8, :]


def _ffn_in_prompt(h, w_up, w_gate, conv_w, conv_b, layer, n_batch, t):
    m, d = h.shape
    f = w_up.shape[2]
    tm = _tile(t, 1024, 8)
    tn = _tile(f, 1408)
    per_seq = t // tm
    return pl.pallas_call(
        functools.partial(_ffn_p_body, seq=t, n_sub=4 if tm % 32 == 0 else 1),
        grid=(f // tn, m // tm),
        in_specs=[
            pl.BlockSpec((tm, d), lambda j, i: (i, 0)),
            pl.BlockSpec((None, d, tn), lambda j, i: (layer, 0, j)),
            pl.BlockSpec((None, d, tn), lambda j, i: (layer, 0, j)),
            pl.BlockSpec((None, CONV_W, tn), lambda j, i: (layer, 0, j)),
            pl.BlockSpec((None, 1, tn), lambda j, i: (layer, 0, j)),
        ],
        out_specs=[
            pl.BlockSpec((tm, tn), lambda j, i: (i, j)),
            pl.BlockSpec((None, CONV_W - 1, tn), lambda j, i: (i // per_seq, 0, j)),
        ],
        out_shape=[
            jax.ShapeDtypeStruct((m, f), _MXU),
            jax.ShapeDtypeStruct((n_batch, CONV_W - 1, f), jnp.float32),
        ],
        scratch_shapes=[pltpu.VMEM((8, tn), jnp.float32)],
        compiler_params=_params(("parallel", "arbitrary")),
        name="ffn_in_prompt",
    )(h, w_up, w_gate, conv_w, conv_b)


def _ffn_s_body(h_ref, wu_ref, wg_ref, cw_ref, cb_ref, p1_ref, p2_ref, a_ref, u_ref, *, seq):
    tm = h_ref.shape[0]
    h = h_ref[...]
    u = jnp.dot(h, wu_ref[...], preferred_element_type=jnp.float32)
    g = jnp.dot(h, wg_ref[...], preferred_element_type=jnp.float32)
    pos = lax.broadcasted_iota(jnp.int32, (tm, 1), 0) % seq
    u1 = jnp.where(pos >= 1, pltpu.roll(u, 1, axis=0), p1_ref[...])
    u2 = jnp.where(pos >= 2, pltpu.roll(u, 2, axis=0), p2_ref[...])
    a_ref[...] = _conv_gate(u, u1, u2, g, cw_ref, cb_ref).astype(a_ref.dtype)
    u_ref[...] = u


def _ffn_in_sample(h, w_up, w_gate, conv_w, conv_b, layer, conv_prev, n_batch, t):
    m, d = h.shape
    f = w_up.shape[2]
    assert t >= CONV_W - 1
    tn = _tile(f, 512)
    zeros = jnp.zeros((n_batch, t, f), jnp.float32)
    p1 = zeros.at[:, 0].set(conv_prev[:, 1]).reshape(m, f)
    p2 = zeros.at[:, 0].set(conv_prev[:, 0]).at[:, 1].set(conv_prev[:, 1]).reshape(m, f)
    full = lambda w: pl.BlockSpec((m, w), lambda j: (0, 0))
    colf = lambda r: pl.BlockSpec((r, tn), lambda j: (0, j))
    colw = lambda r: pl.BlockSpec((None, r, tn), lambda j: (layer, 0, j))
    a, u = pl.pallas_call(
        functools.partial(_ffn_s_body, seq=t),
        grid=(f // tn,),
        in_specs=[full(d), colw(d), colw(d), colw(CONV_W), colw(1), colf(m), colf(m)],
        out_specs=[colf(m), colf(m)],
        out_shape=[jax.ShapeDtypeStruct((m, f), _MXU), jax.ShapeDtypeStruct((m, f), jnp.float32)],
        compiler_params=_params(("parallel",)),
        name="ffn_in_sample",
    )(h, w_up, w_gate, conv_w, conv_b, p1, p2)
    return a, u.reshape(n_batch, t, f)[:, t - (CONV_W - 1):]


def _page_specs(n_pages_per_step, layer, page_shape):
    zeros = (0,) * len(page_shape)
    return [pl.BlockSpec((None, None) + tuple(page_shape),
                         functools.partial(lambda b, s, pt, g: (layer, pt[b, s * n_pages_per_step + g]) + zeros, g=g))
            for g in range(n_pages_per_step)]


def _dsa_s_score_body(pt_ref, qi_ref, kiw_ref, *refs, n_pg):
    pages = refs[:n_pg]
    o_ref = refs[n_pg]
    ts = qi_ref.shape[0]
    ps = pages[0].shape[1]
    qst = jnp.concatenate([qi_ref[:, h * D_IDX:(h + 1) * D_IDX] for h in range(H_IDX)], axis=0).astype(_MXU)
    wi = kiw_ref[:, D_IDX:D_IDX + H_IDX] * (D_IDX ** -0.5 * H_IDX ** -0.5)
    for g in range(n_pg):
        s = jnp.dot(qst, pages[g][...].astype(_MXU), preferred_element_type=jnp.float32)
        sc = jnp.zeros((ts, ps), jnp.float32)
        for h in range(H_IDX):
            sc = sc + wi[:, h:h + 1] * jnp.maximum(s[h * ts:(h + 1) * ts], 0.0)
        o_ref[:, g * ps:(g + 1) * ps] = sc


def _dsa_s_select_body(sc_ref, qi_ref, kiw_ref, o_ref, on_ref, key_ref, *, n_sel, n_kv, ts):
    rows = qi_ref.shape[0]
    lp = sc_ref.shape[1]
    qi = qi_ref[...]
    kx = kiw_ref[:, 0:D_IDX].astype(_MXU)
    kx = jnp.concatenate([kx, jnp.zeros((_LANES - rows, D_IDX), _MXU)], axis=0)
    wi = kiw_ref[:, D_IDX:D_IDX + H_IDX] * (D_IDX ** -0.5 * H_IDX ** -0.5)
    sc = jnp.zeros((rows, _LANES), jnp.float32)
    for h in range(H_IDX):
        s = lax.dot_general(qi[:, h * D_IDX:(h + 1) * D_IDX].astype(_MXU), kx, (((1,), (1,)), ((), ())),
                            preferred_element_type=jnp.float32)
        sc = sc + wi[:, h:h + 1] * jnp.maximum(s, 0.0)
    sc = jnp.concatenate([pltpu.roll(sc[b * ts:(b + 1) * ts], (_LANES - b * ts) % _LANES, axis=1)
                          for b in range(rows // ts)], axis=0)
    r = lax.broadcasted_iota(jnp.int32, (rows, _LANES), 0) % ts
    c = lax.broadcasted_iota(jnp.int32, (rows, _LANES), 1)
    key_ref[:, 0:lp] = _sortable(sc_ref[...])
    key_ref[:, lp:lp + _LANES] = _sortable(jnp.where(c <= r, sc, -jnp.inf))

    width = lp + _LANES
    pos = lambda: lax.broadcasted_iota(jnp.int32, (rows, width), 1)
    count = lambda m: jnp.sum(m.astype(jnp.int32), axis=1, keepdims=True)
    thr = _kth_largest_key(lambda t: count(key_ref[...] >= t), n_sel, (rows, 1))
    finite = thr > _KEY_NEG_INF
    n_ge = count(key_ref[...] >= thr)
    tie = jnp.max(jnp.where(jnp.logical_and(finite, n_ge > n_sel), 1, 0)) > 0

    def with_ties():
        need = n_sel - count(key_ref[...] > thr)
        cut = _first_index_cut(lambda p: count(jnp.logical_and(key_ref[...] == thr, pos() < p)), need, (rows, 1),
                               width)
        return jnp.where(finite, cut, jnp.int32(2 ** 30))

    cut = lax.cond(tie, with_ties, lambda: jnp.full((rows, 1), 2 ** 30, jnp.int32))
    key = key_ref[...]
    keep = jnp.logical_or(key > thr, jnp.logical_and(key == thr, pos() <= cut))
    own = keep[:, lp:width]
    on_ref[...] = jnp.where(jnp.logical_and(own, c <= r), 0.0, _NEG)
    keep_f = jnp.where(keep, 1.0, 0.0).astype(_MXU)
    spread = (lax.broadcasted_iota(jnp.int32, (_LANES, n_kv * _LANES), 1) // n_kv
              == lax.broadcasted_iota(jnp.int32, (_LANES, n_kv * _LANES), 0))
    spread = jnp.where(spread, 1.0, 0.0).astype(_MXU)
    for cc in range(lp // _LANES):
        dup = jnp.dot(keep_f[:, cc * _LANES:(cc + 1) * _LANES], spread, preferred_element_type=jnp.float32)
        o_ref[:, cc * n_kv * _LANES:(cc + 1) * n_kv * _LANES] = jnp.where(dup > 0.5, 0.0, _NEG)


def _head_rows(pieces):
    return jnp.concatenate(pieces, axis=0)


def _parity_neg(n_heads, n_kv, ts, width):
    row_kv = lax.broadcasted_iota(jnp.int32, (n_heads * ts, width), 0) // ((n_heads // n_kv) * ts)
    col_kv = lax.broadcasted_iota(jnp.int32, (n_heads * ts, width), 1) % n_kv
    return jnp.where(row_kv == col_kv, 0.0, _NEG)


def _own_block(q_of_group, kn_ref, vn_ref, extra_of_head, n_kv, n_grp, ts):
    pad = jnp.zeros((_LANES - ts, DH), _MXU)
    ms, ls, os_ = [], [], []
    for g in range(n_kv):
        kt = jnp.concatenate([kn_ref[:, g * DH:(g + 1) * DH].astype(_MXU), pad], axis=0)
        vt = jnp.concatenate([vn_ref[:, g * DH:(g + 1) * DH].astype(_MXU), pad], axis=0)
        logits = lax.dot_general(q_of_group(g), kt, (((1,), (1,)), ((), ())), preferred_element_type=jnp.float32)
        logits = logits + _head_rows([extra_of_head(g * n_grp + hh) for hh in range(n_grp)])
        m = jnp.max(logits, axis=1, keepdims=True)
        p = jnp.exp(logits - m)
        ms.append(m)
        ls.append(jnp.sum(p, axis=1, keepdims=True))
        os_.append(jnp.dot(p.astype(_MXU), vt, preferred_element_type=jnp.float32))
    return _head_rows(ms), _head_rows(ls), _head_rows(os_)


def _dsa_s_attn_body(pt_ref, qa_ref, kn_ref, vn_ref, negn_ref, neg_ref, bias_ref, *refs, n_pg):
    kpages = refs[:n_pg]
    vpages = refs[n_pg:2 * n_pg]
    o_ref, m_ref, l_ref, acc_ref = refs[2 * n_pg:]
    s = pl.program_id(1)
    n_steps = pl.num_programs(1)
    ts = qa_ref.shape[0]
    rows_pg = kpages[0].shape[0]
    n_grp = H_A // HKV_A
    scale = DH ** -0.5
    qs = [_stack_heads(qa_ref, g, n_grp, scale) for g in range(HKV_A)]
    q_all = _head_rows(qs)
    wb = bias_ref.shape[2]
    own_w = HKV_A * _LANES

    @pl.when(s == 0)
    def _():
        m, l, o = _own_block(lambda g: qs[g], kn_ref, vn_ref,
                             lambda h: bias_ref[h, :, wb - own_w:wb - own_w + _LANES] + negn_ref[...],
                             HKV_A, n_grp, ts)
        m_ref[0], l_ref[0], acc_ref[0] = m, l, o

    last = s == n_steps - 1
    w = n_pg * rows_pg
    kt = jnp.concatenate([kpages[p][...] for p in range(n_pg)], axis=0).astype(_MXU)
    vt = jnp.concatenate([vpages[p][...] for p in range(n_pg)], axis=0).astype(_MXU)
    logits = lax.dot_general(q_all, kt, (((1,), (1,)), ((), ())), preferred_element_type=jnp.float32)
    neg = neg_ref[...]

    def extra_of_head(h):
        far = bias_ref[h, :, 0:1]
        base = neg + far
        near = jnp.where(last, bias_ref[h, :, wb - own_w - rows_pg:wb - own_w] - far, 0.0)
        return jnp.concatenate([base[:, 0:w - rows_pg], base[:, w - rows_pg:w] + near], axis=1)

    extra = _head_rows([extra_of_head(h) for h in range(H_A)]) + _parity_neg(H_A, HKV_A, ts, w)
    _flash_update(m_ref, l_ref, acc_ref, 0, logits + extra, vt)

    @pl.when(last)
    def _():
        o = acc_ref[0] / l_ref[0]
        for h in range(H_A):
            o_ref[:, h * DH:(h + 1) * DH] = o[h * ts:(h + 1) * ts].astype(o_ref.dtype)


def _dsa_sample(zs, kiws, cache_k, cache_v, cache_kidx, page_table, bias, layer, n_batch, ts):
    n_pages = page_table.shape[1]
    ps = cache_kidx.shape[3]
    past = n_pages * ps
    assert ps == _LANES and ts <= _LANES and ts % 8 == 0 and cache_k.shape[2] == ps * HKV_A
    n_sel = min(TOPK_IDX, (past + ts) // 4)
    n_grp = H_A // HKV_A

    gi = _tile(n_pages, 32, 1)
    scores = pl.pallas_call(
        functools.partial(_dsa_s_score_body, n_pg=gi),
        grid_spec=pltpu.PrefetchScalarGridSpec(
            num_scalar_prefetch=1,
            grid=(n_batch, n_pages // gi),
            in_specs=[pl.BlockSpec((ts, H_IDX * D_IDX), lambda b, s, pt: (b, _C_QI // (H_IDX * D_IDX))),
                      pl.BlockSpec((ts, _LANES), lambda b, s, pt: (b, 0))]
            + _page_specs(gi, layer, (D_IDX, ps)),
            out_specs=pl.BlockSpec((None, ts, gi * ps), lambda b, s, pt: (b, 0, s)),
        ),
        out_shape=jax.ShapeDtypeStruct((n_batch, ts, past), jnp.float32),
        compiler_params=_params(("parallel", "arbitrary")),
        name="dsa_sample_scores",
    )(page_table, zs, kiws, *([cache_kidx] * gi))

    rows = ts * _tile(n_batch, max(1, 32 // ts), 1)
    assert rows <= _LANES
    neg, neg_own = pl.pallas_call(
        functools.partial(_dsa_s_select_body, n_sel=n_sel, n_kv=HKV_A, ts=ts),
        grid=(n_batch * ts // rows,),
        in_specs=[pl.BlockSpec((rows, past), lambda i: (i, 0)),
                  pl.BlockSpec((rows, H_IDX * D_IDX), lambda i: (i, _C_QI // (H_IDX * D_IDX))),
                  pl.BlockSpec((rows, _LANES), lambda i: (i, 0))],
        out_specs=[pl.BlockSpec((rows, HKV_A * past), lambda i: (i, 0)),
                   pl.BlockSpec((rows, _LANES), lambda i: (i, 0))],
        out_shape=[jax.ShapeDtypeStruct((n_batch * ts, HKV_A * past), jnp.float32),
                   jax.ShapeDtypeStruct((n_batch * ts, _LANES), jnp.float32)],
        scratch_shapes=[pltpu.VMEM((rows, past + _LANES), jnp.int32)],
        compiler_params=_params(("parallel",)),
        name="dsa_sample_select",
    )(scores.reshape(n_batch * ts, past), zs, kiws)

    ga = _tile(n_pages, 32, 1)
    wkv = HKV_A * DH
    rows_pg = ps * HKV_A
    return pl.pallas_call(
        functools.partial(_dsa_s_attn_body, n_pg=ga),
        grid_spec=pltpu.PrefetchScalarGridSpec(
            num_scalar_prefetch=1,
            grid=(n_batch, n_pages // ga),
            in_specs=[pl.BlockSpec((ts, H_A * DH), lambda b, s, pt: (b, _C_QA // (H_A * DH))),
                      pl.BlockSpec((ts, wkv), lambda b, s, pt: (b, _C_KA // wkv)),
                      pl.BlockSpec((ts, wkv), lambda b, s, pt: (b, _C_VA // wkv)),
                      pl.BlockSpec((ts, _LANES), lambda b, s, pt: (b, 0)),
                      pl.BlockSpec((ts, ga * rows_pg), lambda b, s, pt: (b, s)),
                      pl.BlockSpec((H_A, ts, bias.shape[2]), lambda b, s, pt: (0, 0, 0))]
            + _page_specs(ga, layer, (rows_pg, DH)) + _page_specs(ga, layer, (rows_pg, DH)),
            out_specs=pl.BlockSpec((ts, H_A * DH), lambda b, s, pt: (b, 0)),
            scratch_shapes=[pltpu.VMEM((1, H_A * ts, 1), jnp.float32),
                            pltpu.VMEM((1, H_A * ts, 1), jnp.float32),
                            pltpu.VMEM((1, H_A * ts, DH), jnp.float32)],
        ),
        out_shape=jax.ShapeDtypeStruct((n_batch * ts, H_A * DH), _MXU),
        compiler_params=_params(("parallel", "arbitrary")),
        name="dsa_sample_attn",
    )(page_table, zs, zs, zs, neg_own, neg, bias, *([cache_k] * ga), *([cache_v] * ga))


def _moba_s_body(pt_ref, qb_ref, kn_ref, vn_ref, bias_ref, *refs, n_pg, n_sel, n_blocks):
    kpages = refs[:n_pg]
    vpages = refs[n_pg:2 * n_pg]
    o_ref, gate_ref, mb_ref, lb_ref, ob_ref = refs[2 * n_pg:]
    s = pl.program_id(1)
    n_steps = pl.num_programs(1)
    ts = qb_ref.shape[0]
    rows_pg = kpages[0].shape[0]
    n_grp = H_B // HKV_B
    rows = H_B * ts
    blk = MOBA_BLOCK * HKV_B
    bps = n_pg * rows_pg // blk
    scale = DH ** -0.5
    wb = bias_ref.shape[2]
    own_w = HKV_B * _LANES
    qf = [jnp.concatenate([qb_ref[:, (g * n_grp + hh) * DH:(g * n_grp + hh + 1) * DH] for hh in range(n_grp)], axis=0)
          for g in range(HKV_B)]
    qs = [(q * scale).astype(_MXU) for q in qf]
    q_all = _head_rows(qs)
    lane = lax.broadcasted_iota(jnp.int32, (rows, gate_ref.shape[1]), 1)
    last = s == n_steps - 1

    @pl.when(s == 0)
    def _():
        gate_ref[...] = jnp.zeros(gate_ref.shape, jnp.float32)
        mb_ref[...] = jnp.full(mb_ref.shape, _NEG, jnp.float32)
        lb_ref[...] = jnp.zeros(lb_ref.shape, jnp.float32)

    kf = jnp.concatenate([kpages[p][...] for p in range(n_pg)], axis=0)
    vt = jnp.concatenate([vpages[p][...] for p in range(n_pg)], axis=0).astype(_MXU)
    logits = lax.dot_general(q_all, kf.astype(_MXU), (((1,), (1,)), ((), ())), preferred_element_type=jnp.float32)
    far = _head_rows([bias_ref[h, :, 0:1] for h in range(H_B)])
    near = _head_rows([bias_ref[h, :, wb - own_w - blk:wb - own_w] for h in range(H_B)])
    parity = _parity_neg(H_B, HKV_B, ts, blk)
    sub_kv = lax.broadcasted_iota(jnp.int32, (8, DH), 0) % HKV_B
    gates, ms, ls = gate_ref[...], mb_ref[...], lb_ref[...]
    per_dot = 4 if bps % 4 == 0 else bps
    for j0 in range(0, bps, per_dot):
        ps_ = []
        for jb in range(j0, j0 + per_dot):
            seg = logits[:, jb * blk:(jb + 1) * blk] + (far + parity)
            if jb == bps - 1:
                seg = seg + jnp.where(last, near - far, 0.0)
            ksum = jnp.sum(kf[jb * blk:(jb + 1) * blk].reshape(blk // 8, 8, DH), axis=0)
            gate = _head_rows([
                jnp.sum(qf[g] * (jnp.sum(jnp.where(sub_kv == g, ksum, 0.0), axis=0, keepdims=True) / MOBA_BLOCK),
                        axis=1, keepdims=True) for g in range(HKV_B)])
            mj = jnp.max(seg, axis=1, keepdims=True)
            p = jnp.exp(seg - mj)
            lj = jnp.sum(p, axis=1, keepdims=True)
            here = lane == s * bps + jb
            gates = jnp.where(here, gate, gates)
            ms = jnp.where(here, mj, ms)
            ls = jnp.where(here, lj, ls)
            ps_.append(jnp.concatenate([p if k == jb else jnp.zeros_like(p) for k in range(j0, j0 + per_dot)],
                                       axis=1))
        o_blocks = jnp.dot(jnp.concatenate(ps_, axis=0).astype(_MXU), vt[j0 * blk:(j0 + per_dot) * blk],
                           preferred_element_type=jnp.float32)
        for jb in range(j0, j0 + per_dot):
            ob_ref[s * bps + jb] = o_blocks[(jb - j0) * rows:(jb - j0 + 1) * rows]
    gate_ref[...], mb_ref[...], lb_ref[...] = gates, ms, ls

    @pl.when(last)
    def _():
        r = lax.broadcasted_iota(jnp.int32, (ts, _LANES), 0)
        c = lax.broadcasted_iota(jnp.int32, (ts, _LANES), 1)
        causal_neg = jnp.where(c <= r, 0.0, _NEG)
        m_own, l_own, o_own = _own_block(lambda g: qs[g], kn_ref, vn_ref,
                                         lambda h: bias_ref[h, :, wb - own_w:wb - own_w + _LANES] + causal_neg,
                                         HKV_B, n_grp, ts)
        chosen = _topk_lanes(gate_ref[...], n_blocks, n_sel)
        mb = jnp.where(chosen, mb_ref[...], _NEG)
        m_all = jnp.maximum(m_own, jnp.max(mb, axis=1, keepdims=True))
        w = jnp.where(chosen, jnp.exp(mb - m_all), 0.0)
        w_own = jnp.exp(m_own - m_all)
        den = w_own * l_own + jnp.sum(w * lb_ref[...], axis=1, keepdims=True)

        def add_block(j, acc):
            wj = jnp.sum(jnp.where(lane == j, w, 0.0), axis=1, keepdims=True)
            return acc + wj * ob_ref[j]

        num = lax.fori_loop(0, n_blocks, add_block, w_own * o_own)
        o = num / den
        for h in range(H_B):
            o_ref[:, h * DH:(h + 1) * DH] = o[h * ts:(h + 1) * ts].astype(o_ref.dtype)


def _moba_sample(zs, cache_k, cache_v, page_table, bias, layer, n_batch, ts):
    n_pages = page_table.shape[1]
    rows_pg = cache_k.shape[2]
    ps = rows_pg // HKV_B
    past = n_pages * ps
    assert past % MOBA_BLOCK == 0 and MOBA_BLOCK % ps == 0 and ts <= _LANES and ts % 8 == 0
    n_blocks = past // MOBA_BLOCK
    n_sel = min(MOBA_TOPK, (past + ts - 1) // MOBA_BLOCK)
    ppb = MOBA_BLOCK // ps
    g_pg = ppb * _tile(n_blocks, 16, 1)
    lane_w = -(-n_blocks // _LANES) * _LANES
    wkv = HKV_B * DH
    return pl.pallas_call(
        functools.partial(_moba_s_body, n_pg=g_pg, n_sel=n_sel, n_blocks=n_blocks),
        grid_spec=pltpu.PrefetchScalarGridSpec(
            num_scalar_prefetch=1,
            grid=(n_batch, n_pages // g_pg),
            in_specs=[pl.BlockSpec((ts, H_B * DH), lambda b, s, pt: (b, _C_QB // (H_B * DH))),
                      pl.BlockSpec((ts, wkv), lambda b, s, pt: (b, _C_KB // wkv)),
                      pl.BlockSpec((ts, wkv), lambda b, s, pt: (b, _C_VB // wkv)),
                      pl.BlockSpec((H_B, ts, bias.shape[2]), lambda b, s, pt: (1, 0, 0))]
            + _page_specs(g_pg, layer, (rows_pg, DH)) + _page_specs(g_pg, layer, (rows_pg, DH)),
            out_specs=pl.BlockSpec((ts, H_B * DH), lambda b, s, pt: (b, 0)),
            scratch_shapes=[pltpu.VMEM((H_B * ts, lane_w), jnp.float32),
                            pltpu.VMEM((H_B * ts, lane_w), jnp.float32),
                            pltpu.VMEM((H_B * ts, lane_w), jnp.float32),
                            pltpu.VMEM((n_blocks, H_B * ts, DH), jnp.float32)],
        ),
        out_shape=jax.ShapeDtypeStruct((n_batch * ts, H_B * DH), _MXU),
        compiler_params=_params(("parallel", "arbitrary")),
        name="moba_sample",
    )(page_table, zs, zs, zs, bias, *([cache_k] * g_pg), *([cache_v] * g_pg))


def _relayout_w_in(w_in, d):
    w_t = jnp.swapaxes(w_in, 1, 2)
    sizes = dict(qa=H_A * DH, ka=HKV_A * DH, va=HKV_A * DH, qi=H_IDX * D_IDX, ki=D_IDX, wi=H_IDX,
                 qb=H_B * DH, kb=HKV_B * DH, vb=HKV_B * DH, qc=H_C * DK_C, kc=H_C * DK_C, vc=H_C * DV_C,
                 gc=H_C * DV_C, gates=3 * d)
    off, parts = 0, {}
    for name in ("qa", "ka", "va", "qi", "ki", "wi", "qb", "kb", "vb", "qc", "kc", "vc", "gc", "gates"):
        parts[name] = w_t[:, off:off + sizes[name]]
        off += sizes[name]
    assert off == w_t.shape[1]
    main = jnp.concatenate([parts[n] for n in ("vc", "gc", "qa", "qi", "qb", "qc", "kc", "ka", "va", "kb", "vb",
                                               "gates")], axis=1).astype(_MXU)
    kiw = jnp.concatenate([parts["ki"], parts["wi"],
                           jnp.zeros((w_t.shape[0], _LANES - D_IDX - H_IDX, d), w_in.dtype)], axis=1).astype(_MXU)
    return main, kiw


def _pad_last(w, n):
    return jnp.pad(w, [(0, 0)] * (w.ndim - 1) + [(0, n - w.shape[-1])])


def _mix_and_ffn(x, z, o_a, o_b, o_c, wts, layer, ffn_in):
    m = _merge(o_a, o_b, o_c, z, wts["w_pa"], wts["w_pb"], wts["w_pc"], layer)
    x, h2 = _out_proj(m, wts["w_out"], layer, x, wts["norm_ffn"][layer])
    a, conv_new = ffn_in(h2)
    x = _matmul(a, wts["ffn_down"], layer, jnp.float32, residual=x, tk_target=3072)
    return x, conv_new


def _state_slices(z, kiw, n_batch, t):
    ka = z[:, _C_KA:_C_KA + HKV_A * DH].reshape(n_batch, t, HKV_A, DH)
    va = z[:, _C_VA:_C_VA + HKV_A * DH].reshape(n_batch, t, HKV_A, DH)
    ki = kiw[:, :D_IDX].reshape(n_batch, t, D_IDX)
    kb = z[:, _C_KB:_C_KB + HKV_B * DH].reshape(n_batch, t, HKV_B, DH)
    vb = z[:, _C_VB:_C_VB + HKV_B * DH].reshape(n_batch, t, HKV_B, DH)
    return ka, va, ki, kb, vb


def kernel(x_prompt, x_sample, cache_a_k, cache_a_v, cache_a_kidx, cache_b_k, cache_b_v, state_ret, state_conv,
           page_table, rel_bias, norm_mix, w_in, ret_gn, w_pa, w_pb, w_pc, w_out, norm_ffn, ffn_up, ffn_gate,
           conv_w, conv_b, ffn_down, norm_final):
    bp, tp, d = x_prompt.shape
    bs, ts, _ = x_sample.shape
    depth = w_in.shape[0]
    n_pool, ps = cache_a_k.shape[1], cache_a_k.shape[2]
    past = page_table.shape[1] * ps
    f = ffn_up.shape[2]
    fp = -(-f // 512) * 512
    tq = _TQ
    assert d % _LANES == 0 and tp % tq == 0

    ar = jnp.arange
    assert tq >= MAX_DIST
    bias_p = _bias_table(rel_bias, (ar(tq)[None, :] - ar(2 * tq)[:, None] + tq).astype(jnp.int32),
                         log2_rel_to=(0, tq - 1))
    assert H_A == H_B and HKV_A == HKV_B
    rel_cached = (ar(ts)[:, None] + MOBA_BLOCK - ar(MOBA_BLOCK)[None, :]).astype(jnp.int32)
    rel_own = (ar(ts)[:, None] - ar(_LANES)[None, :]).astype(jnp.int32)
    bias_s = _bias_table(rel_bias, jnp.concatenate([jnp.repeat(rel_cached, HKV_A, axis=1), rel_own, rel_own], axis=1))
    cache_a_k = cache_a_k.reshape(depth, n_pool, ps * HKV_A, DH)
    cache_a_v = cache_a_v.reshape(depth, n_pool, ps * HKV_A, DH)
    cache_b_k = cache_b_k.reshape(depth, n_pool, ps * HKV_B, DH)
    cache_b_v = cache_b_v.reshape(depth, n_pool, ps * HKV_B, DH)
    cache_a_kidx = jnp.swapaxes(cache_a_kidx, 2, 3)

    xp = x_prompt.reshape(bp * tp, d)
    xs = x_sample.reshape(bs * ts, d)
    pos_p = jnp.arange(tp, dtype=jnp.int32)
    pos_s = past + jnp.arange(ts, dtype=jnp.int32)
    w_main, w_kiw = _relayout_w_in(w_in, d)
    wts = dict(w_pa=w_pa.astype(_MXU), w_pb=w_pb.astype(_MXU), w_pc=w_pc.astype(_MXU), w_out=w_out.astype(_MXU),
               norm_ffn=norm_ffn, ffn_down=jnp.pad(ffn_down, ((0, 0), (0, fp - f), (0, 0))).astype(_MXU))
    up = _pad_last(ffn_up, fp).astype(_MXU)
    gate = _pad_last(ffn_gate, fp).astype(_MXU)
    cw = _pad_last(conv_w, fp)
    cb = _pad_last(conv_b.reshape(depth, 1, f), fp)
    conv_prev_all = _pad_last(state_conv, fp)

    st_p, st_s = [], []
    kv_p = [jnp.zeros((depth, bp * tp, nh, DH), jnp.float32) for nh in (HKV_A, HKV_A, HKV_B, HKV_B)]
    for l in range(depth):
        z, kiw, *kv_p = _in_proj(xp, norm_mix[l], w_main, w_kiw, l, depth, kv_p)
        o_a = _dsa_prompt(z, kiw, bias_p, bp, tp)
        o_b = _moba_prompt(z, bias_p, bp, tp)
        o_c, ret_new = _retention(z, jnp.zeros((bp, H_C, DK_C, DV_C), jnp.float32), ret_gn[l], pos_p, bp, tp)
        xp, conv_new = _mix_and_ffn(xp, z, o_a, o_b, o_c, wts, l,
                                    lambda h2: _ffn_in_prompt(h2, up, gate, cw, cb, l, bp, tp))
        st_p.append((kiw[:, :D_IDX].reshape(bp, tp, D_IDX), ret_new, conv_new[:, :, :f]))

        h = _rmsnorm(xs, norm_mix[l], _MXU)
        z = _matmul_few_rows(h, w_main, l, jnp.float32)
        kiw = _matmul_few_rows(h, w_kiw, l, jnp.float32)
        o_a = _dsa_sample(z, kiw, cache_a_k, cache_a_v, cache_a_kidx, page_table, bias_s, l, bs, ts)
        o_b = _moba_sample(z, cache_b_k, cache_b_v, page_table, bias_s, l, bs, ts)
        o_c, ret_new = _retention(z, state_ret[l], ret_gn[l], pos_s, bs, ts)
        xs, conv_new = _mix_and_ffn(xs, z, o_a, o_b, o_c, wts, l,
                                    lambda h2: _ffn_in_sample(h2, up, gate, cw, cb, l, conv_prev_all[l], bs, ts))
        st_s.append(_state_slices(z, kiw, bs, ts) + (ret_new, conv_new[:, :, :f]))

    y_prompt = _rmsnorm(xp, norm_final, jnp.float32).reshape(bp, tp, d)
    y_sample = _rmsnorm(xs, norm_final, jnp.float32).reshape(bs, ts, d)
    ka_p, va_p, kb_p, vb_p = kv_p
    ki_p, ret_p, conv_p = [jnp.stack(v) for v in zip(*st_p)]
    outs_s = [jnp.stack(v) for v in zip(*st_s)]
    return (y_prompt, y_sample,
            ka_p.reshape(depth, bp, tp, HKV_A, DH), va_p.reshape(depth, bp, tp, HKV_A, DH), ki_p,
            kb_p.reshape(depth, bp, tp, HKV_B, DH), vb_p.reshape(depth, bp, tp, HKV_B, DH), ret_p, conv_p, *outs_s)
```
